```python
import math
import jax
import jax.numpy as jnp
from jax import lax

D_MODEL = 1024
BATCH = 1
SEQ = 16384
DEPTH = 2

CHUNK = 64
Q_BLOCK = 128
HEAD_DIM = D_MODEL // 16
NEG_INF = -1e30
H_A = 4
H_IDX = 4
D_IDX = HEAD_DIM // 2
TOPK_MAX = 256
H_B = 4
CONV_K = 4
H_C = 4
N_PREV_CHUNKS = 8
MAX_REL_PAST = 128
N_REL = CHUNK + MAX_REL_PAST
H_D = 4
DQ_D = HEAD_DIM // 2
W_A = H_A * HEAD_DIM
W_B = H_B * HEAD_DIM
W_C = H_C * HEAD_DIM
W_D = H_D * 2 * DQ_D
W_BRANCH = W_A
N_BRANCH = 4
COL_SIZES = (W_A, W_A, W_A, H_IDX * D_IDX, D_IDX, H_IDX,
             W_B, W_B, W_B, H_B, H_B, W_B,
             W_C, W_C, W_C,
             W_D, W_D, W_D)
C_IN = sum(COL_SIZES)
N_GROUPS = 4
EXPERTS_PER_GROUP = 8
N_EXPERTS = N_GROUPS * EXPERTS_PER_GROUP
TOP_K_INNER = 2
F_EXPERT = D_MODEL // 2
MOE_BLOCK = 128

kernel_name = 'hybrid_gated_dsa_mlstm_band_diffattn_hmoe'


def rmsnorm(x, g, eps=1e-6):
    xf = x.astype(jnp.float32)
    y = xf * lax.rsqrt(jnp.mean(xf * xf, axis=-1, keepdims=True) + eps)
    return (y * g.astype(jnp.float32)).astype(x.dtype)


def head_rmsnorm(x, g):
    h, d = x.shape[-2:]
    return rmsnorm(x, g.reshape(h, d))


def chunk_visible(t_pos, s_pos):
    return (s_pos // CHUNK) <= (t_pos // CHUNK)


def causal_dwconv(x, w, b):
    k, c = w.shape
    y = lax.conv_general_dilated(x, w[:, None, :].astype(x.dtype), window_strides=(1,),
                                 padding=[(k - 1, 0)], dimension_numbers=('NWC', 'WIO', 'NWC'),
                                 feature_group_count=c)
    return y + b.astype(x.dtype)


def split_columns(proj):
    parts = []
    off = 0
    for sz in COL_SIZES:
        parts.append(proj[..., off:off + sz])
        off += sz
    return parts


def dsa_mixer(q, k, v, q_idx, k_idx, w_idx):
    b, s, h, dh = q.shape
    topk = min(TOPK_MAX, s // 4)
    nqb = s // Q_BLOCK
    s_pos = jnp.arange(s)
    k_idx32 = k_idx.astype(jnp.float32)

    def to_blocks(a):
        return jnp.moveaxis(a.reshape(b, nqb, Q_BLOCK, *a.shape[2:]), 1, 0)

    def gather_rows(a, idx):
        return jax.vmap(lambda ab, ib: ab[ib])(a, idx)

    def block(args):
        qb, qib, wib, blk = args
        t_pos = blk * Q_BLOCK + jnp.arange(Q_BLOCK)
        visible = chunk_visible(t_pos[:, None], s_pos[None, :])
        rel = jnp.maximum(jnp.einsum('bqhd,bsd->bqhs', qib.astype(jnp.float32), k_idx32), 0.0)
        score = jnp.einsum('bqhs,bqh->bqs', rel, wib.astype(jnp.float32))
        score = jnp.where(visible[None], score, NEG_INF)
        _, sel = lax.top_k(score, topk)
        sel_ok = chunk_visible(t_pos[None, :, None], sel)
        k_sel = gather_rows(k, sel)
        v_sel = gather_rows(v, sel)
        logits = jnp.einsum('bqhd,bqkhd->bhqk', qb, k_sel).astype(jnp.float32) * (dh ** -0.5)
        logits = jnp.where(sel_ok[:, None], logits, NEG_INF)
        p = jax.nn.softmax(logits, axis=-1).astype(v.dtype)
        return jnp.einsum('bhqk,bqkhd->bqhd', p, v_sel)

    out = lax.map(block, (to_blocks(q), to_blocks(q_idx), to_blocks(w_idx), jnp.arange(nqb)))
    return jnp.moveaxis(out, 0, 1).reshape(b, s, h, dh)


def mlstm_mixer(q, k, v, i_pre, f_pre, o_pre, norm_g):
    b, s, h, d = q.shape
    n_chunks = s // CHUNK

    def to_chunks(a):
        a = a.reshape(b, n_chunks, CHUNK, h, *a.shape[3:])
        return jnp.moveaxis(jnp.moveaxis(a, 1, 0), 3, 2)

    qc = to_chunks(q.astype(jnp.float32))
    kc = to_chunks(k.astype(jnp.float32) * (d ** -0.5))
    vc = to_chunks(v.astype(jnp.float32))
    ic = to_chunks(i_pre.astype(jnp.float32))
    lfc = to_chunks(jax.nn.log_sigmoid(f_pre.astype(jnp.float32)))
    causal = jnp.tril(jnp.ones((CHUNK, CHUNK), dtype=bool))

    def step(carry, xs):
        c_mat, n_vec, m = carry
        qt, kt, vt, it, lft = xs
        cum = jnp.cumsum(lft, axis=-1)
        dmat = cum[..., :, None] - cum[..., None, :] + it[..., None, :]
        dmat = jnp.where(causal, dmat, NEG_INF)
        m_inter = cum + m[..., None]
        m_t = jnp.maximum(m_inter, jnp.max(dmat, axis=-1))
        w = jnp.exp(dmat - m_t[..., None]) * jnp.einsum('bhtd,bhsd->bhts', qt, kt)
        inter = jnp.exp(m_inter - m_t)
        num = inter[..., None] * jnp.einsum('bhtk,bhvk->bhtv', qt, c_mat) + jnp.einsum('bhts,bhsv->bhtv', w, vt)
        den = inter * jnp.einsum('bhtk,bhk->bht', qt, n_vec) + jnp.sum(w, axis=-1)
        h_t = num / jnp.maximum(jnp.abs(den), jnp.exp(-m_t))[..., None]
        cum_end = cum[..., -1]
        g = cum_end[..., None] - cum + it
        m_new = jnp.maximum(cum_end + m, jnp.max(g, axis=-1))
        carry_scale = jnp.exp(cum_end + m - m_new)
        src = jnp.exp(g - m_new[..., None])
        c_new = carry_scale[..., None, None] * c_mat + jnp.einsum('bhs,bhsv,bhsk->bhvk', src, vt, kt)
        n_new = carry_scale[..., None] * n_vec + jnp.einsum('bhs,bhsk->bhk', src, kt)
        return (c_new, n_new, m_new), h_t

    init = (jnp.zeros((b, h, d, d), jnp.float32), jnp.zeros((b, h, d), jnp.float32),
            jnp.zeros((b, h), jnp.float32))
    _, hc = lax.scan(step, init, (qc, kc, vc, ic, lfc))
    hs = jnp.moveaxis(jnp.moveaxis(hc, 2, 3), 0, 1).reshape(b, s, h, d).astype(q.dtype)
    return jax.nn.sigmoid(o_pre) * head_rmsnorm(hs, norm_g)


def chunk_band_attention(q, k, v, rel_bias):
    b, s, h, d = q.shape
    n_chunks = s // CHUNK
    n_band = N_PREV_CHUNKS + 1
    qc = q.reshape(b, n_chunks, CHUNK, h, d)
    pad = ((0, 0), (N_PREV_CHUNKS, 0), (0, 0), (0, 0), (0, 0))
    kp = jnp.pad(k.reshape(b, n_chunks, CHUNK, h, d), pad)
    vp = jnp.pad(v.reshape(b, n_chunks, CHUNK, h, d), pad)
    band_idx = jnp.arange(n_chunks)[:, None] + jnp.arange(n_band)[None, :]
    kb = kp[:, band_idx].reshape(b, n_chunks, n_band * CHUNK, h, d)
    vb = vp[:, band_idx].reshape(b, n_chunks, n_band * CHUNK, h, d)
    band_ok = (jnp.arange(n_chunks)[:, None] - N_PREV_CHUNKS + jnp.arange(n_band)[None, :]) >= 0
    key_ok = jnp.repeat(band_ok, CHUNK, axis=1)
    dist = N_PREV_CHUNKS * CHUNK + jnp.arange(CHUNK)[:, None] - jnp.arange(n_band * CHUNK)[None, :]
    rel_id = jnp.clip(dist, -(CHUNK - 1), MAX_REL_PAST) + (CHUNK - 1)
    bias = rel_bias[:, rel_id].astype(jnp.float32)
    logits = jnp.einsum('bcihd,bcphd->bhcip', qc, kb).astype(jnp.float32) * (d ** -0.5) + bias[:, None]
    logits = jnp.where(key_ok[None, None, :, None, :], logits, NEG_INF)
    p = jax.nn.softmax(logits, axis=-1).astype(v.dtype)
    return jnp.einsum('bhcip,bcphd->bcihd', p, vb).reshape(b, s, h, d)


def diff_attention(q, k, v, lam_params, norm_g, lam_init):
    b, s, h, _, dq = q.shape
    lp = lam_params.astype(jnp.float32)
    lam = jnp.exp(jnp.sum(lp[0] * lp[1])) - jnp.exp(jnp.sum(lp[2] * lp[3])) + lam_init
    nqb = s // Q_BLOCK
    s_pos = jnp.arange(s)
    q_blocks = jnp.moveaxis(q.reshape(b, nqb, Q_BLOCK, h, 2, dq), 1, 0)

    def block(args):
        qb, blk = args
        t_pos = blk * Q_BLOCK + jnp.arange(Q_BLOCK)
        visible = chunk_visible(t_pos[:, None], s_pos[None, :])
        logits = jnp.einsum('bqhrd,bshrd->bhrqs', qb, k).astype(jnp.float32) * (dq ** -0.5)
        p = jax.nn.softmax(jnp.where(visible, logits, NEG_INF), axis=-1)
        a = (p[:, :, 0] - lam * p[:, :, 1]).astype(v.dtype)
        return jnp.einsum('bhqs,bshd->bqhd', a, v)

    out = lax.map(block, (q_blocks, jnp.arange(nqb)))
    out = jnp.moveaxis(out, 0, 1).reshape(b, s, h, v.shape[-1])
    return head_rmsnorm(out, norm_g) * (1.0 - lam_init)


def hier_moe(h, gw, gb, ew, eb, w1, w3, w2):
    b, s, d = h.shape
    t = b * s
    xt = h.reshape(t, d)
    g_logits = jnp.einsum('td,dg->tg', xt, gw).astype(jnp.float32) + gb.astype(jnp.float32)
    g_sel = jnp.argmax(g_logits, axis=-1)
    g_gate = jnp.take_along_axis(jax.nn.softmax(g_logits, axis=-1), g_sel[:, None], axis=1)
    e_logits = (jnp.einsum('td,de->te', xt, ew).astype(jnp.float32) + eb.astype(jnp.float32))
    e_logits = e_logits.reshape(t, N_GROUPS, EXPERTS_PER_GROUP)
    e_in_group = jnp.take_along_axis(e_logits, g_sel[:, None, None], axis=1)[:, 0]
    top_v, top_i = lax.top_k(e_in_group, TOP_K_INNER)
    gate = jax.nn.softmax(top_v, axis=-1) * g_gate
    n_assign = t * TOP_K_INNER
    eid = (g_sel[:, None] * EXPERTS_PER_GROUP + top_i).reshape(n_assign).astype(jnp.int32)
    tok = jnp.repeat(jnp.arange(t, dtype=jnp.int32), TOP_K_INNER)
    wt = gate.reshape(n_assign)
    order = jnp.argsort(eid)
    se = eid[order]
    counts = jnp.bincount(eid, length=N_EXPERTS)
    padded = (counts + MOE_BLOCK - 1) // MOE_BLOCK * MOE_BLOCK
    ends_pad = jnp.cumsum(padded)
    starts_pad = ends_pad - padded
    starts = jnp.cumsum(counts) - counts
    dest = starts_pad[se] + (jnp.arange(n_assign) - starts[se])
    n_rows = -(-(n_assign + N_EXPERTS * (MOE_BLOCK - 1)) // MOE_BLOCK) * MOE_BLOCK
    n_blocks = n_rows // MOE_BLOCK
    row_tok = jnp.zeros((n_rows,), jnp.int32).at[dest].set(tok[order])
    row_w = jnp.zeros((n_rows,), jnp.float32).at[dest].set(wt[order])
    block_e = jnp.minimum(jnp.searchsorted(ends_pad, jnp.arange(n_blocks) * MOE_BLOCK, side='right'),
                          N_EXPERTS - 1)
    xs = xt[row_tok].reshape(n_blocks, MOE_BLOCK, d)

    def expert_block(args):
        xb, e = args
        return (jax.nn.silu(xb @ w1[e]) * (xb @ w3[e])) @ w2[e]

    ys = lax.map(expert_block, (xs, block_e)).reshape(n_rows, d)
    out = jnp.zeros((t, d), h.dtype).at[row_tok].add((ys * row_w[:, None]).astype(h.dtype))
    return out.reshape(b, s, d)


def hybrid_layer(x, layer_idx, norm_mix_g, w_in, conv_w, conv_b, i_bias, f_bias, mnorm_g,
                 rel_bias, lam_p, dnorm_g, w_branch, w_gate, w_out, norm_ffn_g,
                 rgw, rgb, rew, reb, w1, w3, w2):
    b, s, d = x.shape
    h = rmsnorm(x, norm_mix_g)
    proj = jnp.einsum('bsd,dc->bsc', h, w_in)
    (qa, ka, va, qi, ki, wi, qb, kb, vb, ib, fb, ob, qc, kc, vc, qd, kd, vd) = split_columns(proj)

    y_a = dsa_mixer(qa.reshape(b, s, H_A, HEAD_DIM), ka.reshape(b, s, H_A, HEAD_DIM),
                    va.reshape(b, s, H_A, HEAD_DIM), qi.reshape(b, s, H_IDX, D_IDX), ki, wi)

    qk_b = jax.nn.silu(causal_dwconv(jnp.concatenate([qb, kb], axis=-1), conv_w, conv_b))
    y_b = mlstm_mixer(qk_b[..., :W_B].reshape(b, s, H_B, HEAD_DIM),
                      qk_b[..., W_B:].reshape(b, s, H_B, HEAD_DIM),
                      vb.reshape(b, s, H_B, HEAD_DIM), ib + i_bias, fb + f_bias,
                      ob.reshape(b, s, H_B, HEAD_DIM), mnorm_g)

    y_c = chunk_band_attention(qc.reshape(b, s, H_C, HEAD_DIM), kc.reshape(b, s, H_C, HEAD_DIM),
                               vc.reshape(b, s, H_C, HEAD_DIM), rel_bias)

    lam_init = 0.8 - 0.6 * math.exp(-0.3 * layer_idx)
    y_d = diff_attention(qd.reshape(b, s, H_D, 2, DQ_D), kd.reshape(b, s, H_D, 2, DQ_D),
                         vd.reshape(b, s, H_D, 2 * DQ_D), lam_p, dnorm_g, lam_init)

    branches = jnp.stack([y_a.reshape(b, s, W_BRANCH), y_b.reshape(b, s, W_BRANCH),
                          y_c.reshape(b, s, W_BRANCH), y_d.reshape(b, s, W_BRANCH)], axis=2)
    up = jnp.einsum('bsnw,nwd->bsnd', branches, w_branch)
    gates = jax.nn.sigmoid(jnp.einsum('bsd,de->bse', h, w_gate)).reshape(b, s, N_BRANCH, d)
    mixed = jnp.sum(gates * up, axis=2)
    x = x + jnp.einsum('bsd,de->bse', mixed, w_out)

    h2 = rmsnorm(x, norm_ffn_g)
    return x + hier_moe(h2, rgw, rgb, rew, reb, w1, w3, w2)


def setup_inputs(seed: int = 0) -> dict:
    key = jax.random.key(seed)
    ks = jax.random.split(key, 24)
    f32 = jnp.float32
    nl = DEPTH

    def nrm(k, shape, scale):
        return jax.random.normal(k, shape, f32) * scale

    return {
        'x': nrm(ks[0], (BATCH, SEQ, D_MODEL), 1.0),
        'norm_mix_g': 1.0 + nrm(ks[1], (nl, D_MODEL), 0.01),
        'w_in': nrm(ks[2], (nl, D_MODEL, C_IN), D_MODEL ** -0.5),
        'conv_w': nrm(ks[3], (nl, CONV_K, 2 * W_B), CONV_K ** -0.5),
        'conv_b': nrm(ks[4], (nl, 2 * W_B), 0.01),
        'mlstm_i_bias': nrm(ks[5], (nl, H_B), 0.1),
        'mlstm_f_bias': jnp.linspace(3.0, 6.0, H_B, dtype=f32)[None, :] + nrm(ks[6], (nl, H_B), 0.1),
        'mlstm_norm_g': 1.0 + nrm(ks[7], (nl, W_B), 0.01),
        'relpos_bias': nrm(ks[8], (nl, H_C, N_REL), 0.1),
        'diff_lambda': nrm(ks[9], (nl, 4, DQ_D), 0.1),
        'diff_norm_g': 1.0 + nrm(ks[10], (nl, W_D), 0.01),
        'w_branch': nrm(ks[11], (nl, N_BRANCH, W_BRANCH, D_MODEL), W_BRANCH ** -0.5),
        'w_gate': nrm(ks[12], (nl, D_MODEL, N_BRANCH * D_MODEL), D_MODEL ** -0.5),
        'w_out': nrm(ks[13], (nl, D_MODEL, D_MODEL), (N_BRANCH * D_MODEL) ** -0.5),
        'norm_ffn_g': 1.0 + nrm(ks[14], (nl, D_MODEL), 0.01),
        'router_group_w': nrm(ks[15], (nl, D_MODEL, N_GROUPS), D_MODEL ** -0.5),
        'router_group_b': nrm(ks[16], (nl, N_GROUPS), 0.01),
        'router_expert_w': nrm(ks[17], (nl, D_MODEL, N_EXPERTS), D_MODEL ** -0.5),
        'router_expert_b': nrm(ks[18], (nl, N_EXPERTS), 0.01),
        'expert_w1': nrm(ks[19], (nl, N_EXPERTS, D_MODEL, F_EXPERT), D_MODEL ** -0.5),
        'expert_w3': nrm(ks[20], (nl, N_EXPERTS, D_MODEL, F_EXPERT), D_MODEL ** -0.5),
        'expert_w2': nrm(ks[21], (nl, N_EXPERTS, F_EXPERT, D_MODEL), F_EXPERT ** -0.5),
        'final_norm_g': 1.0 + nrm(ks[22], (D_MODEL,), 0.01),
    }


def reference(x, norm_mix_g, w_in, conv_w, conv_b, mlstm_i_bias, mlstm_f_bias, mlstm_norm_g,
              relpos_bias, diff_lambda, diff_norm_g, w_branch, w_gate, w_out, norm_ffn_g,
              router_group_w, router_group_b, router_expert_w, router_expert_b,
              expert_w1, expert_w3, expert_w2, final_norm_g):
    for l in range(DEPTH):
        x = hybrid_layer(x, l, norm_mix_g[l], w_in[l], conv_w[l], conv_b[l], mlstm_i_bias[l],
                         mlstm_f_bias[l], mlstm_norm_g[l], relpos_bias[l], diff_lambda[l],
                         diff_norm_g[l], w_branch[l], w_gate[l], w_out[l], norm_ffn_g[l],
                         router_group_w[l], router_group_b[l], router_expert_w[l],
                         router_expert_b[l], expert_w1[l], expert_w3[l], expert_w2[l])
    return rmsnorm(x, final_norm_g)
```

```python
import functools
import math

import jax
import jax.numpy as jnp
from jax import lax
from jax.experimental import pallas as pl
from jax.experimental.pallas import tpu as pltpu

F32 = jnp.float32
BF16 = jnp.bfloat16
I32 = jnp.int32

D_MODEL = 1024
CHUNK = 64
HEAD_DIM = 64
NEG_INF = -1e30
H_A = 4
H_IDX = 4
D_IDX = 32
TOPK_MAX = 256
H_B = 4
CONV_K = 4
H_C = 4
N_PREV_CHUNKS = 8
MAX_REL_PAST = 128
H_D = 4
DQ_D = 32
W_BRANCH = 256
N_BRANCH = 4
N_GROUPS = 4
EXPERTS_PER_GROUP = 8
N_EXPERTS = 32
TOP_K_INNER = 2
F_EXPERT = 512
EPS = 1e-6

VMEM_LIMIT_BYTES = 52 * 1024 * 1024
LANES = 128

INT_MIN = -(2 ** 31)
M_INIT = -5e29

_COL_SIZES = (256, 256, 256, 128, 32, 4, 256, 256, 256, 4, 4, 256, 256, 256, 256, 256, 256, 256)
_COL_NAMES = ("qa", "ka", "va", "qi", "ki", "wi", "qb", "kb", "vb", "ib", "fb", "ob",
              "qc", "kc", "vc", "qd", "kd", "vd")
_COL_OFF = {}
_o = 0
for _n, _s in zip(_COL_NAMES, _COL_SIZES):
    _COL_OFF[_n] = (_o, _s)
    _o += _s
C_IN = _o
MISC_KI = 0
MISC_WI = 32
MISC_IB = 36
MISC_FB = 40


def _cparams(sem):
    return pltpu.CompilerParams(dimension_semantics=sem, vmem_limit_bytes=VMEM_LIMIT_BYTES)


def _whole(shape):
    nd = len(shape)
    return pl.BlockSpec(shape, lambda *_: (0,) * nd)


_K1_OUTS = (("qa", 256, BF16), ("ka", 256, BF16), ("va", 256, BF16), ("qi", 128, BF16),
            ("misc", 128, F32), ("qkb", 512, F32), ("vb", 256, F32), ("ob", 256, F32),
            ("qc", 256, BF16), ("kc", 256, BF16), ("vc", 256, BF16),
            ("qd", 256, BF16), ("kd", 256, BF16), ("vd", 256, BF16))
K1_WIDTH = sum(w for _, w, _ in _K1_OUTS)


def _rearrange_w_in(w_in):
    def cols(name):
        o, s = _COL_OFF[name]
        return w_in[:, o:o + s]
    misc = jnp.concatenate([cols("ki"), cols("wi"), cols("ib"), cols("fb"),
                            jnp.zeros((w_in.shape[0], LANES - 44), w_in.dtype)], axis=1)
    parts = [cols("qa"), cols("ka"), cols("va"), cols("qi"), misc, cols("qb"), cols("kb"),
             cols("vb"), cols("ob"), cols("qc"), cols("kc"), cols("vc"), cols("qd"), cols("kd"),
             cols("vd")]
    return jnp.concatenate(parts, axis=1).astype(BF16)


def _rms(x, g):
    ms = jnp.mean(x * x, axis=-1, keepdims=True)
    return x * lax.rsqrt(ms + EPS) * g


def _k1_kernel(x_ref, g_ref, w_ref, *out_refs):
    h = _rms(x_ref[...], g_ref[...]).astype(BF16)
    off = 0
    for o_ref, (_, width, _) in zip(out_refs, _K1_OUTS):
        o_ref[...] = jnp.dot(h, w_ref[:, off:off + width],
                             preferred_element_type=F32).astype(o_ref.dtype)
        off += width


def _k1_call(x2, g, w_r):
    s = x2.shape[0]
    tm = 512
    outs = tuple(jax.ShapeDtypeStruct((s, w), dt) for _, w, dt in _K1_OUTS)
    return pl.pallas_call(
        _k1_kernel,
        out_shape=outs,
        grid=(s // tm,),
        in_specs=[pl.BlockSpec((tm, D_MODEL), lambda i: (i, 0)),
                  _whole((1, D_MODEL)),
                  _whole((D_MODEL, K1_WIDTH))],
        out_specs=tuple(pl.BlockSpec((tm, w), lambda i: (i, 0)) for _, w, _ in _K1_OUTS),
        compiler_params=_cparams(("parallel",)),
        name="k1_norm_proj",
    )(x2, g.reshape(1, D_MODEL), w_r)


A_TQ = 128
A_KT = 512


def _dsa_kernel(topk, qa_ref, qi_ref, misc_ref, kaT_ref, va_ref, kiT_ref, tri_ref, o_ref,
                keys_ref, acc_ref, m_ref, l_ref, carry_ref):
    tq, kt = A_TQ, A_KT
    b = pl.program_id(0)
    ntiles = ((b + 1) * tq + kt - 1) // kt
    t_pos = b * tq + lax.broadcasted_iota(I32, (tq, 1), 0)
    vis_end = (t_pos // CHUNK + 1) * CHUNK
    wi = misc_ref[:, MISC_WI:MISC_WI + H_IDX]

    def p1(j, c):
        s0 = pl.multiple_of(j * kt, kt)
        ki_t = kiT_ref[:, pl.ds(s0, kt)]
        score = jnp.zeros((tq, kt), F32)
        for h in range(H_IDX):
            r = jnp.dot(qi_ref[:, h * D_IDX:(h + 1) * D_IDX], ki_t, preferred_element_type=F32)
            score = score + jnp.maximum(r, 0.0) * wi[:, h:h + 1]
        bits = lax.bitcast_convert_type(score, I32)
        key = bits ^ (lax.shift_right_arithmetic(bits, 31) & 0x7FFFFFFF)
        s_pos = s0 + lax.broadcasted_iota(I32, (1, kt), 1)
        keys_ref[:, pl.ds(s0, kt)] = jnp.where(s_pos < vis_end, key, INT_MIN)
        return c
    lax.fori_loop(0, ntiles, p1, 0)

    def count(cand, strict):
        cb = jnp.broadcast_to(cand, (tq, LANES))

        def body(j, acc):
            s0 = pl.multiple_of(j * kt, kt)
            for u in range(kt // LANES):
                k = keys_ref[:, pl.ds(s0 + u * LANES, LANES)]
                hit = (k > cb) if strict else (k >= cb)
                acc = acc + jnp.where(hit, 1.0, 0.0)
            return acc
        acc = lax.fori_loop(0, ntiles, body, jnp.zeros((tq, LANES), F32))
        return jnp.sum(acc, axis=1, keepdims=True)

    kf = float(topk)
    zero = jnp.zeros((tq, 1), I32)
    tau = jnp.where(count(zero, False) >= kf, zero, INT_MIN)

    def bit_body(i, tau):
        cand = tau | lax.shift_left(jnp.int32(1), 30 - i)
        return jnp.where(count(cand, False) >= kf, cand, tau)
    tau = lax.fori_loop(0, 31, bit_body, tau)
    need = jnp.where(tau == INT_MIN, 0.0, kf - count(tau, True))

    m_ref[...] = jnp.full(m_ref.shape, M_INIT, F32)
    l_ref[...] = jnp.zeros(l_ref.shape, F32)
    acc_ref[...] = jnp.zeros(acc_ref.shape, F32)
    carry_ref[...] = jnp.zeros(carry_ref.shape, F32)
    tau_b = jnp.broadcast_to(tau, (tq, kt))
    q_all = qa_ref[...] * 0.125

    def p3(j, c):
        s0 = pl.multiple_of(j * kt, kt)
        k = keys_ref[:, pl.ds(s0, kt)]
        gt = k > tau_b
        eq = k == tau_b
        eqf = jnp.where(eq, 1.0, 0.0).astype(BF16)
        pref = jnp.dot(eqf, tri_ref[...], preferred_element_type=F32)
        carry = carry_ref[...]
        sel = gt | (eq & (pref + carry < need))
        carry_ref[...] = carry + jnp.sum(jnp.where(eq, 1.0, 0.0), axis=1, keepdims=True)
        for h in range(H_A):
            s = jnp.dot(q_all[:, h * HEAD_DIM:(h + 1) * HEAD_DIM],
                        kaT_ref[h * HEAD_DIM:(h + 1) * HEAD_DIM, pl.ds(s0, kt)],
                        preferred_element_type=F32)
            s = jnp.where(sel, s, NEG_INF)
            m_old = m_ref[h]
            m_new = jnp.maximum(m_old, jnp.max(s, axis=1, keepdims=True))
            alpha = jnp.exp(m_old - m_new)
            p = jnp.exp(s - m_new)
            l_ref[h] = alpha * l_ref[h] + jnp.sum(p, axis=1, keepdims=True)
            pv = jnp.dot(p.astype(BF16), va_ref[pl.ds(s0, kt), (h // 2) * LANES:(h // 2 + 1) * LANES],
                         preferred_element_type=F32)
            acc_ref[h] = alpha * acc_ref[h] + pv
            m_ref[h] = m_new
        return c
    lax.fori_loop(0, ntiles, p3, 0)

    lane = lax.broadcasted_iota(I32, (tq, LANES), 1)
    for pr in range(H_A // 2):
        lo = acc_ref[2 * pr] / l_ref[2 * pr]
        hi = acc_ref[2 * pr + 1] / l_ref[2 * pr + 1]
        o_ref[:, pr * LANES:(pr + 1) * LANES] = jnp.where(lane < HEAD_DIM, lo, hi).astype(o_ref.dtype)


def _dsa_call(qa, qi, misc, kaT, va, kiT):
    s = qa.shape[0]
    topk = min(TOPK_MAX, s // 4)
    tri = jnp.triu(jnp.ones((A_KT, A_KT), F32), k=1).astype(BF16)
    return pl.pallas_call(
        functools.partial(_dsa_kernel, topk),
        out_shape=jax.ShapeDtypeStruct((s, W_BRANCH), BF16),
        grid=(s // A_TQ,),
        in_specs=[pl.BlockSpec((A_TQ, 256), lambda i: (i, 0)),
                  pl.BlockSpec((A_TQ, 128), lambda i: (i, 0)),
                  pl.BlockSpec((A_TQ, 128), lambda i: (i, 0)),
                  pl.BlockSpec(memory_space=pltpu.VMEM),
                  pl.BlockSpec(memory_space=pltpu.VMEM),
                  pl.BlockSpec(memory_space=pltpu.VMEM),
                  pl.BlockSpec(memory_space=pltpu.VMEM)],
        out_specs=pl.BlockSpec((A_TQ, 256), lambda i: (i, 0)),
        scratch_shapes=[pltpu.VMEM((A_TQ, s), I32),
                        pltpu.VMEM((H_A, A_TQ, LANES), F32),
                        pltpu.VMEM((H_A, A_TQ, 1), F32),
                        pltpu.VMEM((H_A, A_TQ, 1), F32),
                        pltpu.VMEM((A_TQ, 1), F32)],
        compiler_params=_cparams(("arbitrary",)),
        name="dsa_mixer",
    )(qa, qi, misc, kaT, va, kiT, tri)


def _pair_select(lo, hi):
    lane = lax.broadcasted_iota(I32, lo.shape, 1)
    return jnp.where(lane < HEAD_DIM, lo, hi)


def _pair_head_rms(o, g):
    lane = lax.broadcasted_iota(I32, o.shape, 1)
    low = lane < HEAD_DIM
    sq = o * o
    ms_lo = jnp.sum(jnp.where(low, sq, 0.0), axis=1, keepdims=True) * (1.0 / HEAD_DIM)
    ms_hi = jnp.sum(jnp.where(low, 0.0, sq), axis=1, keepdims=True) * (1.0 / HEAD_DIM)
    ms = jnp.where(low, ms_lo, ms_hi)
    return o * lax.rsqrt(ms + EPS) * g


D_TQ = 256
D_KT = 256


def _diff_kernel(lam_init, qd_ref, kdT_ref, vd_ref, lam_ref, g_ref, o_ref, acc_ref, m_ref, l_ref):
    tq, kt = D_TQ, D_KT
    b = pl.program_id(0)
    scale = DQ_D ** -0.5
    t_pos = b * tq + lax.broadcasted_iota(I32, (tq, 1), 0)
    vis_end = (t_pos // CHUNK + 1) * CHUNK
    m_ref[...] = jnp.full(m_ref.shape, M_INIT, F32)
    l_ref[...] = jnp.zeros(l_ref.shape, F32)
    acc_ref[...] = jnp.zeros(acc_ref.shape, F32)
    q_all = qd_ref[...]

    def tile(j, masked):
        s0 = pl.multiple_of(j * kt, kt)
        if masked:
            vis = (s0 + lax.broadcasted_iota(I32, (1, kt), 1)) < vis_end
        for h in range(H_D):
            v_t = vd_ref[pl.ds(s0, kt), (h // 2) * LANES:(h // 2 + 1) * LANES]
            for r in range(2):
                c0 = h * 2 * DQ_D + r * DQ_D
                s = jnp.dot(q_all[:, c0:c0 + DQ_D], kdT_ref[c0:c0 + DQ_D, pl.ds(s0, kt)],
                            preferred_element_type=F32) * scale
                if masked:
                    s = jnp.where(vis, s, NEG_INF)
                i = 2 * h + r
                m_old = m_ref[i]
                m_new = jnp.maximum(m_old, jnp.max(s, axis=1, keepdims=True))
                alpha = jnp.exp(m_old - m_new)
                p = jnp.exp(s - m_new)
                l_ref[i] = alpha * l_ref[i] + jnp.sum(p, axis=1, keepdims=True)
                acc_ref[i] = alpha * acc_ref[i] + jnp.dot(p.astype(BF16), v_t,
                                                          preferred_element_type=F32)
                m_ref[i] = m_new

    def full_tile(j, c):
        tile(j, False)
        return c
    lax.fori_loop(0, b, full_tile, 0)
    tile(b, True)

    lp = lam_ref[...]
    lam = (jnp.exp(jnp.sum(lp[0:1] * lp[1:2], axis=1, keepdims=True))
           - jnp.exp(jnp.sum(lp[2:3] * lp[3:4], axis=1, keepdims=True)) + lam_init)
    for pr in range(H_D // 2):
        heads = []
        for h in (2 * pr, 2 * pr + 1):
            heads.append(acc_ref[2 * h] / l_ref[2 * h] - lam * (acc_ref[2 * h + 1] / l_ref[2 * h + 1]))
        o = _pair_select(heads[0], heads[1])
        y = _pair_head_rms(o, g_ref[:, pr * LANES:(pr + 1) * LANES]) * (1.0 - lam_init)
        o_ref[:, pr * LANES:(pr + 1) * LANES] = y.astype(o_ref.dtype)


def _diff_call(qd, kdT, vd, lam_p, dnorm_g, lam_init):
    s = qd.shape[0]
    return pl.pallas_call(
        functools.partial(_diff_kernel, lam_init),
        out_shape=jax.ShapeDtypeStruct((s, W_BRANCH), BF16),
        grid=(s // D_TQ,),
        in_specs=[pl.BlockSpec((D_TQ, 256), lambda i: (i, 0)),
                  pl.BlockSpec(memory_space=pltpu.VMEM),
                  pl.BlockSpec(memory_space=pltpu.VMEM),
                  _whole((4, DQ_D)),
                  _whole((1, 256))],
        out_specs=pl.BlockSpec((D_TQ, 256), lambda i: (i, 0)),
        scratch_shapes=[pltpu.VMEM((2 * H_D, D_TQ, LANES), F32),
                        pltpu.VMEM((2 * H_D, D_TQ, 1), F32),
                        pltpu.VMEM((2 * H_D, D_TQ, 1), F32)],
        compiler_params=_cparams(("parallel",)),
        name="diff_attention",
    )(qd, kdT, vd, lam_p, dnorm_g.reshape(1, 256))


C_TQ = 128
C_PAD = N_PREV_CHUNKS * CHUNK
C_WIN = C_PAD + C_TQ


def _band_bias_table(rel_bias):
    i = jnp.arange(C_TQ)[:, None]
    w = jnp.arange(C_WIN)[None, :]
    dist = i + C_PAD - w
    rel_id = jnp.clip(dist, -(CHUNK - 1), MAX_REL_PAST) + (CHUNK - 1)
    in_band = (w // CHUNK >= i // CHUNK) & (w // CHUNK <= i // CHUNK + N_PREV_CHUNKS)
    return jnp.where(in_band[None], rel_bias[:, rel_id].astype(F32), NEG_INF)


def _band_kernel(qc_ref, kcT_ref, vc_ref, bias_ref, o_ref):
    tq = C_TQ
    b = pl.program_id(0)
    w0 = pl.multiple_of(b * tq, tq)
    q_all = qc_ref[...] * 0.125
    key_abs = b * tq - C_PAD + lax.broadcasted_iota(I32, (1, C_WIN), 1)
    ok = key_abs >= 0
    heads = []
    for h in range(H_C):
        s = jnp.dot(q_all[:, h * HEAD_DIM:(h + 1) * HEAD_DIM],
                    kcT_ref[h * HEAD_DIM:(h + 1) * HEAD_DIM, pl.ds(w0, C_WIN)],
                    preferred_element_type=F32) + bias_ref[h]
        s = jnp.where(ok, s, NEG_INF)
        m = jnp.max(s, axis=1, keepdims=True)
        p = jnp.exp(s - m)
        l = jnp.sum(p, axis=1, keepdims=True)
        pv = jnp.dot(p.astype(BF16), vc_ref[pl.ds(w0, C_WIN), (h // 2) * LANES:(h // 2 + 1) * LANES],
                     preferred_element_type=F32)
        heads.append(pv / l)
    for pr in range(H_C // 2):
        o_ref[:, pr * LANES:(pr + 1) * LANES] = _pair_select(heads[2 * pr], heads[2 * pr + 1]).astype(o_ref.dtype)


def _band_call(qc, kcT_pad, vc_pad, bias_tab):
    s = qc.shape[0]
    return pl.pallas_call(
        _band_kernel,
        out_shape=jax.ShapeDtypeStruct((s, W_BRANCH), BF16),
        grid=(s // C_TQ,),
        in_specs=[pl.BlockSpec((C_TQ, 256), lambda i: (i, 0)),
                  pl.BlockSpec(memory_space=pltpu.VMEM),
                  pl.BlockSpec(memory_space=pltpu.VMEM),
                  _whole((H_C, C_TQ, C_WIN))],
        out_specs=pl.BlockSpec((C_TQ, 256), lambda i: (i, 0)),
        compiler_params=_cparams(("parallel",)),
        name="band_attention",
    )(qc, kcT_pad, vc_pad, bias_tab)


B_L = CHUNK
HIGHEST = lax.Precision.HIGHEST


def _mlstm_kernel(qk_ref, vb_ref, ob_ref, misc_ref, cw_ref, cb_ref, gb_ref, ng_ref, o_ref,
                  tail_ref, ct_ref, n_ref, m_ref):
    L = B_L
    hd = HEAD_DIM

    @pl.when(pl.program_id(0) == 0)
    def _():
        tail_ref[...] = jnp.zeros(tail_ref.shape, F32)
        ct_ref[...] = jnp.zeros(ct_ref.shape, F32)
        n_ref[...] = jnp.zeros(n_ref.shape, F32)
        m_ref[...] = jnp.zeros(m_ref.shape, F32)

    x = qk_ref[...]
    xx = jnp.concatenate([tail_ref[...], x], axis=0)
    y = jnp.broadcast_to(cb_ref[...], x.shape)
    for j in range(CONV_K):
        y = y + cw_ref[j:j + 1, :] * xx[8 - (CONV_K - 1) + j:8 - (CONV_K - 1) + j + L, :]
    tail_ref[...] = x[L - 8:L, :]
    qk = y * jax.nn.sigmoid(y)
    q_all = qk[:, :W_BRANCH]
    k_all = qk[:, W_BRANCH:] * 0.125

    gts = misc_ref[...] + gb_ref[...]
    lf = jnp.minimum(gts, 0.0) - jnp.log1p(jnp.exp(-jnp.abs(gts)))
    r_i = lax.broadcasted_iota(I32, (L, L), 0)
    c_i = lax.broadcasted_iota(I32, (L, L), 1)
    causal = c_i <= r_i
    ltri = jnp.where(causal, 1.0, 0.0)
    cum = jnp.dot(ltri, lf, precision=HIGHEST, preferred_element_type=F32)
    lane = lax.broadcasted_iota(I32, (L, LANES), 1)
    mixed = jnp.where(lane < MISC_FB, gts, cum)
    sel_r = lax.broadcasted_iota(I32, (8, LANES), 0)
    sel_c = lax.broadcasted_iota(I32, (8, LANES), 1)
    sel = jnp.where(sel_c == sel_r + MISC_IB, 1.0, 0.0)
    rows = lax.dot_general(sel, mixed, (((1,), (1,)), ((), ())), precision=HIGHEST,
                           preferred_element_type=F32)

    outs = []
    for h in range(H_B):
        q = q_all[:, h * hd:(h + 1) * hd]
        k = k_all[:, h * hd:(h + 1) * hd]
        v = vb_ref[:, h * hd:(h + 1) * hd]
        it_r = rows[h:h + 1, :]
        cum_r = rows[H_B + h:H_B + h + 1, :]
        it_c = gts[:, MISC_IB + h:MISC_IB + h + 1]
        cum_c = cum[:, MISC_FB + h:MISC_FB + h + 1]
        m_prev = m_ref[h]
        dmat = jnp.where(causal, cum_c - cum_r + it_r, NEG_INF)
        m_inter = cum_c + m_prev
        m_t = jnp.maximum(m_inter, jnp.max(dmat, axis=1, keepdims=True))
        qb, kb, vbf = q.astype(BF16), k.astype(BF16), v.astype(BF16)
        qkt = lax.dot_general(qb, kb, (((1,), (1,)), ((), ())), preferred_element_type=F32)
        w = jnp.exp(dmat - m_t) * qkt
        inter = jnp.exp(m_inter - m_t)
        ct = ct_ref[h]
        num = inter * jnp.dot(qb, ct.astype(BF16), preferred_element_type=F32) \
            + jnp.dot(w.astype(BF16), vbf, preferred_element_type=F32)
        n_row = n_ref[h]
        den = inter * jnp.sum(q * n_row, axis=1, keepdims=True) + jnp.sum(w, axis=1, keepdims=True)
        h_t = num / jnp.maximum(jnp.abs(den), jnp.exp(-m_t))
        cum_end = cum_c[L - 1:L, :]
        g = cum_end - cum_c + it_c
        m_new = jnp.maximum(cum_end + m_prev, jnp.max(g, axis=0, keepdims=True))
        carry_scale = jnp.exp(cum_end + m_prev - m_new)
        src_k = jnp.exp(g - m_new) * k
        ct_ref[h] = carry_scale * ct + lax.dot_general(src_k.astype(BF16), vbf, (((0,), (0,)), ((), ())),
                                                       preferred_element_type=F32)
        n_ref[h] = carry_scale * n_row + jnp.sum(src_k, axis=0, keepdims=True)
        m_ref[h] = m_new
        ms = jnp.mean(h_t * h_t, axis=1, keepdims=True)
        hn = h_t * lax.rsqrt(ms + EPS) * ng_ref[:, h * hd:(h + 1) * hd]
        outs.append(jax.nn.sigmoid(ob_ref[:, h * hd:(h + 1) * hd]) * hn)
    o_ref[...] = jnp.concatenate(outs, axis=1).astype(o_ref.dtype)


def _mlstm_call(qkb, vb, ob, misc, conv_w, conv_b, i_bias, f_bias, norm_g):
    s = qkb.shape[0]
    gbias = jnp.zeros((1, LANES), F32)
    gbias = gbias.at[0, MISC_IB:MISC_IB + H_B].set(i_bias).at[0, MISC_FB:MISC_FB + H_B].set(f_bias)
    row = lambda w: pl.BlockSpec((B_L, w), lambda i: (i, 0))
    return pl.pallas_call(
        _mlstm_kernel,
        out_shape=jax.ShapeDtypeStruct((s, W_BRANCH), BF16),
        grid=(s // B_L,),
        in_specs=[row(512), row(256), row(256), row(128),
                  _whole((CONV_K, 512)), _whole((1, 512)), _whole((1, LANES)), _whole((1, 256))],
        out_specs=row(256),
        scratch_shapes=[pltpu.VMEM((8, 512), F32),
                        pltpu.VMEM((H_B, HEAD_DIM, HEAD_DIM), F32),
                        pltpu.VMEM((H_B, 1, HEAD_DIM), F32),
                        pltpu.VMEM((H_B, 1, 1), F32)],
        compiler_params=_cparams(("arbitrary",)),
        name="mlstm_mixer",
    )(qkb, vb, ob, misc, conv_w, conv_b.reshape(1, 512), gbias, norm_g.reshape(1, 256))


M_TM = 256
ROUTE_LOGIT0 = N_GROUPS
RT_E1, RT_E2, RT_G1, RT_G2 = 0, 1, 2, 3


def _lane_argmax(vals, lane):
    v = jnp.max(vals, axis=1, keepdims=True)
    idx = jnp.min(jnp.where(vals == v, lane, float(LANES)), axis=1, keepdims=True)
    return v, idx


def _merge_kernel(x_ref, ya_ref, yb_ref, yc_ref, yd_ref, gm_ref, wg_ref, wb_ref, wo_ref, gf_ref,
                  wr_ref, rb_ref, xo_ref, h2_ref, rt_ref):
    x = x_ref[...]
    h = _rms(x, gm_ref[...]).astype(BF16)
    mixed = jnp.zeros(x.shape, F32)
    for n, y_ref in enumerate((ya_ref, yb_ref, yc_ref, yd_ref)):
        gate = jax.nn.sigmoid(jnp.dot(h, wg_ref[:, n * D_MODEL:(n + 1) * D_MODEL],
                                      preferred_element_type=F32))
        up = jnp.dot(y_ref[...], wb_ref[n], preferred_element_type=F32)
        mixed = mixed + gate * up
    xn = x + jnp.dot(mixed.astype(BF16), wo_ref[...], preferred_element_type=F32)
    xo_ref[...] = xn
    h2 = _rms(xn, gf_ref[...])
    h2_ref[...] = h2
    logits = jnp.dot(h2.astype(BF16), wr_ref[...], preferred_element_type=F32) + rb_ref[...]

    lane = lax.broadcasted_iota(I32, logits.shape, 1).astype(F32)
    neg = -jnp.inf
    gmask = lane < N_GROUPS
    gmax, g_sel = _lane_argmax(jnp.where(gmask, logits, neg), lane)
    g_gate = 1.0 / jnp.sum(jnp.where(gmask, jnp.exp(logits - gmax), 0.0), axis=1, keepdims=True)
    e_lo = ROUTE_LOGIT0 + EXPERTS_PER_GROUP * g_sel
    el = jnp.where((lane >= e_lo) & (lane < e_lo + EXPERTS_PER_GROUP), logits, neg)
    v1, i1 = _lane_argmax(el, lane)
    v2, i2 = _lane_argmax(jnp.where(lane == i1, neg, el), lane)
    e = jnp.exp(v2 - v1)
    p1 = 1.0 / (1.0 + e)
    p2 = e / (1.0 + e)
    rt = jnp.where(lane == RT_E1, i1 - ROUTE_LOGIT0,
                   jnp.where(lane == RT_E2, i2 - ROUTE_LOGIT0,
                             jnp.where(lane == RT_G1, p1 * g_gate,
                                       jnp.where(lane == RT_G2, p2 * g_gate, 0.0))))
    rt_ref[...] = rt


def _merge_call(x2, ya, yb, yc, yd, g_mix, w_gate, w_branch, w_out, g_ffn, w_route, b_route):
    s = x2.shape[0]
    row = lambda w: pl.BlockSpec((M_TM, w), lambda i: (i, 0))
    vm = pl.BlockSpec(memory_space=pltpu.VMEM)
    return pl.pallas_call(
        _merge_kernel,
        out_shape=(jax.ShapeDtypeStruct((s, D_MODEL), F32), jax.ShapeDtypeStruct((s, D_MODEL), F32),
                   jax.ShapeDtypeStruct((s, LANES), F32)),
        grid=(s // M_TM,),
        in_specs=[row(D_MODEL), row(256), row(256), row(256), row(256),
                  vm, vm, vm, vm, vm, vm, vm],
        out_specs=(row(D_MODEL), row(D_MODEL), row(LANES)),
        compiler_params=_cparams(("parallel",)),
        name="merge_route",
    )(x2, ya, yb, yc, yd, g_mix.reshape(1, D_MODEL), w_gate, w_branch, w_out,
      g_ffn.reshape(1, D_MODEL), w_route, b_route)


E_BM = 128


def _moe_kernel(be_ref, nused_ref, tok_ref, h2_hbm, w1_ref, w3_ref, w2_ref, o_ref,
                xbuf, w1b, w3b, w2b, sem):
    i = pl.program_id(0)
    used = i < nused_ref[0]

    @pl.when(used)
    def _():
        def row_copy(r):
            return pltpu.make_async_copy(h2_hbm.at[pl.ds(tok_ref[0, 0, r], 1)],
                                         xbuf.at[pl.ds(r, 1)], sem)

        def start(r, c):
            row_copy(r).start()
            return c
        lax.fori_loop(0, E_BM, start, 0)

        prev = be_ref[jnp.maximum(i - 1, 0)]

        @pl.when((i == 0) | (be_ref[i] != prev))
        def _():
            w1b[...] = w1_ref[0].astype(BF16)
            w3b[...] = w3_ref[0].astype(BF16)
            w2b[...] = w2_ref[0].astype(BF16)

        def wait(r, c):
            row_copy(r).wait()
            return c
        lax.fori_loop(0, E_BM, wait, 0)

        xb = xbuf[...].astype(BF16)
        a = jnp.dot(xb, w1b[...], preferred_element_type=F32)
        g = jnp.dot(xb, w3b[...], preferred_element_type=F32)
        hid = (a * jax.nn.sigmoid(a) * g).astype(BF16)
        o_ref[...] = jnp.dot(hid, w2b[...], preferred_element_type=F32)

    @pl.when(jnp.logical_not(used))
    def _():
        o_ref[...] = jnp.zeros(o_ref.shape, F32)


def _moe_call(h2, row_tok, block_e, nused, w1, w3, w2):
    n_rows = row_tok.shape[0]
    n_blocks = n_rows // E_BM
    wspec = lambda shp: pl.BlockSpec((1,) + shp, lambda i, be, nu: (be[i], 0, 0))
    grid_spec = pltpu.PrefetchScalarGridSpec(
        num_scalar_prefetch=2,
        grid=(n_blocks,),
        in_specs=[pl.BlockSpec((1, 1, E_BM), lambda i, be, nu: (i, 0, 0), memory_space=pltpu.SMEM),
                  pl.BlockSpec(memory_space=pl.ANY),
                  wspec((D_MODEL, F_EXPERT)), wspec((D_MODEL, F_EXPERT)), wspec((F_EXPERT, D_MODEL))],
        out_specs=pl.BlockSpec((E_BM, D_MODEL), lambda i, be, nu: (i, 0)),
        scratch_shapes=[pltpu.VMEM((E_BM, D_MODEL), F32),
                        pltpu.VMEM((D_MODEL, F_EXPERT), BF16),
                        pltpu.VMEM((D_MODEL, F_EXPERT), BF16),
                        pltpu.VMEM((F_EXPERT, D_MODEL), BF16),
                        pltpu.SemaphoreType.DMA(())],
    )
    return pl.pallas_call(
        _moe_kernel,
        out_shape=jax.ShapeDtypeStruct((n_rows, D_MODEL), F32),
        grid_spec=grid_spec,
        compiler_params=_cparams(("arbitrary",)),
        name="moe_experts",
    )(block_e, nused, row_tok.reshape(n_blocks, 1, E_BM), h2, w1, w3, w2)


CB_TM = 128


def _combine_kernel(final, pos_ref, x_ref, rt_ref, ys_hbm, gfin_ref, o_ref, buf, sem):
    def row_copy(r):
        return pltpu.make_async_copy(ys_hbm.at[pl.ds(pos_ref[0, 0, r], 1)],
                                     buf.at[pl.ds(r, 1)], sem)

    def start(r, c):
        row_copy(r).start()
        return c
    lax.fori_loop(0, 2 * CB_TM, start, 0)

    def wait(r, c):
        row_copy(r).wait()
        return c
    lax.fori_loop(0, 2 * CB_TM, wait, 0)

    rt = rt_ref[...]
    y = (buf[0:CB_TM, :] * rt[:, RT_G1:RT_G1 + 1] + buf[CB_TM:2 * CB_TM, :] * rt[:, RT_G2:RT_G2 + 1])
    xn = x_ref[...] + y
    if final:
        xn = _rms(xn, gfin_ref[...])
    o_ref[...] = xn


def _combine_call(x2, rt, ys, pos, g_final, final):
    s = x2.shape[0]
    nb = s // CB_TM
    pos_b = pos.reshape(nb, CB_TM, 2).transpose(0, 2, 1).reshape(nb, 1, 2 * CB_TM)
    return pl.pallas_call(
        functools.partial(_combine_kernel, final),
        out_shape=jax.ShapeDtypeStruct((s, D_MODEL), F32),
        grid=(nb,),
        in_specs=[pl.BlockSpec((1, 1, 2 * CB_TM), lambda i: (i, 0, 0), memory_space=pltpu.SMEM),
                  pl.BlockSpec((CB_TM, D_MODEL), lambda i: (i, 0)),
                  pl.BlockSpec((CB_TM, LANES), lambda i: (i, 0)),
                  pl.BlockSpec(memory_space=pl.ANY),
                  _whole((1, D_MODEL))],
        out_specs=pl.BlockSpec((CB_TM, D_MODEL), lambda i: (i, 0)),
        scratch_shapes=[pltpu.VMEM((2 * CB_TM, D_MODEL), F32), pltpu.SemaphoreType.DMA(())],
        compiler_params=_cparams(("arbitrary",)),
        name="moe_combine",
    )(pos_b, x2, rt, ys, g_final.reshape(1, D_MODEL))


def _dispatch_plan(rt):
    t = rt.shape[0]
    n_assign = t * TOP_K_INNER
    eid = rt[:, RT_E1:RT_E2 + 1].astype(I32).reshape(n_assign)
    onehot = (eid[:, None] == jnp.arange(N_EXPERTS, dtype=I32)[None, :]).astype(I32)
    csum = jnp.cumsum(onehot, axis=0)
    counts = csum[-1]
    rank = jnp.take_along_axis(csum, eid[:, None], axis=1)[:, 0] - 1
    padded = (counts + E_BM - 1) // E_BM * E_BM
    ends_pad = jnp.cumsum(padded)
    starts_pad = ends_pad - padded
    dest = starts_pad[eid] + rank
    n_rows = -(-(n_assign + N_EXPERTS * (E_BM - 1)) // E_BM) * E_BM
    n_blocks = n_rows // E_BM
    tok = jnp.repeat(jnp.arange(t, dtype=I32), TOP_K_INNER)
    row_tok = jnp.zeros((n_rows,), I32).at[dest].set(tok)
    block_e = jnp.minimum(jnp.searchsorted(ends_pad, jnp.arange(n_blocks, dtype=I32) * E_BM, side='right'),
                          N_EXPERTS - 1).astype(I32)
    nused = (ends_pad[-1] // E_BM).astype(I32).reshape(1)
    return row_tok, block_e, nused, dest.reshape(t, TOP_K_INNER)


def _layer(x2, layer_idx, p, final_g, final):
    (norm_mix_g, w_in, conv_w, conv_b, i_bias, f_bias, mnorm_g, rel_bias, lam_p, dnorm_g,
     w_branch, w_gate, w_out, norm_ffn_g, rgw, rgb, rew, reb, w1, w3, w2) = p
    o = dict(zip([n for n, _, _ in _K1_OUTS], _k1_call(x2, norm_mix_g, _rearrange_w_in(w_in))))
    misc = o["misc"]

    kiT = misc[:, MISC_KI:MISC_KI + D_IDX].T.astype(BF16)
    ya = _dsa_call(o["qa"], o["qi"], misc, o["ka"].T, o["va"], kiT)
    yb = _mlstm_call(o["qkb"], o["vb"], o["ob"], misc, conv_w, conv_b, i_bias, f_bias, mnorm_g)
    kcT_pad = jnp.pad(o["kc"].T, ((0, 0), (C_PAD, 0)))
    vc_pad = jnp.pad(o["vc"], ((C_PAD, 0), (0, 0)))
    yc = _band_call(o["qc"], kcT_pad, vc_pad, _band_bias_table(rel_bias))
    lam_init = 0.8 - 0.6 * math.exp(-0.3 * layer_idx)
    yd = _diff_call(o["qd"], o["kd"].T, o["vd"], lam_p, dnorm_g, lam_init)

    w_route = jnp.concatenate([rgw, rew, jnp.zeros((D_MODEL, LANES - N_GROUPS - N_EXPERTS), F32)],
                              axis=1).astype(BF16)
    b_route = jnp.concatenate([rgb, reb, jnp.zeros((LANES - N_GROUPS - N_EXPERTS,), F32)]).reshape(1, LANES)
    xn, h2, rt = _merge_call(x2, ya, yb, yc, yd, norm_mix_g, w_gate.astype(BF16), w_branch.astype(BF16),
                             w_out.astype(BF16), norm_ffn_g, w_route, b_route)
    row_tok, block_e, nused, pos = _dispatch_plan(rt)
    ys = _moe_call(h2, row_tok, block_e, nused, w1, w3, w2)
    return _combine_call(xn, rt, ys, pos, final_g, final)


def kernel(x, norm_mix_g, w_in, conv_w, conv_b, mlstm_i_bias, mlstm_f_bias, mlstm_norm_g, relpos_bias, diff_lambda, diff_norm_g, w_branch, w_gate, w_out, norm_ffn_g, router_group_w, router_group_b, router_expert_w, router_expert_b, expert_w1, expert_w3, expert_w2, final_norm_g):
    assert x.shape[0] == 1 and x.shape[2] == D_MODEL
    params = (norm_mix_g, w_in, conv_w, conv_b, mlstm_i_bias, mlstm_f_bias, mlstm_norm_g, relpos_bias,
              diff_lambda, diff_norm_g, w_branch, w_gate, w_out, norm_ffn_g, router_group_w,
              router_group_b, router_expert_w, router_expert_b, expert_w1, expert_w3, expert_w2)
    depth = norm_mix_g.shape[0]
    x2 = x[0]
    for l in range(depth):
        x2 = _layer(x2, l, tuple(a[l] for a in params), final_norm_g, l == depth - 1)
    return x2[None]
```

```python
import functools
import math

import jax
import jax.numpy as jnp
from jax import lax
from jax.experimental import pallas as pl
from jax.experimental.pallas import tpu as pltpu

F32 = jnp.float32
BF16 = jnp.bfloat16
I32 = jnp.int32

D_MODEL = 1024
CHUNK = 64
HEAD_DIM = 64
NEG_INF = -1e30
H_A = 4
H_IDX = 4
D_IDX = 32
TOPK_MAX = 256
H_B = 4
CONV_K = 4
H_C = 4
N_PREV_CHUNKS = 8
MAX_REL_PAST = 128
H_D = 4
DQ_D = 32
W_BRANCH = 256
N_BRANCH = 4
N_GROUPS = 4
EXPERTS_PER_GROUP = 8
N_EXPERTS = 32
TOP_K_INNER = 2
F_EXPERT = 512
EPS = 1e-6

VMEM_LIMIT_BYTES = 52 * 1024 * 1024
LANES = 128

INT_MIN = -(2 ** 31)
M_INIT = -5e29

_COL_SIZES = (256, 256, 256, 128, 32, 4, 256, 256, 256, 4, 4, 256, 256, 256, 256, 256, 256, 256)
_COL_NAMES = ("qa", "ka", "va", "qi", "ki", "wi", "qb", "kb", "vb", "ib", "fb", "ob",
              "qc", "kc", "vc", "qd", "kd", "vd")
_COL_OFF = {}
_o = 0
for _n, _s in zip(_COL_NAMES, _COL_SIZES):
    _COL_OFF[_n] = (_o, _s)
    _o += _s
C_IN = _o
MISC_KI = 0
MISC_WI = 32
MISC_IB = 36
MISC_FB = 40


def _cparams(sem):
    return pltpu.CompilerParams(dimension_semantics=sem, vmem_limit_bytes=VMEM_LIMIT_BYTES)


def _whole(shape):
    nd = len(shape)
    return pl.BlockSpec(shape, lambda *_: (0,) * nd)


_K1_OUTS = (("qa", 256, BF16), ("ka", 256, BF16), ("va", 256, BF16), ("qi", 128, BF16),
            ("misc", 128, F32), ("qkb", 512, F32), ("vb", 256, F32), ("ob", 256, F32),
            ("qc", 256, BF16), ("kc", 256, BF16), ("vc", 256, BF16),
            ("qd", 256, BF16), ("kd", 256, BF16), ("vd", 256, BF16))
K1_WIDTH = sum(w for _, w, _ in _K1_OUTS)


def _rearrange_w_in(w_in):
    def cols(name):
        o, s = _COL_OFF[name]
        return w_in[:, o:o + s]
    misc = jnp.concatenate([cols("ki"), cols("wi"), cols("ib"), cols("fb"),
                            jnp.zeros((w_in.shape[0], LANES - 44), w_in.dtype)], axis=1)
    parts = [cols("qa"), cols("ka"), cols("va"), cols("qi"), misc, cols("qb"), cols("kb"),
             cols("vb"), cols("ob"), cols("qc"), cols("kc"), cols("vc"), cols("qd"), cols("kd"),
             cols("vd")]
    return jnp.concatenate(parts, axis=1).astype(BF16)


def _rms(x, g):
    ms = jnp.mean(x * x, axis=-1, keepdims=True)
    return x * lax.rsqrt(ms + EPS) * g


def _k1_kernel(x_ref, g_ref, w_ref, *out_refs):
    h = _rms(x_ref[...], g_ref[...]).astype(BF16)
    off = 0
    for o_ref, (_, width, _) in zip(out_refs, _K1_OUTS):
        o_ref[...] = jnp.dot(h, w_ref[:, off:off + width],
                             preferred_element_type=F32).astype(o_ref.dtype)
        off += width


def _k1_call(x2, g, w_r):
    s = x2.shape[0]
    tm = 512
    outs = tuple(jax.ShapeDtypeStruct((s, w), dt) for _, w, dt in _K1_OUTS)
    return pl.pallas_call(
        _k1_kernel,
        out_shape=outs,
        grid=(s // tm,),
        in_specs=[pl.BlockSpec((tm, D_MODEL), lambda i: (i, 0)),
                  _whole((1, D_MODEL)),
                  _whole((D_MODEL, K1_WIDTH))],
        out_specs=tuple(pl.BlockSpec((tm, w), lambda i: (i, 0)) for _, w, _ in _K1_OUTS),
        compiler_params=_cparams(("parallel",)),
        name="k1_norm_proj",
    )(x2, g.reshape(1, D_MODEL), w_r)


A_TQ = 256
A_KT = 512
A_CNT_ROWS = 32
LOG2E = 1.4426950408889634
QK_AHEAD = 4


def _flash_step(s, i, v_t, c, m_ref, l_ref, acc_ref):
    m_old = m_ref[i]
    m_new = jnp.maximum(m_old, jnp.max(s, axis=0, keepdims=True))
    alpha = jnp.exp2((m_old - m_new) * c)
    p = jnp.exp2((s - m_new) * c)
    l_ref[i] = alpha * l_ref[i] + jnp.sum(p, axis=0, keepdims=True)
    acc_ref[i] = alpha * acc_ref[i] + jnp.dot(v_t, p.astype(BF16), preferred_element_type=F32)
    m_ref[i] = m_new


def _flash_tile(k_t, qpad_ref, n, v_tile, mask, c, m_ref, l_ref, acc_ref):
    pend = [jnp.dot(k_t, qpad_ref[i], preferred_element_type=F32) for i in range(min(QK_AHEAD, n))]
    for i in range(n):
        s = pend.pop(0)
        if i + QK_AHEAD < n:
            pend.append(jnp.dot(k_t, qpad_ref[i + QK_AHEAD], preferred_element_type=F32))
        if mask is not None:
            s = jnp.where(mask, s, NEG_INF)
        _flash_step(s, i, v_tile(i), c, m_ref, l_ref, acc_ref)


def _dsa_kernel(topk, qaT_ref, qiT_ref, wiT_ref, ka_ref, vaT_ref, mb_ref, tri_ref, o_ref,
                keys_ref, qpad_ref, qipad_ref, acc_ref, m_ref, l_ref, carry_ref):
    tq, kt = A_TQ, A_KT
    b = pl.program_id(0)
    ntiles = ((b + 1) * tq + kt - 1) // kt
    q_pos = b * tq + lax.broadcasted_iota(I32, (1, tq), 1)
    vis_end = (q_pos // CHUNK + 1) * CHUNK

    qiT = qiT_ref[...]
    for h in range(H_IDX):
        qipad_ref[h, 0:D_IDX, :] = qiT[h * D_IDX:(h + 1) * D_IDX, :]
        qipad_ref[h, D_IDX:LANES, :] = jnp.zeros((LANES - D_IDX, tq), BF16)

    def p1(j, carry):
        s0 = pl.multiple_of(j * kt, kt)
        mb = mb_ref[pl.ds(s0, kt), :]
        score = jnp.zeros((kt, tq), F32)
        for h in range(H_IDX):
            r = jnp.dot(mb, qipad_ref[h], preferred_element_type=F32)
            score = score + jnp.maximum(r, 0.0) * wiT_ref[h:h + 1, :]
        bits = lax.bitcast_convert_type(score, I32)
        key = bits ^ (lax.shift_right_arithmetic(bits, 31) & 0x7FFFFFFF)
        s_pos = s0 + lax.broadcasted_iota(I32, (kt, 1), 0)
        keys_ref[pl.ds(s0, kt), :] = jnp.where(s_pos < vis_end, key, INT_MIN)
        return carry
    lax.fori_loop(0, ntiles, p1, 0)

    def count(cand, strict):
        def body(j, acc):
            k = keys_ref[pl.ds(pl.multiple_of(j * kt, kt), kt), :]
            hit = (k > cand) if strict else (k >= cand)
            ones = jnp.where(hit, 1.0, 0.0)
            return acc + jnp.sum(ones.reshape(kt // A_CNT_ROWS, A_CNT_ROWS, tq), axis=0)
        acc = lax.fori_loop(0, ntiles, body, jnp.zeros((A_CNT_ROWS, tq), F32))
        return jnp.sum(acc, axis=0, keepdims=True)

    kf = float(topk)
    zero = jnp.zeros((1, tq), I32)
    tau = jnp.where(count(zero, False) >= kf, zero, INT_MIN)

    def bit_body(i, tau):
        cand = tau | lax.shift_left(jnp.int32(1), 30 - i)
        return jnp.where(count(cand, False) >= kf, cand, tau)
    tau = lax.fori_loop(0, 31, bit_body, tau)
    need = jnp.where(tau == INT_MIN, 0.0, kf - count(tau, True))

    m_ref[...] = jnp.full(m_ref.shape, M_INIT, F32)
    l_ref[...] = jnp.zeros(l_ref.shape, F32)
    acc_ref[...] = jnp.zeros(acc_ref.shape, F32)
    carry_ref[...] = jnp.zeros(carry_ref.shape, F32)
    qaT = qaT_ref[...] * 0.125
    row = lax.broadcasted_iota(I32, qaT.shape, 0)
    for h in range(H_A):
        qpad_ref[h] = jnp.where((row >= h * HEAD_DIM) & (row < (h + 1) * HEAD_DIM), qaT,
                                jnp.zeros_like(qaT))

    def p3(j, carry):
        s0 = pl.multiple_of(j * kt, kt)
        k = keys_ref[pl.ds(s0, kt), :]
        gt = k > tau
        eq = k == tau
        eqf = jnp.where(eq, 1.0, 0.0)
        pref = jnp.dot(tri_ref[...], eqf.astype(BF16), preferred_element_type=F32)
        seen = carry_ref[...]
        sel = gt | (eq & (pref + seen < need))
        carry_ref[...] = seen + pref[kt - 1:kt, :] + eqf[kt - 1:kt, :]
        k_t = ka_ref[pl.ds(s0, kt), :]
        _flash_tile(k_t, qpad_ref, H_A, lambda i: vaT_ref[(i // 2) * LANES:(i // 2 + 1) * LANES, pl.ds(s0, kt)],
                    sel, LOG2E, m_ref, l_ref, acc_ref)
        return carry
    lax.fori_loop(0, ntiles, p3, 0)

    ys = []
    for h in range(H_A):
        r0 = (h % 2) * HEAD_DIM
        ys.append(acc_ref[h, r0:r0 + HEAD_DIM, :] / l_ref[h])
    o_ref[...] = jnp.concatenate(ys, axis=0).T.astype(o_ref.dtype)


def _dsa_call(qaT, qiT, wiT, ka, vaT, misc_bf):
    s = ka.shape[0]
    topk = min(TOPK_MAX, s // 4)
    tri = jnp.tril(jnp.ones((A_KT, A_KT), F32), k=-1).astype(BF16)
    vm = pl.BlockSpec(memory_space=pltpu.VMEM)
    return pl.pallas_call(
        functools.partial(_dsa_kernel, topk),
        out_shape=jax.ShapeDtypeStruct((s, W_BRANCH), BF16),
        grid=(s // A_TQ,),
        in_specs=[pl.BlockSpec((256, A_TQ), lambda i: (0, i)),
                  pl.BlockSpec((LANES, A_TQ), lambda i: (0, i)),
                  pl.BlockSpec((8, A_TQ), lambda i: (0, i)),
                  vm, vm, vm, vm],
        out_specs=pl.BlockSpec((A_TQ, 256), lambda i: (i, 0)),
        scratch_shapes=[pltpu.VMEM((s, A_TQ), I32),
                        pltpu.VMEM((H_A, 256, A_TQ), BF16),
                        pltpu.VMEM((H_IDX, LANES, A_TQ), BF16),
                        pltpu.VMEM((H_A, LANES, A_TQ), F32),
                        pltpu.VMEM((H_A, 1, A_TQ), F32),
                        pltpu.VMEM((H_A, 1, A_TQ), F32),
                        pltpu.VMEM((1, A_TQ), F32)],
        compiler_params=_cparams(("arbitrary",)),
        name="dsa_mixer",
    )(qaT, qiT, wiT, ka, vaT, misc_bf, tri)


def _pair_select(lo, hi):
    lane = lax.broadcasted_iota(I32, lo.shape, 1)
    return jnp.where(lane < HEAD_DIM, lo, hi)


def _pair_head_rms(o, g):
    lane = lax.broadcasted_iota(I32, o.shape, 1)
    low = lane < HEAD_DIM
    sq = o * o
    ms_lo = jnp.sum(jnp.where(low, sq, 0.0), axis=1, keepdims=True) * (1.0 / HEAD_DIM)
    ms_hi = jnp.sum(jnp.where(low, 0.0, sq), axis=1, keepdims=True) * (1.0 / HEAD_DIM)
    ms = jnp.where(low, ms_lo, ms_hi)
    return o * lax.rsqrt(ms + EPS) * g


D_TQ = 256
D_KT = 512


def _diff_kernel(lam_init, qT_ref, kd_ref, vT_ref, lam_ref, g_ref, o_ref, qpad_ref, acc_ref, m_ref, l_ref):
    tq = D_TQ
    b = pl.program_id(0)
    c = (DQ_D ** -0.5) * LOG2E
    q_pos = b * tq + lax.broadcasted_iota(I32, (1, tq), 1)
    vis_end = (q_pos // CHUNK + 1) * CHUNK
    m_ref[...] = jnp.full(m_ref.shape, M_INIT, F32)
    l_ref[...] = jnp.zeros(l_ref.shape, F32)
    acc_ref[...] = jnp.zeros(acc_ref.shape, F32)
    qT = qT_ref[...]
    row = lax.broadcasted_iota(I32, qT.shape, 0)
    for i in range(2 * H_D):
        qpad_ref[i] = jnp.where((row >= i * DQ_D) & (row < (i + 1) * DQ_D), qT, jnp.zeros_like(qT))

    def tile(s0, kt, masked):
        k_t = kd_ref[pl.ds(s0, kt), :]
        vis = None
        if masked:
            vis = (s0 + lax.broadcasted_iota(I32, (kt, 1), 0)) < vis_end
        _flash_tile(k_t, qpad_ref, 2 * H_D, lambda i: vT_ref[(i // 4) * LANES:(i // 4 + 1) * LANES, pl.ds(s0, kt)],
                    vis, c, m_ref, l_ref, acc_ref)

    def full_tile(j, carry):
        tile(pl.multiple_of(j * D_KT, D_KT), D_KT, False)
        return carry
    lax.fori_loop(0, b // 2, full_tile, 0)

    @pl.when(b % 2 == 1)
    def _():
        tile(pl.multiple_of((b - 1) * tq, tq), tq, False)
    tile(pl.multiple_of(b * tq, tq), tq, True)

    lp = lam_ref[...]
    lam = (jnp.exp(jnp.sum(lp[0:1] * lp[1:2], axis=1, keepdims=True))
           - jnp.exp(jnp.sum(lp[2:3] * lp[3:4], axis=1, keepdims=True)) + lam_init)
    ys = []
    for h in range(H_D):
        r0 = (h % 2) * HEAD_DIM
        a0 = acc_ref[2 * h, r0:r0 + HEAD_DIM, :] / l_ref[2 * h]
        a1 = acc_ref[2 * h + 1, r0:r0 + HEAD_DIM, :] / l_ref[2 * h + 1]
        o = a0 - lam * a1
        ms = jnp.mean(o * o, axis=0, keepdims=True)
        ys.append(o * lax.rsqrt(ms + EPS) * g_ref[h * HEAD_DIM:(h + 1) * HEAD_DIM, :] * (1.0 - lam_init))
    o_ref[...] = jnp.concatenate(ys, axis=0).T.astype(o_ref.dtype)


def _diff_call(qdT, kd, vdT, lam_p, dnorm_g, lam_init):
    s = kd.shape[0]
    return pl.pallas_call(
        functools.partial(_diff_kernel, lam_init),
        out_shape=jax.ShapeDtypeStruct((s, W_BRANCH), BF16),
        grid=(s // D_TQ,),
        in_specs=[pl.BlockSpec((256, D_TQ), lambda i: (0, i)),
                  pl.BlockSpec(memory_space=pltpu.VMEM),
                  pl.BlockSpec(memory_space=pltpu.VMEM),
                  _whole((4, DQ_D)),
                  _whole((256, 1))],
        out_specs=pl.BlockSpec((D_TQ, 256), lambda i: (i, 0)),
        scratch_shapes=[pltpu.VMEM((2 * H_D, 256, D_TQ), BF16),
                        pltpu.VMEM((2 * H_D, LANES, D_TQ), F32),
                        pltpu.VMEM((2 * H_D, 1, D_TQ), F32),
                        pltpu.VMEM((2 * H_D, 1, D_TQ), F32)],
        compiler_params=_cparams(("parallel",)),
        name="diff_attention",
    )(qdT, kd, vdT, lam_p, dnorm_g.reshape(256, 1))


C_TQ = 128
C_PAD = N_PREV_CHUNKS * CHUNK
C_WIN = C_PAD + C_TQ


def _band_bias_table(rel_bias):
    i = jnp.arange(C_TQ)[:, None]
    w = jnp.arange(C_WIN)[None, :]
    dist = i + C_PAD - w
    rel_id = jnp.clip(dist, -(CHUNK - 1), MAX_REL_PAST) + (CHUNK - 1)
    in_band = (w // CHUNK >= i // CHUNK) & (w // CHUNK <= i // CHUNK + N_PREV_CHUNKS)
    return jnp.where(in_band[None], rel_bias[:, rel_id].astype(F32), NEG_INF)


def _band_kernel(qc_ref, kcT_ref, vc_ref, bias_ref, o_ref):
    tq = C_TQ
    b = pl.program_id(0)
    w0 = pl.multiple_of(b * tq, tq)
    q_all = qc_ref[...] * 0.125
    key_abs = b * tq - C_PAD + lax.broadcasted_iota(I32, (1, C_WIN), 1)
    ok = key_abs >= 0
    heads = []
    for h in range(H_C):
        s = jnp.dot(q_all[:, h * HEAD_DIM:(h + 1) * HEAD_DIM],
                    kcT_ref[h * HEAD_DIM:(h + 1) * HEAD_DIM, pl.ds(w0, C_WIN)],
                    preferred_element_type=F32) + bias_ref[h]
        s = jnp.where(ok, s, NEG_INF)
        m = jnp.max(s, axis=1, keepdims=True)
        p = jnp.exp(s - m)
        l = jnp.sum(p, axis=1, keepdims=True)
        pv = jnp.dot(p.astype(BF16), vc_ref[pl.ds(w0, C_WIN), (h // 2) * LANES:(h // 2 + 1) * LANES],
                     preferred_element_type=F32)
        heads.append(pv / l)
    for pr in range(H_C // 2):
        o_ref[:, pr * LANES:(pr + 1) * LANES] = _pair_select(heads[2 * pr], heads[2 * pr + 1]).astype(o_ref.dtype)


def _band_call(qc, kcT_pad, vc_pad, bias_tab):
    s = qc.shape[0]
    return pl.pallas_call(
        _band_kernel,
        out_shape=jax.ShapeDtypeStruct((s, W_BRANCH), BF16),
        grid=(s // C_TQ,),
        in_specs=[pl.BlockSpec((C_TQ, 256), lambda i: (i, 0)),
                  pl.BlockSpec(memory_space=pltpu.VMEM),
                  pl.BlockSpec(memory_space=pltpu.VMEM),
                  _whole((H_C, C_TQ, C_WIN))],
        out_specs=pl.BlockSpec((C_TQ, 256), lambda i: (i, 0)),
        compiler_params=_cparams(("parallel",)),
        name="band_attention",
    )(qc, kcT_pad, vc_pad, bias_tab)


B_L = CHUNK
HIGHEST = lax.Precision.HIGHEST


def _mlstm_kernel(qk_ref, vb_ref, ob_ref, misc_ref, cw_ref, cb_ref, gb_ref, ng_ref, o_ref,
                  tail_ref, ct_ref, n_ref, m_ref):
    L = B_L
    hd = HEAD_DIM

    @pl.when(pl.program_id(0) == 0)
    def _():
        tail_ref[...] = jnp.zeros(tail_ref.shape, F32)
        ct_ref[...] = jnp.zeros(ct_ref.shape, F32)
        n_ref[...] = jnp.zeros(n_ref.shape, F32)
        m_ref[...] = jnp.zeros(m_ref.shape, F32)

    x = qk_ref[...]
    xx = jnp.concatenate([tail_ref[...], x], axis=0)
    y = jnp.broadcast_to(cb_ref[...], x.shape)
    for j in range(CONV_K):
        y = y + cw_ref[j:j + 1, :] * xx[8 - (CONV_K - 1) + j:8 - (CONV_K - 1) + j + L, :]
    tail_ref[...] = x[L - 8:L, :]
    qk = y * jax.nn.sigmoid(y)
    q_all = qk[:, :W_BRANCH]
    k_all = qk[:, W_BRANCH:] * 0.125

    gts = misc_ref[...] + gb_ref[...]
    lf = jnp.minimum(gts, 0.0) - jnp.log1p(jnp.exp(-jnp.abs(gts)))
    r_i = lax.broadcasted_iota(I32, (L, L), 0)
    c_i = lax.broadcasted_iota(I32, (L, L), 1)
    causal = c_i <= r_i
    ltri = jnp.where(causal, 1.0, 0.0)
    cum = jnp.dot(ltri, lf, precision=HIGHEST, preferred_element_type=F32)
    lane = lax.broadcasted_iota(I32, (L, LANES), 1)
    mixed = jnp.where(lane < MISC_FB, gts, cum)
    sel_r = lax.broadcasted_iota(I32, (8, LANES), 0)
    sel_c = lax.broadcasted_iota(I32, (8, LANES), 1)
    sel = jnp.where(sel_c == sel_r + MISC_IB, 1.0, 0.0)
    rows = lax.dot_general(sel, mixed, (((1,), (1,)), ((), ())), precision=HIGHEST,
                           preferred_element_type=F32)

    outs = []
    for h in range(H_B):
        q = q_all[:, h * hd:(h + 1) * hd]
        k = k_all[:, h * hd:(h + 1) * hd]
        v = vb_ref[:, h * hd:(h + 1) * hd]
        it_r = rows[h:h + 1, :]
        cum_r = rows[H_B + h:H_B + h + 1, :]
        it_c = gts[:, MISC_IB + h:MISC_IB + h + 1]
        cum_c = cum[:, MISC_FB + h:MISC_FB + h + 1]
        m_prev = m_ref[h]
        dmat = jnp.where(causal, cum_c - cum_r + it_r, NEG_INF)
        m_inter = cum_c + m_prev
        m_t = jnp.maximum(m_inter, jnp.max(dmat, axis=1, keepdims=True))
        qb, kb, vbf = q.astype(BF16), k.astype(BF16), v.astype(BF16)
        qkt = lax.dot_general(qb, kb, (((1,), (1,)), ((), ())), preferred_element_type=F32)
        w = jnp.exp(dmat - m_t) * qkt
        inter = jnp.exp(m_inter - m_t)
        ct = ct_ref[h]
        num = inter * jnp.dot(qb, ct.astype(BF16), preferred_element_type=F32) \
            + jnp.dot(w.astype(BF16), vbf, preferred_element_type=F32)
        n_row = n_ref[h]
        den = inter * jnp.sum(q * n_row, axis=1, keepdims=True) + jnp.sum(w, axis=1, keepdims=True)
        h_t = num / jnp.maximum(jnp.abs(den), jnp.exp(-m_t))
        cum_end = cum_c[L - 1:L, :]
        g = cum_end - cum_c + it_c
        m_new = jnp.maximum(cum_end + m_prev, jnp.max(g, axis=0, keepdims=True))
        carry_scale = jnp.exp(cum_end + m_prev - m_new)
        src_k = jnp.exp(g - m_new) * k
        ct_ref[h] = carry_scale * ct + lax.dot_general(src_k.astype(BF16), vbf, (((0,), (0,)), ((), ())),
                                                       preferred_element_type=F32)
        n_ref[h] = carry_scale * n_row + jnp.sum(src_k, axis=0, keepdims=True)
        m_ref[h] = m_new
        ms = jnp.mean(h_t * h_t, axis=1, keepdims=True)
        hn = h_t * lax.rsqrt(ms + EPS) * ng_ref[:, h * hd:(h + 1) * hd]
        outs.append(jax.nn.sigmoid(ob_ref[:, h * hd:(h + 1) * hd]) * hn)
    o_ref[...] = jnp.concatenate(outs, axis=1).astype(o_ref.dtype)


def _mlstm_call(qkb, vb, ob, misc, conv_w, conv_b, i_bias, f_bias, norm_g):
    s = qkb.shape[0]
    gbias = jnp.zeros((1, LANES), F32)
    gbias = gbias.at[0, MISC_IB:MISC_IB + H_B].set(i_bias).at[0, MISC_FB:MISC_FB + H_B].set(f_bias)
    row = lambda w: pl.BlockSpec((B_L, w), lambda i: (i, 0))
    return pl.pallas_call(
        _mlstm_kernel,
        out_shape=jax.ShapeDtypeStruct((s, W_BRANCH), BF16),
        grid=(s // B_L,),
        in_specs=[row(512), row(256), row(256), row(128),
                  _whole((CONV_K, 512)), _whole((1, 512)), _whole((1, LANES)), _whole((1, 256))],
        out_specs=row(256),
        scratch_shapes=[pltpu.VMEM((8, 512), F32),
                        pltpu.VMEM((H_B, HEAD_DIM, HEAD_DIM), F32),
                        pltpu.VMEM((H_B, 1, HEAD_DIM), F32),
                        pltpu.VMEM((H_B, 1, 1), F32)],
        compiler_params=_cparams(("arbitrary",)),
        name="mlstm_mixer",
    )(qkb, vb, ob, misc, conv_w, conv_b.reshape(1, 512), gbias, norm_g.reshape(1, 256))


M_TM = 256
ROUTE_LOGIT0 = N_GROUPS
RT_E1, RT_E2, RT_G1, RT_G2 = 0, 1, 2, 3


def _lane_argmax(vals, lane):
    v = jnp.max(vals, axis=1, keepdims=True)
    idx = jnp.min(jnp.where(vals == v, lane, float(LANES)), axis=1, keepdims=True)
    return v, idx


def _merge_kernel(x_ref, ya_ref, yb_ref, yc_ref, yd_ref, gm_ref, wg_ref, wb_ref, wo_ref, gf_ref,
                  wr_ref, rb_ref, xo_ref, h2_ref, rt_ref):
    x = x_ref[...]
    h = _rms(x, gm_ref[...]).astype(BF16)
    mixed = jnp.zeros(x.shape, F32)
    for n, y_ref in enumerate((ya_ref, yb_ref, yc_ref, yd_ref)):
        gate = jax.nn.sigmoid(jnp.dot(h, wg_ref[:, n * D_MODEL:(n + 1) * D_MODEL],
                                      preferred_element_type=F32))
        up = jnp.dot(y_ref[...], wb_ref[n], preferred_element_type=F32)
        mixed = mixed + gate * up
    xn = x + jnp.dot(mixed.astype(BF16), wo_ref[...], preferred_element_type=F32)
    xo_ref[...] = xn
    h2 = _rms(xn, gf_ref[...])
    h2_ref[...] = h2
    logits = jnp.dot(h2.astype(BF16), wr_ref[...], preferred_element_type=F32) + rb_ref[...]

    lane = lax.broadcasted_iota(I32, logits.shape, 1).astype(F32)
    neg = -jnp.inf
    gmask = lane < N_GROUPS
    gmax, g_sel = _lane_argmax(jnp.where(gmask, logits, neg), lane)
    g_gate = 1.0 / jnp.sum(jnp.where(gmask, jnp.exp(logits - gmax), 0.0), axis=1, keepdims=True)
    e_lo = ROUTE_LOGIT0 + EXPERTS_PER_GROUP * g_sel
    el = jnp.where((lane >= e_lo) & (lane < e_lo + EXPERTS_PER_GROUP), logits, neg)
    v1, i1 = _lane_argmax(el, lane)
    v2, i2 = _lane_argmax(jnp.where(lane == i1, neg, el), lane)
    e = jnp.exp(v2 - v1)
    p1 = 1.0 / (1.0 + e)
    p2 = e / (1.0 + e)
    rt = jnp.where(lane == RT_E1, i1 - ROUTE_LOGIT0,
                   jnp.where(lane == RT_E2, i2 - ROUTE_LOGIT0,
                             jnp.where(lane == RT_G1, p1 * g_gate,
                                       jnp.where(lane == RT_G2, p2 * g_gate, 0.0))))
    rt_ref[...] = rt


def _merge_call(x2, ya, yb, yc, yd, g_mix, w_gate, w_branch, w_out, g_ffn, w_route, b_route):
    s = x2.shape[0]
    row = lambda w: pl.BlockSpec((M_TM, w), lambda i: (i, 0))
    vm = pl.BlockSpec(memory_space=pltpu.VMEM)
    return pl.pallas_call(
        _merge_kernel,
        out_shape=(jax.ShapeDtypeStruct((s, D_MODEL), F32), jax.ShapeDtypeStruct((s, D_MODEL), F32),
                   jax.ShapeDtypeStruct((s, LANES), F32)),
        grid=(s // M_TM,),
        in_specs=[row(D_MODEL), row(256), row(256), row(256), row(256),
                  vm, vm, vm, vm, vm, vm, vm],
        out_specs=(row(D_MODEL), row(D_MODEL), row(LANES)),
        compiler_params=_cparams(("parallel",)),
        name="merge_route",
    )(x2, ya, yb, yc, yd, g_mix.reshape(1, D_MODEL), w_gate, w_branch, w_out,
      g_ffn.reshape(1, D_MODEL), w_route, b_route)


E_BM = 128


def _moe_kernel(be_ref, nused_ref, tok_ref, h2_hbm, w1_ref, w3_ref, w2_ref, o_ref,
                xbuf, w1b, w3b, w2b, sem):
    i = pl.program_id(0)
    used = i < nused_ref[0]

    @pl.when(used)
    def _():
        def row_copy(r):
            return pltpu.make_async_copy(h2_hbm.at[pl.ds(tok_ref[0, 0, r], 1)],
                                         xbuf.at[pl.ds(r, 1)], sem)

        def start(r, c):
            row_copy(r).start()
            return c
        lax.fori_loop(0, E_BM, start, 0)

        prev = be_ref[jnp.maximum(i - 1, 0)]

        @pl.when((i == 0) | (be_ref[i] != prev))
        def _():
            w1b[...] = w1_ref[0].astype(BF16)
            w3b[...] = w3_ref[0].astype(BF16)
            w2b[...] = w2_ref[0].astype(BF16)

        def wait(r, c):
            row_copy(r).wait()
            return c
        lax.fori_loop(0, E_BM, wait, 0)

        xb = xbuf[...].astype(BF16)
        a = jnp.dot(xb, w1b[...], preferred_element_type=F32)
        g = jnp.dot(xb, w3b[...], preferred_element_type=F32)
        hid = (a * jax.nn.sigmoid(a) * g).astype(BF16)
        o_ref[...] = jnp.dot(hid, w2b[...], preferred_element_type=F32)

    @pl.when(jnp.logical_not(used))
    def _():
        o_ref[...] = jnp.zeros(o_ref.shape, F32)


def _moe_call(h2, row_tok, block_e, nused, w1, w3, w2):
    n_rows = row_tok.shape[0]
    n_blocks = n_rows // E_BM
    wspec = lambda shp: pl.BlockSpec((1,) + shp, lambda i, be, nu: (be[i], 0, 0))
    grid_spec = pltpu.PrefetchScalarGridSpec(
        num_scalar_prefetch=2,
        grid=(n_blocks,),
        in_specs=[pl.BlockSpec((1, 1, E_BM), lambda i, be, nu: (i, 0, 0), memory_space=pltpu.SMEM),
                  pl.BlockSpec(memory_space=pl.ANY),
                  wspec((D_MODEL, F_EXPERT)), wspec((D_MODEL, F_EXPERT)), wspec((F_EXPERT, D_MODEL))],
        out_specs=pl.BlockSpec((E_BM, D_MODEL), lambda i, be, nu: (i, 0)),
        scratch_shapes=[pltpu.VMEM((E_BM, D_MODEL), F32),
                        pltpu.VMEM((D_MODEL, F_EXPERT), BF16),
                        pltpu.VMEM((D_MODEL, F_EXPERT), BF16),
                        pltpu.VMEM((F_EXPERT, D_MODEL), BF16),
                        pltpu.SemaphoreType.DMA(())],
    )
    return pl.pallas_call(
        _moe_kernel,
        out_shape=jax.ShapeDtypeStruct((n_rows, D_MODEL), F32),
        grid_spec=grid_spec,
        compiler_params=_cparams(("arbitrary",)),
        name="moe_experts",
    )(block_e, nused, row_tok.reshape(n_blocks, 1, E_BM), h2, w1, w3, w2)


CB_TM = 128


def _combine_kernel(final, pos_ref, x_ref, rt_ref, ys_hbm, gfin_ref, o_ref, buf, sem):
    def row_copy(r):
        return pltpu.make_async_copy(ys_hbm.at[pl.ds(pos_ref[0, 0, r], 1)],
                                     buf.at[pl.ds(r, 1)], sem)

    def start(r, c):
        row_copy(r).start()
        return c
    lax.fori_loop(0, 2 * CB_TM, start, 0)

    def wait(r, c):
        row_copy(r).wait()
        return c
    lax.fori_loop(0, 2 * CB_TM, wait, 0)

    rt = rt_ref[...]
    y = (buf[0:CB_TM, :] * rt[:, RT_G1:RT_G1 + 1] + buf[CB_TM:2 * CB_TM, :] * rt[:, RT_G2:RT_G2 + 1])
    xn = x_ref[...] + y
    if final:
        xn = _rms(xn, gfin_ref[...])
    o_ref[...] = xn


def _combine_call(x2, rt, ys, pos, g_final, final):
    s = x2.shape[0]
    nb = s // CB_TM
    pos_b = pos.reshape(nb, CB_TM, 2).transpose(0, 2, 1).reshape(nb, 1, 2 * CB_TM)
    return pl.pallas_call(
        functools.partial(_combine_kernel, final),
        out_shape=jax.ShapeDtypeStruct((s, D_MODEL), F32),
        grid=(nb,),
        in_specs=[pl.BlockSpec((1, 1, 2 * CB_TM), lambda i: (i, 0, 0), memory_space=pltpu.SMEM),
                  pl.BlockSpec((CB_TM, D_MODEL), lambda i: (i, 0)),
                  pl.BlockSpec((CB_TM, LANES), lambda i: (i, 0)),
                  pl.BlockSpec(memory_space=pl.ANY),
                  _whole((1, D_MODEL))],
        out_specs=pl.BlockSpec((CB_TM, D_MODEL), lambda i: (i, 0)),
        scratch_shapes=[pltpu.VMEM((2 * CB_TM, D_MODEL), F32), pltpu.SemaphoreType.DMA(())],
        compiler_params=_cparams(("arbitrary",)),
        name="moe_combine",
    )(pos_b, x2, rt, ys, g_final.reshape(1, D_MODEL))


def _dispatch_plan(rt):
    t = rt.shape[0]
    n_assign = t * TOP_K_INNER
    eid = rt[:, RT_E1:RT_E2 + 1].astype(I32).reshape(n_assign)
    onehot = (eid[:, None] == jnp.arange(N_EXPERTS, dtype=I32)[None, :]).astype(I32)
    csum = jnp.cumsum(onehot, axis=0)
    counts = csum[-1]
    rank = jnp.take_along_axis(csum, eid[:, None], axis=1)[:, 0] - 1
    padded = (counts + E_BM - 1) // E_BM * E_BM
    ends_pad = jnp.cumsum(padded)
    starts_pad = ends_pad - padded
    dest = starts_pad[eid] + rank
    n_rows = -(-(n_assign + N_EXPERTS * (E_BM - 1)) // E_BM) * E_BM
    n_blocks = n_rows // E_BM
    tok = jnp.repeat(jnp.arange(t, dtype=I32), TOP_K_INNER)
    row_tok = jnp.zeros((n_rows,), I32).at[dest].set(tok)
    block_e = jnp.minimum(jnp.searchsorted(ends_pad, jnp.arange(n_blocks, dtype=I32) * E_BM, side='right'),
                          N_EXPERTS - 1).astype(I32)
    nused = (ends_pad[-1] // E_BM).astype(I32).reshape(1)
    return row_tok, block_e, nused, dest.reshape(t, TOP_K_INNER)


def _layer(x2, layer_idx, p, final_g, final):
    (norm_mix_g, w_in, conv_w, conv_b, i_bias, f_bias, mnorm_g, rel_bias, lam_p, dnorm_g,
     w_branch, w_gate, w_out, norm_ffn_g, rgw, rgb, rew, reb, w1, w3, w2) = p
    o = dict(zip([n for n, _, _ in _K1_OUTS], _k1_call(x2, norm_mix_g, _rearrange_w_in(w_in))))
    misc = o["misc"]

    wiT = jnp.pad(misc[:, MISC_WI:MISC_WI + H_IDX].T, ((0, 8 - H_IDX), (0, 0)))
    ya = _dsa_call(o["qa"].T, o["qi"].T, wiT, o["ka"], o["va"].T, misc.astype(BF16))
    yb = _mlstm_call(o["qkb"], o["vb"], o["ob"], misc, conv_w, conv_b, i_bias, f_bias, mnorm_g)
    kcT_pad = jnp.pad(o["kc"].T, ((0, 0), (C_PAD, 0)))
    vc_pad = jnp.pad(o["vc"], ((C_PAD, 0), (0, 0)))
    yc = _band_call(o["qc"], kcT_pad, vc_pad, _band_bias_table(rel_bias))
    lam_init = 0.8 - 0.6 * math.exp(-0.3 * layer_idx)
    yd = _diff_call(o["qd"].T, o["kd"], o["vd"].T, lam_p, dnorm_g, lam_init)

    w_route = jnp.concatenate([rgw, rew, jnp.zeros((D_MODEL, LANES - N_GROUPS - N_EXPERTS), F32)],
                              axis=1).astype(BF16)
    b_route = jnp.concatenate([rgb, reb, jnp.zeros((LANES - N_GROUPS - N_EXPERTS,), F32)]).reshape(1, LANES)
    xn, h2, rt = _merge_call(x2, ya, yb, yc, yd, norm_mix_g, w_gate.astype(BF16), w_branch.astype(BF16),
                             w_out.astype(BF16), norm_ffn_g, w_route, b_route)
    row_tok, block_e, nused, pos = _dispatch_plan(rt)
    ys = _moe_call(h2, row_tok, block_e, nused, w1, w3, w2)
    return _combine_call(xn, rt, ys, pos, final_g, final)


def kernel(x, norm_mix_g, w_in, conv_w, conv_b, mlstm_i_bias, mlstm_f_bias, mlstm_norm_g, relpos_bias, diff_lambda, diff_norm_g, w_branch, w_gate, w_out, norm_ffn_g, router_group_w, router_group_b, router_expert_w, router_expert_b, expert_w1, expert_w3, expert_w2, final_norm_g):
    assert x.shape[0] == 1 and x.shape[2] == D_MODEL
    params = (norm_mix_g, w_in, conv_w, conv_b, mlstm_i_bias, mlstm_f_bias, mlstm_norm_g, relpos_bias,
              diff_lambda, diff_norm_g, w_branch, w_gate, w_out, norm_ffn_g, router_group_w,
              router_group_b, router_expert_w, router_expert_b, expert_w1, expert_w3, expert_w2)
    depth = norm_mix_g.shape[0]
    x2 = x[0]
    for l in range(depth):
        x2 = _layer(x2, l, tuple(a[l] for a in params), final_norm_g, l == depth - 1)
    return x2[None]
```

```python
import functools
import math

import jax
import jax.numpy as jnp
from jax import lax
from jax.experimental import pallas as pl
from jax.experimental.pallas import tpu as pltpu

F32 = jnp.float32
BF16 = jnp.bfloat16
I32 = jnp.int32

D_MODEL = 1024
CHUNK = 64
HEAD_DIM = 64
NEG_INF = -1e30
H_A = 4
H_IDX = 4
D_IDX = 32
TOPK_MAX = 256
H_B = 4
CONV_K = 4
H_C = 4
N_PREV_CHUNKS = 8
MAX_REL_PAST = 128
H_D = 4
DQ_D = 32
W_BRANCH = 256
N_BRANCH = 4
N_GROUPS = 4
EXPERTS_PER_GROUP = 8
N_EXPERTS = 32
TOP_K_INNER = 2
F_EXPERT = 512
EPS = 1e-6

VMEM_LIMIT_BYTES = 52 * 1024 * 1024
LANES = 128

INT_MIN = -(2 ** 31)
I16_MIN = -(2 ** 15)
I16 = jnp.int16
M_INIT = -5e29

_COL_SIZES = (256, 256, 256, 128, 32, 4, 256, 256, 256, 4, 4, 256, 256, 256, 256, 256, 256, 256)
_COL_NAMES = ("qa", "ka", "va", "qi", "ki", "wi", "qb", "kb", "vb", "ib", "fb", "ob",
              "qc", "kc", "vc", "qd", "kd", "vd")
_COL_OFF = {}
_o = 0
for _n, _s in zip(_COL_NAMES, _COL_SIZES):
    _COL_OFF[_n] = (_o, _s)
    _o += _s
C_IN = _o
MISC_KI = 0
MISC_WI = 32
MISC_IB = 36
MISC_FB = 40


def _cparams(sem):
    return pltpu.CompilerParams(dimension_semantics=sem, vmem_limit_bytes=VMEM_LIMIT_BYTES)


def _whole(shape):
    nd = len(shape)
    return pl.BlockSpec(shape, lambda *_: (0,) * nd)


_K1_NAT = (("ka", 256, BF16), ("misc", 128, F32), ("qkb", 512, F32), ("vb", 256, F32), ("ob", 256, F32),
           ("qc", 256, BF16), ("vc", 256, BF16), ("kd", 256, BF16))
_K1_TR = (("qaT", 256, BF16), ("qiT", 128, BF16), ("vaT", 256, BF16), ("kcT", 256, BF16),
          ("qdT", 256, BF16), ("vdT", 256, BF16), ("wiT", 8, F32))
K1_NAT_WIDTH = sum(w for _, w, _ in _K1_NAT)
K1_TR_WIDTH = sum(w for _, w, _ in _K1_TR)
K1_TM = 512
V_AUG = 80
_K1_VAUG = ("vaT", "vdT")


def _k1_tr_rows(name, width):
    return (width // HEAD_DIM) * V_AUG if name in _K1_VAUG else width


def _rearrange_w_in(w_in):
    def cols(name):
        o, s = _COL_OFF[name]
        return w_in[:, o:o + s]
    d = w_in.shape[0]
    misc = jnp.concatenate([cols("ki"), cols("wi"), cols("ib"), cols("fb"),
                            jnp.zeros((d, LANES - 44), w_in.dtype)], axis=1)
    nat = [cols("ka"), misc, cols("qb"), cols("kb"), cols("vb"), cols("ob"), cols("qc"), cols("vc"),
           cols("kd")]
    tr = [cols("qa"), cols("qi"), cols("va"), cols("kc"), cols("qd"), cols("vd"), cols("wi"),
          jnp.zeros((d, 8 - H_IDX), w_in.dtype)]
    return (jnp.concatenate(nat, axis=1).astype(BF16), jnp.concatenate(tr, axis=1).T.astype(BF16))


def _rms(x, g):
    ms = jnp.mean(x * x, axis=-1, keepdims=True)
    return x * lax.rsqrt(ms + EPS) * g


def _k1_kernel(x_ref, g_ref, wn_ref, wt_ref, *out_refs):
    h = _rms(x_ref[...], g_ref[...]).astype(BF16)
    refs = dict(zip([n for n, _, _ in _K1_NAT] + ["misc_bf"] + [n for n, _, _ in _K1_TR], out_refs))
    off = 0
    for name, width, _ in _K1_NAT:
        r = jnp.dot(h, wn_ref[:, off:off + width], preferred_element_type=F32)
        refs[name][...] = r.astype(refs[name].dtype)
        if name == "misc":
            refs["misc_bf"][...] = r.astype(BF16)
        off += width
    off = 0
    for name, width, _ in _K1_TR:
        r = lax.dot_general(wt_ref[off:off + width, :], h, (((1,), (1,)), ((), ())),
                            preferred_element_type=F32)
        if name in _K1_VAUG:
            tail = (lax.broadcasted_iota(I32, (V_AUG - HEAD_DIM, r.shape[1]), 0) == 0)
            tail = jnp.where(tail, 1.0, 0.0).astype(BF16)
            for hh in range(width // HEAD_DIM):
                refs[name][hh * V_AUG:hh * V_AUG + HEAD_DIM, :] = \
                    r[hh * HEAD_DIM:(hh + 1) * HEAD_DIM, :].astype(BF16)
                refs[name][hh * V_AUG + HEAD_DIM:(hh + 1) * V_AUG, :] = tail
        else:
            refs[name][...] = r.astype(refs[name].dtype)
        off += width


def _k1_call(x2, g, w_nat, w_trT):
    s = x2.shape[0]
    tm = K1_TM
    outs = (tuple(jax.ShapeDtypeStruct((s, w), dt) for _, w, dt in _K1_NAT)
            + (jax.ShapeDtypeStruct((s, LANES), BF16),)
            + tuple(jax.ShapeDtypeStruct((_k1_tr_rows(n, w), s), dt) for n, w, dt in _K1_TR))
    out_specs = (tuple(pl.BlockSpec((tm, w), lambda i: (i, 0)) for _, w, _ in _K1_NAT)
                 + (pl.BlockSpec((tm, LANES), lambda i: (i, 0)),)
                 + tuple(pl.BlockSpec((_k1_tr_rows(n, w), tm), lambda i: (0, i)) for n, w, _ in _K1_TR))
    res = pl.pallas_call(
        _k1_kernel,
        out_shape=outs,
        grid=(s // tm,),
        in_specs=[pl.BlockSpec((tm, D_MODEL), lambda i: (i, 0)),
                  _whole((1, D_MODEL)),
                  _whole((D_MODEL, K1_NAT_WIDTH)),
                  _whole((K1_TR_WIDTH, D_MODEL))],
        out_specs=out_specs,
        compiler_params=_cparams(("parallel",)),
        name="k1_norm_proj",
    )(x2, g.reshape(1, D_MODEL), w_nat, w_trT)
    return dict(zip([n for n, _, _ in _K1_NAT] + ["misc_bf"] + [n for n, _, _ in _K1_TR], res))


A_TQ = 256
A_KT = 512
A_CNT_ROWS = 32
LOG2E = 1.4426950408889634
QK_AHEAD = 4


def _flash_step(s, i, v_t, c, m_ref, acc_ref):
    m_old = m_ref[i]
    m_new = jnp.maximum(m_old, jnp.max(s, axis=0, keepdims=True))
    alpha = jnp.exp2((m_old - m_new) * c)
    p = jnp.exp2((s - m_new) * c)
    acc_ref[i] = alpha * acc_ref[i] + jnp.dot(v_t, p.astype(BF16), preferred_element_type=F32)
    m_ref[i] = m_new


def _flash_result(acc_ref, i):
    return acc_ref[i, 0:HEAD_DIM, :] / acc_ref[i, HEAD_DIM:HEAD_DIM + 1, :]


def _flash_tile(k_t, qpad_ref, n, v_tile, mask, c, m_ref, acc_ref):
    pend = [jnp.dot(k_t, qpad_ref[i], preferred_element_type=F32) for i in range(min(QK_AHEAD, n))]
    for i in range(n):
        s = pend.pop(0)
        if i + QK_AHEAD < n:
            pend.append(jnp.dot(k_t, qpad_ref[i + QK_AHEAD], preferred_element_type=F32))
        if mask is not None:
            s = jnp.where(mask, s, NEG_INF)
        _flash_step(s, i, v_tile(i), c, m_ref, acc_ref)


def _dsa_kernel(topk, qaT_ref, qiT_ref, wiT_ref, ka_ref, vaT_ref, mb_ref, tri_ref, o_ref,
                keys_ref, half_ref, qpad_ref, qipad_ref, acc_ref, m_ref, carry_ref):
    tq, kt = A_TQ, A_KT
    b = pl.program_id(0)
    ntiles = ((b + 1) * tq + kt - 1) // kt
    q_pos = b * tq + lax.broadcasted_iota(I32, (1, tq), 1)
    vis_end = (q_pos // CHUNK + 1) * CHUNK

    qiT = qiT_ref[...]
    for h in range(H_IDX):
        qipad_ref[h, 0:D_IDX, :] = qiT[h * D_IDX:(h + 1) * D_IDX, :]
        qipad_ref[h, D_IDX:LANES, :] = jnp.zeros((LANES - D_IDX, tq), BF16)

    def p1(j, carry):
        s0 = pl.multiple_of(j * kt, kt)
        mb = mb_ref[pl.ds(s0, kt), :]
        score = jnp.zeros((kt, tq), F32)
        for h in range(H_IDX):
            r = jnp.dot(mb, qipad_ref[h], preferred_element_type=F32)
            score = score + jnp.maximum(r, 0.0) * wiT_ref[h:h + 1, :]
        bits = lax.bitcast_convert_type(score, I32)
        key = bits ^ (lax.shift_right_arithmetic(bits, 31) & 0x7FFFFFFF)
        s_pos = s0 + lax.broadcasted_iota(I32, (kt, 1), 0)
        key = jnp.where(s_pos < vis_end, key, INT_MIN)
        keys_ref[pl.ds(s0, kt), :] = key
        half_ref[pl.ds(s0, kt), :] = lax.shift_right_arithmetic(key, 16).astype(I16)
        return carry
    lax.fori_loop(0, ntiles, p1, 0)

    def count16(cand, strict):
        c16 = cand.astype(I16)

        def body(j, acc):
            k = half_ref[pl.ds(pl.multiple_of(j * kt, kt), kt), :]
            hit = (k > c16) if strict else (k >= c16)
            ones = jnp.where(hit, jnp.bfloat16(1), jnp.bfloat16(0)).reshape(kt // A_CNT_ROWS, A_CNT_ROWS, tq)
            parts = [ones[i] for i in range(kt // A_CNT_ROWS)]
            while len(parts) > 1:
                parts = [parts[i] + parts[i + 1] for i in range(0, len(parts), 2)]
            return acc + parts[0].astype(F32)
        acc = lax.fori_loop(0, ntiles, body, jnp.zeros((A_CNT_ROWS, tq), F32))
        return jnp.sum(acc, axis=0, keepdims=True)

    def search16(k_req):
        zero = jnp.zeros((1, tq), I32)
        t = jnp.where(count16(zero, False) >= k_req, zero, I16_MIN)

        def bit_body(i, t):
            cand = t | lax.shift_left(jnp.int32(1), 14 - i)
            return jnp.where(count16(cand, False) >= k_req, cand, t)
        return lax.fori_loop(0, 15, bit_body, t)

    kf = float(topk)
    t_hi = search16(kf)
    n_above_hi = count16(t_hi, True)
    t_hi16 = t_hi.astype(I16)

    def low_halves(j, carry):
        s0 = pl.multiple_of(j * kt, kt)
        lo = ((keys_ref[pl.ds(s0, kt), :] & 0xFFFF) - 32768).astype(I16)
        half_ref[pl.ds(s0, kt), :] = jnp.where(half_ref[pl.ds(s0, kt), :] == t_hi16, lo, jnp.int16(I16_MIN))
        return carry
    lax.fori_loop(0, ntiles, low_halves, 0)
    t_lo = search16(kf - n_above_hi)
    tau = lax.shift_left(t_hi, 16) | ((t_lo + 32768) & 0xFFFF)
    need = jnp.where(tau == INT_MIN, 0.0, kf - n_above_hi - count16(t_lo, True))

    m_ref[...] = jnp.full(m_ref.shape, M_INIT, F32)
    acc_ref[...] = jnp.zeros(acc_ref.shape, F32)
    carry_ref[...] = jnp.zeros(carry_ref.shape, F32)
    qaT = qaT_ref[...] * 0.125
    row = lax.broadcasted_iota(I32, qaT.shape, 0)
    for h in range(H_A):
        qpad_ref[h] = jnp.where((row >= h * HEAD_DIM) & (row < (h + 1) * HEAD_DIM), qaT,
                                jnp.zeros_like(qaT))

    def p3(j, carry):
        s0 = pl.multiple_of(j * kt, kt)
        k = keys_ref[pl.ds(s0, kt), :]
        gt = k > tau
        eq = k == tau
        eqf = jnp.where(eq, 1.0, 0.0)
        pref = jnp.dot(tri_ref[...], eqf.astype(BF16), preferred_element_type=F32)
        seen = carry_ref[...]
        sel = gt | (eq & (pref + seen < need))
        carry_ref[...] = seen + pref[kt - 1:kt, :] + eqf[kt - 1:kt, :]
        k_t = ka_ref[pl.ds(s0, kt), :]
        _flash_tile(k_t, qpad_ref, H_A, lambda i: vaT_ref[i * V_AUG:(i + 1) * V_AUG, pl.ds(s0, kt)],
                    sel, LOG2E, m_ref, acc_ref)
        return carry
    lax.fori_loop(0, ntiles, p3, 0)

    ys = [_flash_result(acc_ref, h) for h in range(H_A)]
    o_ref[...] = jnp.concatenate(ys, axis=0).T.astype(o_ref.dtype)


def _dsa_call(qaT, qiT, wiT, ka, vaT, misc_bf):
    s = ka.shape[0]
    topk = min(TOPK_MAX, s // 4)
    tri = jnp.tril(jnp.ones((A_KT, A_KT), F32), k=-1).astype(BF16)
    vm = pl.BlockSpec(memory_space=pltpu.VMEM)
    return pl.pallas_call(
        functools.partial(_dsa_kernel, topk),
        out_shape=jax.ShapeDtypeStruct((s, W_BRANCH), BF16),
        grid=(s // A_TQ,),
        in_specs=[pl.BlockSpec((256, A_TQ), lambda i: (0, i)),
                  pl.BlockSpec((LANES, A_TQ), lambda i: (0, i)),
                  pl.BlockSpec((8, A_TQ), lambda i: (0, i)),
                  vm, vm, vm, vm],
        out_specs=pl.BlockSpec((A_TQ, 256), lambda i: (i, 0)),
        scratch_shapes=[pltpu.VMEM((s, A_TQ), I32),
                        pltpu.VMEM((s, A_TQ), I16),
                        pltpu.VMEM((H_A, 256, A_TQ), BF16),
                        pltpu.VMEM((H_IDX, LANES, A_TQ), BF16),
                        pltpu.VMEM((H_A, V_AUG, A_TQ), F32),
                        pltpu.VMEM((H_A, 1, A_TQ), F32),
                        pltpu.VMEM((1, A_TQ), F32)],
        compiler_params=_cparams(("arbitrary",)),
        name="dsa_mixer",
    )(qaT, qiT, wiT, ka, vaT, misc_bf, tri)


def _pair_select(lo, hi):
    lane = lax.broadcasted_iota(I32, lo.shape, 1)
    return jnp.where(lane < HEAD_DIM, lo, hi)


def _pair_head_rms(o, g):
    lane = lax.broadcasted_iota(I32, o.shape, 1)
    low = lane < HEAD_DIM
    sq = o * o
    ms_lo = jnp.sum(jnp.where(low, sq, 0.0), axis=1, keepdims=True) * (1.0 / HEAD_DIM)
    ms_hi = jnp.sum(jnp.where(low, 0.0, sq), axis=1, keepdims=True) * (1.0 / HEAD_DIM)
    ms = jnp.where(low, ms_lo, ms_hi)
    return o * lax.rsqrt(ms + EPS) * g


D_TQ = 256
D_KT = 512


def _diff_kernel(lam_init, qT_ref, kd_ref, vT_ref, lam_ref, g_ref, o_ref, qpad_ref, acc_ref, m_ref):
    tq = D_TQ
    b = pl.program_id(0)
    c = (DQ_D ** -0.5) * LOG2E
    q_pos = b * tq + lax.broadcasted_iota(I32, (1, tq), 1)
    vis_end = (q_pos // CHUNK + 1) * CHUNK
    m_ref[...] = jnp.full(m_ref.shape, M_INIT, F32)
    acc_ref[...] = jnp.zeros(acc_ref.shape, F32)
    qT = qT_ref[...]
    row = lax.broadcasted_iota(I32, qT.shape, 0)
    for i in range(2 * H_D):
        qpad_ref[i] = jnp.where((row >= i * DQ_D) & (row < (i + 1) * DQ_D), qT, jnp.zeros_like(qT))

    def tile(s0, kt, masked):
        k_t = kd_ref[pl.ds(s0, kt), :]
        vis = None
        if masked:
            vis = (s0 + lax.broadcasted_iota(I32, (kt, 1), 0)) < vis_end
        _flash_tile(k_t, qpad_ref, 2 * H_D, lambda i: vT_ref[(i // 2) * V_AUG:(i // 2 + 1) * V_AUG, pl.ds(s0, kt)],
                    vis, c, m_ref, acc_ref)

    def full_tile(j, carry):
        tile(pl.multiple_of(j * D_KT, D_KT), D_KT, False)
        return carry
    lax.fori_loop(0, b // 2, full_tile, 0)

    @pl.when(b % 2 == 1)
    def _():
        tile(pl.multiple_of((b - 1) * tq, tq), tq, False)
    tile(pl.multiple_of(b * tq, tq), tq, True)

    lp = lam_ref[...]
    lam = (jnp.exp(jnp.sum(lp[0:1] * lp[1:2], axis=1, keepdims=True))
           - jnp.exp(jnp.sum(lp[2:3] * lp[3:4], axis=1, keepdims=True)) + lam_init)
    ys = []
    for h in range(H_D):
        o = _flash_result(acc_ref, 2 * h) - lam * _flash_result(acc_ref, 2 * h + 1)
        ms = jnp.mean(o * o, axis=0, keepdims=True)
        ys.append(o * lax.rsqrt(ms + EPS) * g_ref[h * HEAD_DIM:(h + 1) * HEAD_DIM, :] * (1.0 - lam_init))
    o_ref[...] = jnp.concatenate(ys, axis=0).T.astype(o_ref.dtype)


def _diff_call(qdT, kd, vdT, lam_p, dnorm_g, lam_init):
    s = kd.shape[0]
    return pl.pallas_call(
        functools.partial(_diff_kernel, lam_init),
        out_shape=jax.ShapeDtypeStruct((s, W_BRANCH), BF16),
        grid=(s // D_TQ,),
        in_specs=[pl.BlockSpec((256, D_TQ), lambda i: (0, i)),
                  pl.BlockSpec(memory_space=pltpu.VMEM),
                  pl.BlockSpec(memory_space=pltpu.VMEM),
                  _whole((4, DQ_D)),
                  _whole((256, 1))],
        out_specs=pl.BlockSpec((D_TQ, 256), lambda i: (i, 0)),
        scratch_shapes=[pltpu.VMEM((2 * H_D, 256, D_TQ), BF16),
                        pltpu.VMEM((2 * H_D, V_AUG, D_TQ), F32),
                        pltpu.VMEM((2 * H_D, 1, D_TQ), F32)],
        compiler_params=_cparams(("parallel",)),
        name="diff_attention",
    )(qdT, kd, vdT, lam_p, dnorm_g.reshape(256, 1))


C_TQ = 128
C_PAD = N_PREV_CHUNKS * CHUNK
C_WIN = C_PAD + C_TQ


def _band_bias_table(rel_bias):
    i = jnp.arange(C_TQ)[:, None]
    w = jnp.arange(C_WIN)[None, :]
    dist = i + C_PAD - w
    rel_id = jnp.clip(dist, -(CHUNK - 1), MAX_REL_PAST) + (CHUNK - 1)
    in_band = (w // CHUNK >= i // CHUNK) & (w // CHUNK <= i // CHUNK + N_PREV_CHUNKS)
    return jnp.where(in_band[None], rel_bias[:, rel_id].astype(F32), NEG_INF)


def _band_kernel(qc_ref, kcT_ref, vc_ref, bias_ref, o_ref):
    tq = C_TQ
    b = pl.program_id(0)
    w0 = pl.multiple_of(b * tq, tq)
    q_all = qc_ref[...] * 0.125
    key_abs = b * tq - C_PAD + lax.broadcasted_iota(I32, (1, C_WIN), 1)
    ok = key_abs >= 0
    heads = []
    for h in range(H_C):
        s = jnp.dot(q_all[:, h * HEAD_DIM:(h + 1) * HEAD_DIM],
                    kcT_ref[h * HEAD_DIM:(h + 1) * HEAD_DIM, pl.ds(w0, C_WIN)],
                    preferred_element_type=F32) + bias_ref[h]
        s = jnp.where(ok, s, NEG_INF)
        m = jnp.max(s, axis=1, keepdims=True)
        p = jnp.exp(s - m)
        l = jnp.sum(p, axis=1, keepdims=True)
        pv = jnp.dot(p.astype(BF16), vc_ref[pl.ds(w0, C_WIN), (h // 2) * LANES:(h // 2 + 1) * LANES],
                     preferred_element_type=F32)
        heads.append(pv / l)
    for pr in range(H_C // 2):
        o_ref[:, pr * LANES:(pr + 1) * LANES] = _pair_select(heads[2 * pr], heads[2 * pr + 1]).astype(o_ref.dtype)


def _band_call(qc, kcT_pad, vc_pad, bias_tab):
    s = qc.shape[0]
    return pl.pallas_call(
        _band_kernel,
        out_shape=jax.ShapeDtypeStruct((s, W_BRANCH), BF16),
        grid=(s // C_TQ,),
        in_specs=[pl.BlockSpec((C_TQ, 256), lambda i: (i, 0)),
                  pl.BlockSpec(memory_space=pltpu.VMEM),
                  pl.BlockSpec(memory_space=pltpu.VMEM),
                  _whole((H_C, C_TQ, C_WIN))],
        out_specs=pl.BlockSpec((C_TQ, 256), lambda i: (i, 0)),
        compiler_params=_cparams(("parallel",)),
        name="band_attention",
    )(qc, kcT_pad, vc_pad, bias_tab)


B_L = CHUNK
HIGHEST = lax.Precision.HIGHEST


def _mlstm_kernel(qk_ref, vb_ref, ob_ref, misc_ref, cw_ref, cb_ref, gb_ref, ng_ref, o_ref,
                  tail_ref, ct_ref, n_ref, m_ref):
    L = B_L
    hd = HEAD_DIM

    @pl.when(pl.program_id(0) == 0)
    def _():
        tail_ref[...] = jnp.zeros(tail_ref.shape, F32)
        ct_ref[...] = jnp.zeros(ct_ref.shape, F32)
        n_ref[...] = jnp.zeros(n_ref.shape, F32)
        m_ref[...] = jnp.zeros(m_ref.shape, F32)

    x = qk_ref[...]
    xx = jnp.concatenate([tail_ref[...], x], axis=0)
    y = jnp.broadcast_to(cb_ref[...], x.shape)
    for j in range(CONV_K):
        y = y + cw_ref[j:j + 1, :] * xx[8 - (CONV_K - 1) + j:8 - (CONV_K - 1) + j + L, :]
    tail_ref[...] = x[L - 8:L, :]
    qk = y * jax.nn.sigmoid(y)
    q_all = qk[:, :W_BRANCH]
    k_all = qk[:, W_BRANCH:] * 0.125

    gts = misc_ref[...] + gb_ref[...]
    lf = jnp.minimum(gts, 0.0) - jnp.log1p(jnp.exp(-jnp.abs(gts)))
    r_i = lax.broadcasted_iota(I32, (L, L), 0)
    c_i = lax.broadcasted_iota(I32, (L, L), 1)
    causal = c_i <= r_i
    ltri = jnp.where(causal, 1.0, 0.0)
    cum = jnp.dot(ltri, lf, precision=HIGHEST, preferred_element_type=F32)
    lane = lax.broadcasted_iota(I32, (L, LANES), 1)
    mixed = jnp.where(lane < MISC_FB, gts, cum)
    sel_r = lax.broadcasted_iota(I32, (8, LANES), 0)
    sel_c = lax.broadcasted_iota(I32, (8, LANES), 1)
    sel = jnp.where(sel_c == sel_r + MISC_IB, 1.0, 0.0)
    rows = lax.dot_general(sel, mixed, (((1,), (1,)), ((), ())), precision=HIGHEST,
                           preferred_element_type=F32)

    outs = []
    for h in range(H_B):
        q = q_all[:, h * hd:(h + 1) * hd]
        k = k_all[:, h * hd:(h + 1) * hd]
        v = vb_ref[:, h * hd:(h + 1) * hd]
        it_r = rows[h:h + 1, :]
        cum_r = rows[H_B + h:H_B + h + 1, :]
        it_c = gts[:, MISC_IB + h:MISC_IB + h + 1]
        cum_c = cum[:, MISC_FB + h:MISC_FB + h + 1]
        m_prev = m_ref[h]
        dmat = jnp.where(causal, cum_c - cum_r + it_r, NEG_INF)
        m_inter = cum_c + m_prev
        m_t = jnp.maximum(m_inter, jnp.max(dmat, axis=1, keepdims=True))
        qb, kb, vbf = q.astype(BF16), k.astype(BF16), v.astype(BF16)
        qkt = lax.dot_general(qb, kb, (((1,), (1,)), ((), ())), preferred_element_type=F32)
        w = jnp.exp(dmat - m_t) * qkt
        inter = jnp.exp(m_inter - m_t)
        ct = ct_ref[h]
        num = inter * jnp.dot(qb, ct.astype(BF16), preferred_element_type=F32) \
            + jnp.dot(w.astype(BF16), vbf, preferred_element_type=F32)
        n_row = n_ref[h]
        den = inter * jnp.sum(q * n_row, axis=1, keepdims=True) + jnp.sum(w, axis=1, keepdims=True)
        h_t = num / jnp.maximum(jnp.abs(den), jnp.exp(-m_t))
        cum_end = cum_c[L - 1:L, :]
        g = cum_end - cum_c + it_c
        m_new = jnp.maximum(cum_end + m_prev, jnp.max(g, axis=0, keepdims=True))
        carry_scale = jnp.exp(cum_end + m_prev - m_new)
        src_k = jnp.exp(g - m_new) * k
        ct_ref[h] = carry_scale * ct + lax.dot_general(src_k.astype(BF16), vbf, (((0,), (0,)), ((), ())),
                                                       preferred_element_type=F32)
        n_ref[h] = carry_scale * n_row + jnp.sum(src_k, axis=0, keepdims=True)
        m_ref[h] = m_new
        ms = jnp.mean(h_t * h_t, axis=1, keepdims=True)
        hn = h_t * lax.rsqrt(ms + EPS) * ng_ref[:, h * hd:(h + 1) * hd]
        outs.append(jax.nn.sigmoid(ob_ref[:, h * hd:(h + 1) * hd]) * hn)
    o_ref[...] = jnp.concatenate(outs, axis=1).astype(o_ref.dtype)


def _mlstm_call(qkb, vb, ob, misc, conv_w, conv_b, i_bias, f_bias, norm_g):
    s = qkb.shape[0]
    gbias = jnp.zeros((1, LANES), F32)
    gbias = gbias.at[0, MISC_IB:MISC_IB + H_B].set(i_bias).at[0, MISC_FB:MISC_FB + H_B].set(f_bias)
    row = lambda w: pl.BlockSpec((B_L, w), lambda i: (i, 0))
    return pl.pallas_call(
        _mlstm_kernel,
        out_shape=jax.ShapeDtypeStruct((s, W_BRANCH), BF16),
        grid=(s // B_L,),
        in_specs=[row(512), row(256), row(256), row(128),
                  _whole((CONV_K, 512)), _whole((1, 512)), _whole((1, LANES)), _whole((1, 256))],
        out_specs=row(256),
        scratch_shapes=[pltpu.VMEM((8, 512), F32),
                        pltpu.VMEM((H_B, HEAD_DIM, HEAD_DIM), F32),
                        pltpu.VMEM((H_B, 1, HEAD_DIM), F32),
                        pltpu.VMEM((H_B, 1, 1), F32)],
        compiler_params=_cparams(("arbitrary",)),
        name="mlstm_mixer",
    )(qkb, vb, ob, misc, conv_w, conv_b.reshape(1, 512), gbias, norm_g.reshape(1, 256))


M_TM = 256
ROUTE_LOGIT0 = N_GROUPS
RT_E1, RT_E2, RT_G1, RT_G2 = 0, 1, 2, 3


def _lane_argmax(vals, lane):
    v = jnp.max(vals, axis=1, keepdims=True)
    idx = jnp.min(jnp.where(vals == v, lane, float(LANES)), axis=1, keepdims=True)
    return v, idx


def _merge_kernel(x_ref, ya_ref, yb_ref, yc_ref, yd_ref, gm_ref, wg_ref, wb_ref, wo_ref, gf_ref,
                  wr_ref, rb_ref, xo_ref, h2_ref, rt_ref):
    x = x_ref[...]
    h = _rms(x, gm_ref[...]).astype(BF16)
    mixed = jnp.zeros(x.shape, F32)
    for n, y_ref in enumerate((ya_ref, yb_ref, yc_ref, yd_ref)):
        gate = jax.nn.sigmoid(jnp.dot(h, wg_ref[:, n * D_MODEL:(n + 1) * D_MODEL],
                                      preferred_element_type=F32))
        up = jnp.dot(y_ref[...], wb_ref[n], preferred_element_type=F32)
        mixed = mixed + gate * up
    xn = x + jnp.dot(mixed.astype(BF16), wo_ref[...], preferred_element_type=F32)
    xo_ref[...] = xn
    h2 = _rms(xn, gf_ref[...])
    h2_ref[...] = h2
    logits = jnp.dot(h2.astype(BF16), wr_ref[...], preferred_element_type=F32) + rb_ref[...]

    lane = lax.broadcasted_iota(I32, logits.shape, 1).astype(F32)
    neg = -jnp.inf
    gmask = lane < N_GROUPS
    gmax, g_sel = _lane_argmax(jnp.where(gmask, logits, neg), lane)
    g_gate = 1.0 / jnp.sum(jnp.where(gmask, jnp.exp(logits - gmax), 0.0), axis=1, keepdims=True)
    e_lo = ROUTE_LOGIT0 + EXPERTS_PER_GROUP * g_sel
    el = jnp.where((lane >= e_lo) & (lane < e_lo + EXPERTS_PER_GROUP), logits, neg)
    v1, i1 = _lane_argmax(el, lane)
    v2, i2 = _lane_argmax(jnp.where(lane == i1, neg, el), lane)
    e = jnp.exp(v2 - v1)
    p1 = 1.0 / (1.0 + e)
    p2 = e / (1.0 + e)
    rt = jnp.where(lane == RT_E1, i1 - ROUTE_LOGIT0,
                   jnp.where(lane == RT_E2, i2 - ROUTE_LOGIT0,
                             jnp.where(lane == RT_G1, p1 * g_gate,
                                       jnp.where(lane == RT_G2, p2 * g_gate, 0.0))))
    rt_ref[...] = rt


def _merge_call(x2, ya, yb, yc, yd, g_mix, w_gate, w_branch, w_out, g_ffn, w_route, b_route):
    s = x2.shape[0]
    row = lambda w: pl.BlockSpec((M_TM, w), lambda i: (i, 0))
    vm = pl.BlockSpec(memory_space=pltpu.VMEM)
    return pl.pallas_call(
        _merge_kernel,
        out_shape=(jax.ShapeDtypeStruct((s, D_MODEL), F32), jax.ShapeDtypeStruct((s, D_MODEL), F32),
                   jax.ShapeDtypeStruct((s, LANES), F32)),
        grid=(s // M_TM,),
        in_specs=[row(D_MODEL), row(256), row(256), row(256), row(256),
                  vm, vm, vm, vm, vm, vm, vm],
        out_specs=(row(D_MODEL), row(D_MODEL), row(LANES)),
        compiler_params=_cparams(("parallel",)),
        name="merge_route",
    )(x2, ya, yb, yc, yd, g_mix.reshape(1, D_MODEL), w_gate, w_branch, w_out,
      g_ffn.reshape(1, D_MODEL), w_route, b_route)


E_BM = 128


def _gather_rows(src_hbm, idx_ref, dst, sem, n, wait):
    def one(r, c):
        cp = pltpu.make_async_copy(src_hbm.at[pl.ds(idx_ref[0, 0, r], 1)], dst.at[pl.ds(r, 1)], sem)
        if wait:
            cp.wait()
        else:
            cp.start()
        return c
    lax.fori_loop(0, n, one, 0, unroll=8)


def _moe_kernel(be_ref, nused_ref, tok_ref, tok_next_ref, h2_hbm, w1_ref, w3_ref, w2_ref, o_ref,
                xbuf, w1b, w3b, w2b, sem):
    i = pl.program_id(0)
    used = i < nused_ref[0]
    slot = i % 2

    @pl.when((i == 0) & used)
    def _():
        _gather_rows(h2_hbm, tok_ref, xbuf.at[0], sem.at[0], E_BM, wait=False)

    @pl.when(i + 1 < nused_ref[0])
    def _():
        _gather_rows(h2_hbm, tok_next_ref, xbuf.at[1 - slot], sem.at[1 - slot], E_BM, wait=False)

    @pl.when(used)
    def _():
        prev = be_ref[jnp.maximum(i - 1, 0)]

        @pl.when((i == 0) | (be_ref[i] != prev))
        def _():
            w1b[...] = w1_ref[0].astype(BF16)
            w3b[...] = w3_ref[0].astype(BF16)
            w2b[...] = w2_ref[0].astype(BF16)

        _gather_rows(h2_hbm, tok_ref, xbuf.at[slot], sem.at[slot], E_BM, wait=True)
        xb = xbuf[slot].astype(BF16)
        a = jnp.dot(xb, w1b[...], preferred_element_type=F32)
        g = jnp.dot(xb, w3b[...], preferred_element_type=F32)
        hid = (a * jax.nn.sigmoid(a) * g).astype(BF16)
        o_ref[...] = jnp.dot(hid, w2b[...], preferred_element_type=F32)

    @pl.when(jnp.logical_not(used))
    def _():
        o_ref[...] = jnp.zeros(o_ref.shape, F32)


def _moe_call(h2, row_tok, block_e, nused, w1, w3, w2):
    n_rows = row_tok.shape[0]
    n_blocks = n_rows // E_BM
    wspec = lambda shp: pl.BlockSpec((1,) + shp, lambda i, be, nu: (be[i], 0, 0))
    grid_spec = pltpu.PrefetchScalarGridSpec(
        num_scalar_prefetch=2,
        grid=(n_blocks,),
        in_specs=[pl.BlockSpec((1, 1, E_BM), lambda i, be, nu: (i, 0, 0), memory_space=pltpu.SMEM),
                  pl.BlockSpec((1, 1, E_BM), lambda i, be, nu: (jnp.minimum(i + 1, n_blocks - 1), 0, 0),
                               memory_space=pltpu.SMEM),
                  pl.BlockSpec(memory_space=pl.ANY),
                  wspec((D_MODEL, F_EXPERT)), wspec((D_MODEL, F_EXPERT)), wspec((F_EXPERT, D_MODEL))],
        out_specs=pl.BlockSpec((E_BM, D_MODEL), lambda i, be, nu: (i, 0)),
        scratch_shapes=[pltpu.VMEM((2, E_BM, D_MODEL), F32),
                        pltpu.VMEM((D_MODEL, F_EXPERT), BF16),
                        pltpu.VMEM((D_MODEL, F_EXPERT), BF16),
                        pltpu.VMEM((F_EXPERT, D_MODEL), BF16),
                        pltpu.SemaphoreType.DMA((2,))],
    )
    tok3 = row_tok.reshape(n_blocks, 1, E_BM)
    return pl.pallas_call(
        _moe_kernel,
        out_shape=jax.ShapeDtypeStruct((n_rows, D_MODEL), F32),
        grid_spec=grid_spec,
        compiler_params=_cparams(("arbitrary",)),
        name="moe_experts",
    )(block_e, nused, tok3, tok3, h2, w1, w3, w2)


CB_TM = 128


def _combine_kernel(final, pos_ref, pos_next_ref, x_ref, rt_ref, ys_hbm, gfin_ref, o_ref, buf, sem):
    i = pl.program_id(0)
    slot = i % 2
    n = 2 * CB_TM

    @pl.when(i == 0)
    def _():
        _gather_rows(ys_hbm, pos_ref, buf.at[0], sem.at[0], n, wait=False)

    @pl.when(i + 1 < pl.num_programs(0))
    def _():
        _gather_rows(ys_hbm, pos_next_ref, buf.at[1 - slot], sem.at[1 - slot], n, wait=False)
    _gather_rows(ys_hbm, pos_ref, buf.at[slot], sem.at[slot], n, wait=True)

    rt = rt_ref[...]
    y = (buf[slot, 0:CB_TM, :] * rt[:, RT_G1:RT_G1 + 1]
         + buf[slot, CB_TM:2 * CB_TM, :] * rt[:, RT_G2:RT_G2 + 1])
    xn = x_ref[...] + y
    if final:
        xn = _rms(xn, gfin_ref[...])
    o_ref[...] = xn


def _combine_call(x2, rt, ys, pos, g_final, final):
    s = x2.shape[0]
    nb = s // CB_TM
    pos_b = pos.reshape(nb, CB_TM, 2).transpose(0, 2, 1).reshape(nb, 1, 2 * CB_TM)
    return pl.pallas_call(
        functools.partial(_combine_kernel, final),
        out_shape=jax.ShapeDtypeStruct((s, D_MODEL), F32),
        grid=(nb,),
        in_specs=[pl.BlockSpec((1, 1, 2 * CB_TM), lambda i: (i, 0, 0), memory_space=pltpu.SMEM),
                  pl.BlockSpec((1, 1, 2 * CB_TM), lambda i: (jnp.minimum(i + 1, nb - 1), 0, 0),
                               memory_space=pltpu.SMEM),
                  pl.BlockSpec((CB_TM, D_MODEL), lambda i: (i, 0)),
                  pl.BlockSpec((CB_TM, LANES), lambda i: (i, 0)),
                  pl.BlockSpec(memory_space=pl.ANY),
                  _whole((1, D_MODEL))],
        out_specs=pl.BlockSpec((CB_TM, D_MODEL), lambda i: (i, 0)),
        scratch_shapes=[pltpu.VMEM((2, 2 * CB_TM, D_MODEL), F32), pltpu.SemaphoreType.DMA((2,))],
        compiler_params=_cparams(("arbitrary",)),
        name="moe_combine",
    )(pos_b, pos_b, x2, rt, ys, g_final.reshape(1, D_MODEL))


PLAN_BLOCK = 256


def _dispatch_plan(rt):
    t = rt.shape[0]
    n_assign = t * TOP_K_INNER
    eid = rt[:, RT_E1:RT_E2 + 1].astype(I32).reshape(n_assign)
    onehot = (eid[:, None] == jnp.arange(N_EXPERTS, dtype=I32)[None, :]).astype(F32)
    pb = PLAN_BLOCK
    oh3 = onehot.reshape(n_assign // pb, pb, N_EXPERTS)
    tri = jnp.tril(jnp.ones((pb, pb), F32), k=-1)
    within = jnp.einsum('ij,bjk->bik', tri, oh3)
    block_tot = jnp.sum(oh3, axis=1)
    block_off = jnp.cumsum(block_tot, axis=0) - block_tot
    counts = (block_off[-1] + block_tot[-1]).astype(I32)
    rank = jnp.sum(oh3 * (within + block_off[:, None, :]), axis=-1).reshape(n_assign)
    padded = (counts + E_BM - 1) // E_BM * E_BM
    ends_pad = jnp.cumsum(padded)
    starts_pad = ends_pad - padded
    dest = (jnp.sum(onehot * starts_pad.astype(F32)[None, :], axis=-1) + rank).astype(I32)
    n_rows = -(-(n_assign + N_EXPERTS * (E_BM - 1)) // E_BM) * E_BM
    n_blocks = n_rows // E_BM
    tok = jnp.repeat(jnp.arange(t, dtype=I32), TOP_K_INNER)
    row_tok = jnp.zeros((n_rows,), I32).at[dest].set(tok)
    block_start = jnp.arange(n_blocks, dtype=I32) * E_BM
    block_e = jnp.minimum(jnp.sum((ends_pad[None, :] <= block_start[:, None]).astype(I32), axis=1),
                          N_EXPERTS - 1)
    nused = (ends_pad[-1] // E_BM).astype(I32).reshape(1)
    return row_tok, block_e, nused, dest.reshape(t, TOP_K_INNER)


def _layer(x2, layer_idx, p, final_g, final):
    (norm_mix_g, w_in, conv_w, conv_b, i_bias, f_bias, mnorm_g, rel_bias, lam_p, dnorm_g,
     w_branch, w_gate, w_out, norm_ffn_g, rgw, rgb, rew, reb, w1, w3, w2) = p
    o = _k1_call(x2, norm_mix_g, *_rearrange_w_in(w_in))

    ya = _dsa_call(o["qaT"], o["qiT"], o["wiT"], o["ka"], o["vaT"], o["misc_bf"])
    yb = _mlstm_call(o["qkb"], o["vb"], o["ob"], o["misc"], conv_w, conv_b, i_bias, f_bias, mnorm_g)
    kcT_pad = jnp.pad(o["kcT"], ((0, 0), (C_PAD, 0)))
    vc_pad = jnp.pad(o["vc"], ((C_PAD, 0), (0, 0)))
    yc = _band_call(o["qc"], kcT_pad, vc_pad, _band_bias_table(rel_bias))
    lam_init = 0.8 - 0.6 * math.exp(-0.3 * layer_idx)
    yd = _diff_call(o["qdT"], o["kd"], o["vdT"], lam_p, dnorm_g, lam_init)

    w_route = jnp.concatenate([rgw, rew, jnp.zeros((D_MODEL, LANES - N_GROUPS - N_EXPERTS), F32)],
                              axis=1).astype(BF16)
    b_route = jnp.concatenate([rgb, reb, jnp.zeros((LANES - N_GROUPS - N_EXPERTS,), F32)]).reshape(1, LANES)
    xn, h2, rt = _merge_call(x2, ya, yb, yc, yd, norm_mix_g, w_gate.astype(BF16), w_branch.astype(BF16),
                             w_out.astype(BF16), norm_ffn_g, w_route, b_route)
    row_tok, block_e, nused, pos = _dispatch_plan(rt)
    ys = _moe_call(h2, row_tok, block_e, nused, w1, w3, w2)
    return _combine_call(xn, rt, ys, pos, final_g, final)


def kernel(x, norm_mix_g, w_in, conv_w, conv_b, mlstm_i_bias, mlstm_f_bias, mlstm_norm_g, relpos_bias, diff_lambda, diff_norm_g, w_branch, w_gate, w_out, norm_ffn_g, router_group_w, router_group_b, router_expert_w, router_expert_b, expert_w1, expert_w3, expert_w2, final_norm_g):
    assert x.shape[0] == 1 and x.shape[2] == D_MODEL
    params = (norm_mix_g, w_in, conv_w, conv_b, mlstm_i_bias, mlstm_f_bias, mlstm_norm_g, relpos_bias,
              diff_lambda, diff_norm_g, w_branch, w_gate, w_out, norm_ffn_g, router_group_w,
              router_group_b, router_expert_w, router_expert_b, expert_w1, expert_w3, expert_w2)
    depth = norm_mix_g.shape[0]
    x2 = x[0]
    for l in range(depth):
        x2 = _layer(x2, l, tuple(a[l] for a in params), final_norm_g, l == depth - 1)
    return x2[None]
```

```python
import functools
import math

import jax
import jax.numpy as jnp
from jax import lax
from jax.experimental import pallas as pl
from jax.experimental.pallas import tpu as pltpu

F32 = jnp.float32
BF16 = jnp.bfloat16
I32 = jnp.int32

D_MODEL = 1024
CHUNK = 64
HEAD_DIM = 64
NEG_INF = -1e30
H_A = 4
H_IDX = 4
D_IDX = 32
TOPK_MAX = 256
H_B = 4
CONV_K = 4
H_C = 4
N_PREV_CHUNKS = 8
MAX_REL_PAST = 128
H_D = 4
DQ_D = 32
W_BRANCH = 256
N_BRANCH = 4
N_GROUPS = 4
EXPERTS_PER_GROUP = 8
N_EXPERTS = 32
TOP_K_INNER = 2
F_EXPERT = 512
EPS = 1e-6

VMEM_LIMIT_BYTES = 52 * 1024 * 1024
LANES = 128

INT_MIN = -(2 ** 31)
I16_MIN = -(2 ** 15)
I16 = jnp.int16
M_INIT = -5e29

_COL_SIZES = (256, 256, 256, 128, 32, 4, 256, 256, 256, 4, 4, 256, 256, 256, 256, 256, 256, 256)
_COL_NAMES = ("qa", "ka", "va", "qi", "ki", "wi", "qb", "kb", "vb", "ib", "fb", "ob",
              "qc", "kc", "vc", "qd", "kd", "vd")
_COL_OFF = {}
_o = 0
for _n, _s in zip(_COL_NAMES, _COL_SIZES):
    _COL_OFF[_n] = (_o, _s)
    _o += _s
C_IN = _o
MISC_KI = 0
MISC_WI = 32
MISC_IB = 36
MISC_FB = 40


def _cparams(sem):
    return pltpu.CompilerParams(dimension_semantics=sem, vmem_limit_bytes=VMEM_LIMIT_BYTES)


def _whole(shape):
    nd = len(shape)
    return pl.BlockSpec(shape, lambda *_: (0,) * nd)


_K1_NAT = (("ka", 256, BF16), ("misc", 128, F32), ("qkb", 512, F32), ("vb", 256, F32), ("ob", 256, F32),
           ("qc", 256, BF16), ("vc", 256, BF16), ("kd", 256, BF16))
_K1_TR = (("qaT", 256, BF16), ("qiT", 128, BF16), ("vaT", 256, BF16), ("kcT", 256, BF16),
          ("qdT", 256, BF16), ("vdT", 256, BF16), ("wiT", 8, F32))
K1_NAT_WIDTH = sum(w for _, w, _ in _K1_NAT)
K1_TR_WIDTH = sum(w for _, w, _ in _K1_TR)
K1_TM = 512
V_AUG = 80
_K1_VAUG = ("vaT", "vdT")


def _k1_tr_rows(name, width):
    return (width // HEAD_DIM) * V_AUG if name in _K1_VAUG else width


def _rearrange_w_in(w_in):
    def cols(name):
        o, s = _COL_OFF[name]
        return w_in[:, o:o + s]
    d = w_in.shape[0]
    misc = jnp.concatenate([cols("ki"), cols("wi"), cols("ib"), cols("fb"),
                            jnp.zeros((d, LANES - 44), w_in.dtype)], axis=1)
    nat = [cols("ka"), misc, cols("qb"), cols("kb"), cols("vb"), cols("ob"), cols("qc"), cols("vc"),
           cols("kd")]
    tr = [cols("qa"), cols("qi"), cols("va"), cols("kc"), cols("qd"), cols("vd"), cols("wi"),
          jnp.zeros((d, 8 - H_IDX), w_in.dtype)]
    return (jnp.concatenate(nat, axis=1).astype(BF16), jnp.concatenate(tr, axis=1).T.astype(BF16))


def _rms(x, g):
    ms = jnp.mean(x * x, axis=-1, keepdims=True)
    return x * lax.rsqrt(ms + EPS) * g


def _k1_kernel(x_ref, g_ref, wn_ref, wt_ref, *out_refs):
    h = _rms(x_ref[...], g_ref[...]).astype(BF16)
    refs = dict(zip([n for n, _, _ in _K1_NAT] + ["misc_bf"] + [n for n, _, _ in _K1_TR], out_refs))
    off = 0
    for name, width, _ in _K1_NAT:
        r = jnp.dot(h, wn_ref[:, off:off + width], preferred_element_type=F32)
        refs[name][...] = r.astype(refs[name].dtype)
        if name == "misc":
            refs["misc_bf"][...] = r.astype(BF16)
        off += width
    off = 0
    for name, width, _ in _K1_TR:
        r = lax.dot_general(wt_ref[off:off + width, :], h, (((1,), (1,)), ((), ())),
                            preferred_element_type=F32)
        if name in _K1_VAUG:
            tail = (lax.broadcasted_iota(I32, (V_AUG - HEAD_DIM, r.shape[1]), 0) == 0)
            tail = jnp.where(tail, 1.0, 0.0).astype(BF16)
            for hh in range(width // HEAD_DIM):
                refs[name][hh * V_AUG:hh * V_AUG + HEAD_DIM, :] = \
                    r[hh * HEAD_DIM:(hh + 1) * HEAD_DIM, :].astype(BF16)
                refs[name][hh * V_AUG + HEAD_DIM:(hh + 1) * V_AUG, :] = tail
        else:
            refs[name][...] = r.astype(refs[name].dtype)
        off += width


def _k1_call(x2, g, w_nat, w_trT):
    s = x2.shape[0]
    tm = K1_TM
    outs = (tuple(jax.ShapeDtypeStruct((s, w), dt) for _, w, dt in _K1_NAT)
            + (jax.ShapeDtypeStruct((s, LANES), BF16),)
            + tuple(jax.ShapeDtypeStruct((_k1_tr_rows(n, w), s), dt) for n, w, dt in _K1_TR))
    out_specs = (tuple(pl.BlockSpec((tm, w), lambda i: (i, 0)) for _, w, _ in _K1_NAT)
                 + (pl.BlockSpec((tm, LANES), lambda i: (i, 0)),)
                 + tuple(pl.BlockSpec((_k1_tr_rows(n, w), tm), lambda i: (0, i)) for n, w, _ in _K1_TR))
    res = pl.pallas_call(
        _k1_kernel,
        out_shape=outs,
        grid=(s // tm,),
        in_specs=[pl.BlockSpec((tm, D_MODEL), lambda i: (i, 0)),
                  _whole((1, D_MODEL)),
                  _whole((D_MODEL, K1_NAT_WIDTH)),
                  _whole((K1_TR_WIDTH, D_MODEL))],
        out_specs=out_specs,
        compiler_params=_cparams(("parallel",)),
        name="k1_norm_proj",
    )(x2, g.reshape(1, D_MODEL), w_nat, w_trT)
    return dict(zip([n for n, _, _ in _K1_NAT] + ["misc_bf"] + [n for n, _, _ in _K1_TR], res))


A_TQ = 256
A_KT = 512
A_CNT_ROWS = 32
LOG2E = 1.4426950408889634
QK_AHEAD = 4


def _flash_step(s, i, v_t, c, m_ref, acc_ref):
    m_old = m_ref[i]
    m_new = jnp.maximum(m_old, jnp.max(s, axis=0, keepdims=True))
    alpha = jnp.exp2((m_old - m_new) * c)
    p = jnp.exp2((s - m_new) * c)
    acc_ref[i] = alpha * acc_ref[i] + jnp.dot(v_t, p.astype(BF16), preferred_element_type=F32)
    m_ref[i] = m_new


def _flash_result(acc_ref, i):
    return acc_ref[i, 0:HEAD_DIM, :] / acc_ref[i, HEAD_DIM:HEAD_DIM + 1, :]


def _qk_prologue(k_t, qpad_ref, n, s_ref):
    for i in range(min(QK_AHEAD, n)):
        s_ref[i] = jnp.dot(k_t, qpad_ref[i], preferred_element_type=F32)


def _flash_tile(k_t, qpad_ref, n, v_tile, mask, c, m_ref, acc_ref, s_ref=None, k_next=None):
    a = min(QK_AHEAD, n)
    if s_ref is None:
        pend = [jnp.dot(k_t, qpad_ref[i], preferred_element_type=F32) for i in range(a)]
    else:
        pend = [s_ref[i] for i in range(a)]
    for i in range(n):
        s = pend.pop(0)
        if i + a < n:
            pend.append(jnp.dot(k_t, qpad_ref[i + a], preferred_element_type=F32))
        elif s_ref is not None:
            s_ref[i + a - n] = jnp.dot(k_next, qpad_ref[i + a - n], preferred_element_type=F32)
        if mask is not None:
            s = jnp.where(mask, s, NEG_INF)
        _flash_step(s, i, v_tile(i), c, m_ref, acc_ref)


def _dsa_kernel(topk, qaT_ref, qiT_ref, wiT_ref, ka_ref, vaT_ref, mb_ref, tri_ref, o_ref,
                keys_ref, half_ref, qpad_ref, qipad_ref, acc_ref, m_ref, carry_ref):
    tq, kt = A_TQ, A_KT
    b = pl.program_id(0)
    ntiles = ((b + 1) * tq + kt - 1) // kt
    q_pos = b * tq + lax.broadcasted_iota(I32, (1, tq), 1)
    vis_end = (q_pos // CHUNK + 1) * CHUNK

    qiT = qiT_ref[...]
    for h in range(H_IDX):
        qipad_ref[h, 0:D_IDX, :] = qiT[h * D_IDX:(h + 1) * D_IDX, :]
        qipad_ref[h, D_IDX:LANES, :] = jnp.zeros((LANES - D_IDX, tq), BF16)

    def p1(j, carry):
        s0 = pl.multiple_of(j * kt, kt)
        mb = mb_ref[pl.ds(s0, kt), :]
        score = jnp.zeros((kt, tq), F32)
        for h in range(H_IDX):
            r = jnp.dot(mb, qipad_ref[h], preferred_element_type=F32)
            score = score + jnp.maximum(r, 0.0) * wiT_ref[h:h + 1, :]
        bits = lax.bitcast_convert_type(score, I32)
        key = bits ^ (lax.shift_right_arithmetic(bits, 31) & 0x7FFFFFFF)
        s_pos = s0 + lax.broadcasted_iota(I32, (kt, 1), 0)
        key = jnp.where(s_pos < vis_end, key, INT_MIN)
        keys_ref[pl.ds(s0, kt), :] = key
        half_ref[pl.ds(s0, kt), :] = lax.shift_right_arithmetic(key, 16).astype(I16)
        return carry
    lax.fori_loop(0, ntiles, p1, 0)

    def count16(cand, strict):
        c16 = cand.astype(I16)

        def body(j, acc):
            k = half_ref[pl.ds(pl.multiple_of(j * kt, kt), kt), :]
            hit = (k > c16) if strict else (k >= c16)
            ones = jnp.where(hit, jnp.bfloat16(1), jnp.bfloat16(0))
            ones = ones.reshape(kt // A_CNT_ROWS, A_CNT_ROWS, tq)
            parts = [ones[i] for i in range(kt // A_CNT_ROWS)]
            while len(parts) > 1:
                parts = [parts[i] + parts[i + 1] for i in range(0, len(parts), 2)]
            return acc + parts[0].astype(F32)
        acc = lax.fori_loop(0, ntiles, body, jnp.zeros((A_CNT_ROWS, tq), F32))
        return jnp.sum(acc, axis=0, keepdims=True)

    def search16(k_req):
        zero = jnp.zeros((1, tq), I32)
        t = jnp.where(count16(zero, False) >= k_req, zero, I16_MIN)

        def bit_body(i, t):
            cand = t | lax.shift_left(jnp.int32(1), 14 - i)
            return jnp.where(count16(cand, False) >= k_req, cand, t)
        return lax.fori_loop(0, 15, bit_body, t)

    kf = float(topk)
    t_hi = search16(kf)
    n_above_hi = count16(t_hi, True)
    t_hi16 = t_hi.astype(I16)

    def low_halves(j, carry):
        s0 = pl.multiple_of(j * kt, kt)
        lo = ((keys_ref[pl.ds(s0, kt), :] & 0xFFFF) - 32768).astype(I16)
        half_ref[pl.ds(s0, kt), :] = jnp.where(half_ref[pl.ds(s0, kt), :] == t_hi16, lo, jnp.int16(I16_MIN))
        return carry
    lax.fori_loop(0, ntiles, low_halves, 0)
    t_lo = search16(kf - n_above_hi)
    tau = lax.shift_left(t_hi, 16) | ((t_lo + 32768) & 0xFFFF)
    need = jnp.where(tau == INT_MIN, 0.0, kf - n_above_hi - count16(t_lo, True))

    m_ref[...] = jnp.full(m_ref.shape, M_INIT, F32)
    acc_ref[...] = jnp.zeros(acc_ref.shape, F32)
    carry_ref[...] = jnp.zeros(carry_ref.shape, F32)
    qaT = qaT_ref[...] * 0.125
    row = lax.broadcasted_iota(I32, qaT.shape, 0)
    for h in range(H_A):
        qpad_ref[h] = jnp.where((row >= h * HEAD_DIM) & (row < (h + 1) * HEAD_DIM), qaT,
                                jnp.zeros_like(qaT))

    def p3(j, carry):
        s0 = pl.multiple_of(j * kt, kt)
        k = keys_ref[pl.ds(s0, kt), :]
        gt = k > tau
        eq = k == tau
        eqf = jnp.where(eq, 1.0, 0.0)
        pref = jnp.dot(tri_ref[...], eqf.astype(BF16), preferred_element_type=F32)
        seen = carry_ref[...]
        sel = gt | (eq & (pref + seen < need))
        carry_ref[...] = seen + pref[kt - 1:kt, :] + eqf[kt - 1:kt, :]
        k_t = ka_ref[pl.ds(s0, kt), :]
        _flash_tile(k_t, qpad_ref, H_A, lambda i: vaT_ref[i * V_AUG:(i + 1) * V_AUG, pl.ds(s0, kt)],
                    sel, LOG2E, m_ref, acc_ref)
        return carry
    lax.fori_loop(0, ntiles, p3, 0)

    ys = [_flash_result(acc_ref, h) for h in range(H_A)]
    o_ref[...] = jnp.concatenate(ys, axis=0).T.astype(o_ref.dtype)


def _dsa_call(qaT, qiT, wiT, ka, vaT, misc_bf):
    s = ka.shape[0]
    topk = min(TOPK_MAX, s // 4)
    tri = jnp.tril(jnp.ones((A_KT, A_KT), F32), k=-1).astype(BF16)
    vm = pl.BlockSpec(memory_space=pltpu.VMEM)
    return pl.pallas_call(
        functools.partial(_dsa_kernel, topk),
        out_shape=jax.ShapeDtypeStruct((s, W_BRANCH), BF16),
        grid=(s // A_TQ,),
        in_specs=[pl.BlockSpec((256, A_TQ), lambda i: (0, i)),
                  pl.BlockSpec((LANES, A_TQ), lambda i: (0, i)),
                  pl.BlockSpec((8, A_TQ), lambda i: (0, i)),
                  vm, vm, vm, vm],
        out_specs=pl.BlockSpec((A_TQ, 256), lambda i: (i, 0)),
        scratch_shapes=[pltpu.VMEM((s, A_TQ), I32),
                        pltpu.VMEM((s, A_TQ), I16),
                        pltpu.VMEM((H_A, 256, A_TQ), BF16),
                        pltpu.VMEM((H_IDX, LANES, A_TQ), BF16),
                        pltpu.VMEM((H_A, V_AUG, A_TQ), F32),
                        pltpu.VMEM((H_A, 1, A_TQ), F32),
                        pltpu.VMEM((1, A_TQ), F32)],
        compiler_params=_cparams(("arbitrary",)),
        name="dsa_mixer",
    )(qaT, qiT, wiT, ka, vaT, misc_bf, tri)


def _pair_select(lo, hi):
    lane = lax.broadcasted_iota(I32, lo.shape, 1)
    return jnp.where(lane < HEAD_DIM, lo, hi)


def _pair_head_rms(o, g):
    lane = lax.broadcasted_iota(I32, o.shape, 1)
    low = lane < HEAD_DIM
    sq = o * o
    ms_lo = jnp.sum(jnp.where(low, sq, 0.0), axis=1, keepdims=True) * (1.0 / HEAD_DIM)
    ms_hi = jnp.sum(jnp.where(low, 0.0, sq), axis=1, keepdims=True) * (1.0 / HEAD_DIM)
    ms = jnp.where(low, ms_lo, ms_hi)
    return o * lax.rsqrt(ms + EPS) * g


D_TQ = 256
D_KT = 1024


def _diff_kernel(lam_init, qT_ref, kd_ref, vT_ref, lam_ref, g_ref, o_ref, qpad_ref, acc_ref, m_ref, s_ref):
    tq = D_TQ
    b = pl.program_id(0)
    c = (DQ_D ** -0.5) * LOG2E
    q_pos = b * tq + lax.broadcasted_iota(I32, (1, tq), 1)
    vis_end = (q_pos // CHUNK + 1) * CHUNK
    m_ref[...] = jnp.full(m_ref.shape, M_INIT, F32)
    acc_ref[...] = jnp.zeros(acc_ref.shape, F32)
    qT = qT_ref[...]
    row = lax.broadcasted_iota(I32, qT.shape, 0)
    for i in range(2 * H_D):
        qpad_ref[i] = jnp.where((row >= i * DQ_D) & (row < (i + 1) * DQ_D), qT, jnp.zeros_like(qT))

    def tile(s0, kt, masked, s_ref=None, s0_next=None):
        k_t = kd_ref[pl.ds(s0, kt), :]
        k_next = None if s0_next is None else kd_ref[pl.ds(s0_next, kt), :]
        vis = None
        if masked:
            vis = (s0 + lax.broadcasted_iota(I32, (kt, 1), 0)) < vis_end
        _flash_tile(k_t, qpad_ref, 2 * H_D, lambda i: vT_ref[(i // 2) * V_AUG:(i // 2 + 1) * V_AUG, pl.ds(s0, kt)],
                    vis, c, m_ref, acc_ref, s_ref, k_next)

    n_big = (b * tq) // D_KT

    @pl.when(n_big > 0)
    def _():
        _qk_prologue(kd_ref[pl.ds(0, D_KT), :], qpad_ref, 2 * H_D, s_ref)

    def full_tile(j, carry):
        j_next = jnp.minimum(j + 1, n_big - 1)
        tile(pl.multiple_of(j * D_KT, D_KT), D_KT, False, s_ref, pl.multiple_of(j_next * D_KT, D_KT))
        return carry
    lax.fori_loop(0, n_big, full_tile, 0)

    def small_tile(j, carry):
        tile(pl.multiple_of(j * tq, tq), tq, False)
        return carry
    lax.fori_loop(n_big * (D_KT // tq), b, small_tile, 0)
    tile(pl.multiple_of(b * tq, tq), tq, True)

    lp = lam_ref[...]
    lam = (jnp.exp(jnp.sum(lp[0:1] * lp[1:2], axis=1, keepdims=True))
           - jnp.exp(jnp.sum(lp[2:3] * lp[3:4], axis=1, keepdims=True)) + lam_init)
    ys = []
    for h in range(H_D):
        o = _flash_result(acc_ref, 2 * h) - lam * _flash_result(acc_ref, 2 * h + 1)
        ms = jnp.mean(o * o, axis=0, keepdims=True)
        ys.append(o * lax.rsqrt(ms + EPS) * g_ref[h * HEAD_DIM:(h + 1) * HEAD_DIM, :] * (1.0 - lam_init))
    o_ref[...] = jnp.concatenate(ys, axis=0).T.astype(o_ref.dtype)


def _diff_call(qdT, kd, vdT, lam_p, dnorm_g, lam_init):
    s = kd.shape[0]
    return pl.pallas_call(
        functools.partial(_diff_kernel, lam_init),
        out_shape=jax.ShapeDtypeStruct((s, W_BRANCH), BF16),
        grid=(s // D_TQ,),
        in_specs=[pl.BlockSpec((256, D_TQ), lambda i: (0, i)),
                  pl.BlockSpec(memory_space=pltpu.VMEM),
                  pl.BlockSpec(memory_space=pltpu.VMEM),
                  _whole((4, DQ_D)),
                  _whole((256, 1))],
        out_specs=pl.BlockSpec((D_TQ, 256), lambda i: (i, 0)),
        scratch_shapes=[pltpu.VMEM((2 * H_D, 256, D_TQ), BF16),
                        pltpu.VMEM((2 * H_D, V_AUG, D_TQ), F32),
                        pltpu.VMEM((2 * H_D, 1, D_TQ), F32),
                        pltpu.VMEM((QK_AHEAD, D_KT, D_TQ), F32)],
        compiler_params=_cparams(("parallel",)),
        name="diff_attention",
    )(qdT, kd, vdT, lam_p, dnorm_g.reshape(256, 1))


C_TQ = 128
C_PAD = N_PREV_CHUNKS * CHUNK
C_WIN = C_PAD + C_TQ


def _band_bias_table(rel_bias):
    i = jnp.arange(C_TQ)[:, None]
    w = jnp.arange(C_WIN)[None, :]
    dist_desc = jnp.arange(C_TQ - 1 + C_PAD, C_PAD - C_WIN, -1)
    line = rel_bias[:, jnp.clip(dist_desc, -(CHUNK - 1), MAX_REL_PAST) + (CHUNK - 1)].astype(F32)
    rows = jax.vmap(lambda r: lax.dynamic_slice_in_dim(line, C_TQ - 1 - r, C_WIN, axis=1))(jnp.arange(C_TQ))
    bias = jnp.transpose(rows, (1, 0, 2))
    in_band = (w // CHUNK >= i // CHUNK) & (w // CHUNK <= i // CHUNK + N_PREV_CHUNKS)
    return jnp.where(in_band[None], bias, NEG_INF)


def _band_kernel(qc_ref, kcT_ref, vc_ref, bias_ref, o_ref):
    tq = C_TQ
    b = pl.program_id(0)
    w0 = pl.multiple_of(b * tq, tq)
    q_all = qc_ref[...] * 0.125
    key_abs = b * tq - C_PAD + lax.broadcasted_iota(I32, (1, C_WIN), 1)
    ok = key_abs >= 0
    heads = []
    for h in range(H_C):
        s = jnp.dot(q_all[:, h * HEAD_DIM:(h + 1) * HEAD_DIM],
                    kcT_ref[h * HEAD_DIM:(h + 1) * HEAD_DIM, pl.ds(w0, C_WIN)],
                    preferred_element_type=F32) + bias_ref[h]
        s = jnp.where(ok, s, NEG_INF)
        m = jnp.max(s, axis=1, keepdims=True)
        p = jnp.exp(s - m)
        l = jnp.sum(p, axis=1, keepdims=True)
        pv = jnp.dot(p.astype(BF16), vc_ref[pl.ds(w0, C_WIN), (h // 2) * LANES:(h // 2 + 1) * LANES],
                     preferred_element_type=F32)
        heads.append(pv / l)
    for pr in range(H_C // 2):
        o_ref[:, pr * LANES:(pr + 1) * LANES] = _pair_select(heads[2 * pr], heads[2 * pr + 1]).astype(o_ref.dtype)


def _band_call(qc, kcT_pad, vc_pad, bias_tab):
    s = qc.shape[0]
    return pl.pallas_call(
        _band_kernel,
        out_shape=jax.ShapeDtypeStruct((s, W_BRANCH), BF16),
        grid=(s // C_TQ,),
        in_specs=[pl.BlockSpec((C_TQ, 256), lambda i: (i, 0)),
                  pl.BlockSpec(memory_space=pltpu.VMEM),
                  pl.BlockSpec(memory_space=pltpu.VMEM),
                  _whole((H_C, C_TQ, C_WIN))],
        out_specs=pl.BlockSpec((C_TQ, 256), lambda i: (i, 0)),
        compiler_params=_cparams(("parallel",)),
        name="band_attention",
    )(qc, kcT_pad, vc_pad, bias_tab)


B_NCH = 2
B_L = B_NCH * CHUNK
HIGHEST = lax.Precision.HIGHEST


def _mlstm_kernel(qk_ref, vb_ref, ob_ref, misc_ref, cw_ref, cb_ref, gb_ref, ng_ref, o_ref,
                  tail_ref, ct_ref, n_ref, m_ref):
    L = CHUNK
    T = B_L
    hd = HEAD_DIM

    @pl.when(pl.program_id(0) == 0)
    def _():
        tail_ref[...] = jnp.zeros(tail_ref.shape, F32)
        ct_ref[...] = jnp.zeros(ct_ref.shape, F32)
        n_ref[...] = jnp.zeros(n_ref.shape, F32)
        m_ref[...] = jnp.zeros(m_ref.shape, F32)

    x = qk_ref[...]
    xx = jnp.concatenate([tail_ref[...], x], axis=0)
    y = jnp.broadcast_to(cb_ref[...], x.shape)
    for j in range(CONV_K):
        y = y + cw_ref[j:j + 1, :] * xx[8 - (CONV_K - 1) + j:8 - (CONV_K - 1) + j + T, :]
    tail_ref[...] = x[T - 8:T, :]
    qk = y * jax.nn.sigmoid(y)
    q_all = qk[:, :W_BRANCH]
    k_all = qk[:, W_BRANCH:] * 0.125
    v_all = vb_ref[...]
    o_gate = jax.nn.sigmoid(ob_ref[...])

    gts = misc_ref[...] + gb_ref[...]
    lf = jnp.minimum(gts, 0.0) - jnp.log1p(jnp.exp(-jnp.abs(gts)))
    r_t = lax.broadcasted_iota(I32, (T, T), 0)
    c_t = lax.broadcasted_iota(I32, (T, T), 1)
    ltri = jnp.where((c_t <= r_t) & (c_t // L == r_t // L), 1.0, 0.0)
    cum = jnp.dot(ltri, lf, precision=HIGHEST, preferred_element_type=F32)
    lane = lax.broadcasted_iota(I32, (T, LANES), 1)
    mixed = jnp.where(lane < MISC_FB, gts, cum)
    sel_r = lax.broadcasted_iota(I32, (8, LANES), 0)
    sel_c = lax.broadcasted_iota(I32, (8, LANES), 1)
    sel = jnp.where(sel_c == sel_r + MISC_IB, 1.0, 0.0)
    rows = lax.dot_general(sel, mixed, (((1,), (1,)), ((), ())), precision=HIGHEST,
                           preferred_element_type=F32)
    causal = lax.broadcasted_iota(I32, (L, L), 1) <= lax.broadcasted_iota(I32, (L, L), 0)

    heads = range(H_B)
    sl = lambda a, t0, h: a[t0:t0 + L, h * hd:(h + 1) * hd]
    pre = []
    for ci in range(B_NCH):
        t0 = ci * L
        per_head = []
        for h in heads:
            q, k, v = sl(q_all, t0, h), sl(k_all, t0, h), sl(v_all, t0, h)
            qb, kb, vbf = q.astype(BF16), k.astype(BF16), v.astype(BF16)
            qkt = lax.dot_general(qb, kb, (((1,), (1,)), ((), ())), preferred_element_type=F32)
            it_r = rows[h:h + 1, t0:t0 + L]
            cum_r = rows[H_B + h:H_B + h + 1, t0:t0 + L]
            it_c = gts[t0:t0 + L, MISC_IB + h:MISC_IB + h + 1]
            cum_c = cum[t0:t0 + L, MISC_FB + h:MISC_FB + h + 1]
            dmat = jnp.where(causal, cum_c - cum_r + it_r, NEG_INF)
            dmax = jnp.max(dmat, axis=1, keepdims=True)
            cum_end = cum_c[L - 1:L, :]
            g = cum_end - cum_c + it_c
            gmax = jnp.max(g, axis=0, keepdims=True)
            per_head.append((q, k, qb, vbf, qkt, dmat, dmax, cum_c, cum_end, g, gmax))
        pre.append(per_head)

    state = [(ct_ref[h], n_ref[h], m_ref[h]) for h in heads]
    out_chunks = []
    for ci in range(B_NCH):
        t0 = ci * L
        qc = [jnp.dot(pre[ci][h][2], state[h][0].astype(BF16), preferred_element_type=F32) for h in heads]
        mid = []
        for h in heads:
            q, k, qb, vbf, qkt, dmat, dmax, cum_c, cum_end, g, gmax = pre[ci][h]
            ct, n_row, m_prev = state[h]
            m_inter = cum_c + m_prev
            m_t = jnp.maximum(m_inter, dmax)
            w = jnp.exp(dmat - m_t) * qkt
            inter = jnp.exp(m_inter - m_t)
            m_new = jnp.maximum(cum_end + m_prev, gmax)
            carry_scale = jnp.exp(cum_end + m_prev - m_new)
            src_k = jnp.exp(g - m_new) * k
            mid.append((w, inter, m_t, m_new, carry_scale, src_k))
        wv = [jnp.dot(mid[h][0].astype(BF16), pre[ci][h][3], preferred_element_type=F32) for h in heads]
        upd = [lax.dot_general(mid[h][5].astype(BF16), pre[ci][h][3], (((0,), (0,)), ((), ())),
                               preferred_element_type=F32) for h in heads]
        outs = []
        for h in heads:
            q = pre[ci][h][0]
            w, inter, m_t, m_new, carry_scale, src_k = mid[h]
            ct, n_row, _ = state[h]
            num = inter * qc[h] + wv[h]
            den = inter * jnp.sum(q * n_row, axis=1, keepdims=True) + jnp.sum(w, axis=1, keepdims=True)
            h_t = num / jnp.maximum(jnp.abs(den), jnp.exp(-m_t))
            state[h] = (carry_scale * ct + upd[h],
                        carry_scale * n_row + jnp.sum(src_k, axis=0, keepdims=True), m_new)
            ms = jnp.mean(h_t * h_t, axis=1, keepdims=True)
            hn = h_t * lax.rsqrt(ms + EPS) * ng_ref[:, h * hd:(h + 1) * hd]
            outs.append(sl(o_gate, t0, h) * hn)
        out_chunks.append(jnp.concatenate(outs, axis=1))
    for h in heads:
        ct_ref[h], n_ref[h], m_ref[h] = state[h]
    o_ref[...] = jnp.concatenate(out_chunks, axis=0).astype(o_ref.dtype)


def _mlstm_call(qkb, vb, ob, misc, conv_w, conv_b, i_bias, f_bias, norm_g):
    s = qkb.shape[0]
    gbias = jnp.zeros((1, LANES), F32)
    gbias = gbias.at[0, MISC_IB:MISC_IB + H_B].set(i_bias).at[0, MISC_FB:MISC_FB + H_B].set(f_bias)
    row = lambda w: pl.BlockSpec((B_L, w), lambda i: (i, 0))
    return pl.pallas_call(
        _mlstm_kernel,
        out_shape=jax.ShapeDtypeStruct((s, W_BRANCH), BF16),
        grid=(s // B_L,),
        in_specs=[row(512), row(256), row(256), row(128),
                  _whole((CONV_K, 512)), _whole((1, 512)), _whole((1, LANES)), _whole((1, 256))],
        out_specs=row(256),
        scratch_shapes=[pltpu.VMEM((8, 512), F32),
                        pltpu.VMEM((H_B, HEAD_DIM, HEAD_DIM), F32),
                        pltpu.VMEM((H_B, 1, HEAD_DIM), F32),
                        pltpu.VMEM((H_B, 1, 1), F32)],
        compiler_params=_cparams(("arbitrary",)),
        name="mlstm_mixer",
    )(qkb, vb, ob, misc, conv_w, conv_b.reshape(1, 512), gbias, norm_g.reshape(1, 256))


M_TM = 256
ROUTE_LOGIT0 = N_GROUPS
RT_E1, RT_E2, RT_G1, RT_G2 = 0, 1, 2, 3


def _lane_argmax(vals, lane):
    v = jnp.max(vals, axis=1, keepdims=True)
    idx = jnp.min(jnp.where(vals == v, lane, float(LANES)), axis=1, keepdims=True)
    return v, idx


def _merge_kernel(x_ref, ya_ref, yb_ref, yc_ref, yd_ref, gm_ref, wg_ref, wb_ref, wo_ref, gf_ref,
                  wr_ref, rb_ref, xo_ref, h2_ref, rt_ref):
    x = x_ref[...]
    h = _rms(x, gm_ref[...]).astype(BF16)
    mixed = jnp.zeros(x.shape, F32)
    for n, y_ref in enumerate((ya_ref, yb_ref, yc_ref, yd_ref)):
        gate = jax.nn.sigmoid(jnp.dot(h, wg_ref[:, n * D_MODEL:(n + 1) * D_MODEL],
                                      preferred_element_type=F32))
        up = jnp.dot(y_ref[...], wb_ref[n], preferred_element_type=F32)
        mixed = mixed + gate * up
    xn = x + jnp.dot(mixed.astype(BF16), wo_ref[...], preferred_element_type=F32)
    xo_ref[...] = xn
    h2 = _rms(xn, gf_ref[...])
    h2_ref[...] = h2
    logits = jnp.dot(h2.astype(BF16), wr_ref[...], preferred_element_type=F32) + rb_ref[...]

    lane = lax.broadcasted_iota(I32, logits.shape, 1).astype(F32)
    neg = -jnp.inf
    gmask = lane < N_GROUPS
    gmax, g_sel = _lane_argmax(jnp.where(gmask, logits, neg), lane)
    g_gate = 1.0 / jnp.sum(jnp.where(gmask, jnp.exp(logits - gmax), 0.0), axis=1, keepdims=True)
    e_lo = ROUTE_LOGIT0 + EXPERTS_PER_GROUP * g_sel
    el = jnp.where((lane >= e_lo) & (lane < e_lo + EXPERTS_PER_GROUP), logits, neg)
    v1, i1 = _lane_argmax(el, lane)
    v2, i2 = _lane_argmax(jnp.where(lane == i1, neg, el), lane)
    e = jnp.exp(v2 - v1)
    p1 = 1.0 / (1.0 + e)
    p2 = e / (1.0 + e)
    rt = jnp.where(lane == RT_E1, i1 - ROUTE_LOGIT0,
                   jnp.where(lane == RT_E2, i2 - ROUTE_LOGIT0,
                             jnp.where(lane == RT_G1, p1 * g_gate,
                                       jnp.where(lane == RT_G2, p2 * g_gate, 0.0))))
    rt_ref[...] = rt


def _merge_call(x2, ya, yb, yc, yd, g_mix, w_gate, w_branch, w_out, g_ffn, w_route, b_route):
    s = x2.shape[0]
    row = lambda w: pl.BlockSpec((M_TM, w), lambda i: (i, 0))
    vm = pl.BlockSpec(memory_space=pltpu.VMEM)
    return pl.pallas_call(
        _merge_kernel,
        out_shape=(jax.ShapeDtypeStruct((s, D_MODEL), F32), jax.ShapeDtypeStruct((s, D_MODEL), F32),
                   jax.ShapeDtypeStruct((s, LANES), F32)),
        grid=(s // M_TM,),
        in_specs=[row(D_MODEL), row(256), row(256), row(256), row(256),
                  vm, vm, vm, vm, vm, vm, vm],
        out_specs=(row(D_MODEL), row(D_MODEL), row(LANES)),
        compiler_params=_cparams(("parallel",)),
        name="merge_route",
    )(x2, ya, yb, yc, yd, g_mix.reshape(1, D_MODEL), w_gate, w_branch, w_out,
      g_ffn.reshape(1, D_MODEL), w_route, b_route)


E_BM = 128


def _gather_rows(src_hbm, idx_ref, dst, sem, n, wait, inline=False):
    def one(r, c):
        cp = pltpu.make_async_copy(src_hbm.at[pl.ds(idx_ref[0, 0, r], 1)], dst.at[pl.ds(r, 1)], sem)
        if wait:
            cp.wait()
        else:
            cp.start()
        return c
    if inline:
        for r in range(n):
            one(r, 0)
    else:
        lax.fori_loop(0, n, one, 0, unroll=8)


def _moe_kernel(be_ref, nused_ref, tok_ref, tok_next_ref, h2_hbm, w1_ref, w3_ref, w2_ref, o_ref,
                xbuf, w1b, w3b, w2b, sem):
    i = pl.program_id(0)
    nused = nused_ref[0]
    used = i < nused
    slot = i % 2

    @pl.when((i == 0) & used)
    def _():
        _gather_rows(h2_hbm, tok_ref, xbuf.at[0], sem.at[0], E_BM, wait=False)

    @pl.when(used)
    def _():
        prev = be_ref[jnp.maximum(i - 1, 0)]

        @pl.when((i == 0) | (be_ref[i] != prev))
        def _():
            w1b[...] = w1_ref[0, 0].astype(BF16)
            w3b[...] = w3_ref[0, 0].astype(BF16)
            w2b[...] = w2_ref[0, 0].astype(BF16)

        _gather_rows(h2_hbm, tok_ref, xbuf.at[slot], sem.at[slot], E_BM, wait=True)
        _gather_rows(h2_hbm, tok_next_ref, xbuf.at[1 - slot], sem.at[1 - slot], E_BM, wait=False, inline=True)
        xb = xbuf[slot].astype(BF16)
        a = jnp.dot(xb, w1b[...], preferred_element_type=F32)
        g = jnp.dot(xb, w3b[...], preferred_element_type=F32)
        hid = (a * jax.nn.sigmoid(a) * g).astype(BF16)
        o_ref[...] = jnp.dot(hid, w2b[...], preferred_element_type=F32)

    @pl.when(jnp.logical_not(used))
    def _():
        @pl.when(i == nused)
        def _():
            _gather_rows(h2_hbm, tok_ref, xbuf.at[slot], sem.at[slot], E_BM, wait=True)
        o_ref[...] = jnp.zeros(o_ref.shape, F32)


def _moe_call(h2, row_tok, block_e, nused, w1, w3, w2, layer):
    n_rows = row_tok.shape[0]
    n_blocks = n_rows // E_BM
    wspec = lambda shp: pl.BlockSpec((1, 1) + shp, lambda i, be, nu: (layer, be[i], 0, 0))
    grid_spec = pltpu.PrefetchScalarGridSpec(
        num_scalar_prefetch=2,
        grid=(n_blocks,),
        in_specs=[pl.BlockSpec((1, 1, E_BM), lambda i, be, nu: (i, 0, 0), memory_space=pltpu.SMEM),
                  pl.BlockSpec((1, 1, E_BM), lambda i, be, nu: (jnp.minimum(i + 1, n_blocks - 1), 0, 0),
                               memory_space=pltpu.SMEM),
                  pl.BlockSpec(memory_space=pl.ANY),
                  wspec((D_MODEL, F_EXPERT)), wspec((D_MODEL, F_EXPERT)), wspec((F_EXPERT, D_MODEL))],
        out_specs=pl.BlockSpec((E_BM, D_MODEL), lambda i, be, nu: (i, 0)),
        scratch_shapes=[pltpu.VMEM((2, E_BM, D_MODEL), F32),
                        pltpu.VMEM((D_MODEL, F_EXPERT), BF16),
                        pltpu.VMEM((D_MODEL, F_EXPERT), BF16),
                        pltpu.VMEM((F_EXPERT, D_MODEL), BF16),
                        pltpu.SemaphoreType.DMA((2,))],
    )
    tok3 = row_tok.reshape(n_blocks, 1, E_BM)
    return pl.pallas_call(
        _moe_kernel,
        out_shape=jax.ShapeDtypeStruct((n_rows, D_MODEL), F32),
        grid_spec=grid_spec,
        compiler_params=_cparams(("arbitrary",)),
        name="moe_experts",
    )(block_e, nused, tok3, tok3, h2, w1, w3, w2)


CB_TM = 128


def _combine_kernel(final, pos_ref, pos_next_ref, x_ref, rt_ref, ys_hbm, gfin_ref, o_ref, buf, sem):
    i = pl.program_id(0)
    slot = i % 2
    n = 2 * CB_TM

    @pl.when(i == 0)
    def _():
        _gather_rows(ys_hbm, pos_ref, buf.at[0], sem.at[0], n, wait=False)

    @pl.when(i + 1 < pl.num_programs(0))
    def _():
        _gather_rows(ys_hbm, pos_next_ref, buf.at[1 - slot], sem.at[1 - slot], n, wait=False)
    _gather_rows(ys_hbm, pos_ref, buf.at[slot], sem.at[slot], n, wait=True)

    rt = rt_ref[...]
    y = (buf[slot, 0:CB_TM, :] * rt[:, RT_G1:RT_G1 + 1]
         + buf[slot, CB_TM:2 * CB_TM, :] * rt[:, RT_G2:RT_G2 + 1])
    xn = x_ref[...] + y
    if final:
        xn = _rms(xn, gfin_ref[...])
    o_ref[...] = xn


def _combine_call(x2, rt, ys, pos, g_final, final):
    s = x2.shape[0]
    nb = s // CB_TM
    pos_b = pos.reshape(nb, CB_TM, 2).transpose(0, 2, 1).reshape(nb, 1, 2 * CB_TM)
    return pl.pallas_call(
        functools.partial(_combine_kernel, final),
        out_shape=jax.ShapeDtypeStruct((s, D_MODEL), F32),
        grid=(nb,),
        in_specs=[pl.BlockSpec((1, 1, 2 * CB_TM), lambda i: (i, 0, 0), memory_space=pltpu.SMEM),
                  pl.BlockSpec((1, 1, 2 * CB_TM), lambda i: (jnp.minimum(i + 1, nb - 1), 0, 0),
                               memory_space=pltpu.SMEM),
                  pl.BlockSpec((CB_TM, D_MODEL), lambda i: (i, 0)),
                  pl.BlockSpec((CB_TM, LANES), lambda i: (i, 0)),
                  pl.BlockSpec(memory_space=pl.ANY),
                  _whole((1, D_MODEL))],
        out_specs=pl.BlockSpec((CB_TM, D_MODEL), lambda i: (i, 0)),
        scratch_shapes=[pltpu.VMEM((2, 2 * CB_TM, D_MODEL), F32), pltpu.SemaphoreType.DMA((2,))],
        compiler_params=_cparams(("arbitrary",)),
        name="moe_combine",
    )(pos_b, pos_b, x2, rt, ys, g_final.reshape(1, D_MODEL))


PLAN_BLOCK = 256


def _dispatch_plan(rt):
    t = rt.shape[0]
    n_assign = t * TOP_K_INNER
    eid = rt[:, RT_E1:RT_E2 + 1].astype(I32).reshape(n_assign)
    onehot = (eid[:, None] == jnp.arange(N_EXPERTS, dtype=I32)[None, :]).astype(F32)
    pb = PLAN_BLOCK
    oh3 = onehot.reshape(n_assign // pb, pb, N_EXPERTS)
    tri = jnp.tril(jnp.ones((pb, pb), F32), k=-1)
    within = jnp.einsum('ij,bjk->bik', tri, oh3)
    block_tot = jnp.sum(oh3, axis=1)
    block_off = jnp.cumsum(block_tot, axis=0) - block_tot
    counts = (block_off[-1] + block_tot[-1]).astype(I32)
    rank = jnp.sum(oh3 * (within + block_off[:, None, :]), axis=-1).reshape(n_assign)
    padded = (counts + E_BM - 1) // E_BM * E_BM
    ends_pad = jnp.cumsum(padded)
    starts_pad = ends_pad - padded
    dest = (jnp.sum(onehot * starts_pad.astype(F32)[None, :], axis=-1) + rank).astype(I32)
    n_rows = (-(-(n_assign + N_EXPERTS * (E_BM - 1)) // E_BM) + 1) * E_BM
    n_blocks = n_rows // E_BM
    tok = jnp.repeat(jnp.arange(t, dtype=I32), TOP_K_INNER)
    row_tok = jnp.zeros((n_rows,), I32).at[dest].set(tok)
    block_start = jnp.arange(n_blocks, dtype=I32) * E_BM
    block_e = jnp.minimum(jnp.sum((ends_pad[None, :] <= block_start[:, None]).astype(I32), axis=1),
                          N_EXPERTS - 1)
    nused = (ends_pad[-1] // E_BM).astype(I32).reshape(1)
    return row_tok, block_e, nused, dest.reshape(t, TOP_K_INNER)


def _layer(x2, layer_idx, p, experts, final_g, final):
    (norm_mix_g, w_in, conv_w, conv_b, i_bias, f_bias, mnorm_g, rel_bias, lam_p, dnorm_g,
     w_branch, w_gate, w_out, norm_ffn_g, rgw, rgb, rew, reb) = p
    o = _k1_call(x2, norm_mix_g, *_rearrange_w_in(w_in))

    ya = _dsa_call(o["qaT"], o["qiT"], o["wiT"], o["ka"], o["vaT"], o["misc_bf"])
    yb = _mlstm_call(o["qkb"], o["vb"], o["ob"], o["misc"], conv_w, conv_b, i_bias, f_bias, mnorm_g)
    kcT_pad = jnp.pad(o["kcT"], ((0, 0), (C_PAD, 0)))
    vc_pad = jnp.pad(o["vc"], ((C_PAD, 0), (0, 0)))
    yc = _band_call(o["qc"], kcT_pad, vc_pad, _band_bias_table(rel_bias))
    lam_init = 0.8 - 0.6 * math.exp(-0.3 * layer_idx)
    yd = _diff_call(o["qdT"], o["kd"], o["vdT"], lam_p, dnorm_g, lam_init)

    w_route = jnp.concatenate([rgw, rew, jnp.zeros((D_MODEL, LANES - N_GROUPS - N_EXPERTS), F32)],
                              axis=1).astype(BF16)
    b_route = jnp.concatenate([rgb, reb, jnp.zeros((LANES - N_GROUPS - N_EXPERTS,), F32)]).reshape(1, LANES)
    xn, h2, rt = _merge_call(x2, ya, yb, yc, yd, norm_mix_g, w_gate.astype(BF16), w_branch.astype(BF16),
                             w_out.astype(BF16), norm_ffn_g, w_route, b_route)
    row_tok, block_e, nused, pos = _dispatch_plan(rt)
    ys = _moe_call(h2, row_tok, block_e, nused, *experts, layer_idx)
    return _combine_call(xn, rt, ys, pos, final_g, final)


def kernel(x, norm_mix_g, w_in, conv_w, conv_b, mlstm_i_bias, mlstm_f_bias, mlstm_norm_g, relpos_bias, diff_lambda, diff_norm_g, w_branch, w_gate, w_out, norm_ffn_g, router_group_w, router_group_b, router_expert_w, router_expert_b, expert_w1, expert_w3, expert_w2, final_norm_g):
    assert x.shape[0] == 1 and x.shape[2] == D_MODEL
    params = (norm_mix_g, w_in, conv_w, conv_b, mlstm_i_bias, mlstm_f_bias, mlstm_norm_g, relpos_bias,
              diff_lambda, diff_norm_g, w_branch, w_gate, w_out, norm_ffn_g, router_group_w,
              router_group_b, router_expert_w, router_expert_b)
    experts = (expert_w1, expert_w3, expert_w2)
    depth = norm_mix_g.shape[0]
    x2 = x[0]
    for l in range(depth):
        x2 = _layer(x2, l, tuple(a[l] for a in params), experts, final_norm_g, l == depth - 1)
    return x2[None]
```

```python
import functools
import math

import jax
import jax.numpy as jnp
from jax import lax
from jax.experimental import pallas as pl
from jax.experimental.pallas import tpu as pltpu

F32 = jnp.float32
BF16 = jnp.bfloat16
I32 = jnp.int32

D_MODEL = 1024
CHUNK = 64
HEAD_DIM = 64
NEG_INF = -1e30
H_A = 4
H_IDX = 4
D_IDX = 32
TOPK_MAX = 256
H_B = 4
CONV_K = 4
H_C = 4
N_PREV_CHUNKS = 8
MAX_REL_PAST = 128
H_D = 4
DQ_D = 32
W_BRANCH = 256
N_BRANCH = 4
N_GROUPS = 4
EXPERTS_PER_GROUP = 8
N_EXPERTS = 32
TOP_K_INNER = 2
F_EXPERT = 512
EPS = 1e-6

VMEM_LIMIT_BYTES = 52 * 1024 * 1024
LANES = 128

INT_MIN = -(2 ** 31)
I16_MIN = -(2 ** 15)
I16 = jnp.int16
M_INIT = -5e29

_COL_SIZES = (256, 256, 256, 128, 32, 4, 256, 256, 256, 4, 4, 256, 256, 256, 256, 256, 256, 256)
_COL_NAMES = ("qa", "ka", "va", "qi", "ki", "wi", "qb", "kb", "vb", "ib", "fb", "ob",
              "qc", "kc", "vc", "qd", "kd", "vd")
_COL_OFF = {}
_o = 0
for _n, _s in zip(_COL_NAMES, _COL_SIZES):
    _COL_OFF[_n] = (_o, _s)
    _o += _s
C_IN = _o
MISC_KI = 0
MISC_WI = 32
MISC_IB = 36
MISC_FB = 40


def _cparams(sem):
    return pltpu.CompilerParams(dimension_semantics=sem, vmem_limit_bytes=VMEM_LIMIT_BYTES)


def _whole(shape):
    nd = len(shape)
    return pl.BlockSpec(shape, lambda *_: (0,) * nd)


_K1_NAT = (("ka", 256, BF16), ("misc", 128, F32), ("qkb", 512, F32), ("vb", 256, F32), ("ob", 256, F32),
           ("qc", 256, BF16), ("vc", 256, BF16), ("kd", 256, BF16))
_K1_TR = (("qaT", 256, BF16), ("qiT", 128, BF16), ("vaT", 256, BF16), ("kcT", 256, BF16),
          ("qdT", 256, BF16), ("vdT", 256, BF16), ("wiT", 8, F32))
K1_NAT_WIDTH = sum(w for _, w, _ in _K1_NAT)
K1_TR_WIDTH = sum(w for _, w, _ in _K1_TR)
K1_TM = 512
V_AUG = 80
_K1_VAUG = ("vaT", "vdT")


def _k1_tr_rows(name, width):
    return (width // HEAD_DIM) * V_AUG if name in _K1_VAUG else width


def _rearrange_w_in(w_in):
    def cols(name):
        o, s = _COL_OFF[name]
        return w_in[:, o:o + s]
    d = w_in.shape[0]
    misc = jnp.concatenate([cols("ki"), cols("wi"), cols("ib"), cols("fb"),
                            jnp.zeros((d, LANES - 44), w_in.dtype)], axis=1)
    nat = [cols("ka"), misc, cols("qb"), cols("kb"), cols("vb"), cols("ob"), cols("qc"), cols("vc"),
           cols("kd")]
    tr = [cols("qa"), cols("qi"), cols("va"), cols("kc"), cols("qd"), cols("vd"), cols("wi"),
          jnp.zeros((d, 8 - H_IDX), w_in.dtype)]
    return (jnp.concatenate(nat, axis=1).astype(BF16), jnp.concatenate(tr, axis=1).T.astype(BF16))


def _rms(x, g):
    ms = jnp.mean(x * x, axis=-1, keepdims=True)
    return x * lax.rsqrt(ms + EPS) * g


def _k1_kernel(x_ref, g_ref, wn_ref, wt_ref, *out_refs):
    h = _rms(x_ref[...], g_ref[...]).astype(BF16)
    refs = dict(zip([n for n, _, _ in _K1_NAT] + ["misc_bf"] + [n for n, _, _ in _K1_TR], out_refs))
    off = 0
    for name, width, _ in _K1_NAT:
        r = jnp.dot(h, wn_ref[:, off:off + width], preferred_element_type=F32)
        refs[name][...] = r.astype(refs[name].dtype)
        if name == "misc":
            refs["misc_bf"][...] = r.astype(BF16)
        off += width
    off = 0
    for name, width, _ in _K1_TR:
        r = lax.dot_general(wt_ref[off:off + width, :], h, (((1,), (1,)), ((), ())),
                            preferred_element_type=F32)
        if name in _K1_VAUG:
            tail = (lax.broadcasted_iota(I32, (V_AUG - HEAD_DIM, r.shape[1]), 0) == 0)
            tail = jnp.where(tail, 1.0, 0.0).astype(BF16)
            for hh in range(width // HEAD_DIM):
                refs[name][hh * V_AUG:hh * V_AUG + HEAD_DIM, :] = \
                    r[hh * HEAD_DIM:(hh + 1) * HEAD_DIM, :].astype(BF16)
                refs[name][hh * V_AUG + HEAD_DIM:(hh + 1) * V_AUG, :] = tail
        else:
            refs[name][...] = r.astype(refs[name].dtype)
        off += width


def _k1_call(x2, g, w_nat, w_trT):
    s = x2.shape[0]
    tm = K1_TM
    outs = (tuple(jax.ShapeDtypeStruct((s, w), dt) for _, w, dt in _K1_NAT)
            + (jax.ShapeDtypeStruct((s, LANES), BF16),)
            + tuple(jax.ShapeDtypeStruct((_k1_tr_rows(n, w), s), dt) for n, w, dt in _K1_TR))
    out_specs = (tuple(pl.BlockSpec((tm, w), lambda i: (i, 0)) for _, w, _ in _K1_NAT)
                 + (pl.BlockSpec((tm, LANES), lambda i: (i, 0)),)
                 + tuple(pl.BlockSpec((_k1_tr_rows(n, w), tm), lambda i: (0, i)) for n, w, _ in _K1_TR))
    res = pl.pallas_call(
        _k1_kernel,
        out_shape=outs,
        grid=(s // tm,),
        in_specs=[pl.BlockSpec((tm, D_MODEL), lambda i: (i, 0)),
                  _whole((1, D_MODEL)),
                  _whole((D_MODEL, K1_NAT_WIDTH)),
                  _whole((K1_TR_WIDTH, D_MODEL))],
        out_specs=out_specs,
        compiler_params=_cparams(("parallel",)),
        name="k1_norm_proj",
    )(x2, g.reshape(1, D_MODEL), w_nat, w_trT)
    return dict(zip([n for n, _, _ in _K1_NAT] + ["misc_bf"] + [n for n, _, _ in _K1_TR], res))


A_TQ = 256
A_KT = 512
A_SLAB = 256
A_SWEEP_SLABS = 4
LOG2E = 1.4426950408889634
QK_AHEAD = 4


def _flash_step(s, i, v_t, c, m_ref, acc_ref):
    m_old = m_ref[i]
    m_new = jnp.maximum(m_old, jnp.max(s, axis=0, keepdims=True))
    alpha = jnp.exp2((m_old - m_new) * c)
    p = jnp.exp2((s - m_new) * c)
    acc_ref[i] = alpha * acc_ref[i] + jnp.dot(v_t, p.astype(BF16), preferred_element_type=F32)
    m_ref[i] = m_new


def _flash_result(acc_ref, i):
    return acc_ref[i, 0:HEAD_DIM, :] / acc_ref[i, HEAD_DIM:HEAD_DIM + 1, :]


def _qk_prologue(k_t, qpad_ref, n, s_ref):
    for i in range(min(QK_AHEAD, n)):
        s_ref[i] = jnp.dot(k_t, qpad_ref[i], preferred_element_type=F32)


def _flash_tile(k_t, qpad_ref, n, v_tile, mask, c, m_ref, acc_ref, s_ref=None, k_next=None):
    a = min(QK_AHEAD, n)
    if s_ref is None:
        pend = [jnp.dot(k_t, qpad_ref[i], preferred_element_type=F32) for i in range(a)]
    else:
        pend = [s_ref[i] for i in range(a)]
    for i in range(n):
        s = pend.pop(0)
        if i + a < n:
            pend.append(jnp.dot(k_t, qpad_ref[i + a], preferred_element_type=F32))
        elif s_ref is not None:
            s_ref[i + a - n] = jnp.dot(k_next, qpad_ref[i + a - n], preferred_element_type=F32)
        if mask is not None:
            s = jnp.where(mask, s, NEG_INF)
        _flash_step(s, i, v_tile(i), c, m_ref, acc_ref)


def _bit_transpose32(words):
    a = list(words)
    j, m = 16, 0x0000FFFF
    while j:
        k = 0
        while k < 32:
            t = (a[k] ^ lax.shift_right_logical(a[k + j], j)) & m
            a[k] = a[k] ^ t
            a[k + j] = a[k + j] ^ lax.shift_left(t, j)
            k = (k + j + 1) & ~j
        j >>= 1
        m = (m ^ (m << j)) & 0xFFFFFFFF
    return a


def _dsa_kernel(topk, qaT_ref, qiT_ref, wiT_ref, ka_ref, vaT_ref, mb_ref, tri_ref, o_ref,
                planes_ref, cand_ref, above_ref, qpad_ref, qipad_ref, acc_ref, m_ref, carry_ref):
    tq, kt = A_TQ, A_KT
    b = pl.program_id(0)
    ntiles = ((b + 1) * tq + kt - 1) // kt
    q_pos = b * tq + lax.broadcasted_iota(I32, (1, tq), 1)
    vis_end = (q_pos // CHUNK + 1) * CHUNK

    qiT = qiT_ref[...]
    for h in range(H_IDX):
        qipad_ref[h, 0:D_IDX, :] = qiT[h * D_IDX:(h + 1) * D_IDX, :]
        qipad_ref[h, D_IDX:LANES, :] = jnp.zeros((LANES - D_IDX, tq), BF16)

    def p1(j, carry):
        s0 = pl.multiple_of(j * kt, kt)
        mb = mb_ref[pl.ds(s0, kt), :]
        score = jnp.zeros((kt, tq), F32)
        for h in range(H_IDX):
            r = jnp.dot(mb, qipad_ref[h], preferred_element_type=F32)
            score = score + jnp.maximum(r, 0.0) * wiT_ref[h:h + 1, :]
        bits = lax.bitcast_convert_type(score, I32)
        ukey = bits ^ (lax.shift_right_arithmetic(bits, 31) | INT_MIN)
        s_pos = s0 + lax.broadcasted_iota(I32, (kt, 1), 0)
        ukey = jnp.where(s_pos < vis_end, ukey, 0)
        u4 = ukey.reshape(kt // A_SLAB, 32, 8, tq)
        for s2 in range(kt // A_SLAB):
            planes = _bit_transpose32([u4[s2, v] for v in range(32)])
            for r in range(32):
                planes_ref[j * (kt // A_SLAB) + s2, r] = planes[r]
        return carry
    lax.fori_loop(0, ntiles, p1, 0)

    kf = float(topk)
    nslab = ntiles * (kt // A_SLAB)
    nstep = (nslab + A_SWEEP_SLABS - 1) // A_SWEEP_SLABS

    def init_sets(sl, carry):
        cand_ref[sl] = jnp.full((8, tq), -1, I32)
        above_ref[sl] = jnp.zeros((8, tq), I32)
        return carry
    lax.fori_loop(0, nslab, init_sets, 0)

    def init_pad(sl, carry):
        cand_ref[sl] = jnp.zeros((8, tq), I32)
        above_ref[sl] = jnp.zeros((8, tq), I32)
        planes_ref[sl] = jnp.zeros((32, 8, tq), I32)
        return carry
    lax.fori_loop(nslab, nstep * A_SWEEP_SLABS, init_pad, 0)

    def apply_decision(sl, prev_plane, took_one):
        cand = cand_ref[sl]
        ones = cand & prev_plane
        cand = jnp.where(took_one, ones, cand ^ ones)
        above_ref[sl] = jnp.where(took_one, above_ref[sl], above_ref[sl] | ones)
        cand_ref[sl] = cand
        return cand

    def sweep(i, carry):
        n_above, took, tau = carry
        took_one = jnp.broadcast_to(took, (8, tq)) != 0
        first = i == 0

        def step(jj, acc):
            for s2 in range(A_SWEEP_SLABS):
                sl = jj * A_SWEEP_SLABS + s2
                prev_plane = jnp.where(first, -1, planes_ref[sl, jnp.maximum(i - 1, 0)])
                cand = apply_decision(sl, prev_plane, took_one)
                acc = acc + lax.population_count(cand & planes_ref[sl, i])
            return acc
        acc = lax.fori_loop(0, nstep, step, jnp.zeros((8, tq), I32))
        n_one = jnp.sum(acc.astype(F32), axis=0, keepdims=True)
        take = (n_above + n_one) >= kf
        n_above = jnp.where(take, n_above, n_above + n_one)
        tau = jnp.where(take, tau | lax.shift_left(jnp.int32(1), 31 - i), tau)
        return n_above, jnp.where(take, 1, 0), tau
    n_above, took, tau = lax.fori_loop(
        0, 32, sweep, (jnp.zeros((1, tq), F32), jnp.ones((1, tq), I32), jnp.zeros((1, tq), I32)))

    def last_decision(sl, carry):
        apply_decision(sl, planes_ref[sl, 31], jnp.broadcast_to(took, (8, tq)) != 0)
        return carry
    lax.fori_loop(0, nslab, last_decision, 0)
    need = jnp.where(tau == 0, 0.0, kf - n_above)

    m_ref[...] = jnp.full(m_ref.shape, M_INIT, F32)
    acc_ref[...] = jnp.zeros(acc_ref.shape, F32)
    carry_ref[...] = jnp.zeros(carry_ref.shape, F32)
    qaT = qaT_ref[...] * 0.125
    row = lax.broadcasted_iota(I32, qaT.shape, 0)
    for h in range(H_A):
        qpad_ref[h] = jnp.where((row >= h * HEAD_DIM) & (row < (h + 1) * HEAD_DIM), qaT,
                                jnp.zeros_like(qaT))

    def slab_rows(ref, sl):
        word = ref[sl]
        return jnp.concatenate([lax.shift_right_logical(word, 31 - v) & 1 for v in range(32)], axis=0)

    def p3(j, carry):
        s0 = pl.multiple_of(j * kt, kt)
        slabs = [j * (kt // A_SLAB) + s2 for s2 in range(kt // A_SLAB)]
        gt = jnp.concatenate([slab_rows(above_ref, sl) for sl in slabs], axis=0) != 0
        eq_i = jnp.concatenate([slab_rows(cand_ref, sl) for sl in slabs], axis=0)
        eq = eq_i != 0
        eqf = eq_i.astype(F32)
        pref = jnp.dot(tri_ref[...], eqf.astype(BF16), preferred_element_type=F32)
        seen = carry_ref[...]
        sel = gt | (eq & (pref + seen < need))
        carry_ref[...] = seen + pref[kt - 1:kt, :] + eqf[kt - 1:kt, :]
        k_t = ka_ref[pl.ds(s0, kt), :]
        _flash_tile(k_t, qpad_ref, H_A, lambda i: vaT_ref[i * V_AUG:(i + 1) * V_AUG, pl.ds(s0, kt)],
                    sel, LOG2E, m_ref, acc_ref)
        return carry
    lax.fori_loop(0, ntiles, p3, 0)

    ys = [_flash_result(acc_ref, h) for h in range(H_A)]
    o_ref[...] = jnp.concatenate(ys, axis=0).T.astype(o_ref.dtype)


def _dsa_call(qaT, qiT, wiT, ka, vaT, misc_bf):
    s = ka.shape[0]
    topk = min(TOPK_MAX, s // 4)
    tri = jnp.tril(jnp.ones((A_KT, A_KT), F32), k=-1).astype(BF16)
    vm = pl.BlockSpec(memory_space=pltpu.VMEM)
    return pl.pallas_call(
        functools.partial(_dsa_kernel, topk),
        out_shape=jax.ShapeDtypeStruct((s, W_BRANCH), BF16),
        grid=(s // A_TQ,),
        in_specs=[pl.BlockSpec((256, A_TQ), lambda i: (0, i)),
                  pl.BlockSpec((LANES, A_TQ), lambda i: (0, i)),
                  pl.BlockSpec((8, A_TQ), lambda i: (0, i)),
                  vm, vm, vm, vm],
        out_specs=pl.BlockSpec((A_TQ, 256), lambda i: (i, 0)),
        scratch_shapes=[pltpu.VMEM((s // A_SLAB, 32, 8, A_TQ), I32),
                        pltpu.VMEM((s // A_SLAB, 8, A_TQ), I32),
                        pltpu.VMEM((s // A_SLAB, 8, A_TQ), I32),
                        pltpu.VMEM((H_A, 256, A_TQ), BF16),
                        pltpu.VMEM((H_IDX, LANES, A_TQ), BF16),
                        pltpu.VMEM((H_A, V_AUG, A_TQ), F32),
                        pltpu.VMEM((H_A, 1, A_TQ), F32),
                        pltpu.VMEM((1, A_TQ), F32)],
        compiler_params=_cparams(("arbitrary",)),
        name="dsa_mixer",
    )(qaT, qiT, wiT, ka, vaT, misc_bf, tri)


def _pair_select(lo, hi):
    lane = lax.broadcasted_iota(I32, lo.shape, 1)
    return jnp.where(lane < HEAD_DIM, lo, hi)


def _pair_head_rms(o, g):
    lane = lax.broadcasted_iota(I32, o.shape, 1)
    low = lane < HEAD_DIM
    sq = o * o
    ms_lo = jnp.sum(jnp.where(low, sq, 0.0), axis=1, keepdims=True) * (1.0 / HEAD_DIM)
    ms_hi = jnp.sum(jnp.where(low, 0.0, sq), axis=1, keepdims=True) * (1.0 / HEAD_DIM)
    ms = jnp.where(low, ms_lo, ms_hi)
    return o * lax.rsqrt(ms + EPS) * g


D_TQ = 256
D_KT = 1024


def _diff_kernel(lam_init, qT_ref, kd_ref, vT_ref, lam_ref, g_ref, o_ref, qpad_ref, acc_ref, m_ref, s_ref):
    tq = D_TQ
    b = pl.program_id(0)
    c = (DQ_D ** -0.5) * LOG2E
    q_pos = b * tq + lax.broadcasted_iota(I32, (1, tq), 1)
    vis_end = (q_pos // CHUNK + 1) * CHUNK
    m_ref[...] = jnp.full(m_ref.shape, M_INIT, F32)
    acc_ref[...] = jnp.zeros(acc_ref.shape, F32)
    qT = qT_ref[...]
    row = lax.broadcasted_iota(I32, qT.shape, 0)
    for i in range(2 * H_D):
        qpad_ref[i] = jnp.where((row >= i * DQ_D) & (row < (i + 1) * DQ_D), qT, jnp.zeros_like(qT))

    def tile(s0, kt, masked, s_ref=None, s0_next=None):
        k_t = kd_ref[pl.ds(s0, kt), :]
        k_next = None if s0_next is None else kd_ref[pl.ds(s0_next, kt), :]
        vis = None
        if masked:
            vis = (s0 + lax.broadcasted_iota(I32, (kt, 1), 0)) < vis_end
        _flash_tile(k_t, qpad_ref, 2 * H_D, lambda i: vT_ref[(i // 2) * V_AUG:(i // 2 + 1) * V_AUG, pl.ds(s0, kt)],
                    vis, c, m_ref, acc_ref, s_ref, k_next)

    n_big = (b * tq) // D_KT

    @pl.when(n_big > 0)
    def _():
        _qk_prologue(kd_ref[pl.ds(0, D_KT), :], qpad_ref, 2 * H_D, s_ref)

    def full_tile(j, carry):
        j_next = jnp.minimum(j + 1, n_big - 1)
        tile(pl.multiple_of(j * D_KT, D_KT), D_KT, False, s_ref, pl.multiple_of(j_next * D_KT, D_KT))
        return carry
    lax.fori_loop(0, n_big, full_tile, 0)

    def small_tile(j, carry):
        tile(pl.multiple_of(j * tq, tq), tq, False)
        return carry
    lax.fori_loop(n_big * (D_KT // tq), b, small_tile, 0)
    tile(pl.multiple_of(b * tq, tq), tq, True)

    lp = lam_ref[...]
    lam = (jnp.exp(jnp.sum(lp[0:1] * lp[1:2], axis=1, keepdims=True))
           - jnp.exp(jnp.sum(lp[2:3] * lp[3:4], axis=1, keepdims=True)) + lam_init)
    ys = []
    for h in range(H_D):
        o = _flash_result(acc_ref, 2 * h) - lam * _flash_result(acc_ref, 2 * h + 1)
        ms = jnp.mean(o * o, axis=0, keepdims=True)
        ys.append(o * lax.rsqrt(ms + EPS) * g_ref[h * HEAD_DIM:(h + 1) * HEAD_DIM, :] * (1.0 - lam_init))
    o_ref[...] = jnp.concatenate(ys, axis=0).T.astype(o_ref.dtype)


def _diff_call(qdT, kd, vdT, lam_p, dnorm_g, lam_init):
    s = kd.shape[0]
    return pl.pallas_call(
        functools.partial(_diff_kernel, lam_init),
        out_shape=jax.ShapeDtypeStruct((s, W_BRANCH), BF16),
        grid=(s // D_TQ,),
        in_specs=[pl.BlockSpec((256, D_TQ), lambda i: (0, i)),
                  pl.BlockSpec(memory_space=pltpu.VMEM),
                  pl.BlockSpec(memory_space=pltpu.VMEM),
                  _whole((4, DQ_D)),
                  _whole((256, 1))],
        out_specs=pl.BlockSpec((D_TQ, 256), lambda i: (i, 0)),
        scratch_shapes=[pltpu.VMEM((2 * H_D, 256, D_TQ), BF16),
                        pltpu.VMEM((2 * H_D, V_AUG, D_TQ), F32),
                        pltpu.VMEM((2 * H_D, 1, D_TQ), F32),
                        pltpu.VMEM((QK_AHEAD, D_KT, D_TQ), F32)],
        compiler_params=_cparams(("parallel",)),
        name="diff_attention",
    )(qdT, kd, vdT, lam_p, dnorm_g.reshape(256, 1))


C_TQ = 128
C_PAD = N_PREV_CHUNKS * CHUNK
C_WIN = C_PAD + C_TQ
C_LINE = C_TQ + C_WIN


def _band_bias_line(rel_bias):
    dist_desc = jnp.arange(C_TQ - 1 + C_PAD, C_PAD - C_WIN, -1)
    line = rel_bias[:, jnp.clip(dist_desc, -(CHUNK - 1), MAX_REL_PAST) + (CHUNK - 1)].astype(F32)
    return jnp.pad(line, ((0, 0), (0, C_LINE - line.shape[1])))


def _band_kernel(qc_ref, kcT_ref, vc_ref, line_ref, o_ref, bias_ref):
    tq = C_TQ
    b = pl.program_id(0)

    @pl.when(b == 0)
    def _():
        i = lax.broadcasted_iota(I32, (tq, C_WIN), 0)
        w = lax.broadcasted_iota(I32, (tq, C_WIN), 1)
        in_band = (w // CHUNK >= i // CHUNK) & (w // CHUNK <= i // CHUNK + N_PREV_CHUNKS)
        for h in range(H_C):
            x = jnp.broadcast_to(line_ref[h:h + 1, :], (tq, C_LINE))
            y = pltpu.roll(x, 1, 1, stride=1, stride_axis=0)
            bias_ref[h] = jnp.where(in_band, y[:, C_LINE - C_WIN:], NEG_INF)

    w0 = pl.multiple_of(b * tq, tq)
    q_all = qc_ref[...] * 0.125
    key_abs = b * tq - C_PAD + lax.broadcasted_iota(I32, (1, C_WIN), 1)
    ok = key_abs >= 0
    logits = [jnp.dot(q_all[:, h * HEAD_DIM:(h + 1) * HEAD_DIM],
                      kcT_ref[h * HEAD_DIM:(h + 1) * HEAD_DIM, pl.ds(w0, C_WIN)],
                      preferred_element_type=F32) for h in range(H_C)]
    heads = []
    for h in range(H_C):
        s = jnp.where(ok, logits[h] + bias_ref[h], NEG_INF)
        m = jnp.max(s, axis=1, keepdims=True)
        p = jnp.exp(s - m)
        l = jnp.sum(p, axis=1, keepdims=True)
        pv = jnp.dot(p.astype(BF16), vc_ref[pl.ds(w0, C_WIN), (h // 2) * LANES:(h // 2 + 1) * LANES],
                     preferred_element_type=F32)
        heads.append(pv / l)
    for pr in range(H_C // 2):
        o_ref[:, pr * LANES:(pr + 1) * LANES] = _pair_select(heads[2 * pr], heads[2 * pr + 1]).astype(o_ref.dtype)


def _band_call(qc, kcT_pad, vc_pad, bias_line):
    s = qc.shape[0]
    return pl.pallas_call(
        _band_kernel,
        out_shape=jax.ShapeDtypeStruct((s, W_BRANCH), BF16),
        grid=(s // C_TQ,),
        in_specs=[pl.BlockSpec((C_TQ, 256), lambda i: (i, 0)),
                  pl.BlockSpec(memory_space=pltpu.VMEM),
                  pl.BlockSpec(memory_space=pltpu.VMEM),
                  _whole((H_C, C_LINE))],
        out_specs=pl.BlockSpec((C_TQ, 256), lambda i: (i, 0)),
        scratch_shapes=[pltpu.VMEM((H_C, C_TQ, C_WIN), F32)],
        compiler_params=_cparams(("arbitrary",)),
        name="band_attention",
    )(qc, kcT_pad, vc_pad, bias_line)


B_NCH = 2
B_L = B_NCH * CHUNK
HIGHEST = lax.Precision.HIGHEST


def _mlstm_kernel(qk_ref, vb_ref, ob_ref, misc_ref, cw_ref, cb_ref, gb_ref, ng_ref, o_ref,
                  tail_ref, ct_ref, n_ref, m_ref):
    L = CHUNK
    T = B_L
    hd = HEAD_DIM

    @pl.when(pl.program_id(0) == 0)
    def _():
        tail_ref[...] = jnp.zeros(tail_ref.shape, F32)
        ct_ref[...] = jnp.zeros(ct_ref.shape, F32)
        n_ref[...] = jnp.zeros(n_ref.shape, F32)
        m_ref[...] = jnp.zeros(m_ref.shape, F32)

    x = qk_ref[...]
    xx = jnp.concatenate([tail_ref[...], x], axis=0)
    y = jnp.broadcast_to(cb_ref[...], x.shape)
    for j in range(CONV_K):
        y = y + cw_ref[j:j + 1, :] * xx[8 - (CONV_K - 1) + j:8 - (CONV_K - 1) + j + T, :]
    tail_ref[...] = x[T - 8:T, :]
    qk = y * jax.nn.sigmoid(y)
    q_all = qk[:, :W_BRANCH]
    k_all = qk[:, W_BRANCH:] * 0.125
    v_all = vb_ref[...]
    o_gate = jax.nn.sigmoid(ob_ref[...])

    gts = misc_ref[...] + gb_ref[...]
    lf = jnp.minimum(gts, 0.0) - jnp.log1p(jnp.exp(-jnp.abs(gts)))
    r_t = lax.broadcasted_iota(I32, (T, T), 0)
    c_t = lax.broadcasted_iota(I32, (T, T), 1)
    ltri = jnp.where((c_t <= r_t) & (c_t // L == r_t // L), 1.0, 0.0)
    cum = jnp.dot(ltri, lf, precision=HIGHEST, preferred_element_type=F32)
    lane = lax.broadcasted_iota(I32, (T, LANES), 1)
    mixed = jnp.where(lane < MISC_FB, gts, cum)
    sel_r = lax.broadcasted_iota(I32, (8, LANES), 0)
    sel_c = lax.broadcasted_iota(I32, (8, LANES), 1)
    sel = jnp.where(sel_c == sel_r + MISC_IB, 1.0, 0.0)
    rows = lax.dot_general(sel, mixed, (((1,), (1,)), ((), ())), precision=HIGHEST,
                           preferred_element_type=F32)
    causal = lax.broadcasted_iota(I32, (L, L), 1) <= lax.broadcasted_iota(I32, (L, L), 0)

    heads = range(H_B)
    sl = lambda a, t0, h: a[t0:t0 + L, h * hd:(h + 1) * hd]
    pre = []
    for ci in range(B_NCH):
        t0 = ci * L
        per_head = []
        for h in heads:
            q, k, v = sl(q_all, t0, h), sl(k_all, t0, h), sl(v_all, t0, h)
            qb, kb, vbf = q.astype(BF16), k.astype(BF16), v.astype(BF16)
            qkt = lax.dot_general(qb, kb, (((1,), (1,)), ((), ())), preferred_element_type=F32)
            it_r = rows[h:h + 1, t0:t0 + L]
            cum_r = rows[H_B + h:H_B + h + 1, t0:t0 + L]
            it_c = gts[t0:t0 + L, MISC_IB + h:MISC_IB + h + 1]
            cum_c = cum[t0:t0 + L, MISC_FB + h:MISC_FB + h + 1]
            dmat = jnp.where(causal, cum_c - cum_r + it_r, NEG_INF)
            dmax = jnp.max(dmat, axis=1, keepdims=True)
            cum_end = cum_c[L - 1:L, :]
            g = cum_end - cum_c + it_c
            gmax = jnp.max(g, axis=0, keepdims=True)
            per_head.append((q, k, qb, vbf, qkt, dmat, dmax, cum_c, cum_end, g, gmax))
        pre.append(per_head)

    state = [(ct_ref[h], n_ref[h], m_ref[h]) for h in heads]
    out_chunks = []
    for ci in range(B_NCH):
        t0 = ci * L
        qc = [jnp.dot(pre[ci][h][2], state[h][0].astype(BF16), preferred_element_type=F32) for h in heads]
        mid = []
        for h in heads:
            q, k, qb, vbf, qkt, dmat, dmax, cum_c, cum_end, g, gmax = pre[ci][h]
            ct, n_row, m_prev = state[h]
            m_inter = cum_c + m_prev
            m_t = jnp.maximum(m_inter, dmax)
            w = jnp.exp(dmat - m_t) * qkt
            inter = jnp.exp(m_inter - m_t)
            m_new = jnp.maximum(cum_end + m_prev, gmax)
            carry_scale = jnp.exp(cum_end + m_prev - m_new)
            src_k = jnp.exp(g - m_new) * k
            mid.append((w, inter, m_t, m_new, carry_scale, src_k))
        wv = [jnp.dot(mid[h][0].astype(BF16), pre[ci][h][3], preferred_element_type=F32) for h in heads]
        upd = [lax.dot_general(mid[h][5].astype(BF16), pre[ci][h][3], (((0,), (0,)), ((), ())),
                               preferred_element_type=F32) for h in heads]
        outs = []
        for h in heads:
            q = pre[ci][h][0]
            w, inter, m_t, m_new, carry_scale, src_k = mid[h]
            ct, n_row, _ = state[h]
            num = inter * qc[h] + wv[h]
            den = inter * jnp.sum(q * n_row, axis=1, keepdims=True) + jnp.sum(w, axis=1, keepdims=True)
            h_t = num / jnp.maximum(jnp.abs(den), jnp.exp(-m_t))
            state[h] = (carry_scale * ct + upd[h],
                        carry_scale * n_row + jnp.sum(src_k, axis=0, keepdims=True), m_new)
            ms = jnp.mean(h_t * h_t, axis=1, keepdims=True)
            hn = h_t * lax.rsqrt(ms + EPS) * ng_ref[:, h * hd:(h + 1) * hd]
            outs.append(sl(o_gate, t0, h) * hn)
        out_chunks.append(jnp.concatenate(outs, axis=1))
    for h in heads:
        ct_ref[h], n_ref[h], m_ref[h] = state[h]
    o_ref[...] = jnp.concatenate(out_chunks, axis=0).astype(o_ref.dtype)


def _mlstm_call(qkb, vb, ob, misc, conv_w, conv_b, i_bias, f_bias, norm_g):
    s = qkb.shape[0]
    gbias = jnp.zeros((1, LANES), F32)
    gbias = gbias.at[0, MISC_IB:MISC_IB + H_B].set(i_bias).at[0, MISC_FB:MISC_FB + H_B].set(f_bias)
    row = lambda w: pl.BlockSpec((B_L, w), lambda i: (i, 0))
    return pl.pallas_call(
        _mlstm_kernel,
        out_shape=jax.ShapeDtypeStruct((s, W_BRANCH), BF16),
        grid=(s // B_L,),
        in_specs=[row(512), row(256), row(256), row(128),
                  _whole((CONV_K, 512)), _whole((1, 512)), _whole((1, LANES)), _whole((1, 256))],
        out_specs=row(256),
        scratch_shapes=[pltpu.VMEM((8, 512), F32),
                        pltpu.VMEM((H_B, HEAD_DIM, HEAD_DIM), F32),
                        pltpu.VMEM((H_B, 1, HEAD_DIM), F32),
                        pltpu.VMEM((H_B, 1, 1), F32)],
        compiler_params=_cparams(("arbitrary",)),
        name="mlstm_mixer",
    )(qkb, vb, ob, misc, conv_w, conv_b.reshape(1, 512), gbias, norm_g.reshape(1, 256))


M_TM = 256
ROUTE_LOGIT0 = N_GROUPS
RT_E1, RT_E2, RT_G1, RT_G2 = 0, 1, 2, 3


def _lane_argmax(vals, lane):
    v = jnp.max(vals, axis=1, keepdims=True)
    idx = jnp.min(jnp.where(vals == v, lane, float(LANES)), axis=1, keepdims=True)
    return v, idx


def _merge_kernel(x_ref, ya_ref, yb_ref, yc_ref, yd_ref, gm_ref, wg_ref, wb_ref, wo_ref, gf_ref,
                  wr_ref, rb_ref, xo_ref, h2_ref, rt_ref):
    x = x_ref[...]
    h = _rms(x, gm_ref[...]).astype(BF16)
    mixed = jnp.zeros(x.shape, F32)
    for n, y_ref in enumerate((ya_ref, yb_ref, yc_ref, yd_ref)):
        gate = jax.nn.sigmoid(jnp.dot(h, wg_ref[:, n * D_MODEL:(n + 1) * D_MODEL],
                                      preferred_element_type=F32))
        up = jnp.dot(y_ref[...], wb_ref[n], preferred_element_type=F32)
        mixed = mixed + gate * up
    xn = x + jnp.dot(mixed.astype(BF16), wo_ref[...], preferred_element_type=F32)
    xo_ref[...] = xn
    h2 = _rms(xn, gf_ref[...])
    h2_ref[...] = h2
    logits = jnp.dot(h2.astype(BF16), wr_ref[...], preferred_element_type=F32) + rb_ref[...]

    lane = lax.broadcasted_iota(I32, logits.shape, 1).astype(F32)
    neg = -jnp.inf
    gmask = lane < N_GROUPS
    gmax, g_sel = _lane_argmax(jnp.where(gmask, logits, neg), lane)
    g_gate = 1.0 / jnp.sum(jnp.where(gmask, jnp.exp(logits - gmax), 0.0), axis=1, keepdims=True)
    e_lo = ROUTE_LOGIT0 + EXPERTS_PER_GROUP * g_sel
    el = jnp.where((lane >= e_lo) & (lane < e_lo + EXPERTS_PER_GROUP), logits, neg)
    v1, i1 = _lane_argmax(el, lane)
    v2, i2 = _lane_argmax(jnp.where(lane == i1, neg, el), lane)
    e = jnp.exp(v2 - v1)
    p1 = 1.0 / (1.0 + e)
    p2 = e / (1.0 + e)
    rt = jnp.where(lane == RT_E1, i1 - ROUTE_LOGIT0,
                   jnp.where(lane == RT_E2, i2 - ROUTE_LOGIT0,
                             jnp.where(lane == RT_G1, p1 * g_gate,
                                       jnp.where(lane == RT_G2, p2 * g_gate, 0.0))))
    rt_ref[...] = rt


def _merge_call(x2, ya, yb, yc, yd, g_mix, w_gate, w_branch, w_out, g_ffn, w_route, b_route):
    s = x2.shape[0]
    row = lambda w: pl.BlockSpec((M_TM, w), lambda i: (i, 0))
    vm = pl.BlockSpec(memory_space=pltpu.VMEM)
    return pl.pallas_call(
        _merge_kernel,
        out_shape=(jax.ShapeDtypeStruct((s, D_MODEL), F32), jax.ShapeDtypeStruct((s, D_MODEL), F32),
                   jax.ShapeDtypeStruct((s, LANES), F32)),
        grid=(s // M_TM,),
        in_specs=[row(D_MODEL), row(256), row(256), row(256), row(256),
                  vm, vm, vm, vm, vm, vm, vm],
        out_specs=(row(D_MODEL), row(D_MODEL), row(LANES)),
        compiler_params=_cparams(("parallel",)),
        name="merge_route",
    )(x2, ya, yb, yc, yd, g_mix.reshape(1, D_MODEL), w_gate, w_branch, w_out,
      g_ffn.reshape(1, D_MODEL), w_route, b_route)


E_BM = 128


def _gather_rows(src_hbm, idx_ref, dst, sem, n, wait, inline=False):
    def one(r, c):
        cp = pltpu.make_async_copy(src_hbm.at[pl.ds(idx_ref[0, 0, r], 1)], dst.at[pl.ds(r, 1)], sem)
        if wait:
            cp.wait()
        else:
            cp.start()
        return c
    if inline:
        for r in range(n):
            one(r, 0)
    else:
        lax.fori_loop(0, n, one, 0, unroll=8)


def _moe_kernel(be_ref, nused_ref, tok_ref, tok_next_ref, h2_hbm, w1_ref, w3_ref, w2_ref, o_ref,
                xbuf, w1b, w3b, w2b, sem):
    i = pl.program_id(0)
    nused = nused_ref[0]
    used = i < nused
    slot = i % 2

    @pl.when((i == 0) & used)
    def _():
        _gather_rows(h2_hbm, tok_ref, xbuf.at[0], sem.at[0], E_BM, wait=False)

    @pl.when(used)
    def _():
        prev = be_ref[jnp.maximum(i - 1, 0)]

        @pl.when((i == 0) | (be_ref[i] != prev))
        def _():
            w1b[...] = w1_ref[0, 0].astype(BF16)
            w3b[...] = w3_ref[0, 0].astype(BF16)
            w2b[...] = w2_ref[0, 0].astype(BF16)

        _gather_rows(h2_hbm, tok_ref, xbuf.at[slot], sem.at[slot], E_BM, wait=True)
        _gather_rows(h2_hbm, tok_next_ref, xbuf.at[1 - slot], sem.at[1 - slot], E_BM, wait=False, inline=True)
        xb = xbuf[slot].astype(BF16)
        a = jnp.dot(xb, w1b[...], preferred_element_type=F32)
        g = jnp.dot(xb, w3b[...], preferred_element_type=F32)
        hid = (a * jax.nn.sigmoid(a) * g).astype(BF16)
        o_ref[...] = jnp.dot(hid, w2b[...], preferred_element_type=F32)

    @pl.when(jnp.logical_not(used))
    def _():
        @pl.when(i == nused)
        def _():
            _gather_rows(h2_hbm, tok_ref, xbuf.at[slot], sem.at[slot], E_BM, wait=True)
        o_ref[...] = jnp.zeros(o_ref.shape, F32)


def _moe_call(h2, row_tok, block_e, nused, w1, w3, w2, layer):
    n_rows = row_tok.shape[0]
    n_blocks = n_rows // E_BM
    wspec = lambda shp: pl.BlockSpec((1, 1) + shp, lambda i, be, nu: (layer, be[i], 0, 0))
    grid_spec = pltpu.PrefetchScalarGridSpec(
        num_scalar_prefetch=2,
        grid=(n_blocks,),
        in_specs=[pl.BlockSpec((1, 1, E_BM), lambda i, be, nu: (i, 0, 0), memory_space=pltpu.SMEM),
                  pl.BlockSpec((1, 1, E_BM), lambda i, be, nu: (jnp.minimum(i + 1, n_blocks - 1), 0, 0),
                               memory_space=pltpu.SMEM),
                  pl.BlockSpec(memory_space=pl.ANY),
                  wspec((D_MODEL, F_EXPERT)), wspec((D_MODEL, F_EXPERT)), wspec((F_EXPERT, D_MODEL))],
        out_specs=pl.BlockSpec((E_BM, D_MODEL), lambda i, be, nu: (i, 0)),
        scratch_shapes=[pltpu.VMEM((2, E_BM, D_MODEL), F32),
                        pltpu.VMEM((D_MODEL, F_EXPERT), BF16),
                        pltpu.VMEM((D_MODEL, F_EXPERT), BF16),
                        pltpu.VMEM((F_EXPERT, D_MODEL), BF16),
                        pltpu.SemaphoreType.DMA((2,))],
    )
    tok3 = row_tok.reshape(n_blocks, 1, E_BM)
    return pl.pallas_call(
        _moe_kernel,
        out_shape=jax.ShapeDtypeStruct((n_rows, D_MODEL), F32),
        grid_spec=grid_spec,
        compiler_params=_cparams(("arbitrary",)),
        name="moe_experts",
    )(block_e, nused, tok3, tok3, h2, w1, w3, w2)


CB_TM = 128


def _combine_kernel(final, pos_ref, pos_next_ref, x_ref, rt_ref, ys_hbm, gfin_ref, o_ref, buf, sem):
    i = pl.program_id(0)
    slot = i % 2
    n = 2 * CB_TM

    @pl.when(i == 0)
    def _():
        _gather_rows(ys_hbm, pos_ref, buf.at[0], sem.at[0], n, wait=False)

    @pl.when(i + 1 < pl.num_programs(0))
    def _():
        _gather_rows(ys_hbm, pos_next_ref, buf.at[1 - slot], sem.at[1 - slot], n, wait=False)
    _gather_rows(ys_hbm, pos_ref, buf.at[slot], sem.at[slot], n, wait=True)

    rt = rt_ref[...]
    y = (buf[slot, 0:CB_TM, :] * rt[:, RT_G1:RT_G1 + 1]
         + buf[slot, CB_TM:2 * CB_TM, :] * rt[:, RT_G2:RT_G2 + 1])
    xn = x_ref[...] + y
    if final:
        xn = _rms(xn, gfin_ref[...])
    o_ref[...] = xn


def _combine_call(x2, rt, ys, pos, g_final, final):
    s = x2.shape[0]
    nb = s // CB_TM
    pos_b = pos.reshape(nb, CB_TM, 2).transpose(0, 2, 1).reshape(nb, 1, 2 * CB_TM)
    return pl.pallas_call(
        functools.partial(_combine_kernel, final),
        out_shape=jax.ShapeDtypeStruct((s, D_MODEL), F32),
        grid=(nb,),
        in_specs=[pl.BlockSpec((1, 1, 2 * CB_TM), lambda i: (i, 0, 0), memory_space=pltpu.SMEM),
                  pl.BlockSpec((1, 1, 2 * CB_TM), lambda i: (jnp.minimum(i + 1, nb - 1), 0, 0),
                               memory_space=pltpu.SMEM),
                  pl.BlockSpec((CB_TM, D_MODEL), lambda i: (i, 0)),
                  pl.BlockSpec((CB_TM, LANES), lambda i: (i, 0)),
                  pl.BlockSpec(memory_space=pl.ANY),
                  _whole((1, D_MODEL))],
        out_specs=pl.BlockSpec((CB_TM, D_MODEL), lambda i: (i, 0)),
        scratch_shapes=[pltpu.VMEM((2, 2 * CB_TM, D_MODEL), F32), pltpu.SemaphoreType.DMA((2,))],
        compiler_params=_cparams(("arbitrary",)),
        name="moe_combine",
    )(pos_b, pos_b, x2, rt, ys, g_final.reshape(1, D_MODEL))


PLAN_BLOCK = 256


def _dispatch_plan(rt):
    t = rt.shape[0]
    n_assign = t * TOP_K_INNER
    eid = rt[:, RT_E1:RT_E2 + 1].astype(I32).reshape(n_assign)
    onehot = (eid[:, None] == jnp.arange(N_EXPERTS, dtype=I32)[None, :]).astype(F32)
    pb = PLAN_BLOCK
    oh3 = onehot.reshape(n_assign // pb, pb, N_EXPERTS)
    tri = jnp.tril(jnp.ones((pb, pb), F32), k=-1)
    within = jnp.einsum('ij,bjk->bik', tri, oh3)
    block_tot = jnp.sum(oh3, axis=1)
    block_off = jnp.cumsum(block_tot, axis=0) - block_tot
    counts = (block_off[-1] + block_tot[-1]).astype(I32)
    rank = jnp.sum(oh3 * (within + block_off[:, None, :]), axis=-1).reshape(n_assign)
    padded = (counts + E_BM - 1) // E_BM * E_BM
    ends_pad = jnp.cumsum(padded)
    starts_pad = ends_pad - padded
    dest = (jnp.sum(onehot * starts_pad.astype(F32)[None, :], axis=-1) + rank).astype(I32)
    n_rows = (-(-(n_assign + N_EXPERTS * (E_BM - 1)) // E_BM) + 1) * E_BM
    n_blocks = n_rows // E_BM
    tok = jnp.repeat(jnp.arange(t, dtype=I32), TOP_K_INNER)
    row_tok = jnp.zeros((n_rows,), I32).at[dest].set(tok)
    block_start = jnp.arange(n_blocks, dtype=I32) * E_BM
    block_e = jnp.minimum(jnp.sum((ends_pad[None, :] <= block_start[:, None]).astype(I32), axis=1),
                          N_EXPERTS - 1)
    nused = (ends_pad[-1] // E_BM).astype(I32).reshape(1)
    return row_tok, block_e, nused, dest.reshape(t, TOP_K_INNER)


def _layer(x2, layer_idx, p, experts, final_g, final):
    (norm_mix_g, w_in, conv_w, conv_b, i_bias, f_bias, mnorm_g, rel_bias, lam_p, dnorm_g,
     w_branch, w_gate, w_out, norm_ffn_g, rgw, rgb, rew, reb) = p
    o = _k1_call(x2, norm_mix_g, *_rearrange_w_in(w_in))

    ya = _dsa_call(o["qaT"], o["qiT"], o["wiT"], o["ka"], o["vaT"], o["misc_bf"])
    yb = _mlstm_call(o["qkb"], o["vb"], o["ob"], o["misc"], conv_w, conv_b, i_bias, f_bias, mnorm_g)
    kcT_pad = jnp.pad(o["kcT"], ((0, 0), (C_PAD, 0)))
    vc_pad = jnp.pad(o["vc"], ((C_PAD, 0), (0, 0)))
    yc = _band_call(o["qc"], kcT_pad, vc_pad, _band_bias_line(rel_bias))
    lam_init = 0.8 - 0.6 * math.exp(-0.3 * layer_idx)
    yd = _diff_call(o["qdT"], o["kd"], o["vdT"], lam_p, dnorm_g, lam_init)

    w_route = jnp.concatenate([rgw, rew, jnp.zeros((D_MODEL, LANES - N_GROUPS - N_EXPERTS), F32)],
                              axis=1).astype(BF16)
    b_route = jnp.concatenate([rgb, reb, jnp.zeros((LANES - N_GROUPS - N_EXPERTS,), F32)]).reshape(1, LANES)
    xn, h2, rt = _merge_call(x2, ya, yb, yc, yd, norm_mix_g, w_gate.astype(BF16), w_branch.astype(BF16),
                             w_out.astype(BF16), norm_ffn_g, w_route, b_route)
    row_tok, block_e, nused, pos = _dispatch_plan(rt)
    ys = _moe_call(h2, row_tok, block_e, nused, *experts, layer_idx)
    return _combine_call(xn, rt, ys, pos, final_g, final)


def kernel(x, norm_mix_g, w_in, conv_w, conv_b, mlstm_i_bias, mlstm_f_bias, mlstm_norm_g, relpos_bias, diff_lambda, diff_norm_g, w_branch, w_gate, w_out, norm_ffn_g, router_group_w, router_group_b, router_expert_w, router_expert_b, expert_w1, expert_w3, expert_w2, final_norm_g):
    assert x.shape[0] == 1 and x.shape[2] == D_MODEL
    params = (norm_mix_g, w_in, conv_w, conv_b, mlstm_i_bias, mlstm_f_bias, mlstm_norm_g, relpos_bias,
              diff_lambda, diff_norm_g, w_branch, w_gate, w_out, norm_ffn_g, router_group_w,
              router_group_b, router_expert_w, router_expert_b)
    experts = (expert_w1, expert_w3, expert_w2)
    depth = norm_mix_g.shape[0]
    x2 = x[0]
    for l in range(depth):
        x2 = _layer(x2, l, tuple(a[l] for a in params), experts, final_norm_g, l == depth - 1)
    return x2[None]
```

```python
import functools
import math

import jax
import jax.numpy as jnp
from jax import lax
from jax.experimental import pallas as pl
from jax.experimental.pallas import tpu as pltpu

F32 = jnp.float32
BF16 = jnp.bfloat16
I32 = jnp.int32

D_MODEL = 1024
CHUNK = 64
HEAD_DIM = 64
NEG_INF = -1e30
H_A = 4
H_IDX = 4
D_IDX = 32
TOPK_MAX = 256
H_B = 4
CONV_K = 4
H_C = 4
N_PREV_CHUNKS = 8
MAX_REL_PAST = 128
H_D = 4
DQ_D = 32
W_BRANCH = 256
N_BRANCH = 4
N_GROUPS = 4
EXPERTS_PER_GROUP = 8
N_EXPERTS = 32
TOP_K_INNER = 2
F_EXPERT = 512
EPS = 1e-6

VMEM_LIMIT_BYTES = 52 * 1024 * 1024
LANES = 128

INT_MIN = -(2 ** 31)
I16_MIN = -(2 ** 15)
I16 = jnp.int16
M_INIT = -5e29

_COL_SIZES = (256, 256, 256, 128, 32, 4, 256, 256, 256, 4, 4, 256, 256, 256, 256, 256, 256, 256)
_COL_NAMES = ("qa", "ka", "va", "qi", "ki", "wi", "qb", "kb", "vb", "ib", "fb", "ob",
              "qc", "kc", "vc", "qd", "kd", "vd")
_COL_OFF = {}
_o = 0
for _n, _s in zip(_COL_NAMES, _COL_SIZES):
    _COL_OFF[_n] = (_o, _s)
    _o += _s
C_IN = _o
MISC_KI = 0
MISC_WI = 32
MISC_IB = 36
MISC_FB = 40


def _cparams(sem):
    return pltpu.CompilerParams(dimension_semantics=sem, vmem_limit_bytes=VMEM_LIMIT_BYTES)


def _whole(shape):
    nd = len(shape)
    return pl.BlockSpec(shape, lambda *_: (0,) * nd)


_K1_NAT = (("ka", 256, BF16), ("misc", 128, F32), ("qkb", 512, F32), ("vb", 256, F32), ("ob", 256, F32),
           ("qc", 256, BF16), ("vc", 256, BF16), ("kd", 256, BF16))
_K1_TR = (("qaT", 256, BF16), ("qiT", 128, BF16), ("vaT", 256, BF16), ("kcT", 256, BF16),
          ("qdT", 256, BF16), ("vdT", 256, BF16), ("wiT", 8, F32))
K1_NAT_WIDTH = sum(w for _, w, _ in _K1_NAT)
K1_TR_WIDTH = sum(w for _, w, _ in _K1_TR)
K1_TM = 512
V_AUG = 80
_K1_VAUG = ("vaT", "vdT")


def _k1_tr_rows(name, width):
    return (width // HEAD_DIM) * V_AUG if name in _K1_VAUG else width


def _rearrange_w_in(w_in):
    def cols(name):
        o, s = _COL_OFF[name]
        return w_in[:, o:o + s]
    d = w_in.shape[0]
    misc = jnp.concatenate([cols("ki"), cols("wi"), cols("ib"), cols("fb"),
                            jnp.zeros((d, LANES - 44), w_in.dtype)], axis=1)
    nat = [cols("ka"), misc, cols("qb"), cols("kb"), cols("vb"), cols("ob"), cols("qc"), cols("vc"),
           cols("kd")]
    tr = [cols("qa"), cols("qi"), cols("va"), cols("kc"), cols("qd"), cols("vd"), cols("wi"),
          jnp.zeros((d, 8 - H_IDX), w_in.dtype)]
    return (jnp.concatenate(nat, axis=1).astype(BF16), jnp.concatenate(tr, axis=1).T.astype(BF16))


def _rms(x, g):
    ms = jnp.mean(x * x, axis=-1, keepdims=True)
    return x * lax.rsqrt(ms + EPS) * g


def _k1_kernel(x_ref, g_ref, wn_ref, wt_ref, *out_refs):
    h = _rms(x_ref[...], g_ref[...]).astype(BF16)
    refs = dict(zip([n for n, _, _ in _K1_NAT] + ["misc_bf"] + [n for n, _, _ in _K1_TR], out_refs))
    off = 0
    for name, width, _ in _K1_NAT:
        r = jnp.dot(h, wn_ref[:, off:off + width], preferred_element_type=F32)
        refs[name][...] = r.astype(refs[name].dtype)
        if name == "misc":
            refs["misc_bf"][...] = r.astype(BF16)
        off += width
    off = 0
    for name, width, _ in _K1_TR:
        r = lax.dot_general(wt_ref[off:off + width, :], h, (((1,), (1,)), ((), ())),
                            preferred_element_type=F32)
        if name in _K1_VAUG:
            tail = (lax.broadcasted_iota(I32, (V_AUG - HEAD_DIM, r.shape[1]), 0) == 0)
            tail = jnp.where(tail, 1.0, 0.0).astype(BF16)
            for hh in range(width // HEAD_DIM):
                refs[name][hh * V_AUG:hh * V_AUG + HEAD_DIM, :] = \
                    r[hh * HEAD_DIM:(hh + 1) * HEAD_DIM, :].astype(BF16)
                refs[name][hh * V_AUG + HEAD_DIM:(hh + 1) * V_AUG, :] = tail
        else:
            refs[name][...] = r.astype(refs[name].dtype)
        off += width


def _k1_call(x2, g, w_nat, w_trT):
    s = x2.shape[0]
    tm = K1_TM
    outs = (tuple(jax.ShapeDtypeStruct((s, w), dt) for _, w, dt in _K1_NAT)
            + (jax.ShapeDtypeStruct((s, LANES), BF16),)
            + tuple(jax.ShapeDtypeStruct((_k1_tr_rows(n, w), s), dt) for n, w, dt in _K1_TR))
    out_specs = (tuple(pl.BlockSpec((tm, w), lambda i: (i, 0)) for _, w, _ in _K1_NAT)
                 + (pl.BlockSpec((tm, LANES), lambda i: (i, 0)),)
                 + tuple(pl.BlockSpec((_k1_tr_rows(n, w), tm), lambda i: (0, i)) for n, w, _ in _K1_TR))
    res = pl.pallas_call(
        _k1_kernel,
        out_shape=outs,
        grid=(s // tm,),
        in_specs=[pl.BlockSpec((tm, D_MODEL), lambda i: (i, 0)),
                  _whole((1, D_MODEL)),
                  _whole((D_MODEL, K1_NAT_WIDTH)),
                  _whole((K1_TR_WIDTH, D_MODEL))],
        out_specs=out_specs,
        compiler_params=_cparams(("parallel",)),
        name="k1_norm_proj",
    )(x2, g.reshape(1, D_MODEL), w_nat, w_trT)
    return dict(zip([n for n, _, _ in _K1_NAT] + ["misc_bf"] + [n for n, _, _ in _K1_TR], res))


A_TQ = 256
A_KT = 512
A_SLAB = 256
A_SWEEP_SLABS = 4
LOG2E = 1.4426950408889634
QK_AHEAD = 4
BF16_ROWS = 16


def _flash_step(s, i, v_t, m_ref, acc_ref):
    kt, tq = s.shape
    parts = s.reshape(kt // BF16_ROWS, BF16_ROWS, tq)
    parts = [parts[g] for g in range(kt // BF16_ROWS)]
    while len(parts) > 1:
        parts = [jnp.maximum(parts[g], parts[g + 1]) for g in range(0, len(parts), 2)]
    m_old = m_ref[i]
    m_new = jnp.maximum(m_old, jnp.max(parts[0].astype(F32), axis=0, keepdims=True))
    alpha = jnp.exp2(m_old - m_new)
    p = jnp.exp2(s - m_new.astype(BF16))
    acc_ref[i] = alpha * acc_ref[i] + jnp.dot(v_t, p, preferred_element_type=F32)
    m_ref[i] = m_new


def _flash_result(acc_ref, i):
    return acc_ref[i, 0:HEAD_DIM, :] / acc_ref[i, HEAD_DIM:HEAD_DIM + 1, :]


def _logits(k_t, q_pad):
    return jnp.dot(k_t, q_pad, preferred_element_type=F32).astype(BF16)


def _qk_prologue(k_t, qpad_ref, n, s_ref):
    for i in range(min(QK_AHEAD, n)):
        s_ref[i] = _logits(k_t, qpad_ref[i])


def _flash_tile(k_t, qpad_ref, n, v_tile, bias, m_ref, acc_ref, s_ref=None, k_next=None):
    a = min(QK_AHEAD, n)
    if s_ref is None:
        pend = [_logits(k_t, qpad_ref[i]) for i in range(a)]
    else:
        pend = [s_ref[i] for i in range(a)]
    for i in range(n):
        s = pend.pop(0)
        if i + a < n:
            pend.append(_logits(k_t, qpad_ref[i + a]))
        elif s_ref is not None:
            s_ref[i + a - n] = _logits(k_next, qpad_ref[i + a - n])
        if bias is not None:
            s = s + bias
        _flash_step(s, i, v_tile(i), m_ref, acc_ref)


def _bit_transpose32(words):
    a = list(words)
    j, m = 16, 0x0000FFFF
    while j:
        k = 0
        while k < 32:
            t = (a[k] ^ lax.shift_right_logical(a[k + j], j)) & m
            a[k] = a[k] ^ t
            a[k + j] = a[k + j] ^ lax.shift_left(t, j)
            k = (k + j + 1) & ~j
        j >>= 1
        m = (m ^ (m << j)) & 0xFFFFFFFF
    return a


def _dsa_kernel(topk, qaT_ref, qiT_ref, wiT_ref, ka_ref, vaT_ref, mb_ref, tri_ref, o_ref,
                planes_ref, cand_ref, above_ref, qpad_ref, qipad_ref, acc_ref, m_ref, carry_ref):
    tq, kt = A_TQ, A_KT
    b = pl.program_id(0)
    ntiles = ((b + 1) * tq + kt - 1) // kt
    q_pos = b * tq + lax.broadcasted_iota(I32, (1, tq), 1)
    vis_end = (q_pos // CHUNK + 1) * CHUNK

    qiT = qiT_ref[...]
    for h in range(H_IDX):
        qipad_ref[h, 0:D_IDX, :] = qiT[h * D_IDX:(h + 1) * D_IDX, :]
        qipad_ref[h, D_IDX:LANES, :] = jnp.zeros((LANES - D_IDX, tq), BF16)

    def p1(j, carry):
        s0 = pl.multiple_of(j * kt, kt)
        mb = mb_ref[pl.ds(s0, kt), :]
        score = jnp.zeros((kt, tq), F32)
        for h in range(H_IDX):
            r = jnp.dot(mb, qipad_ref[h], preferred_element_type=F32)
            score = score + jnp.maximum(r, 0.0) * wiT_ref[h:h + 1, :]
        bits = lax.bitcast_convert_type(score, I32)
        ukey = bits ^ (lax.shift_right_arithmetic(bits, 31) | INT_MIN)
        s_pos = s0 + lax.broadcasted_iota(I32, (kt, 1), 0)
        ukey = jnp.where(s_pos < vis_end, ukey, 0)
        u4 = ukey.reshape(kt // A_SLAB, 32, 8, tq)
        for s2 in range(kt // A_SLAB):
            planes = _bit_transpose32([u4[s2, v] for v in range(32)])
            for r in range(32):
                planes_ref[j * (kt // A_SLAB) + s2, r] = planes[r]
        return carry
    lax.fori_loop(0, ntiles, p1, 0)

    kf = float(topk)
    nslab = ntiles * (kt // A_SLAB)
    nstep = (nslab + A_SWEEP_SLABS - 1) // A_SWEEP_SLABS

    def init_sets(sl, carry):
        cand_ref[sl] = jnp.full((8, tq), -1, I32)
        above_ref[sl] = jnp.zeros((8, tq), I32)
        return carry
    lax.fori_loop(0, nslab, init_sets, 0)

    def init_pad(sl, carry):
        cand_ref[sl] = jnp.zeros((8, tq), I32)
        above_ref[sl] = jnp.zeros((8, tq), I32)
        planes_ref[sl] = jnp.zeros((32, 8, tq), I32)
        return carry
    lax.fori_loop(nslab, nstep * A_SWEEP_SLABS, init_pad, 0)

    def apply_decision(sl, prev_plane, took_one):
        cand = cand_ref[sl]
        ones = cand & prev_plane
        cand = jnp.where(took_one, ones, cand ^ ones)
        above_ref[sl] = jnp.where(took_one, above_ref[sl], above_ref[sl] | ones)
        cand_ref[sl] = cand
        return cand

    def sweep(i, carry):
        n_above, took, tau = carry
        took_one = jnp.broadcast_to(took, (8, tq)) != 0
        first = i == 0

        def step(jj, acc):
            for s2 in range(A_SWEEP_SLABS):
                sl = jj * A_SWEEP_SLABS + s2
                prev_plane = jnp.where(first, -1, planes_ref[sl, jnp.maximum(i - 1, 0)])
                cand = apply_decision(sl, prev_plane, took_one)
                acc = acc + lax.population_count(cand & planes_ref[sl, i])
            return acc
        acc = lax.fori_loop(0, nstep, step, jnp.zeros((8, tq), I32))
        n_one = jnp.sum(acc.astype(F32), axis=0, keepdims=True)
        take = (n_above + n_one) >= kf
        n_above = jnp.where(take, n_above, n_above + n_one)
        tau = jnp.where(take, tau | lax.shift_left(jnp.int32(1), 31 - i), tau)
        return n_above, jnp.where(take, 1, 0), tau
    n_above, took, tau = lax.fori_loop(
        0, 32, sweep, (jnp.zeros((1, tq), F32), jnp.ones((1, tq), I32), jnp.zeros((1, tq), I32)))

    def last_decision(sl, carry):
        apply_decision(sl, planes_ref[sl, 31], jnp.broadcast_to(took, (8, tq)) != 0)
        return carry
    lax.fori_loop(0, nslab, last_decision, 0)
    need = jnp.where(tau == 0, 0.0, kf - n_above)

    m_ref[...] = jnp.full(m_ref.shape, M_INIT, F32)
    acc_ref[...] = jnp.zeros(acc_ref.shape, F32)
    carry_ref[...] = jnp.zeros(carry_ref.shape, F32)
    qaT = (qaT_ref[...].astype(F32) * (HEAD_DIM ** -0.5 * LOG2E)).astype(BF16)
    row = lax.broadcasted_iota(I32, qaT.shape, 0)
    for h in range(H_A):
        qpad_ref[h] = jnp.where((row >= h * HEAD_DIM) & (row < (h + 1) * HEAD_DIM), qaT,
                                jnp.zeros_like(qaT))

    def slab_rows(ref, sl):
        word = ref[sl]
        return jnp.concatenate([lax.shift_right_logical(word, 31 - v) & 1 for v in range(32)], axis=0)

    def p3(j, carry):
        s0 = pl.multiple_of(j * kt, kt)
        slabs = [j * (kt // A_SLAB) + s2 for s2 in range(kt // A_SLAB)]
        gt = jnp.concatenate([slab_rows(above_ref, sl) for sl in slabs], axis=0) != 0
        eq_i = jnp.concatenate([slab_rows(cand_ref, sl) for sl in slabs], axis=0)
        eq = eq_i != 0
        eqf = eq_i.astype(F32)
        pref = jnp.dot(tri_ref[...], eqf.astype(BF16), preferred_element_type=F32)
        seen = carry_ref[...]
        sel = gt | (eq & (pref + seen < need))
        bias = jnp.where(sel, 0.0, NEG_INF).astype(BF16)
        carry_ref[...] = seen + pref[kt - 1:kt, :] + eqf[kt - 1:kt, :]
        k_t = ka_ref[pl.ds(s0, kt), :]
        _flash_tile(k_t, qpad_ref, H_A, lambda i: vaT_ref[i * V_AUG:(i + 1) * V_AUG, pl.ds(s0, kt)],
                    bias, m_ref, acc_ref)
        return carry
    lax.fori_loop(0, ntiles, p3, 0)

    ys = [_flash_result(acc_ref, h) for h in range(H_A)]
    o_ref[...] = jnp.concatenate(ys, axis=0).T.astype(o_ref.dtype)


def _dsa_call(qaT, qiT, wiT, ka, vaT, misc_bf):
    s = ka.shape[0]
    topk = min(TOPK_MAX, s // 4)
    tri = jnp.tril(jnp.ones((A_KT, A_KT), F32), k=-1).astype(BF16)
    vm = pl.BlockSpec(memory_space=pltpu.VMEM)
    return pl.pallas_call(
        functools.partial(_dsa_kernel, topk),
        out_shape=jax.ShapeDtypeStruct((s, W_BRANCH), BF16),
        grid=(s // A_TQ,),
        in_specs=[pl.BlockSpec((256, A_TQ), lambda i: (0, i)),
                  pl.BlockSpec((LANES, A_TQ), lambda i: (0, i)),
                  pl.BlockSpec((8, A_TQ), lambda i: (0, i)),
                  vm, vm, vm, vm],
        out_specs=pl.BlockSpec((A_TQ, 256), lambda i: (i, 0)),
        scratch_shapes=[pltpu.VMEM((s // A_SLAB, 32, 8, A_TQ), I32),
                        pltpu.VMEM((s // A_SLAB, 8, A_TQ), I32),
                        pltpu.VMEM((s // A_SLAB, 8, A_TQ), I32),
                        pltpu.VMEM((H_A, 256, A_TQ), BF16),
                        pltpu.VMEM((H_IDX, LANES, A_TQ), BF16),
                        pltpu.VMEM((H_A, V_AUG, A_TQ), F32),
                        pltpu.VMEM((H_A, 1, A_TQ), F32),
                        pltpu.VMEM((1, A_TQ), F32)],
        compiler_params=_cparams(("arbitrary",)),
        name="dsa_mixer",
    )(qaT, qiT, wiT, ka, vaT, misc_bf, tri)


def _pair_select(lo, hi):
    lane = lax.broadcasted_iota(I32, lo.shape, 1)
    return jnp.where(lane < HEAD_DIM, lo, hi)


def _pair_head_rms(o, g):
    lane = lax.broadcasted_iota(I32, o.shape, 1)
    low = lane < HEAD_DIM
    sq = o * o
    ms_lo = jnp.sum(jnp.where(low, sq, 0.0), axis=1, keepdims=True) * (1.0 / HEAD_DIM)
    ms_hi = jnp.sum(jnp.where(low, 0.0, sq), axis=1, keepdims=True) * (1.0 / HEAD_DIM)
    ms = jnp.where(low, ms_lo, ms_hi)
    return o * lax.rsqrt(ms + EPS) * g


D_TQ = 256
D_KT = 1024


def _diff_kernel(lam_init, qT_ref, kd_ref, vT_ref, lam_ref, g_ref, o_ref, qpad_ref, acc_ref, m_ref, s_ref):
    tq = D_TQ
    b = pl.program_id(0)
    q_pos = b * tq + lax.broadcasted_iota(I32, (1, tq), 1)
    vis_end = (q_pos // CHUNK + 1) * CHUNK
    m_ref[...] = jnp.full(m_ref.shape, M_INIT, F32)
    acc_ref[...] = jnp.zeros(acc_ref.shape, F32)
    qT = (qT_ref[...].astype(F32) * (DQ_D ** -0.5 * LOG2E)).astype(BF16)
    row = lax.broadcasted_iota(I32, qT.shape, 0)
    for i in range(2 * H_D):
        qpad_ref[i] = jnp.where((row >= i * DQ_D) & (row < (i + 1) * DQ_D), qT, jnp.zeros_like(qT))

    def tile(s0, kt, masked, s_ref=None, s0_next=None):
        k_t = kd_ref[pl.ds(s0, kt), :]
        k_next = None if s0_next is None else kd_ref[pl.ds(s0_next, kt), :]
        bias = None
        if masked:
            vis = (s0 + lax.broadcasted_iota(I32, (kt, 1), 0)) < vis_end
            bias = jnp.where(vis, 0.0, NEG_INF).astype(BF16)
        _flash_tile(k_t, qpad_ref, 2 * H_D, lambda i: vT_ref[(i // 2) * V_AUG:(i // 2 + 1) * V_AUG, pl.ds(s0, kt)],
                    bias, m_ref, acc_ref, s_ref, k_next)

    n_big = (b * tq) // D_KT

    @pl.when(n_big > 0)
    def _():
        _qk_prologue(kd_ref[pl.ds(0, D_KT), :], qpad_ref, 2 * H_D, s_ref)

    def full_tile(j, carry):
        j_next = jnp.minimum(j + 1, n_big - 1)
        tile(pl.multiple_of(j * D_KT, D_KT), D_KT, False, s_ref, pl.multiple_of(j_next * D_KT, D_KT))
        return carry
    lax.fori_loop(0, n_big, full_tile, 0)

    def small_tile(j, carry):
        tile(pl.multiple_of(j * tq, tq), tq, False)
        return carry
    lax.fori_loop(n_big * (D_KT // tq), b, small_tile, 0)
    tile(pl.multiple_of(b * tq, tq), tq, True)

    lp = lam_ref[...]
    lam = (jnp.exp(jnp.sum(lp[0:1] * lp[1:2], axis=1, keepdims=True))
           - jnp.exp(jnp.sum(lp[2:3] * lp[3:4], axis=1, keepdims=True)) + lam_init)
    ys = []
    for h in range(H_D):
        o = _flash_result(acc_ref, 2 * h) - lam * _flash_result(acc_ref, 2 * h + 1)
        ms = jnp.mean(o * o, axis=0, keepdims=True)
        ys.append(o * lax.rsqrt(ms + EPS) * g_ref[h * HEAD_DIM:(h + 1) * HEAD_DIM, :] * (1.0 - lam_init))
    o_ref[...] = jnp.concatenate(ys, axis=0).T.astype(o_ref.dtype)


def _diff_call(qdT, kd, vdT, lam_p, dnorm_g, lam_init):
    s = kd.shape[0]
    return pl.pallas_call(
        functools.partial(_diff_kernel, lam_init),
        out_shape=jax.ShapeDtypeStruct((s, W_BRANCH), BF16),
        grid=(s // D_TQ,),
        in_specs=[pl.BlockSpec((256, D_TQ), lambda i: (0, i)),
                  pl.BlockSpec(memory_space=pltpu.VMEM),
                  pl.BlockSpec(memory_space=pltpu.VMEM),
                  _whole((4, DQ_D)),
                  _whole((256, 1))],
        out_specs=pl.BlockSpec((D_TQ, 256), lambda i: (i, 0)),
        scratch_shapes=[pltpu.VMEM((2 * H_D, 256, D_TQ), BF16),
                        pltpu.VMEM((2 * H_D, V_AUG, D_TQ), F32),
                        pltpu.VMEM((2 * H_D, 1, D_TQ), F32),
                        pltpu.VMEM((QK_AHEAD, D_KT, D_TQ), BF16)],
        compiler_params=_cparams(("parallel",)),
        name="diff_attention",
    )(qdT, kd, vdT, lam_p, dnorm_g.reshape(256, 1))


C_TQ = 128
C_PAD = N_PREV_CHUNKS * CHUNK
C_WIN = C_PAD + C_TQ
C_LINE = C_TQ + C_WIN


def _band_bias_line(rel_bias):
    dist_desc = jnp.arange(C_TQ - 1 + C_PAD, C_PAD - C_WIN, -1)
    line = rel_bias[:, jnp.clip(dist_desc, -(CHUNK - 1), MAX_REL_PAST) + (CHUNK - 1)].astype(F32)
    return jnp.pad(line, ((0, 0), (0, C_LINE - line.shape[1])))


def _band_kernel(qc_ref, kcT_ref, vc_ref, line_ref, o_ref, bias_ref):
    tq = C_TQ
    b = pl.program_id(0)

    @pl.when(b == 0)
    def _():
        i = lax.broadcasted_iota(I32, (tq, C_WIN), 0)
        w = lax.broadcasted_iota(I32, (tq, C_WIN), 1)
        in_band = (w // CHUNK >= i // CHUNK) & (w // CHUNK <= i // CHUNK + N_PREV_CHUNKS)
        for h in range(H_C):
            x = jnp.broadcast_to(line_ref[h:h + 1, :], (tq, C_LINE))
            y = pltpu.roll(x, 1, 1, stride=1, stride_axis=0)
            bias_ref[h] = jnp.where(in_band, y[:, C_LINE - C_WIN:], NEG_INF)

    w0 = pl.multiple_of(b * tq, tq)
    q_all = qc_ref[...] * 0.125
    key_abs = b * tq - C_PAD + lax.broadcasted_iota(I32, (1, C_WIN), 1)
    ok = key_abs >= 0
    logits = [jnp.dot(q_all[:, h * HEAD_DIM:(h + 1) * HEAD_DIM],
                      kcT_ref[h * HEAD_DIM:(h + 1) * HEAD_DIM, pl.ds(w0, C_WIN)],
                      preferred_element_type=F32) for h in range(H_C)]
    heads = []
    for h in range(H_C):
        s = jnp.where(ok, logits[h] + bias_ref[h], NEG_INF)
        m = jnp.max(s, axis=1, keepdims=True)
        p = jnp.exp(s - m)
        l = jnp.sum(p, axis=1, keepdims=True)
        pv = jnp.dot(p.astype(BF16), vc_ref[pl.ds(w0, C_WIN), (h // 2) * LANES:(h // 2 + 1) * LANES],
                     preferred_element_type=F32)
        heads.append(pv / l)
    for pr in range(H_C // 2):
        o_ref[:, pr * LANES:(pr + 1) * LANES] = _pair_select(heads[2 * pr], heads[2 * pr + 1]).astype(o_ref.dtype)


def _band_call(qc, kcT_pad, vc_pad, bias_line):
    s = qc.shape[0]
    return pl.pallas_call(
        _band_kernel,
        out_shape=jax.ShapeDtypeStruct((s, W_BRANCH), BF16),
        grid=(s // C_TQ,),
        in_specs=[pl.BlockSpec((C_TQ, 256), lambda i: (i, 0)),
                  pl.BlockSpec(memory_space=pltpu.VMEM),
                  pl.BlockSpec(memory_space=pltpu.VMEM),
                  _whole((H_C, C_LINE))],
        out_specs=pl.BlockSpec((C_TQ, 256), lambda i: (i, 0)),
        scratch_shapes=[pltpu.VMEM((H_C, C_TQ, C_WIN), F32)],
        compiler_params=_cparams(("arbitrary",)),
        name="band_attention",
    )(qc, kcT_pad, vc_pad, bias_line)


B_NCH = 2
B_L = B_NCH * CHUNK
HIGHEST = lax.Precision.HIGHEST


def _mlstm_kernel(qk_ref, vb_ref, ob_ref, misc_ref, cw_ref, cb_ref, gb_ref, ng_ref, o_ref,
                  tail_ref, ct_ref, n_ref, m_ref):
    L = CHUNK
    T = B_L
    hd = HEAD_DIM

    @pl.when(pl.program_id(0) == 0)
    def _():
        tail_ref[...] = jnp.zeros(tail_ref.shape, F32)
        ct_ref[...] = jnp.zeros(ct_ref.shape, F32)
        n_ref[...] = jnp.zeros(n_ref.shape, F32)
        m_ref[...] = jnp.zeros(m_ref.shape, F32)

    x = qk_ref[...]
    xx = jnp.concatenate([tail_ref[...], x], axis=0)
    y = jnp.broadcast_to(cb_ref[...], x.shape)
    for j in range(CONV_K):
        y = y + cw_ref[j:j + 1, :] * xx[8 - (CONV_K - 1) + j:8 - (CONV_K - 1) + j + T, :]
    tail_ref[...] = x[T - 8:T, :]
    qk = y * jax.nn.sigmoid(y)
    q_all = qk[:, :W_BRANCH]
    k_all = qk[:, W_BRANCH:] * 0.125
    v_all = vb_ref[...]
    o_gate = jax.nn.sigmoid(ob_ref[...])

    gts = misc_ref[...] + gb_ref[...]
    lf = jnp.minimum(gts, 0.0) - jnp.log1p(jnp.exp(-jnp.abs(gts)))
    r_t = lax.broadcasted_iota(I32, (T, T), 0)
    c_t = lax.broadcasted_iota(I32, (T, T), 1)
    ltri = jnp.where((c_t <= r_t) & (c_t // L == r_t // L), 1.0, 0.0)
    cum = jnp.dot(ltri, lf, precision=HIGHEST, preferred_element_type=F32)
    lane = lax.broadcasted_iota(I32, (T, LANES), 1)
    mixed = jnp.where(lane < MISC_FB, gts, cum)
    sel_r = lax.broadcasted_iota(I32, (8, LANES), 0)
    sel_c = lax.broadcasted_iota(I32, (8, LANES), 1)
    sel = jnp.where(sel_c == sel_r + MISC_IB, 1.0, 0.0)
    rows = lax.dot_general(sel, mixed, (((1,), (1,)), ((), ())), precision=HIGHEST,
                           preferred_element_type=F32)
    causal = lax.broadcasted_iota(I32, (L, L), 1) <= lax.broadcasted_iota(I32, (L, L), 0)

    heads = range(H_B)
    sl = lambda a, t0, h: a[t0:t0 + L, h * hd:(h + 1) * hd]
    pre = []
    for ci in range(B_NCH):
        t0 = ci * L
        per_head = []
        for h in heads:
            q, k, v = sl(q_all, t0, h), sl(k_all, t0, h), sl(v_all, t0, h)
            qb, kb, vbf = q.astype(BF16), k.astype(BF16), v.astype(BF16)
            qkt = lax.dot_general(qb, kb, (((1,), (1,)), ((), ())), preferred_element_type=F32)
            it_r = rows[h:h + 1, t0:t0 + L]
            cum_r = rows[H_B + h:H_B + h + 1, t0:t0 + L]
            it_c = gts[t0:t0 + L, MISC_IB + h:MISC_IB + h + 1]
            cum_c = cum[t0:t0 + L, MISC_FB + h:MISC_FB + h + 1]
            dmat = jnp.where(causal, cum_c - cum_r + it_r, NEG_INF)
            dmax = jnp.max(dmat, axis=1, keepdims=True)
            cum_end = cum_c[L - 1:L, :]
            g = cum_end - cum_c + it_c
            gmax = jnp.max(g, axis=0, keepdims=True)
            per_head.append((q, k, qb, vbf, qkt, dmat, dmax, cum_c, cum_end, g, gmax))
        pre.append(per_head)

    state = [(ct_ref[h], n_ref[h], m_ref[h]) for h in heads]
    out_chunks = []
    for ci in range(B_NCH):
        t0 = ci * L
        qc = [jnp.dot(pre[ci][h][2], state[h][0].astype(BF16), preferred_element_type=F32) for h in heads]
        mid = []
        for h in heads:
            q, k, qb, vbf, qkt, dmat, dmax, cum_c, cum_end, g, gmax = pre[ci][h]
            ct, n_row, m_prev = state[h]
            m_inter = cum_c + m_prev
            m_t = jnp.maximum(m_inter, dmax)
            w = jnp.exp(dmat - m_t) * qkt
            inter = jnp.exp(m_inter - m_t)
            m_new = jnp.maximum(cum_end + m_prev, gmax)
            carry_scale = jnp.exp(cum_end + m_prev - m_new)
            src_k = jnp.exp(g - m_new) * k
            mid.append((w, inter, m_t, m_new, carry_scale, src_k))
        wv = [jnp.dot(mid[h][0].astype(BF16), pre[ci][h][3], preferred_element_type=F32) for h in heads]
        upd = [lax.dot_general(mid[h][5].astype(BF16), pre[ci][h][3], (((0,), (0,)), ((), ())),
                               preferred_element_type=F32) for h in heads]
        outs = []
        for h in heads:
            q = pre[ci][h][0]
            w, inter, m_t, m_new, carry_scale, src_k = mid[h]
            ct, n_row, _ = state[h]
            num = inter * qc[h] + wv[h]
            den = inter * jnp.sum(q * n_row, axis=1, keepdims=True) + jnp.sum(w, axis=1, keepdims=True)
            h_t = num / jnp.maximum(jnp.abs(den), jnp.exp(-m_t))
            state[h] = (carry_scale * ct + upd[h],
                        carry_scale * n_row + jnp.sum(src_k, axis=0, keepdims=True), m_new)
            ms = jnp.mean(h_t * h_t, axis=1, keepdims=True)
            hn = h_t * lax.rsqrt(ms + EPS) * ng_ref[:, h * hd:(h + 1) * hd]
            outs.append(sl(o_gate, t0, h) * hn)
        out_chunks.append(jnp.concatenate(outs, axis=1))
    for h in heads:
        ct_ref[h], n_ref[h], m_ref[h] = state[h]
    o_ref[...] = jnp.concatenate(out_chunks, axis=0).astype(o_ref.dtype)


def _mlstm_call(qkb, vb, ob, misc, conv_w, conv_b, i_bias, f_bias, norm_g):
    s = qkb.shape[0]
    gbias = jnp.zeros((1, LANES), F32)
    gbias = gbias.at[0, MISC_IB:MISC_IB + H_B].set(i_bias).at[0, MISC_FB:MISC_FB + H_B].set(f_bias)
    row = lambda w: pl.BlockSpec((B_L, w), lambda i: (i, 0))
    return pl.pallas_call(
        _mlstm_kernel,
        out_shape=jax.ShapeDtypeStruct((s, W_BRANCH), BF16),
        grid=(s // B_L,),
        in_specs=[row(512), row(256), row(256), row(128),
                  _whole((CONV_K, 512)), _whole((1, 512)), _whole((1, LANES)), _whole((1, 256))],
        out_specs=row(256),
        scratch_shapes=[pltpu.VMEM((8, 512), F32),
                        pltpu.VMEM((H_B, HEAD_DIM, HEAD_DIM), F32),
                        pltpu.VMEM((H_B, 1, HEAD_DIM), F32),
                        pltpu.VMEM((H_B, 1, 1), F32)],
        compiler_params=_cparams(("arbitrary",)),
        name="mlstm_mixer",
    )(qkb, vb, ob, misc, conv_w, conv_b.reshape(1, 512), gbias, norm_g.reshape(1, 256))


M_TM = 256
ROUTE_LOGIT0 = N_GROUPS
RT_E1, RT_E2, RT_G1, RT_G2 = 0, 1, 2, 3


def _lane_argmax(vals, lane):
    v = jnp.max(vals, axis=1, keepdims=True)
    idx = jnp.min(jnp.where(vals == v, lane, float(LANES)), axis=1, keepdims=True)
    return v, idx


def _merge_kernel(x_ref, ya_ref, yb_ref, yc_ref, yd_ref, gm_ref, wg_ref, wb_ref, wo_ref, gf_ref,
                  wr_ref, rb_ref, xo_ref, h2_ref, rt_ref):
    x = x_ref[...]
    h = _rms(x, gm_ref[...]).astype(BF16)
    mixed = jnp.zeros(x.shape, F32)
    for n, y_ref in enumerate((ya_ref, yb_ref, yc_ref, yd_ref)):
        gate = jax.nn.sigmoid(jnp.dot(h, wg_ref[:, n * D_MODEL:(n + 1) * D_MODEL],
                                      preferred_element_type=F32))
        up = jnp.dot(y_ref[...], wb_ref[n], preferred_element_type=F32)
        mixed = mixed + gate * up
    xn = x + jnp.dot(mixed.astype(BF16), wo_ref[...], preferred_element_type=F32)
    xo_ref[...] = xn
    h2 = _rms(xn, gf_ref[...])
    h2_ref[...] = h2
    logits = jnp.dot(h2.astype(BF16), wr_ref[...], preferred_element_type=F32) + rb_ref[...]

    lane = lax.broadcasted_iota(I32, logits.shape, 1).astype(F32)
    neg = -jnp.inf
    gmask = lane < N_GROUPS
    gmax, g_sel = _lane_argmax(jnp.where(gmask, logits, neg), lane)
    g_gate = 1.0 / jnp.sum(jnp.where(gmask, jnp.exp(logits - gmax), 0.0), axis=1, keepdims=True)
    e_lo = ROUTE_LOGIT0 + EXPERTS_PER_GROUP * g_sel
    el = jnp.where((lane >= e_lo) & (lane < e_lo + EXPERTS_PER_GROUP), logits, neg)
    v1, i1 = _lane_argmax(el, lane)
    v2, i2 = _lane_argmax(jnp.where(lane == i1, neg, el), lane)
    e = jnp.exp(v2 - v1)
    p1 = 1.0 / (1.0 + e)
    p2 = e / (1.0 + e)
    rt = jnp.where(lane == RT_E1, i1 - ROUTE_LOGIT0,
                   jnp.where(lane == RT_E2, i2 - ROUTE_LOGIT0,
                             jnp.where(lane == RT_G1, p1 * g_gate,
                                       jnp.where(lane == RT_G2, p2 * g_gate, 0.0))))
    rt_ref[...] = rt


def _merge_call(x2, ya, yb, yc, yd, g_mix, w_gate, w_branch, w_out, g_ffn, w_route, b_route):
    s = x2.shape[0]
    row = lambda w: pl.BlockSpec((M_TM, w), lambda i: (i, 0))
    vm = pl.BlockSpec(memory_space=pltpu.VMEM)
    return pl.pallas_call(
        _merge_kernel,
        out_shape=(jax.ShapeDtypeStruct((s, D_MODEL), F32), jax.ShapeDtypeStruct((s, D_MODEL), F32),
                   jax.ShapeDtypeStruct((s, LANES), F32)),
        grid=(s // M_TM,),
        in_specs=[row(D_MODEL), row(256), row(256), row(256), row(256),
                  vm, vm, vm, vm, vm, vm, vm],
        out_specs=(row(D_MODEL), row(D_MODEL), row(LANES)),
        compiler_params=_cparams(("parallel",)),
        name="merge_route",
    )(x2, ya, yb, yc, yd, g_mix.reshape(1, D_MODEL), w_gate, w_branch, w_out,
      g_ffn.reshape(1, D_MODEL), w_route, b_route)


E_BM = 128


def _gather_rows(src_hbm, idx_ref, dst, sem, n, wait, inline=False):
    def one(r, c):
        cp = pltpu.make_async_copy(src_hbm.at[pl.ds(idx_ref[0, 0, r], 1)], dst.at[pl.ds(r, 1)], sem)
        if wait:
            cp.wait()
        else:
            cp.start()
        return c
    if inline:
        for r in range(n):
            one(r, 0)
    else:
        lax.fori_loop(0, n, one, 0, unroll=8)


def _moe_kernel(be_ref, nused_ref, tok_ref, tok_next_ref, h2_hbm, w1_ref, w3_ref, w2_ref, o_ref,
                xbuf, w1b, w3b, w2b, sem):
    i = pl.program_id(0)
    nused = nused_ref[0]
    used = i < nused
    slot = i % 2

    @pl.when((i == 0) & used)
    def _():
        _gather_rows(h2_hbm, tok_ref, xbuf.at[0], sem.at[0], E_BM, wait=False)

    @pl.when(used)
    def _():
        prev = be_ref[jnp.maximum(i - 1, 0)]

        @pl.when((i == 0) | (be_ref[i] != prev))
        def _():
            w1b[...] = w1_ref[0, 0].astype(BF16)
            w3b[...] = w3_ref[0, 0].astype(BF16)
            w2b[...] = w2_ref[0, 0].astype(BF16)

        _gather_rows(h2_hbm, tok_ref, xbuf.at[slot], sem.at[slot], E_BM, wait=True)
        _gather_rows(h2_hbm, tok_next_ref, xbuf.at[1 - slot], sem.at[1 - slot], E_BM, wait=False, inline=True)
        xb = xbuf[slot].astype(BF16)
        a = jnp.dot(xb, w1b[...], preferred_element_type=F32)
        g = jnp.dot(xb, w3b[...], preferred_element_type=F32)
        hid = (a * jax.nn.sigmoid(a) * g).astype(BF16)
        o_ref[...] = jnp.dot(hid, w2b[...], preferred_element_type=F32)

    @pl.when(jnp.logical_not(used))
    def _():
        @pl.when(i == nused)
        def _():
            _gather_rows(h2_hbm, tok_ref, xbuf.at[slot], sem.at[slot], E_BM, wait=True)
        o_ref[...] = jnp.zeros(o_ref.shape, F32)


def _moe_call(h2, row_tok, block_e, nused, w1, w3, w2, layer):
    n_rows = row_tok.shape[0]
    n_blocks = n_rows // E_BM
    wspec = lambda shp: pl.BlockSpec((1, 1) + shp, lambda i, be, nu: (layer, be[i], 0, 0))
    grid_spec = pltpu.PrefetchScalarGridSpec(
        num_scalar_prefetch=2,
        grid=(n_blocks,),
        in_specs=[pl.BlockSpec((1, 1, E_BM), lambda i, be, nu: (i, 0, 0), memory_space=pltpu.SMEM),
                  pl.BlockSpec((1, 1, E_BM), lambda i, be, nu: (jnp.minimum(i + 1, n_blocks - 1), 0, 0),
                               memory_space=pltpu.SMEM),
                  pl.BlockSpec(memory_space=pl.ANY),
                  wspec((D_MODEL, F_EXPERT)), wspec((D_MODEL, F_EXPERT)), wspec((F_EXPERT, D_MODEL))],
        out_specs=pl.BlockSpec((E_BM, D_MODEL), lambda i, be, nu: (i, 0)),
        scratch_shapes=[pltpu.VMEM((2, E_BM, D_MODEL), F32),
                        pltpu.VMEM((D_MODEL, F_EXPERT), BF16),
                        pltpu.VMEM((D_MODEL, F_EXPERT), BF16),
                        pltpu.VMEM((F_EXPERT, D_MODEL), BF16),
                        pltpu.SemaphoreType.DMA((2,))],
    )
    tok3 = row_tok.reshape(n_blocks, 1, E_BM)
    return pl.pallas_call(
        _moe_kernel,
        out_shape=jax.ShapeDtypeStruct((n_rows, D_MODEL), F32),
        grid_spec=grid_spec,
        compiler_params=_cparams(("arbitrary",)),
        name="moe_experts",
    )(block_e, nused, tok3, tok3, h2, w1, w3, w2)


CB_TM = 128


def _combine_kernel(final, pos_ref, pos_next_ref, x_ref, rt_ref, ys_hbm, gfin_ref, o_ref, buf, sem):
    i = pl.program_id(0)
    slot = i % 2
    n = 2 * CB_TM

    @pl.when(i == 0)
    def _():
        _gather_rows(ys_hbm, pos_ref, buf.at[0], sem.at[0], n, wait=False)

    @pl.when(i + 1 < pl.num_programs(0))
    def _():
        _gather_rows(ys_hbm, pos_next_ref, buf.at[1 - slot], sem.at[1 - slot], n, wait=False)
    _gather_rows(ys_hbm, pos_ref, buf.at[slot], sem.at[slot], n, wait=True)

    rt = rt_ref[...]
    y = (buf[slot, 0:CB_TM, :] * rt[:, RT_G1:RT_G1 + 1]
         + buf[slot, CB_TM:2 * CB_TM, :] * rt[:, RT_G2:RT_G2 + 1])
    xn = x_ref[...] + y
    if final:
        xn = _rms(xn, gfin_ref[...])
    o_ref[...] = xn


def _combine_call(x2, rt, ys, pos, g_final, final):
    s = x2.shape[0]
    nb = s // CB_TM
    pos_b = pos.reshape(nb, CB_TM, 2).transpose(0, 2, 1).reshape(nb, 1, 2 * CB_TM)
    return pl.pallas_call(
        functools.partial(_combine_kernel, final),
        out_shape=jax.ShapeDtypeStruct((s, D_MODEL), F32),
        grid=(nb,),
        in_specs=[pl.BlockSpec((1, 1, 2 * CB_TM), lambda i: (i, 0, 0), memory_space=pltpu.SMEM),
                  pl.BlockSpec((1, 1, 2 * CB_TM), lambda i: (jnp.minimum(i + 1, nb - 1), 0, 0),
                               memory_space=pltpu.SMEM),
                  pl.BlockSpec((CB_TM, D_MODEL), lambda i: (i, 0)),
                  pl.BlockSpec((CB_TM, LANES), lambda i: (i, 0)),
                  pl.BlockSpec(memory_space=pl.ANY),
                  _whole((1, D_MODEL))],
        out_specs=pl.BlockSpec((CB_TM, D_MODEL), lambda i: (i, 0)),
        scratch_shapes=[pltpu.VMEM((2, 2 * CB_TM, D_MODEL), F32), pltpu.SemaphoreType.DMA((2,))],
        compiler_params=_cparams(("arbitrary",)),
        name="moe_combine",
    )(pos_b, pos_b, x2, rt, ys, g_final.reshape(1, D_MODEL))


PLAN_BLOCK = 256


def _dispatch_plan(rt):
    t = rt.shape[0]
    n_assign = t * TOP_K_INNER
    eid = rt[:, RT_E1:RT_E2 + 1].astype(I32).reshape(n_assign)
    onehot = (eid[:, None] == jnp.arange(N_EXPERTS, dtype=I32)[None, :]).astype(F32)
    pb = PLAN_BLOCK
    oh3 = onehot.reshape(n_assign // pb, pb, N_EXPERTS)
    tri = jnp.tril(jnp.ones((pb, pb), F32), k=-1)
    within = jnp.einsum('ij,bjk->bik', tri, oh3)
    block_tot = jnp.sum(oh3, axis=1)
    block_off = jnp.cumsum(block_tot, axis=0) - block_tot
    counts = (block_off[-1] + block_tot[-1]).astype(I32)
    rank = jnp.sum(oh3 * (within + block_off[:, None, :]), axis=-1).reshape(n_assign)
    padded = (counts + E_BM - 1) // E_BM * E_BM
    ends_pad = jnp.cumsum(padded)
    starts_pad = ends_pad - padded
    dest = (jnp.sum(onehot * starts_pad.astype(F32)[None, :], axis=-1) + rank).astype(I32)
    n_rows = (-(-(n_assign + N_EXPERTS * (E_BM - 1)) // E_BM) + 1) * E_BM
    n_blocks = n_rows // E_BM
    tok = jnp.repeat(jnp.arange(t, dtype=I32), TOP_K_INNER)
    row_tok = jnp.zeros((n_rows,), I32).at[dest].set(tok)
    block_start = jnp.arange(n_blocks, dtype=I32) * E_BM
    block_e = jnp.minimum(jnp.sum((ends_pad[None, :] <= block_start[:, None]).astype(I32), axis=1),
                          N_EXPERTS - 1)
    nused = (ends_pad[-1] // E_BM).astype(I32).reshape(1)
    return row_tok, block_e, nused, dest.reshape(t, TOP_K_INNER)


def _layer(x2, layer_idx, p, experts, final_g, final):
    (norm_mix_g, w_in, conv_w, conv_b, i_bias, f_bias, mnorm_g, rel_bias, lam_p, dnorm_g,
     w_branch, w_gate, w_out, norm_ffn_g, rgw, rgb, rew, reb) = p
    o = _k1_call(x2, norm_mix_g, *_rearrange_w_in(w_in))

    ya = _dsa_call(o["qaT"], o["qiT"], o["wiT"], o["ka"], o["vaT"], o["misc_bf"])
    yb = _mlstm_call(o["qkb"], o["vb"], o["ob"], o["misc"], conv_w, conv_b, i_bias, f_bias, mnorm_g)
    kcT_pad = jnp.pad(o["kcT"], ((0, 0), (C_PAD, 0)))
    vc_pad = jnp.pad(o["vc"], ((C_PAD, 0), (0, 0)))
    yc = _band_call(o["qc"], kcT_pad, vc_pad, _band_bias_line(rel_bias))
    lam_init = 0.8 - 0.6 * math.exp(-0.3 * layer_idx)
    yd = _diff_call(o["qdT"], o["kd"], o["vdT"], lam_p, dnorm_g, lam_init)

    w_route = jnp.concatenate([rgw, rew, jnp.zeros((D_MODEL, LANES - N_GROUPS - N_EXPERTS), F32)],
                              axis=1).astype(BF16)
    b_route = jnp.concatenate([rgb, reb, jnp.zeros((LANES - N_GROUPS - N_EXPERTS,), F32)]).reshape(1, LANES)
    xn, h2, rt = _merge_call(x2, ya, yb, yc, yd, norm_mix_g, w_gate.astype(BF16), w_branch.astype(BF16),
                             w_out.astype(BF16), norm_ffn_g, w_route, b_route)
    row_tok, block_e, nused, pos = _dispatch_plan(rt)
    ys = _moe_call(h2, row_tok, block_e, nused, *experts, layer_idx)
    return _combine_call(xn, rt, ys, pos, final_g, final)


def kernel(x, norm_mix_g, w_in, conv_w, conv_b, mlstm_i_bias, mlstm_f_bias, mlstm_norm_g, relpos_bias, diff_lambda, diff_norm_g, w_branch, w_gate, w_out, norm_ffn_g, router_group_w, router_group_b, router_expert_w, router_expert_b, expert_w1, expert_w3, expert_w2, final_norm_g):
    assert x.shape[0] == 1 and x.shape[2] == D_MODEL
    params = (norm_mix_g, w_in, conv_w, conv_b, mlstm_i_bias, mlstm_f_bias, mlstm_norm_g, relpos_bias,
              diff_lambda, diff_norm_g, w_branch, w_gate, w_out, norm_ffn_g, router_group_w,
              router_group_b, router_expert_w, router_expert_b)
    experts = (expert_w1, expert_w3, expert_w2)
    depth = norm_mix_g.shape[0]
    x2 = x[0]
    for l in range(depth):
        x2 = _layer(x2, l, tuple(a[l] for a in params), experts, final_norm_g, l == depth - 1)
    return x2[None]
```

```python
import functools
import math

import jax
import jax.numpy as jnp
from jax import lax
from jax.experimental import pallas as pl
from jax.experimental.pallas import tpu as pltpu

F32 = jnp.float32
BF16 = jnp.bfloat16
I32 = jnp.int32

D_MODEL = 1024
CHUNK = 64
HEAD_DIM = 64
NEG_INF = -1e30
H_A = 4
H_IDX = 4
D_IDX = 32
TOPK_MAX = 256
H_B = 4
CONV_K = 4
H_C = 4
N_PREV_CHUNKS = 8
MAX_REL_PAST = 128
H_D = 4
DQ_D = 32
W_BRANCH = 256
N_BRANCH = 4
N_GROUPS = 4
EXPERTS_PER_GROUP = 8
N_EXPERTS = 32
TOP_K_INNER = 2
F_EXPERT = 512
EPS = 1e-6

VMEM_LIMIT_BYTES = 52 * 1024 * 1024
LANES = 128

INT_MIN = -(2 ** 31)
I16_MIN = -(2 ** 15)
I16 = jnp.int16
M_INIT = -5e29

_COL_SIZES = (256, 256, 256, 128, 32, 4, 256, 256, 256, 4, 4, 256, 256, 256, 256, 256, 256, 256)
_COL_NAMES = ("qa", "ka", "va", "qi", "ki", "wi", "qb", "kb", "vb", "ib", "fb", "ob",
              "qc", "kc", "vc", "qd", "kd", "vd")
_COL_OFF = {}
_o = 0
for _n, _s in zip(_COL_NAMES, _COL_SIZES):
    _COL_OFF[_n] = (_o, _s)
    _o += _s
C_IN = _o
MISC_KI = 0
MISC_WI = 32
MISC_IB = 36
MISC_FB = 40


def _cparams(sem):
    return pltpu.CompilerParams(dimension_semantics=sem, vmem_limit_bytes=VMEM_LIMIT_BYTES)


def _whole(shape):
    nd = len(shape)
    return pl.BlockSpec(shape, lambda *_: (0,) * nd)


_K1_NAT = (("ka", 256, BF16), ("misc", 128, F32), ("qkb", 512, F32), ("vb", 256, F32), ("ob", 256, F32),
           ("qc", 256, BF16), ("vc", 256, BF16), ("kd", 256, BF16))
_K1_TR = (("qaT", 256, BF16), ("qiT", 128, BF16), ("vaT", 256, BF16), ("kcT", 256, BF16),
          ("qdT", 256, BF16), ("vdT", 256, BF16), ("wiT", 8, F32))
K1_NAT_WIDTH = sum(w for _, w, _ in _K1_NAT)
K1_TR_WIDTH = sum(w for _, w, _ in _K1_TR)
K1_TM = 512
V_AUG = 80
_K1_VAUG = ("vaT", "vdT")


def _k1_tr_rows(name, width):
    return (width // HEAD_DIM) * V_AUG if name in _K1_VAUG else width


def _rearrange_w_in(w_in):
    def cols(name):
        o, s = _COL_OFF[name]
        return w_in[:, o:o + s]
    d = w_in.shape[0]
    misc = jnp.concatenate([cols("ki"), cols("wi"), cols("ib"), cols("fb"),
                            jnp.zeros((d, LANES - 44), w_in.dtype)], axis=1)
    nat = [cols("ka"), misc, cols("qb"), cols("kb"), cols("vb"), cols("ob"), cols("qc"), cols("vc"),
           cols("kd")]
    tr = [cols("qa"), cols("qi"), cols("va"), cols("kc"), cols("qd"), cols("vd"), cols("wi"),
          jnp.zeros((d, 8 - H_IDX), w_in.dtype)]
    return (jnp.concatenate(nat, axis=1).astype(BF16), jnp.concatenate(tr, axis=1).T.astype(BF16))


def _rms(x, g):
    ms = jnp.mean(x * x, axis=-1, keepdims=True)
    return x * lax.rsqrt(ms + EPS) * g


def _k1_kernel(x_ref, g_ref, wn_ref, wt_ref, *out_refs):
    h = _rms(x_ref[...], g_ref[...]).astype(BF16)
    refs = dict(zip([n for n, _, _ in _K1_NAT] + ["misc_bf"] + [n for n, _, _ in _K1_TR], out_refs))
    off = 0
    for name, width, _ in _K1_NAT:
        r = jnp.dot(h, wn_ref[:, off:off + width], preferred_element_type=F32)
        refs[name][...] = r.astype(refs[name].dtype)
        if name == "misc":
            refs["misc_bf"][...] = r.astype(BF16)
        off += width
    off = 0
    for name, width, _ in _K1_TR:
        r = lax.dot_general(wt_ref[off:off + width, :], h, (((1,), (1,)), ((), ())),
                            preferred_element_type=F32)
        if name in _K1_VAUG:
            tail = (lax.broadcasted_iota(I32, (V_AUG - HEAD_DIM, r.shape[1]), 0) == 0)
            tail = jnp.where(tail, 1.0, 0.0).astype(BF16)
            for hh in range(width // HEAD_DIM):
                refs[name][hh * V_AUG:hh * V_AUG + HEAD_DIM, :] = \
                    r[hh * HEAD_DIM:(hh + 1) * HEAD_DIM, :].astype(BF16)
                refs[name][hh * V_AUG + HEAD_DIM:(hh + 1) * V_AUG, :] = tail
        else:
            refs[name][...] = r.astype(refs[name].dtype)
        off += width


def _k1_call(x2, g, w_nat, w_trT):
    s = x2.shape[0]
    tm = K1_TM
    outs = (tuple(jax.ShapeDtypeStruct((s, w), dt) for _, w, dt in _K1_NAT)
            + (jax.ShapeDtypeStruct((s, LANES), BF16),)
            + tuple(jax.ShapeDtypeStruct((_k1_tr_rows(n, w), s), dt) for n, w, dt in _K1_TR))
    out_specs = (tuple(pl.BlockSpec((tm, w), lambda i: (i, 0)) for _, w, _ in _K1_NAT)
                 + (pl.BlockSpec((tm, LANES), lambda i: (i, 0)),)
                 + tuple(pl.BlockSpec((_k1_tr_rows(n, w), tm), lambda i: (0, i)) for n, w, _ in _K1_TR))
    res = pl.pallas_call(
        _k1_kernel,
        out_shape=outs,
        grid=(s // tm,),
        in_specs=[pl.BlockSpec((tm, D_MODEL), lambda i: (i, 0)),
                  _whole((1, D_MODEL)),
                  _whole((D_MODEL, K1_NAT_WIDTH)),
                  _whole((K1_TR_WIDTH, D_MODEL))],
        out_specs=out_specs,
        compiler_params=_cparams(("parallel",)),
        name="k1_norm_proj",
    )(x2, g.reshape(1, D_MODEL), w_nat, w_trT)
    return dict(zip([n for n, _, _ in _K1_NAT] + ["misc_bf"] + [n for n, _, _ in _K1_TR], res))


A_TQ = 256
A_KT = 512
A_SLAB = 256
A_SWEEP_SLABS = 4
LOG2E = 1.4426950408889634
QK_AHEAD = 4
BF16_ROWS = 16


def _flash_step(s, i, v_t, m_ref, acc_ref):
    kt, tq = s.shape
    parts = s.reshape(kt // BF16_ROWS, BF16_ROWS, tq)
    parts = [parts[g] for g in range(kt // BF16_ROWS)]
    while len(parts) > 1:
        parts = [jnp.maximum(parts[g], parts[g + 1]) for g in range(0, len(parts), 2)]
    m_old = m_ref[i]
    m_new = jnp.maximum(m_old, jnp.max(parts[0].astype(F32), axis=0, keepdims=True))
    alpha = jnp.exp2(m_old - m_new)
    p = jnp.exp2(s - m_new.astype(BF16))
    acc_ref[i] = alpha * acc_ref[i] + jnp.dot(v_t, p, preferred_element_type=F32)
    m_ref[i] = m_new


def _flash_result(acc_ref, i):
    return acc_ref[i, 0:HEAD_DIM, :] / acc_ref[i, HEAD_DIM:HEAD_DIM + 1, :]


def _logits(k_t, q_pad):
    return jnp.dot(k_t, q_pad, preferred_element_type=F32).astype(BF16)


def _qk_prologue(k_t, qpad_ref, n, s_ref):
    for i in range(min(QK_AHEAD, n)):
        s_ref[i] = _logits(k_t, qpad_ref[i])


def _flash_tile(k_t, qpad_ref, n, v_tile, bias, m_ref, acc_ref, s_ref=None, k_next=None):
    a = min(QK_AHEAD, n)
    if s_ref is None:
        pend = [_logits(k_t, qpad_ref[i]) for i in range(a)]
    else:
        pend = [s_ref[i] for i in range(a)]
    for i in range(n):
        s = pend.pop(0)
        if i + a < n:
            pend.append(_logits(k_t, qpad_ref[i + a]))
        elif s_ref is not None:
            s_ref[i + a - n] = _logits(k_next, qpad_ref[i + a - n])
        if bias is not None:
            s = s + bias
        _flash_step(s, i, v_tile(i), m_ref, acc_ref)


def _bit_transpose32(words):
    a = list(words)
    j, m = 16, 0x0000FFFF
    while j:
        k = 0
        while k < 32:
            t = (a[k] ^ lax.shift_right_logical(a[k + j], j)) & m
            a[k] = a[k] ^ t
            a[k + j] = a[k + j] ^ lax.shift_left(t, j)
            k = (k + j + 1) & ~j
        j >>= 1
        m = (m ^ (m << j)) & 0xFFFFFFFF
    return a


def _dsa_kernel(topk, qaT_ref, qiT_ref, wiT_ref, ka_ref, vaT_ref, mb_ref, tri_ref, o_ref,
                planes_ref, cand_ref, above_ref, qpad_ref, qipad_ref, acc_ref, m_ref, carry_ref):
    tq, kt = A_TQ, A_KT
    b = pl.program_id(0)
    ntiles = ((b + 1) * tq + kt - 1) // kt
    q_pos = b * tq + lax.broadcasted_iota(I32, (1, tq), 1)
    vis_end = (q_pos // CHUNK + 1) * CHUNK

    qiT = qiT_ref[...]
    for h in range(H_IDX):
        qipad_ref[h, 0:D_IDX, :] = qiT[h * D_IDX:(h + 1) * D_IDX, :]
        qipad_ref[h, D_IDX:LANES, :] = jnp.zeros((LANES - D_IDX, tq), BF16)

    def p1(j, masked):
        s0 = pl.multiple_of(j * kt, kt)
        mb = mb_ref[pl.ds(s0, kt), :]
        score = jnp.zeros((kt, tq), F32)
        for h in range(H_IDX):
            r = jnp.dot(mb, qipad_ref[h], preferred_element_type=F32)
            score = score + jnp.maximum(r, 0.0) * wiT_ref[h:h + 1, :]
        bits = lax.bitcast_convert_type(score, I32)
        ukey = bits ^ (lax.shift_right_arithmetic(bits, 31) | INT_MIN)
        if masked:
            s_pos = s0 + lax.broadcasted_iota(I32, (kt, 1), 0)
            ukey = jnp.where(s_pos < vis_end, ukey, 0)
        u4 = ukey.reshape(kt // A_SLAB, 32, 8, tq)
        for s2 in range(kt // A_SLAB):
            planes = _bit_transpose32([u4[s2, v] for v in range(32)])
            for r in range(32):
                planes_ref[j * (kt // A_SLAB) + s2, r] = planes[r]

    nfull = (b * tq) // kt

    def p1_full(j, carry):
        p1(j, False)
        return carry
    lax.fori_loop(0, nfull, p1_full, 0)

    def p1_masked(j, carry):
        p1(j, True)
        return carry
    lax.fori_loop(nfull, ntiles, p1_masked, 0)

    kf = float(topk)
    nslab = ntiles * (kt // A_SLAB)
    nstep = (nslab + A_SWEEP_SLABS - 1) // A_SWEEP_SLABS

    def init_sets(sl, carry):
        cand_ref[sl] = jnp.full((8, tq), -1, I32)
        above_ref[sl] = jnp.zeros((8, tq), I32)
        return carry
    lax.fori_loop(0, nslab, init_sets, 0)

    def init_pad(sl, carry):
        cand_ref[sl] = jnp.zeros((8, tq), I32)
        above_ref[sl] = jnp.zeros((8, tq), I32)
        planes_ref[sl] = jnp.zeros((32, 8, tq), I32)
        return carry
    lax.fori_loop(nslab, nstep * A_SWEEP_SLABS, init_pad, 0)

    def apply_decision(sl, prev_plane, took_one):
        cand = cand_ref[sl]
        ones = cand & prev_plane
        cand = jnp.where(took_one, ones, cand ^ ones)
        above_ref[sl] = jnp.where(took_one, above_ref[sl], above_ref[sl] | ones)
        cand_ref[sl] = cand
        return cand

    def sweep(i, carry):
        n_above, took, tau = carry
        took_one = jnp.broadcast_to(took, (8, tq)) != 0
        first = i == 0

        def step(jj, acc):
            for s2 in range(A_SWEEP_SLABS):
                sl = jj * A_SWEEP_SLABS + s2
                prev_plane = jnp.where(first, -1, planes_ref[sl, jnp.maximum(i - 1, 0)])
                cand = apply_decision(sl, prev_plane, took_one)
                acc = acc + lax.population_count(cand & planes_ref[sl, i])
            return acc
        acc = lax.fori_loop(0, nstep, step, jnp.zeros((8, tq), I32))
        n_one = jnp.sum(acc.astype(F32), axis=0, keepdims=True)
        take = (n_above + n_one) >= kf
        n_above = jnp.where(take, n_above, n_above + n_one)
        tau = jnp.where(take, tau | lax.shift_left(jnp.int32(1), 31 - i), tau)
        return n_above, jnp.where(take, 1, 0), tau
    n_above, took, tau = lax.fori_loop(
        0, 32, sweep, (jnp.zeros((1, tq), F32), jnp.ones((1, tq), I32), jnp.zeros((1, tq), I32)))

    def last_decision(sl, carry):
        apply_decision(sl, planes_ref[sl, 31], jnp.broadcast_to(took, (8, tq)) != 0)
        return carry
    lax.fori_loop(0, nslab, last_decision, 0)
    need = jnp.where(tau == 0, 0.0, kf - n_above)

    m_ref[...] = jnp.full(m_ref.shape, M_INIT, F32)
    acc_ref[...] = jnp.zeros(acc_ref.shape, F32)
    carry_ref[...] = jnp.zeros(carry_ref.shape, F32)
    qaT = (qaT_ref[...].astype(F32) * (HEAD_DIM ** -0.5 * LOG2E)).astype(BF16)
    row = lax.broadcasted_iota(I32, qaT.shape, 0)
    for h in range(H_A):
        qpad_ref[h] = jnp.where((row >= h * HEAD_DIM) & (row < (h + 1) * HEAD_DIM), qaT,
                                jnp.zeros_like(qaT))

    def slab_rows(ref, sl):
        word = ref[sl]
        return jnp.concatenate([lax.shift_right_logical(word, 31 - v) & 1 for v in range(32)], axis=0)

    def p3(j, carry):
        s0 = pl.multiple_of(j * kt, kt)
        slabs = [j * (kt // A_SLAB) + s2 for s2 in range(kt // A_SLAB)]
        gt = jnp.concatenate([slab_rows(above_ref, sl) for sl in slabs], axis=0) != 0
        eq_i = jnp.concatenate([slab_rows(cand_ref, sl) for sl in slabs], axis=0)
        eq = eq_i != 0
        eqf = eq_i.astype(F32)
        pref = jnp.dot(tri_ref[...], eqf.astype(BF16), preferred_element_type=F32)
        seen = carry_ref[...]
        sel = gt | (eq & (pref + seen < need))
        bias = jnp.where(sel, 0.0, NEG_INF).astype(BF16)
        carry_ref[...] = seen + pref[kt - 1:kt, :] + eqf[kt - 1:kt, :]
        k_t = ka_ref[pl.ds(s0, kt), :]
        _flash_tile(k_t, qpad_ref, H_A, lambda i: vaT_ref[i * V_AUG:(i + 1) * V_AUG, pl.ds(s0, kt)],
                    bias, m_ref, acc_ref)
        return carry
    lax.fori_loop(0, ntiles, p3, 0)

    ys = [_flash_result(acc_ref, h) for h in range(H_A)]
    o_ref[...] = jnp.concatenate(ys, axis=0).T.astype(o_ref.dtype)


def _dsa_call(qaT, qiT, wiT, ka, vaT, misc_bf):
    s = ka.shape[0]
    topk = min(TOPK_MAX, s // 4)
    tri = jnp.tril(jnp.ones((A_KT, A_KT), F32), k=-1).astype(BF16)
    vm = pl.BlockSpec(memory_space=pltpu.VMEM)
    return pl.pallas_call(
        functools.partial(_dsa_kernel, topk),
        out_shape=jax.ShapeDtypeStruct((s, W_BRANCH), BF16),
        grid=(s // A_TQ,),
        in_specs=[pl.BlockSpec((256, A_TQ), lambda i: (0, i)),
                  pl.BlockSpec((LANES, A_TQ), lambda i: (0, i)),
                  pl.BlockSpec((8, A_TQ), lambda i: (0, i)),
                  vm, vm, vm, vm],
        out_specs=pl.BlockSpec((A_TQ, 256), lambda i: (i, 0)),
        scratch_shapes=[pltpu.VMEM((s // A_SLAB, 32, 8, A_TQ), I32),
                        pltpu.VMEM((s // A_SLAB, 8, A_TQ), I32),
                        pltpu.VMEM((s // A_SLAB, 8, A_TQ), I32),
                        pltpu.VMEM((H_A, 256, A_TQ), BF16),
                        pltpu.VMEM((H_IDX, LANES, A_TQ), BF16),
                        pltpu.VMEM((H_A, V_AUG, A_TQ), F32),
                        pltpu.VMEM((H_A, 1, A_TQ), F32),
                        pltpu.VMEM((1, A_TQ), F32)],
        compiler_params=_cparams(("arbitrary",)),
        name="dsa_mixer",
    )(qaT, qiT, wiT, ka, vaT, misc_bf, tri)


def _pair_select(lo, hi):
    lane = lax.broadcasted_iota(I32, lo.shape, 1)
    return jnp.where(lane < HEAD_DIM, lo, hi)


def _pair_head_rms(o, g):
    lane = lax.broadcasted_iota(I32, o.shape, 1)
    low = lane < HEAD_DIM
    sq = o * o
    ms_lo = jnp.sum(jnp.where(low, sq, 0.0), axis=1, keepdims=True) * (1.0 / HEAD_DIM)
    ms_hi = jnp.sum(jnp.where(low, 0.0, sq), axis=1, keepdims=True) * (1.0 / HEAD_DIM)
    ms = jnp.where(low, ms_lo, ms_hi)
    return o * lax.rsqrt(ms + EPS) * g


D_TQ = 256
D_KT = 1024


def _diff_kernel(lam_init, qT_ref, kd_ref, vT_ref, lam_ref, g_ref, o_ref, qpad_ref, acc_ref, m_ref, s_ref):
    tq = D_TQ
    b = pl.program_id(0)
    q_pos = b * tq + lax.broadcasted_iota(I32, (1, tq), 1)
    vis_end = (q_pos // CHUNK + 1) * CHUNK
    m_ref[...] = jnp.full(m_ref.shape, M_INIT, F32)
    acc_ref[...] = jnp.zeros(acc_ref.shape, F32)
    qT = (qT_ref[...].astype(F32) * (DQ_D ** -0.5 * LOG2E)).astype(BF16)
    row = lax.broadcasted_iota(I32, qT.shape, 0)
    for i in range(2 * H_D):
        qpad_ref[i] = jnp.where((row >= i * DQ_D) & (row < (i + 1) * DQ_D), qT, jnp.zeros_like(qT))

    def tile(s0, kt, masked, s_ref=None, s0_next=None):
        k_t = kd_ref[pl.ds(s0, kt), :]
        k_next = None if s0_next is None else kd_ref[pl.ds(s0_next, kt), :]
        bias = None
        if masked:
            vis = (s0 + lax.broadcasted_iota(I32, (kt, 1), 0)) < vis_end
            bias = jnp.where(vis, 0.0, NEG_INF).astype(BF16)
        _flash_tile(k_t, qpad_ref, 2 * H_D, lambda i: vT_ref[(i // 2) * V_AUG:(i // 2 + 1) * V_AUG, pl.ds(s0, kt)],
                    bias, m_ref, acc_ref, s_ref, k_next)

    n_big = (b * tq) // D_KT

    @pl.when(n_big > 0)
    def _():
        _qk_prologue(kd_ref[pl.ds(0, D_KT), :], qpad_ref, 2 * H_D, s_ref)

    def full_tile(j, carry):
        j_next = jnp.minimum(j + 1, n_big - 1)
        tile(pl.multiple_of(j * D_KT, D_KT), D_KT, False, s_ref, pl.multiple_of(j_next * D_KT, D_KT))
        return carry
    lax.fori_loop(0, n_big, full_tile, 0)

    def small_tile(j, carry):
        tile(pl.multiple_of(j * tq, tq), tq, False)
        return carry
    lax.fori_loop(n_big * (D_KT // tq), b, small_tile, 0)
    tile(pl.multiple_of(b * tq, tq), tq, True)

    lp = lam_ref[...]
    lam = (jnp.exp(jnp.sum(lp[0:1] * lp[1:2], axis=1, keepdims=True))
           - jnp.exp(jnp.sum(lp[2:3] * lp[3:4], axis=1, keepdims=True)) + lam_init)
    ys = []
    for h in range(H_D):
        o = _flash_result(acc_ref, 2 * h) - lam * _flash_result(acc_ref, 2 * h + 1)
        ms = jnp.mean(o * o, axis=0, keepdims=True)
        ys.append(o * lax.rsqrt(ms + EPS) * g_ref[h * HEAD_DIM:(h + 1) * HEAD_DIM, :] * (1.0 - lam_init))
    o_ref[...] = jnp.concatenate(ys, axis=0).T.astype(o_ref.dtype)


def _diff_call(qdT, kd, vdT, lam_p, dnorm_g, lam_init):
    s = kd.shape[0]
    return pl.pallas_call(
        functools.partial(_diff_kernel, lam_init),
        out_shape=jax.ShapeDtypeStruct((s, W_BRANCH), BF16),
        grid=(s // D_TQ,),
        in_specs=[pl.BlockSpec((256, D_TQ), lambda i: (0, i)),
                  pl.BlockSpec(memory_space=pltpu.VMEM),
                  pl.BlockSpec(memory_space=pltpu.VMEM),
                  _whole((4, DQ_D)),
                  _whole((256, 1))],
        out_specs=pl.BlockSpec((D_TQ, 256), lambda i: (i, 0)),
        scratch_shapes=[pltpu.VMEM((2 * H_D, 256, D_TQ), BF16),
                        pltpu.VMEM((2 * H_D, V_AUG, D_TQ), F32),
                        pltpu.VMEM((2 * H_D, 1, D_TQ), F32),
                        pltpu.VMEM((QK_AHEAD, D_KT, D_TQ), BF16)],
        compiler_params=_cparams(("parallel",)),
        name="diff_attention",
    )(qdT, kd, vdT, lam_p, dnorm_g.reshape(256, 1))


C_TQ = 128
C_PAD = N_PREV_CHUNKS * CHUNK
C_WIN = C_PAD + C_TQ
C_LINE = C_TQ + C_WIN


def _band_bias_line(rel_bias):
    dist_desc = jnp.arange(C_TQ - 1 + C_PAD, C_PAD - C_WIN, -1)
    line = rel_bias[:, jnp.clip(dist_desc, -(CHUNK - 1), MAX_REL_PAST) + (CHUNK - 1)].astype(F32)
    return jnp.pad(line, ((0, 0), (0, C_LINE - line.shape[1])))


def _band_table_init(line_ref, bias_ref):
    tq = C_TQ

    @pl.when(pl.program_id(0) == 0)
    def _():
        i = lax.broadcasted_iota(I32, (tq, C_WIN), 0)
        w = lax.broadcasted_iota(I32, (tq, C_WIN), 1)
        in_band = (w // CHUNK >= i // CHUNK) & (w // CHUNK <= i // CHUNK + N_PREV_CHUNKS)
        for h in range(H_C):
            x = jnp.broadcast_to(line_ref[h:h + 1, :], (tq, C_LINE))
            y = pltpu.roll(x, 1, 1, stride=1, stride_axis=0)
            bias_ref[h] = jnp.where(in_band, y[:, C_LINE - C_WIN:], NEG_INF)


def _band_body(qc_ref, kcT_ref, vc_ref, o_ref, bias_ref):
    tq = C_TQ
    b = pl.program_id(0)
    w0 = pl.multiple_of(b * tq, tq)
    q_all = qc_ref[...] * 0.125
    key_abs = b * tq - C_PAD + lax.broadcasted_iota(I32, (1, C_WIN), 1)
    ok = key_abs >= 0
    logits = [jnp.dot(q_all[:, h * HEAD_DIM:(h + 1) * HEAD_DIM],
                      kcT_ref[h * HEAD_DIM:(h + 1) * HEAD_DIM, pl.ds(w0, C_WIN)],
                      preferred_element_type=F32) for h in range(H_C)]
    heads = []
    for h in range(H_C):
        s = jnp.where(ok, logits[h] + bias_ref[h], NEG_INF)
        m = jnp.max(s, axis=1, keepdims=True)
        p = jnp.exp(s - m)
        l = jnp.sum(p, axis=1, keepdims=True)
        pv = jnp.dot(p.astype(BF16), vc_ref[pl.ds(w0, C_WIN), (h // 2) * LANES:(h // 2 + 1) * LANES],
                     preferred_element_type=F32)
        heads.append(pv / l)
    for pr in range(H_C // 2):
        o_ref[:, pr * LANES:(pr + 1) * LANES] = _pair_select(heads[2 * pr], heads[2 * pr + 1]).astype(o_ref.dtype)


def _band_kernel(qc_ref, kcT_ref, vc_ref, line_ref, o_ref, bias_ref):
    _band_table_init(line_ref, bias_ref)
    _band_body(qc_ref, kcT_ref, vc_ref, o_ref, bias_ref)


def _band_call(qc, kcT_pad, vc_pad, bias_line):
    s = qc.shape[0]
    return pl.pallas_call(
        _band_kernel,
        out_shape=jax.ShapeDtypeStruct((s, W_BRANCH), BF16),
        grid=(s // C_TQ,),
        in_specs=[pl.BlockSpec((C_TQ, 256), lambda i: (i, 0)),
                  pl.BlockSpec(memory_space=pltpu.VMEM),
                  pl.BlockSpec(memory_space=pltpu.VMEM),
                  _whole((H_C, C_LINE))],
        out_specs=pl.BlockSpec((C_TQ, 256), lambda i: (i, 0)),
        scratch_shapes=[pltpu.VMEM((H_C, C_TQ, C_WIN), F32)],
        compiler_params=_cparams(("arbitrary",)),
        name="band_attention",
    )(qc, kcT_pad, vc_pad, bias_line)


B_NCH = 2
B_L = B_NCH * CHUNK
HIGHEST = lax.Precision.HIGHEST


def _mlstm_state_init(tail_ref, ct_ref, n_ref, m_ref):
    @pl.when(pl.program_id(0) == 0)
    def _():
        tail_ref[...] = jnp.zeros(tail_ref.shape, F32)
        ct_ref[...] = jnp.zeros(ct_ref.shape, F32)
        n_ref[...] = jnp.zeros(n_ref.shape, F32)
        m_ref[...] = jnp.zeros(m_ref.shape, F32)


def _mlstm_kernel(qk_ref, vb_ref, ob_ref, misc_ref, cw_ref, cb_ref, gb_ref, ng_ref, o_ref,
                  tail_ref, ct_ref, n_ref, m_ref):
    _mlstm_state_init(tail_ref, ct_ref, n_ref, m_ref)
    _mlstm_body(qk_ref, vb_ref, ob_ref, misc_ref, cw_ref, cb_ref, gb_ref, ng_ref, o_ref,
                tail_ref, ct_ref, n_ref, m_ref)


def _mlstm_body(qk_ref, vb_ref, ob_ref, misc_ref, cw_ref, cb_ref, gb_ref, ng_ref, o_ref,
                tail_ref, ct_ref, n_ref, m_ref):
    L = CHUNK
    T = B_L
    hd = HEAD_DIM

    x = qk_ref[...]
    xx = jnp.concatenate([tail_ref[...], x], axis=0)
    y = jnp.broadcast_to(cb_ref[...], x.shape)
    for j in range(CONV_K):
        y = y + cw_ref[j:j + 1, :] * xx[8 - (CONV_K - 1) + j:8 - (CONV_K - 1) + j + T, :]
    tail_ref[...] = x[T - 8:T, :]
    qk = y * jax.nn.sigmoid(y)
    q_all = qk[:, :W_BRANCH]
    k_all = qk[:, W_BRANCH:] * 0.125
    v_all = vb_ref[...]
    o_gate = jax.nn.sigmoid(ob_ref[...])

    gts = misc_ref[...] + gb_ref[...]
    lf = jnp.minimum(gts, 0.0) - jnp.log1p(jnp.exp(-jnp.abs(gts)))
    r_t = lax.broadcasted_iota(I32, (T, T), 0)
    c_t = lax.broadcasted_iota(I32, (T, T), 1)
    ltri = jnp.where((c_t <= r_t) & (c_t // L == r_t // L), 1.0, 0.0)
    cum = jnp.dot(ltri, lf, precision=HIGHEST, preferred_element_type=F32)
    lane = lax.broadcasted_iota(I32, (T, LANES), 1)
    mixed = jnp.where(lane < MISC_FB, gts, cum)
    sel_r = lax.broadcasted_iota(I32, (8, LANES), 0)
    sel_c = lax.broadcasted_iota(I32, (8, LANES), 1)
    sel = jnp.where(sel_c == sel_r + MISC_IB, 1.0, 0.0)
    rows = lax.dot_general(sel, mixed, (((1,), (1,)), ((), ())), precision=HIGHEST,
                           preferred_element_type=F32)
    causal = lax.broadcasted_iota(I32, (L, L), 1) <= lax.broadcasted_iota(I32, (L, L), 0)

    heads = range(H_B)
    sl = lambda a, t0, h: a[t0:t0 + L, h * hd:(h + 1) * hd]
    pre = []
    for ci in range(B_NCH):
        t0 = ci * L
        per_head = []
        for h in heads:
            q, k, v = sl(q_all, t0, h), sl(k_all, t0, h), sl(v_all, t0, h)
            qb, kb, vbf = q.astype(BF16), k.astype(BF16), v.astype(BF16)
            qkt = lax.dot_general(qb, kb, (((1,), (1,)), ((), ())), preferred_element_type=F32)
            it_r = rows[h:h + 1, t0:t0 + L]
            cum_r = rows[H_B + h:H_B + h + 1, t0:t0 + L]
            it_c = gts[t0:t0 + L, MISC_IB + h:MISC_IB + h + 1]
            cum_c = cum[t0:t0 + L, MISC_FB + h:MISC_FB + h + 1]
            dmat = jnp.where(causal, cum_c - cum_r + it_r, NEG_INF)
            dmax = jnp.max(dmat, axis=1, keepdims=True)
            cum_end = cum_c[L - 1:L, :]
            g = cum_end - cum_c + it_c
            gmax = jnp.max(g, axis=0, keepdims=True)
            per_head.append((q, k, qb, vbf, qkt, dmat, dmax, cum_c, cum_end, g, gmax))
        pre.append(per_head)

    state = [(ct_ref[h], n_ref[h], m_ref[h]) for h in heads]
    out_chunks = []
    for ci in range(B_NCH):
        t0 = ci * L
        qc = [jnp.dot(pre[ci][h][2], state[h][0].astype(BF16), preferred_element_type=F32) for h in heads]
        mid = []
        for h in heads:
            q, k, qb, vbf, qkt, dmat, dmax, cum_c, cum_end, g, gmax = pre[ci][h]
            ct, n_row, m_prev = state[h]
            m_inter = cum_c + m_prev
            m_t = jnp.maximum(m_inter, dmax)
            w = jnp.exp(dmat - m_t) * qkt
            inter = jnp.exp(m_inter - m_t)
            m_new = jnp.maximum(cum_end + m_prev, gmax)
            carry_scale = jnp.exp(cum_end + m_prev - m_new)
            src_k = jnp.exp(g - m_new) * k
            mid.append((w, inter, m_t, m_new, carry_scale, src_k))
        wv = [jnp.dot(mid[h][0].astype(BF16), pre[ci][h][3], preferred_element_type=F32) for h in heads]
        upd = [lax.dot_general(mid[h][5].astype(BF16), pre[ci][h][3], (((0,), (0,)), ((), ())),
                               preferred_element_type=F32) for h in heads]
        outs = []
        for h in heads:
            q = pre[ci][h][0]
            w, inter, m_t, m_new, carry_scale, src_k = mid[h]
            ct, n_row, _ = state[h]
            num = inter * qc[h] + wv[h]
            den = inter * jnp.sum(q * n_row, axis=1, keepdims=True) + jnp.sum(w, axis=1, keepdims=True)
            h_t = num / jnp.maximum(jnp.abs(den), jnp.exp(-m_t))
            state[h] = (carry_scale * ct + upd[h],
                        carry_scale * n_row + jnp.sum(src_k, axis=0, keepdims=True), m_new)
            ms = jnp.mean(h_t * h_t, axis=1, keepdims=True)
            hn = h_t * lax.rsqrt(ms + EPS) * ng_ref[:, h * hd:(h + 1) * hd]
            outs.append(sl(o_gate, t0, h) * hn)
        out_chunks.append(jnp.concatenate(outs, axis=1))
    for h in heads:
        ct_ref[h], n_ref[h], m_ref[h] = state[h]
    o_ref[...] = jnp.concatenate(out_chunks, axis=0).astype(o_ref.dtype)


def _mlstm_call(qkb, vb, ob, misc, conv_w, conv_b, i_bias, f_bias, norm_g):
    s = qkb.shape[0]
    gbias = jnp.zeros((1, LANES), F32)
    gbias = gbias.at[0, MISC_IB:MISC_IB + H_B].set(i_bias).at[0, MISC_FB:MISC_FB + H_B].set(f_bias)
    row = lambda w: pl.BlockSpec((B_L, w), lambda i: (i, 0))
    return pl.pallas_call(
        _mlstm_kernel,
        out_shape=jax.ShapeDtypeStruct((s, W_BRANCH), BF16),
        grid=(s // B_L,),
        in_specs=[row(512), row(256), row(256), row(128),
                  _whole((CONV_K, 512)), _whole((1, 512)), _whole((1, LANES)), _whole((1, 256))],
        out_specs=row(256),
        scratch_shapes=[pltpu.VMEM((8, 512), F32),
                        pltpu.VMEM((H_B, HEAD_DIM, HEAD_DIM), F32),
                        pltpu.VMEM((H_B, 1, HEAD_DIM), F32),
                        pltpu.VMEM((H_B, 1, 1), F32)],
        compiler_params=_cparams(("arbitrary",)),
        name="mlstm_mixer",
    )(qkb, vb, ob, misc, conv_w, conv_b.reshape(1, 512), gbias, norm_g.reshape(1, 256))


N_B_IN, N_C_IN = 8, 4


def _mlstm_band_kernel(*refs):
    b_in = refs[:N_B_IN]
    c_in = refs[N_B_IN:N_B_IN + N_C_IN]
    yb_ref, yc_ref = refs[N_B_IN + N_C_IN:N_B_IN + N_C_IN + 2]
    b_scr = refs[N_B_IN + N_C_IN + 2:N_B_IN + N_C_IN + 6]
    (bias_ref,) = refs[N_B_IN + N_C_IN + 6:]
    qc_ref, kcT_ref, vc_ref, line_ref = c_in
    _mlstm_state_init(*b_scr)
    _band_table_init(line_ref, bias_ref)
    _mlstm_body(*b_in, yb_ref, *b_scr)
    _band_body(qc_ref, kcT_ref, vc_ref, yc_ref, bias_ref)


def _mlstm_band_call(qkb, vb, ob, misc, conv_w, conv_b, i_bias, f_bias, norm_g,
                     qc, kcT_pad, vc_pad, bias_line):
    assert B_L == C_TQ
    s = qkb.shape[0]
    gbias = jnp.zeros((1, LANES), F32)
    gbias = gbias.at[0, MISC_IB:MISC_IB + H_B].set(i_bias).at[0, MISC_FB:MISC_FB + H_B].set(f_bias)
    row = lambda w: pl.BlockSpec((B_L, w), lambda i: (i, 0))
    vm = pl.BlockSpec(memory_space=pltpu.VMEM)
    return pl.pallas_call(
        _mlstm_band_kernel,
        out_shape=(jax.ShapeDtypeStruct((s, W_BRANCH), BF16), jax.ShapeDtypeStruct((s, W_BRANCH), BF16)),
        grid=(s // B_L,),
        in_specs=[row(512), row(256), row(256), row(128),
                  _whole((CONV_K, 512)), _whole((1, 512)), _whole((1, LANES)), _whole((1, 256)),
                  row(256), vm, vm, _whole((H_C, C_LINE))],
        out_specs=(row(256), row(256)),
        scratch_shapes=[pltpu.VMEM((8, 512), F32),
                        pltpu.VMEM((H_B, HEAD_DIM, HEAD_DIM), F32),
                        pltpu.VMEM((H_B, 1, HEAD_DIM), F32),
                        pltpu.VMEM((H_B, 1, 1), F32),
                        pltpu.VMEM((H_C, C_TQ, C_WIN), F32)],
        compiler_params=_cparams(("arbitrary",)),
        name="mlstm_band",
    )(qkb, vb, ob, misc, conv_w, conv_b.reshape(1, 512), gbias, norm_g.reshape(1, 256),
      qc, kcT_pad, vc_pad, bias_line)


M_TM = 256
ROUTE_LOGIT0 = N_GROUPS
RT_E1, RT_E2, RT_G1, RT_G2 = 0, 1, 2, 3


def _lane_argmax(vals, lane):
    v = jnp.max(vals, axis=1, keepdims=True)
    idx = jnp.min(jnp.where(vals == v, lane, float(LANES)), axis=1, keepdims=True)
    return v, idx


def _merge_kernel(x_ref, ya_ref, yb_ref, yc_ref, yd_ref, gm_ref, wg_ref, wb_ref, wo_ref, gf_ref,
                  wr_ref, rb_ref, xo_ref, h2_ref, rt_ref):
    x = x_ref[...]
    h = _rms(x, gm_ref[...]).astype(BF16)
    mixed = jnp.zeros(x.shape, F32)
    for n, y_ref in enumerate((ya_ref, yb_ref, yc_ref, yd_ref)):
        gate = jax.nn.sigmoid(jnp.dot(h, wg_ref[:, n * D_MODEL:(n + 1) * D_MODEL],
                                      preferred_element_type=F32))
        up = jnp.dot(y_ref[...], wb_ref[n], preferred_element_type=F32)
        mixed = mixed + gate * up
    xn = x + jnp.dot(mixed.astype(BF16), wo_ref[...], preferred_element_type=F32)
    xo_ref[...] = xn
    h2 = _rms(xn, gf_ref[...])
    h2_ref[...] = h2
    logits = jnp.dot(h2.astype(BF16), wr_ref[...], preferred_element_type=F32) + rb_ref[...]

    lane = lax.broadcasted_iota(I32, logits.shape, 1).astype(F32)
    neg = -jnp.inf
    gmask = lane < N_GROUPS
    gmax, g_sel = _lane_argmax(jnp.where(gmask, logits, neg), lane)
    g_gate = 1.0 / jnp.sum(jnp.where(gmask, jnp.exp(logits - gmax), 0.0), axis=1, keepdims=True)
    e_lo = ROUTE_LOGIT0 + EXPERTS_PER_GROUP * g_sel
    el = jnp.where((lane >= e_lo) & (lane < e_lo + EXPERTS_PER_GROUP), logits, neg)
    v1, i1 = _lane_argmax(el, lane)
    v2, i2 = _lane_argmax(jnp.where(lane == i1, neg, el), lane)
    e = jnp.exp(v2 - v1)
    p1 = 1.0 / (1.0 + e)
    p2 = e / (1.0 + e)
    rt = jnp.where(lane == RT_E1, i1 - ROUTE_LOGIT0,
                   jnp.where(lane == RT_E2, i2 - ROUTE_LOGIT0,
                             jnp.where(lane == RT_G1, p1 * g_gate,
                                       jnp.where(lane == RT_G2, p2 * g_gate, 0.0))))
    rt_ref[...] = rt


def _merge_call(x2, ya, yb, yc, yd, g_mix, w_gate, w_branch, w_out, g_ffn, w_route, b_route):
    s = x2.shape[0]
    row = lambda w: pl.BlockSpec((M_TM, w), lambda i: (i, 0))
    vm = pl.BlockSpec(memory_space=pltpu.VMEM)
    return pl.pallas_call(
        _merge_kernel,
        out_shape=(jax.ShapeDtypeStruct((s, D_MODEL), F32), jax.ShapeDtypeStruct((s, D_MODEL), F32),
                   jax.ShapeDtypeStruct((s, LANES), F32)),
        grid=(s // M_TM,),
        in_specs=[row(D_MODEL), row(256), row(256), row(256), row(256),
                  vm, vm, vm, vm, vm, vm, vm],
        out_specs=(row(D_MODEL), row(D_MODEL), row(LANES)),
        compiler_params=_cparams(("parallel",)),
        name="merge_route",
    )(x2, ya, yb, yc, yd, g_mix.reshape(1, D_MODEL), w_gate, w_branch, w_out,
      g_ffn.reshape(1, D_MODEL), w_route, b_route)


E_BM = 128
GATHER_UNROLL = 8


def _gather_rows(src_hbm, idx_ref, dst, sem, n, wait, inline=False):
    def one(r, parity):
        cp = pltpu.make_async_copy(src_hbm.at[pl.ds(idx_ref[0, 0, r], 1)], dst.at[pl.ds(r, 1)], sem)
        if wait:
            cp.wait()
        else:
            cp.start(priority=parity)

    if inline:
        for r in range(n):
            one(r, r % 2)
    else:
        def group(g, c):
            for u in range(GATHER_UNROLL):
                one(g * GATHER_UNROLL + u, u % 2)
            return c
        lax.fori_loop(0, n // GATHER_UNROLL, group, 0)


def _moe_kernel(be_ref, nused_ref, tok_ref, tok_next_ref, h2_hbm, w1_ref, w3_ref, w2_ref, o_ref,
                xbuf, w1b, w3b, w2b, sem):
    i = pl.program_id(0)
    nused = nused_ref[0]
    used = i < nused
    slot = i % 2

    @pl.when((i == 0) & used)
    def _():
        _gather_rows(h2_hbm, tok_ref, xbuf.at[0], sem.at[0], E_BM, wait=False)

    @pl.when(used)
    def _():
        prev = be_ref[jnp.maximum(i - 1, 0)]

        @pl.when((i == 0) | (be_ref[i] != prev))
        def _():
            w1b[...] = w1_ref[0, 0].astype(BF16)
            w3b[...] = w3_ref[0, 0].astype(BF16)
            w2b[...] = w2_ref[0, 0].astype(BF16)

        _gather_rows(h2_hbm, tok_ref, xbuf.at[slot], sem.at[slot], E_BM, wait=True)
        _gather_rows(h2_hbm, tok_next_ref, xbuf.at[1 - slot], sem.at[1 - slot], E_BM, wait=False, inline=True)
        xb = xbuf[slot].astype(BF16)
        a = jnp.dot(xb, w1b[...], preferred_element_type=F32)
        g = jnp.dot(xb, w3b[...], preferred_element_type=F32)
        hid = (a * jax.nn.sigmoid(a) * g).astype(BF16)
        o_ref[...] = jnp.dot(hid, w2b[...], preferred_element_type=F32)

    @pl.when(jnp.logical_not(used))
    def _():
        @pl.when(i == nused)
        def _():
            _gather_rows(h2_hbm, tok_ref, xbuf.at[slot], sem.at[slot], E_BM, wait=True)
        o_ref[...] = jnp.zeros(o_ref.shape, F32)


def _moe_call(h2, row_tok, block_e, nused, w1, w3, w2, layer):
    n_rows = row_tok.shape[0]
    n_blocks = n_rows // E_BM
    wspec = lambda shp: pl.BlockSpec((1, 1) + shp, lambda i, be, nu: (layer, be[i], 0, 0))
    grid_spec = pltpu.PrefetchScalarGridSpec(
        num_scalar_prefetch=2,
        grid=(n_blocks,),
        in_specs=[pl.BlockSpec((1, 1, E_BM), lambda i, be, nu: (i, 0, 0), memory_space=pltpu.SMEM),
                  pl.BlockSpec((1, 1, E_BM), lambda i, be, nu: (jnp.minimum(i + 1, n_blocks - 1), 0, 0),
                               memory_space=pltpu.SMEM),
                  pl.BlockSpec(memory_space=pl.ANY),
                  wspec((D_MODEL, F_EXPERT)), wspec((D_MODEL, F_EXPERT)), wspec((F_EXPERT, D_MODEL))],
        out_specs=pl.BlockSpec((E_BM, D_MODEL), lambda i, be, nu: (i, 0)),
        scratch_shapes=[pltpu.VMEM((2, E_BM, D_MODEL), F32),
                        pltpu.VMEM((D_MODEL, F_EXPERT), BF16),
                        pltpu.VMEM((D_MODEL, F_EXPERT), BF16),
                        pltpu.VMEM((F_EXPERT, D_MODEL), BF16),
                        pltpu.SemaphoreType.DMA((2,))],
    )
    tok3 = row_tok.reshape(n_blocks, 1, E_BM)
    return pl.pallas_call(
        _moe_kernel,
        out_shape=jax.ShapeDtypeStruct((n_rows, D_MODEL), F32),
        grid_spec=grid_spec,
        compiler_params=_cparams(("arbitrary",)),
        name="moe_experts",
    )(block_e, nused, tok3, tok3, h2, w1, w3, w2)


CB_TM = 128


def _combine_kernel(final, pos_ref, pos_next_ref, x_ref, rt_ref, ys_hbm, gfin_ref, o_ref, buf, sem):
    i = pl.program_id(0)
    slot = i % 2
    n = 2 * CB_TM

    @pl.when(i == 0)
    def _():
        _gather_rows(ys_hbm, pos_ref, buf.at[0], sem.at[0], n, wait=False)

    @pl.when(i + 1 < pl.num_programs(0))
    def _():
        _gather_rows(ys_hbm, pos_next_ref, buf.at[1 - slot], sem.at[1 - slot], n, wait=False)
    _gather_rows(ys_hbm, pos_ref, buf.at[slot], sem.at[slot], n, wait=True)

    rt = rt_ref[...]
    y = (buf[slot, 0:CB_TM, :] * rt[:, RT_G1:RT_G1 + 1]
         + buf[slot, CB_TM:2 * CB_TM, :] * rt[:, RT_G2:RT_G2 + 1])
    xn = x_ref[...] + y
    if final:
        xn = _rms(xn, gfin_ref[...])
    o_ref[...] = xn


def _combine_call(x2, rt, ys, pos, g_final, final):
    s = x2.shape[0]
    nb = s // CB_TM
    pos_b = pos.reshape(nb, CB_TM, 2).transpose(0, 2, 1).reshape(nb, 1, 2 * CB_TM)
    return pl.pallas_call(
        functools.partial(_combine_kernel, final),
        out_shape=jax.ShapeDtypeStruct((s, D_MODEL), F32),
        grid=(nb,),
        in_specs=[pl.BlockSpec((1, 1, 2 * CB_TM), lambda i: (i, 0, 0), memory_space=pltpu.SMEM),
                  pl.BlockSpec((1, 1, 2 * CB_TM), lambda i: (jnp.minimum(i + 1, nb - 1), 0, 0),
                               memory_space=pltpu.SMEM),
                  pl.BlockSpec((CB_TM, D_MODEL), lambda i: (i, 0)),
                  pl.BlockSpec((CB_TM, LANES), lambda i: (i, 0)),
                  pl.BlockSpec(memory_space=pl.ANY),
                  _whole((1, D_MODEL))],
        out_specs=pl.BlockSpec((CB_TM, D_MODEL), lambda i: (i, 0)),
        scratch_shapes=[pltpu.VMEM((2, 2 * CB_TM, D_MODEL), F32), pltpu.SemaphoreType.DMA((2,))],
        compiler_params=_cparams(("arbitrary",)),
        name="moe_combine",
    )(pos_b, pos_b, x2, rt, ys, g_final.reshape(1, D_MODEL))


PLAN_BLOCK = 256


def _dispatch_plan(rt):
    t = rt.shape[0]
    n_assign = t * TOP_K_INNER
    eid = rt[:, RT_E1:RT_E2 + 1].astype(I32).reshape(n_assign)
    onehot = (eid[:, None] == jnp.arange(N_EXPERTS, dtype=I32)[None, :]).astype(F32)
    pb = PLAN_BLOCK
    oh3 = onehot.reshape(n_assign // pb, pb, N_EXPERTS)
    tri = jnp.tril(jnp.ones((pb, pb), F32), k=-1)
    within = jnp.einsum('ij,bjk->bik', tri, oh3)
    block_tot = jnp.sum(oh3, axis=1)
    block_off = jnp.cumsum(block_tot, axis=0) - block_tot
    counts = (block_off[-1] + block_tot[-1]).astype(I32)
    rank = jnp.sum(oh3 * (within + block_off[:, None, :]), axis=-1).reshape(n_assign)
    padded = (counts + E_BM - 1) // E_BM * E_BM
    ends_pad = jnp.cumsum(padded)
    starts_pad = ends_pad - padded
    dest = (jnp.sum(onehot * starts_pad.astype(F32)[None, :], axis=-1) + rank).astype(I32)
    n_rows = (-(-(n_assign + N_EXPERTS * (E_BM - 1)) // E_BM) + 1) * E_BM
    n_blocks = n_rows // E_BM
    tok = jnp.repeat(jnp.arange(t, dtype=I32), TOP_K_INNER)
    row_tok = jnp.zeros((n_rows,), I32).at[dest].set(tok)
    block_start = jnp.arange(n_blocks, dtype=I32) * E_BM
    block_e = jnp.minimum(jnp.sum((ends_pad[None, :] <= block_start[:, None]).astype(I32), axis=1),
                          N_EXPERTS - 1)
    nused = (ends_pad[-1] // E_BM).astype(I32).reshape(1)
    return row_tok, block_e, nused, dest.reshape(t, TOP_K_INNER)


def _layer(x2, layer_idx, p, experts, final_g, final):
    (norm_mix_g, w_in, conv_w, conv_b, i_bias, f_bias, mnorm_g, rel_bias, lam_p, dnorm_g,
     w_branch, w_gate, w_out, norm_ffn_g, rgw, rgb, rew, reb) = p
    o = _k1_call(x2, norm_mix_g, *_rearrange_w_in(w_in))

    ya = _dsa_call(o["qaT"], o["qiT"], o["wiT"], o["ka"], o["vaT"], o["misc_bf"])
    kcT_pad = jnp.pad(o["kcT"], ((0, 0), (C_PAD, 0)))
    vc_pad = jnp.pad(o["vc"], ((C_PAD, 0), (0, 0)))
    yb, yc = _mlstm_band_call(o["qkb"], o["vb"], o["ob"], o["misc"], conv_w, conv_b, i_bias, f_bias, mnorm_g,
                              o["qc"], kcT_pad, vc_pad, _band_bias_line(rel_bias))
    lam_init = 0.8 - 0.6 * math.exp(-0.3 * layer_idx)
    yd = _diff_call(o["qdT"], o["kd"], o["vdT"], lam_p, dnorm_g, lam_init)

    w_route = jnp.concatenate([rgw, rew, jnp.zeros((D_MODEL, LANES - N_GROUPS - N_EXPERTS), F32)],
                              axis=1).astype(BF16)
    b_route = jnp.concatenate([rgb, reb, jnp.zeros((LANES - N_GROUPS - N_EXPERTS,), F32)]).reshape(1, LANES)
    xn, h2, rt = _merge_call(x2, ya, yb, yc, yd, norm_mix_g, w_gate.astype(BF16), w_branch.astype(BF16),
                             w_out.astype(BF16), norm_ffn_g, w_route, b_route)
    row_tok, block_e, nused, pos = _dispatch_plan(rt)
    ys = _moe_call(h2, row_tok, block_e, nused, *experts, layer_idx)
    return _combine_call(xn, rt, ys, pos, final_g, final)


def kernel(x, norm_mix_g, w_in, conv_w, conv_b, mlstm_i_bias, mlstm_f_bias, mlstm_norm_g, relpos_bias, diff_lambda, diff_norm_g, w_branch, w_gate, w_out, norm_ffn_g, router_group_w, router_group_b, router_expert_w, router_expert_b, expert_w1, expert_w3, expert_w2, final_norm_g):
    assert x.shape[0] == 1 and x.shape[2] == D_MODEL
    params = (norm_mix_g, w_in, conv_w, conv_b, mlstm_i_bias, mlstm_f_bias, mlstm_norm_g, relpos_bias,
              diff_lambda, diff_norm_g, w_branch, w_gate, w_out, norm_ffn_g, router_group_w,
              router_group_b, router_expert_w, router_expert_b)
    experts = (expert_w1, expert_w3, expert_w2)
    depth = norm_mix_g.shape[0]
    x2 = x[0]
    for l in range(depth):
        x2 = _layer(x2, l, tuple(a[l] for a in params), experts, final_norm_g, l == depth - 1)
    return x2[None]
```

```python
import functools
import math

import jax
import jax.numpy as jnp
from jax import lax
from jax.experimental import pallas as pl
from jax.experimental.pallas import tpu as pltpu

F32 = jnp.float32
BF16 = jnp.bfloat16
I32 = jnp.int32

D_MODEL = 1024
CHUNK = 64
HEAD_DIM = 64
NEG_INF = -1e30
H_A = 4
H_IDX = 4
D_IDX = 32
TOPK_MAX = 256
H_B = 4
CONV_K = 4
H_C = 4
N_PREV_CHUNKS = 8
MAX_REL_PAST = 128
H_D = 4
DQ_D = 32
W_BRANCH = 256
N_BRANCH = 4
N_GROUPS = 4
EXPERTS_PER_GROUP = 8
N_EXPERTS = 32
TOP_K_INNER = 2
F_EXPERT = 512
EPS = 1e-6

VMEM_LIMIT_BYTES = 52 * 1024 * 1024
LANES = 128

INT_MIN = -(2 ** 31)
I16_MIN = -(2 ** 15)
I16 = jnp.int16
M_INIT = -5e29

_COL_SIZES = (256, 256, 256, 128, 32, 4, 256, 256, 256, 4, 4, 256, 256, 256, 256, 256, 256, 256)
_COL_NAMES = ("qa", "ka", "va", "qi", "ki", "wi", "qb", "kb", "vb", "ib", "fb", "ob",
              "qc", "kc", "vc", "qd", "kd", "vd")
_COL_OFF = {}
_o = 0
for _n, _s in zip(_COL_NAMES, _COL_SIZES):
    _COL_OFF[_n] = (_o, _s)
    _o += _s
C_IN = _o
MISC_KI = 0
MISC_WI = 32
MISC_IB = 36
MISC_FB = 40


def _cparams(sem):
    return pltpu.CompilerParams(dimension_semantics=sem, vmem_limit_bytes=VMEM_LIMIT_BYTES)


def _whole(shape):
    nd = len(shape)
    return pl.BlockSpec(shape, lambda *_: (0,) * nd)


_K1_NAT = (("ka", 256, BF16), ("misc", 128, F32), ("qkb", 512, F32), ("vb", 256, F32), ("ob", 256, F32),
           ("qc", 256, BF16), ("vc", 256, BF16), ("kd", 256, BF16))
_K1_TR = (("qaT", 256, BF16), ("qiT", 128, BF16), ("vaT", 256, BF16), ("kcT", 256, BF16),
          ("qdT", 256, BF16), ("vdT", 256, BF16), ("wiT", 8, F32))
K1_NAT_WIDTH = sum(w for _, w, _ in _K1_NAT)
K1_TR_WIDTH = sum(w for _, w, _ in _K1_TR)
K1_TM = 512
V_AUG = 80
_K1_VAUG = ("vaT", "vdT")


def _k1_tr_rows(name, width):
    return (width // HEAD_DIM) * V_AUG if name in _K1_VAUG else width


def _rearrange_w_in(w_in):
    def cols(name):
        o, s = _COL_OFF[name]
        return w_in[:, o:o + s]
    d = w_in.shape[0]
    misc = jnp.concatenate([cols("ki"), cols("wi"), cols("ib"), cols("fb"),
                            jnp.zeros((d, LANES - 44), w_in.dtype)], axis=1)
    nat = [cols("ka"), misc, cols("qb"), cols("kb"), cols("vb"), cols("ob"), cols("qc"), cols("vc"),
           cols("kd")]
    tr = [cols("qa"), cols("qi"), cols("va"), cols("kc"), cols("qd"), cols("vd"), cols("wi"),
          jnp.zeros((d, 8 - H_IDX), w_in.dtype)]
    return (jnp.concatenate(nat, axis=1).astype(BF16), jnp.concatenate(tr, axis=1).T.astype(BF16))


def _rms(x, g):
    ms = jnp.mean(x * x, axis=-1, keepdims=True)
    return x * lax.rsqrt(ms + EPS) * g


def _k1_kernel(x_ref, g_ref, wn_ref, wt_ref, *out_refs):
    h = _rms(x_ref[...], g_ref[...]).astype(BF16)
    refs = dict(zip([n for n, _, _ in _K1_NAT] + ["misc_bf"] + [n for n, _, _ in _K1_TR], out_refs))
    off = 0
    for name, width, _ in _K1_NAT:
        r = jnp.dot(h, wn_ref[:, off:off + width], preferred_element_type=F32)
        refs[name][...] = r.astype(refs[name].dtype)
        if name == "misc":
            refs["misc_bf"][...] = r.astype(BF16)
        off += width
    off = 0
    for name, width, _ in _K1_TR:
        r = lax.dot_general(wt_ref[off:off + width, :], h, (((1,), (1,)), ((), ())),
                            preferred_element_type=F32)
        if name in _K1_VAUG:
            tail = (lax.broadcasted_iota(I32, (V_AUG - HEAD_DIM, r.shape[1]), 0) == 0)
            tail = jnp.where(tail, 1.0, 0.0).astype(BF16)
            for hh in range(width // HEAD_DIM):
                refs[name][hh * V_AUG:hh * V_AUG + HEAD_DIM, :] = \
                    r[hh * HEAD_DIM:(hh + 1) * HEAD_DIM, :].astype(BF16)
                refs[name][hh * V_AUG + HEAD_DIM:(hh + 1) * V_AUG, :] = tail
        else:
            refs[name][...] = r.astype(refs[name].dtype)
        off += width


def _k1_call(x2, g, w_nat, w_trT):
    s = x2.shape[0]
    tm = K1_TM
    outs = (tuple(jax.ShapeDtypeStruct((s, w), dt) for _, w, dt in _K1_NAT)
            + (jax.ShapeDtypeStruct((s, LANES), BF16),)
            + tuple(jax.ShapeDtypeStruct((_k1_tr_rows(n, w), s), dt) for n, w, dt in _K1_TR))
    out_specs = (tuple(pl.BlockSpec((tm, w), lambda i: (i, 0)) for _, w, _ in _K1_NAT)
                 + (pl.BlockSpec((tm, LANES), lambda i: (i, 0)),)
                 + tuple(pl.BlockSpec((_k1_tr_rows(n, w), tm), lambda i: (0, i)) for n, w, _ in _K1_TR))
    res = pl.pallas_call(
        _k1_kernel,
        out_shape=outs,
        grid=(s // tm,),
        in_specs=[pl.BlockSpec((tm, D_MODEL), lambda i: (i, 0)),
                  _whole((1, D_MODEL)),
                  _whole((D_MODEL, K1_NAT_WIDTH)),
                  _whole((K1_TR_WIDTH, D_MODEL))],
        out_specs=out_specs,
        compiler_params=_cparams(("parallel",)),
        name="k1_norm_proj",
    )(x2, g.reshape(1, D_MODEL), w_nat, w_trT)
    return dict(zip([n for n, _, _ in _K1_NAT] + ["misc_bf"] + [n for n, _, _ in _K1_TR], res))


A_TQ = 256
A_KT = 512
A_SLAB = 256
A_SWEEP_SLABS = 4
LOG2E = 1.4426950408889634
QK_AHEAD = 4
BF16_ROWS = 16


def _flash_step(s, i, v_t, m_ref, acc_ref):
    kt, tq = s.shape
    parts = s.reshape(kt // BF16_ROWS, BF16_ROWS, tq)
    parts = [parts[g] for g in range(kt // BF16_ROWS)]
    while len(parts) > 1:
        parts = [jnp.maximum(parts[g], parts[g + 1]) for g in range(0, len(parts), 2)]
    m_old = m_ref[i]
    m_new = jnp.maximum(m_old, jnp.max(parts[0].astype(F32), axis=0, keepdims=True))
    alpha = jnp.exp2(m_old - m_new)
    p = jnp.exp2(s - m_new.astype(BF16))
    acc_ref[i] = alpha * acc_ref[i] + jnp.dot(v_t, p, preferred_element_type=F32)
    m_ref[i] = m_new


def _flash_result(acc_ref, i):
    return acc_ref[i, 0:HEAD_DIM, :] / acc_ref[i, HEAD_DIM:HEAD_DIM + 1, :]


def _logits(k_t, q_pad):
    return jnp.dot(k_t, q_pad, preferred_element_type=F32).astype(BF16)


def _qk_prologue(k_t, qpad_ref, n, s_ref):
    for i in range(min(QK_AHEAD, n)):
        s_ref[i] = _logits(k_t, qpad_ref[i])


def _flash_tile(k_t, qpad_ref, n, v_tile, bias, m_ref, acc_ref, s_ref=None, k_next=None):
    a = min(QK_AHEAD, n)
    if s_ref is None:
        pend = [_logits(k_t, qpad_ref[i]) for i in range(a)]
    else:
        pend = [s_ref[i] for i in range(a)]
    for i in range(n):
        s = pend.pop(0)
        if i + a < n:
            pend.append(_logits(k_t, qpad_ref[i + a]))
        elif s_ref is not None:
            s_ref[i + a - n] = _logits(k_next, qpad_ref[i + a - n])
        if bias is not None:
            s = s + bias
        _flash_step(s, i, v_tile(i), m_ref, acc_ref)


def _bit_transpose32(words):
    a = list(words)
    j, m = 16, 0x0000FFFF
    while j:
        k = 0
        while k < 32:
            t = (a[k] ^ lax.shift_right_logical(a[k + j], j)) & m
            a[k] = a[k] ^ t
            a[k + j] = a[k + j] ^ lax.shift_left(t, j)
            k = (k + j + 1) & ~j
        j >>= 1
        m = (m ^ (m << j)) & 0xFFFFFFFF
    return a


def _dsa_kernel(topk, qaT_ref, qiT_ref, wiT_ref, qiTn_ref, wiTn_ref, ka_ref, vaT_ref, mb_ref, tri_ref,
                o_ref, planes_ref, cand_ref, above_ref, qpad_ref, qipad_ref, acc_ref, m_ref, carry_ref):
    tq, kt = A_TQ, A_KT
    b = pl.program_id(0)
    nb = pl.num_programs(0)
    ntiles = ((b + 1) * tq + kt - 1) // kt

    def vis_end_of(blk):
        q_pos = blk * tq + lax.broadcasted_iota(I32, (1, tq), 1)
        return (q_pos // CHUNK + 1) * CHUNK

    def set_qipad(src_ref):
        qiT = src_ref[...]
        for h in range(H_IDX):
            qipad_ref[h, 0:D_IDX, :] = qiT[h * D_IDX:(h + 1) * D_IDX, :]
            qipad_ref[h, D_IDX:LANES, :] = jnp.zeros((LANES - D_IDX, tq), BF16)

    def p1(j, w_ref, vis_end):
        s0 = pl.multiple_of(j * kt, kt)
        mb = mb_ref[pl.ds(s0, kt), :]
        score = jnp.zeros((kt, tq), F32)
        for h in range(H_IDX):
            r = jnp.dot(mb, qipad_ref[h], preferred_element_type=F32)
            score = score + jnp.maximum(r, 0.0) * w_ref[h:h + 1, :]
        bits = lax.bitcast_convert_type(score, I32)
        ukey = bits ^ (lax.shift_right_arithmetic(bits, 31) | INT_MIN)
        if vis_end is not None:
            s_pos = s0 + lax.broadcasted_iota(I32, (kt, 1), 0)
            ukey = jnp.where(s_pos < vis_end, ukey, 0)
        u4 = ukey.reshape(kt // A_SLAB, 32, 8, tq)
        for s2 in range(kt // A_SLAB):
            planes = _bit_transpose32([u4[s2, v] for v in range(32)])
            for r in range(32):
                planes_ref[j * (kt // A_SLAB) + s2, r] = planes[r]

    @pl.when(b == 0)
    def _():
        set_qipad(qiT_ref)
        vis0 = vis_end_of(0)

        def first_block(j, carry):
            p1(j, wiT_ref, vis0)
            return carry
        lax.fori_loop(0, ntiles, first_block, 0)

    kf = float(topk)
    nslab = ntiles * (kt // A_SLAB)
    nstep = (nslab + A_SWEEP_SLABS - 1) // A_SWEEP_SLABS

    def init_sets(sl, carry):
        cand_ref[sl] = jnp.full((8, tq), -1, I32)
        above_ref[sl] = jnp.zeros((8, tq), I32)
        return carry
    lax.fori_loop(0, nslab, init_sets, 0)

    def init_pad(sl, carry):
        cand_ref[sl] = jnp.zeros((8, tq), I32)
        above_ref[sl] = jnp.zeros((8, tq), I32)
        planes_ref[sl] = jnp.zeros((32, 8, tq), I32)
        return carry
    lax.fori_loop(nslab, nstep * A_SWEEP_SLABS, init_pad, 0)

    def apply_decision(sl, prev_plane, took_one):
        cand = cand_ref[sl]
        ones = cand & prev_plane
        cand = jnp.where(took_one, ones, cand ^ ones)
        above_ref[sl] = jnp.where(took_one, above_ref[sl], above_ref[sl] | ones)
        cand_ref[sl] = cand
        return cand

    def sweep(i, carry):
        n_above, took, tau = carry
        took_one = jnp.broadcast_to(took, (8, tq)) != 0
        first = i == 0

        def step(jj, acc):
            for s2 in range(A_SWEEP_SLABS):
                sl = jj * A_SWEEP_SLABS + s2
                prev_plane = jnp.where(first, -1, planes_ref[sl, jnp.maximum(i - 1, 0)])
                cand = apply_decision(sl, prev_plane, took_one)
                acc = acc + lax.population_count(cand & planes_ref[sl, i])
            return acc
        acc = lax.fori_loop(0, nstep, step, jnp.zeros((8, tq), I32))
        n_one = jnp.sum(acc.astype(F32), axis=0, keepdims=True)
        take = (n_above + n_one) >= kf
        n_above = jnp.where(take, n_above, n_above + n_one)
        tau = jnp.where(take, tau | lax.shift_left(jnp.int32(1), 31 - i), tau)
        return n_above, jnp.where(take, 1, 0), tau
    n_above, took, tau = lax.fori_loop(
        0, 32, sweep, (jnp.zeros((1, tq), F32), jnp.ones((1, tq), I32), jnp.zeros((1, tq), I32)))

    def last_decision(sl, carry):
        apply_decision(sl, planes_ref[sl, 31], jnp.broadcast_to(took, (8, tq)) != 0)
        return carry
    lax.fori_loop(0, nslab, last_decision, 0)
    need = jnp.where(tau == 0, 0.0, kf - n_above)

    m_ref[...] = jnp.full(m_ref.shape, M_INIT, F32)
    acc_ref[...] = jnp.zeros(acc_ref.shape, F32)
    carry_ref[...] = jnp.zeros(carry_ref.shape, F32)
    qaT = (qaT_ref[...].astype(F32) * (HEAD_DIM ** -0.5 * LOG2E)).astype(BF16)
    row = lax.broadcasted_iota(I32, qaT.shape, 0)
    for h in range(H_A):
        qpad_ref[h] = jnp.where((row >= h * HEAD_DIM) & (row < (h + 1) * HEAD_DIM), qaT,
                                jnp.zeros_like(qaT))

    def slab_rows(ref, sl):
        word = ref[sl]
        return jnp.concatenate([lax.shift_right_logical(word, 31 - v) & 1 for v in range(32)], axis=0)

    def p3(j):
        s0 = pl.multiple_of(j * kt, kt)
        slabs = [j * (kt // A_SLAB) + s2 for s2 in range(kt // A_SLAB)]
        gt = jnp.concatenate([slab_rows(above_ref, sl) for sl in slabs], axis=0) != 0
        eq_i = jnp.concatenate([slab_rows(cand_ref, sl) for sl in slabs], axis=0)
        eq = eq_i != 0
        eqf = eq_i.astype(F32)
        pref = jnp.dot(tri_ref[...], eqf.astype(BF16), preferred_element_type=F32)
        seen = carry_ref[...]
        sel = gt | (eq & (pref < need - seen))
        bias = jnp.where(sel, 0.0, NEG_INF).astype(BF16)
        carry_ref[...] = seen + pref[kt - 1:kt, :] + eqf[kt - 1:kt, :]
        k_t = ka_ref[pl.ds(s0, kt), :]
        _flash_tile(k_t, qpad_ref, H_A, lambda i: vaT_ref[i * V_AUG:(i + 1) * V_AUG, pl.ds(s0, kt)],
                    bias, m_ref, acc_ref)

    def p3_only(j, carry):
        p3(j)
        return carry

    @pl.when(b + 1 < nb)
    def _():
        set_qipad(qiTn_ref)
        ntiles_next = ((b + 2) * tq + kt - 1) // kt
        nfull_next = ((b + 1) * tq) // kt
        vis_next = vis_end_of(b + 1)

        def both(j, carry):
            p1(j, wiTn_ref, None)
            p3(j)
            return carry
        lax.fori_loop(0, nfull_next, both, 0)
        lax.fori_loop(nfull_next, ntiles, p3_only, 0)

        def next_masked(j, carry):
            p1(j, wiTn_ref, vis_next)
            return carry
        lax.fori_loop(nfull_next, ntiles_next, next_masked, 0)

    @pl.when(b + 1 == nb)
    def _():
        lax.fori_loop(0, ntiles, p3_only, 0)

    ys = [_flash_result(acc_ref, h) for h in range(H_A)]
    o_ref[...] = jnp.concatenate(ys, axis=0).T.astype(o_ref.dtype)


def _dsa_call(qaT, qiT, wiT, ka, vaT, misc_bf):
    s = ka.shape[0]
    topk = min(TOPK_MAX, s // 4)
    tri = jnp.tril(jnp.ones((A_KT, A_KT), F32), k=-1).astype(BF16)
    vm = pl.BlockSpec(memory_space=pltpu.VMEM)
    nb = s // A_TQ
    nxt = lambda i: (0, jnp.minimum(i + 1, nb - 1))
    return pl.pallas_call(
        functools.partial(_dsa_kernel, topk),
        out_shape=jax.ShapeDtypeStruct((s, W_BRANCH), BF16),
        grid=(nb,),
        in_specs=[pl.BlockSpec((256, A_TQ), lambda i: (0, i)),
                  pl.BlockSpec((LANES, A_TQ), lambda i: (0, i)),
                  pl.BlockSpec((8, A_TQ), lambda i: (0, i)),
                  pl.BlockSpec((LANES, A_TQ), nxt),
                  pl.BlockSpec((8, A_TQ), nxt),
                  vm, vm, vm, vm],
        out_specs=pl.BlockSpec((A_TQ, 256), lambda i: (i, 0)),
        scratch_shapes=[pltpu.VMEM((s // A_SLAB, 32, 8, A_TQ), I32),
                        pltpu.VMEM((s // A_SLAB, 8, A_TQ), I32),
                        pltpu.VMEM((s // A_SLAB, 8, A_TQ), I32),
                        pltpu.VMEM((H_A, 256, A_TQ), BF16),
                        pltpu.VMEM((H_IDX, LANES, A_TQ), BF16),
                        pltpu.VMEM((H_A, V_AUG, A_TQ), F32),
                        pltpu.VMEM((H_A, 1, A_TQ), F32),
                        pltpu.VMEM((1, A_TQ), F32)],
        compiler_params=_cparams(("arbitrary",)),
        name="dsa_mixer",
    )(qaT, qiT, wiT, qiT, wiT, ka, vaT, misc_bf, tri)


def _pair_select(lo, hi):
    lane = lax.broadcasted_iota(I32, lo.shape, 1)
    return jnp.where(lane < HEAD_DIM, lo, hi)


def _pair_head_rms(o, g):
    lane = lax.broadcasted_iota(I32, o.shape, 1)
    low = lane < HEAD_DIM
    sq = o * o
    ms_lo = jnp.sum(jnp.where(low, sq, 0.0), axis=1, keepdims=True) * (1.0 / HEAD_DIM)
    ms_hi = jnp.sum(jnp.where(low, 0.0, sq), axis=1, keepdims=True) * (1.0 / HEAD_DIM)
    ms = jnp.where(low, ms_lo, ms_hi)
    return o * lax.rsqrt(ms + EPS) * g


D_TQ = 256
D_KT = 1024


def _diff_kernel(lam_init, qT_ref, kd_ref, vT_ref, lam_ref, g_ref, o_ref, qpad_ref, acc_ref, m_ref, s_ref):
    tq = D_TQ
    b = pl.program_id(0)
    q_pos = b * tq + lax.broadcasted_iota(I32, (1, tq), 1)
    vis_end = (q_pos // CHUNK + 1) * CHUNK
    m_ref[...] = jnp.full(m_ref.shape, M_INIT, F32)
    acc_ref[...] = jnp.zeros(acc_ref.shape, F32)
    qT = (qT_ref[...].astype(F32) * (DQ_D ** -0.5 * LOG2E)).astype(BF16)
    row = lax.broadcasted_iota(I32, qT.shape, 0)
    for i in range(2 * H_D):
        qpad_ref[i] = jnp.where((row >= i * DQ_D) & (row < (i + 1) * DQ_D), qT, jnp.zeros_like(qT))

    def tile(s0, kt, masked, s_ref=None, s0_next=None):
        k_t = kd_ref[pl.ds(s0, kt), :]
        k_next = None if s0_next is None else kd_ref[pl.ds(s0_next, kt), :]
        bias = None
        if masked:
            vis = (s0 + lax.broadcasted_iota(I32, (kt, 1), 0)) < vis_end
            bias = jnp.where(vis, 0.0, NEG_INF).astype(BF16)
        _flash_tile(k_t, qpad_ref, 2 * H_D, lambda i: vT_ref[(i // 2) * V_AUG:(i // 2 + 1) * V_AUG, pl.ds(s0, kt)],
                    bias, m_ref, acc_ref, s_ref, k_next)

    n_big = (b * tq) // D_KT

    @pl.when(n_big > 0)
    def _():
        _qk_prologue(kd_ref[pl.ds(0, D_KT), :], qpad_ref, 2 * H_D, s_ref)

    def full_tile(j, carry):
        j_next = jnp.minimum(j + 1, n_big - 1)
        tile(pl.multiple_of(j * D_KT, D_KT), D_KT, False, s_ref, pl.multiple_of(j_next * D_KT, D_KT))
        return carry
    lax.fori_loop(0, n_big, full_tile, 0)

    def small_tile(j, carry):
        tile(pl.multiple_of(j * tq, tq), tq, False)
        return carry
    lax.fori_loop(n_big * (D_KT // tq), b, small_tile, 0)
    tile(pl.multiple_of(b * tq, tq), tq, True)

    lp = lam_ref[...]
    lam = (jnp.exp(jnp.sum(lp[0:1] * lp[1:2], axis=1, keepdims=True))
           - jnp.exp(jnp.sum(lp[2:3] * lp[3:4], axis=1, keepdims=True)) + lam_init)
    ys = []
    for h in range(H_D):
        o = _flash_result(acc_ref, 2 * h) - lam * _flash_result(acc_ref, 2 * h + 1)
        ms = jnp.mean(o * o, axis=0, keepdims=True)
        ys.append(o * lax.rsqrt(ms + EPS) * g_ref[h * HEAD_DIM:(h + 1) * HEAD_DIM, :] * (1.0 - lam_init))
    o_ref[...] = jnp.concatenate(ys, axis=0).T.astype(o_ref.dtype)


def _diff_call(qdT, kd, vdT, lam_p, dnorm_g, lam_init):
    s = kd.shape[0]
    return pl.pallas_call(
        functools.partial(_diff_kernel, lam_init),
        out_shape=jax.ShapeDtypeStruct((s, W_BRANCH), BF16),
        grid=(s // D_TQ,),
        in_specs=[pl.BlockSpec((256, D_TQ), lambda i: (0, i)),
                  pl.BlockSpec(memory_space=pltpu.VMEM),
                  pl.BlockSpec(memory_space=pltpu.VMEM),
                  _whole((4, DQ_D)),
                  _whole((256, 1))],
        out_specs=pl.BlockSpec((D_TQ, 256), lambda i: (i, 0)),
        scratch_shapes=[pltpu.VMEM((2 * H_D, 256, D_TQ), BF16),
                        pltpu.VMEM((2 * H_D, V_AUG, D_TQ), F32),
                        pltpu.VMEM((2 * H_D, 1, D_TQ), F32),
                        pltpu.VMEM((QK_AHEAD, D_KT, D_TQ), BF16)],
        compiler_params=_cparams(("parallel",)),
        name="diff_attention",
    )(qdT, kd, vdT, lam_p, dnorm_g.reshape(256, 1))


C_TQ = 128
C_PAD = N_PREV_CHUNKS * CHUNK
C_WIN = C_PAD + C_TQ
C_LINE = C_TQ + C_WIN


def _band_bias_line(rel_bias):
    dist_desc = jnp.arange(C_TQ - 1 + C_PAD, C_PAD - C_WIN, -1)
    line = rel_bias[:, jnp.clip(dist_desc, -(CHUNK - 1), MAX_REL_PAST) + (CHUNK - 1)].astype(F32)
    return jnp.pad(line, ((0, 0), (0, C_LINE - line.shape[1])))


def _band_table_init(line_ref, bias_ref):
    tq = C_TQ

    @pl.when(pl.program_id(0) == 0)
    def _():
        i = lax.broadcasted_iota(I32, (tq, C_WIN), 0)
        w = lax.broadcasted_iota(I32, (tq, C_WIN), 1)
        in_band = (w // CHUNK >= i // CHUNK) & (w // CHUNK <= i // CHUNK + N_PREV_CHUNKS)
        for h in range(H_C):
            x = jnp.broadcast_to(line_ref[h:h + 1, :], (tq, C_LINE))
            y = pltpu.roll(x, 1, 1, stride=1, stride_axis=0)
            bias_ref[h] = jnp.where(in_band, y[:, C_LINE - C_WIN:], NEG_INF)


def _band_body(qc_ref, kcT_ref, vc_ref, o_ref, bias_ref):
    tq = C_TQ
    b = pl.program_id(0)
    w0 = pl.multiple_of(b * tq, tq)
    q_all = qc_ref[...] * 0.125
    key_abs = b * tq - C_PAD + lax.broadcasted_iota(I32, (1, C_WIN), 1)
    ok = key_abs >= 0
    logits = [jnp.dot(q_all[:, h * HEAD_DIM:(h + 1) * HEAD_DIM],
                      kcT_ref[h * HEAD_DIM:(h + 1) * HEAD_DIM, pl.ds(w0, C_WIN)],
                      preferred_element_type=F32) for h in range(H_C)]
    heads = []
    for h in range(H_C):
        s = jnp.where(ok, logits[h] + bias_ref[h], NEG_INF)
        m = jnp.max(s, axis=1, keepdims=True)
        p = jnp.exp(s - m)
        l = jnp.sum(p, axis=1, keepdims=True)
        pv = jnp.dot(p.astype(BF16), vc_ref[pl.ds(w0, C_WIN), (h // 2) * LANES:(h // 2 + 1) * LANES],
                     preferred_element_type=F32)
        heads.append(pv / l)
    for pr in range(H_C // 2):
        o_ref[:, pr * LANES:(pr + 1) * LANES] = _pair_select(heads[2 * pr], heads[2 * pr + 1]).astype(o_ref.dtype)


def _band_kernel(qc_ref, kcT_ref, vc_ref, line_ref, o_ref, bias_ref):
    _band_table_init(line_ref, bias_ref)
    _band_body(qc_ref, kcT_ref, vc_ref, o_ref, bias_ref)


def _band_call(qc, kcT_pad, vc_pad, bias_line):
    s = qc.shape[0]
    return pl.pallas_call(
        _band_kernel,
        out_shape=jax.ShapeDtypeStruct((s, W_BRANCH), BF16),
        grid=(s // C_TQ,),
        in_specs=[pl.BlockSpec((C_TQ, 256), lambda i: (i, 0)),
                  pl.BlockSpec(memory_space=pltpu.VMEM),
                  pl.BlockSpec(memory_space=pltpu.VMEM),
                  _whole((H_C, C_LINE))],
        out_specs=pl.BlockSpec((C_TQ, 256), lambda i: (i, 0)),
        scratch_shapes=[pltpu.VMEM((H_C, C_TQ, C_WIN), F32)],
        compiler_params=_cparams(("arbitrary",)),
        name="band_attention",
    )(qc, kcT_pad, vc_pad, bias_line)


B_NCH = 2
B_L = B_NCH * CHUNK
HIGHEST = lax.Precision.HIGHEST


def _mlstm_state_init(tail_ref, ct_ref, n_ref, m_ref):
    @pl.when(pl.program_id(0) == 0)
    def _():
        tail_ref[...] = jnp.zeros(tail_ref.shape, F32)
        ct_ref[...] = jnp.zeros(ct_ref.shape, F32)
        n_ref[...] = jnp.zeros(n_ref.shape, F32)
        m_ref[...] = jnp.zeros(m_ref.shape, F32)


def _mlstm_kernel(qk_ref, vb_ref, ob_ref, misc_ref, cw_ref, cb_ref, gb_ref, ng_ref, o_ref,
                  tail_ref, ct_ref, n_ref, m_ref):
    _mlstm_state_init(tail_ref, ct_ref, n_ref, m_ref)
    _mlstm_body(qk_ref, vb_ref, ob_ref, misc_ref, cw_ref, cb_ref, gb_ref, ng_ref, o_ref,
                tail_ref, ct_ref, n_ref, m_ref)


def _mlstm_body(qk_ref, vb_ref, ob_ref, misc_ref, cw_ref, cb_ref, gb_ref, ng_ref, o_ref,
                tail_ref, ct_ref, n_ref, m_ref):
    L = CHUNK
    T = B_L
    hd = HEAD_DIM

    x = qk_ref[...]
    xx = jnp.concatenate([tail_ref[...], x], axis=0)
    y = jnp.broadcast_to(cb_ref[...], x.shape)
    for j in range(CONV_K):
        y = y + cw_ref[j:j + 1, :] * xx[8 - (CONV_K - 1) + j:8 - (CONV_K - 1) + j + T, :]
    tail_ref[...] = x[T - 8:T, :]
    qk = y * jax.nn.sigmoid(y)
    q_all = qk[:, :W_BRANCH]
    k_all = qk[:, W_BRANCH:] * 0.125
    v_all = vb_ref[...]
    o_gate = jax.nn.sigmoid(ob_ref[...])

    gts = misc_ref[...] + gb_ref[...]
    lf = jnp.minimum(gts, 0.0) - jnp.log1p(jnp.exp(-jnp.abs(gts)))
    r_t = lax.broadcasted_iota(I32, (T, T), 0)
    c_t = lax.broadcasted_iota(I32, (T, T), 1)
    ltri = jnp.where((c_t <= r_t) & (c_t // L == r_t // L), 1.0, 0.0)
    cum = jnp.dot(ltri, lf, precision=HIGHEST, preferred_element_type=F32)
    lane = lax.broadcasted_iota(I32, (T, LANES), 1)
    mixed = jnp.where(lane < MISC_FB, gts, cum)
    sel_r = lax.broadcasted_iota(I32, (8, LANES), 0)
    sel_c = lax.broadcasted_iota(I32, (8, LANES), 1)
    sel = jnp.where(sel_c == sel_r + MISC_IB, 1.0, 0.0)
    rows = lax.dot_general(sel, mixed, (((1,), (1,)), ((), ())), precision=HIGHEST,
                           preferred_element_type=F32)
    causal = lax.broadcasted_iota(I32, (L, L), 1) <= lax.broadcasted_iota(I32, (L, L), 0)

    heads = range(H_B)
    sl = lambda a, t0, h: a[t0:t0 + L, h * hd:(h + 1) * hd]
    pre = []
    for ci in range(B_NCH):
        t0 = ci * L
        per_head = []
        for h in heads:
            q, k, v = sl(q_all, t0, h), sl(k_all, t0, h), sl(v_all, t0, h)
            qb, kb, vbf = q.astype(BF16), k.astype(BF16), v.astype(BF16)
            qkt = lax.dot_general(qb, kb, (((1,), (1,)), ((), ())), preferred_element_type=F32)
            it_r = rows[h:h + 1, t0:t0 + L]
            cum_r = rows[H_B + h:H_B + h + 1, t0:t0 + L]
            it_c = gts[t0:t0 + L, MISC_IB + h:MISC_IB + h + 1]
            cum_c = cum[t0:t0 + L, MISC_FB + h:MISC_FB + h + 1]
            dmat = jnp.where(causal, cum_c - cum_r + it_r, NEG_INF)
            dmax = jnp.max(dmat, axis=1, keepdims=True)
            cum_end = cum_c[L - 1:L, :]
            g = cum_end - cum_c + it_c
            gmax = jnp.max(g, axis=0, keepdims=True)
            per_head.append((q, k, qb, vbf, qkt, dmat, dmax, cum_c, cum_end, g, gmax))
        pre.append(per_head)

    state = [(ct_ref[h], n_ref[h], m_ref[h]) for h in heads]
    out_chunks = []
    for ci in range(B_NCH):
        t0 = ci * L
        qc = [jnp.dot(pre[ci][h][2], state[h][0].astype(BF16), preferred_element_type=F32) for h in heads]
        mid = []
        for h in heads:
            q, k, qb, vbf, qkt, dmat, dmax, cum_c, cum_end, g, gmax = pre[ci][h]
            ct, n_row, m_prev = state[h]
            m_inter = cum_c + m_prev
            m_t = jnp.maximum(m_inter, dmax)
            w = jnp.exp(dmat - m_t) * qkt
            inter = jnp.exp(m_inter - m_t)
            m_new = jnp.maximum(cum_end + m_prev, gmax)
            carry_scale = jnp.exp(cum_end + m_prev - m_new)
            src_k = jnp.exp(g - m_new) * k
            mid.append((w, inter, m_t, m_new, carry_scale, src_k))
        wv = [jnp.dot(mid[h][0].astype(BF16), pre[ci][h][3], preferred_element_type=F32) for h in heads]
        upd = [lax.dot_general(mid[h][5].astype(BF16), pre[ci][h][3], (((0,), (0,)), ((), ())),
                               preferred_element_type=F32) for h in heads]
        outs = []
        for h in heads:
            q = pre[ci][h][0]
            w, inter, m_t, m_new, carry_scale, src_k = mid[h]
            ct, n_row, _ = state[h]
            num = inter * qc[h] + wv[h]
            den = inter * jnp.sum(q * n_row, axis=1, keepdims=True) + jnp.sum(w, axis=1, keepdims=True)
            h_t = num / jnp.maximum(jnp.abs(den), jnp.exp(-m_t))
            state[h] = (carry_scale * ct + upd[h],
                        carry_scale * n_row + jnp.sum(src_k, axis=0, keepdims=True), m_new)
            ms = jnp.mean(h_t * h_t, axis=1, keepdims=True)
            hn = h_t * lax.rsqrt(ms + EPS) * ng_ref[:, h * hd:(h + 1) * hd]
            outs.append(sl(o_gate, t0, h) * hn)
        out_chunks.append(jnp.concatenate(outs, axis=1))
    for h in heads:
        ct_ref[h], n_ref[h], m_ref[h] = state[h]
    o_ref[...] = jnp.concatenate(out_chunks, axis=0).astype(o_ref.dtype)


def _mlstm_call(qkb, vb, ob, misc, conv_w, conv_b, i_bias, f_bias, norm_g):
    s = qkb.shape[0]
    gbias = jnp.zeros((1, LANES), F32)
    gbias = gbias.at[0, MISC_IB:MISC_IB + H_B].set(i_bias).at[0, MISC_FB:MISC_FB + H_B].set(f_bias)
    row = lambda w: pl.BlockSpec((B_L, w), lambda i: (i, 0))
    return pl.pallas_call(
        _mlstm_kernel,
        out_shape=jax.ShapeDtypeStruct((s, W_BRANCH), BF16),
        grid=(s // B_L,),
        in_specs=[row(512), row(256), row(256), row(128),
                  _whole((CONV_K, 512)), _whole((1, 512)), _whole((1, LANES)), _whole((1, 256))],
        out_specs=row(256),
        scratch_shapes=[pltpu.VMEM((8, 512), F32),
                        pltpu.VMEM((H_B, HEAD_DIM, HEAD_DIM), F32),
                        pltpu.VMEM((H_B, 1, HEAD_DIM), F32),
                        pltpu.VMEM((H_B, 1, 1), F32)],
        compiler_params=_cparams(("arbitrary",)),
        name="mlstm_mixer",
    )(qkb, vb, ob, misc, conv_w, conv_b.reshape(1, 512), gbias, norm_g.reshape(1, 256))


N_B_IN, N_C_IN = 8, 4


def _mlstm_band_kernel(*refs):
    b_in = refs[:N_B_IN]
    c_in = refs[N_B_IN:N_B_IN + N_C_IN]
    yb_ref, yc_ref = refs[N_B_IN + N_C_IN:N_B_IN + N_C_IN + 2]
    b_scr = refs[N_B_IN + N_C_IN + 2:N_B_IN + N_C_IN + 6]
    (bias_ref,) = refs[N_B_IN + N_C_IN + 6:]
    qc_ref, kcT_ref, vc_ref, line_ref = c_in
    _mlstm_state_init(*b_scr)
    _band_table_init(line_ref, bias_ref)
    _mlstm_body(*b_in, yb_ref, *b_scr)
    _band_body(qc_ref, kcT_ref, vc_ref, yc_ref, bias_ref)


def _mlstm_band_call(qkb, vb, ob, misc, conv_w, conv_b, i_bias, f_bias, norm_g,
                     qc, kcT_pad, vc_pad, bias_line):
    assert B_L == C_TQ
    s = qkb.shape[0]
    gbias = jnp.zeros((1, LANES), F32)
    gbias = gbias.at[0, MISC_IB:MISC_IB + H_B].set(i_bias).at[0, MISC_FB:MISC_FB + H_B].set(f_bias)
    row = lambda w: pl.BlockSpec((B_L, w), lambda i: (i, 0))
    vm = pl.BlockSpec(memory_space=pltpu.VMEM)
    return pl.pallas_call(
        _mlstm_band_kernel,
        out_shape=(jax.ShapeDtypeStruct((s, W_BRANCH), BF16), jax.ShapeDtypeStruct((s, W_BRANCH), BF16)),
        grid=(s // B_L,),
        in_specs=[row(512), row(256), row(256), row(128),
                  _whole((CONV_K, 512)), _whole((1, 512)), _whole((1, LANES)), _whole((1, 256)),
                  row(256), vm, vm, _whole((H_C, C_LINE))],
        out_specs=(row(256), row(256)),
        scratch_shapes=[pltpu.VMEM((8, 512), F32),
                        pltpu.VMEM((H_B, HEAD_DIM, HEAD_DIM), F32),
                        pltpu.VMEM((H_B, 1, HEAD_DIM), F32),
                        pltpu.VMEM((H_B, 1, 1), F32),
                        pltpu.VMEM((H_C, C_TQ, C_WIN), F32)],
        compiler_params=_cparams(("arbitrary",)),
        name="mlstm_band",
    )(qkb, vb, ob, misc, conv_w, conv_b.reshape(1, 512), gbias, norm_g.reshape(1, 256),
      qc, kcT_pad, vc_pad, bias_line)


M_TM = 256
ROUTE_LOGIT0 = N_GROUPS
RT_E1, RT_E2, RT_G1, RT_G2 = 0, 1, 2, 3


def _lane_argmax(vals, lane):
    v = jnp.max(vals, axis=1, keepdims=True)
    idx = jnp.min(jnp.where(vals == v, lane, float(LANES)), axis=1, keepdims=True)
    return v, idx


def _merge_kernel(x_ref, ya_ref, yb_ref, yc_ref, yd_ref, gm_ref, wg_ref, wb_ref, wo_ref, gf_ref,
                  wr_ref, rb_ref, xo_ref, h2_ref, rt_ref):
    x = x_ref[...]
    h = _rms(x, gm_ref[...]).astype(BF16)
    mixed = jnp.zeros(x.shape, F32)
    for n, y_ref in enumerate((ya_ref, yb_ref, yc_ref, yd_ref)):
        gate = jax.nn.sigmoid(jnp.dot(h, wg_ref[:, n * D_MODEL:(n + 1) * D_MODEL],
                                      preferred_element_type=F32))
        up = jnp.dot(y_ref[...], wb_ref[n], preferred_element_type=F32)
        mixed = mixed + gate * up
    xn = x + jnp.dot(mixed.astype(BF16), wo_ref[...], preferred_element_type=F32)
    xo_ref[...] = xn
    h2 = _rms(xn, gf_ref[...])
    h2_ref[...] = h2
    logits = jnp.dot(h2.astype(BF16), wr_ref[...], preferred_element_type=F32) + rb_ref[...]

    lane = lax.broadcasted_iota(I32, logits.shape, 1).astype(F32)
    neg = -jnp.inf
    gmask = lane < N_GROUPS
    gmax, g_sel = _lane_argmax(jnp.where(gmask, logits, neg), lane)
    g_gate = 1.0 / jnp.sum(jnp.where(gmask, jnp.exp(logits - gmax), 0.0), axis=1, keepdims=True)
    e_lo = ROUTE_LOGIT0 + EXPERTS_PER_GROUP * g_sel
    el = jnp.where((lane >= e_lo) & (lane < e_lo + EXPERTS_PER_GROUP), logits, neg)
    v1, i1 = _lane_argmax(el, lane)
    v2, i2 = _lane_argmax(jnp.where(lane == i1, neg, el), lane)
    e = jnp.exp(v2 - v1)
    p1 = 1.0 / (1.0 + e)
    p2 = e / (1.0 + e)
    rt = jnp.where(lane == RT_E1, i1 - ROUTE_LOGIT0,
                   jnp.where(lane == RT_E2, i2 - ROUTE_LOGIT0,
                             jnp.where(lane == RT_G1, p1 * g_gate,
                                       jnp.where(lane == RT_G2, p2 * g_gate, 0.0))))
    rt_ref[...] = rt


def _merge_call(x2, ya, yb, yc, yd, g_mix, w_gate, w_branch, w_out, g_ffn, w_route, b_route):
    s = x2.shape[0]
    row = lambda w: pl.BlockSpec((M_TM, w), lambda i: (i, 0))
    vm = pl.BlockSpec(memory_space=pltpu.VMEM)
    return pl.pallas_call(
        _merge_kernel,
        out_shape=(jax.ShapeDtypeStruct((s, D_MODEL), F32), jax.ShapeDtypeStruct((s, D_MODEL), F32),
                   jax.ShapeDtypeStruct((s, LANES), F32)),
        grid=(s // M_TM,),
        in_specs=[row(D_MODEL), row(256), row(256), row(256), row(256),
                  vm, vm, vm, vm, vm, vm, vm],
        out_specs=(row(D_MODEL), row(D_MODEL), row(LANES)),
        compiler_params=_cparams(("parallel",)),
        name="merge_route",
    )(x2, ya, yb, yc, yd, g_mix.reshape(1, D_MODEL), w_gate, w_branch, w_out,
      g_ffn.reshape(1, D_MODEL), w_route, b_route)


E_BM = 128
GATHER_UNROLL = 8


def _gather_rows(src_hbm, idx_ref, dst, sem, n, wait, inline=False):
    def one(r, parity):
        cp = pltpu.make_async_copy(src_hbm.at[pl.ds(idx_ref[0, 0, r], 1)], dst.at[pl.ds(r, 1)], sem)
        if wait:
            cp.wait()
        else:
            cp.start(priority=parity)

    if inline:
        for r in range(n):
            one(r, r % 2)
    else:
        def group(g, c):
            for u in range(GATHER_UNROLL):
                one(g * GATHER_UNROLL + u, u % 2)
            return c
        lax.fori_loop(0, n // GATHER_UNROLL, group, 0)


def _moe_kernel(be_ref, nused_ref, tok_ref, tok_next_ref, h2_hbm, w1_ref, w3_ref, w2_ref, o_ref,
                xbuf, w1b, w3b, w2b, sem):
    i = pl.program_id(0)
    nused = nused_ref[0]
    used = i < nused
    slot = i % 2

    @pl.when((i == 0) & used)
    def _():
        _gather_rows(h2_hbm, tok_ref, xbuf.at[0], sem.at[0], E_BM, wait=False)

    @pl.when(used)
    def _():
        prev = be_ref[jnp.maximum(i - 1, 0)]

        @pl.when((i == 0) | (be_ref[i] != prev))
        def _():
            w1b[...] = w1_ref[0, 0].astype(BF16)
            w3b[...] = w3_ref[0, 0].astype(BF16)
            w2b[...] = w2_ref[0, 0].astype(BF16)

        _gather_rows(h2_hbm, tok_ref, xbuf.at[slot], sem.at[slot], E_BM, wait=True)
        _gather_rows(h2_hbm, tok_next_ref, xbuf.at[1 - slot], sem.at[1 - slot], E_BM, wait=False, inline=True)
        xb = xbuf[slot].astype(BF16)
        a = jnp.dot(xb, w1b[...], preferred_element_type=F32)
        g = jnp.dot(xb, w3b[...], preferred_element_type=F32)
        hid = (a * jax.nn.sigmoid(a) * g).astype(BF16)
        o_ref[...] = jnp.dot(hid, w2b[...], preferred_element_type=F32)

    @pl.when(jnp.logical_not(used))
    def _():
        @pl.when(i == nused)
        def _():
            _gather_rows(h2_hbm, tok_ref, xbuf.at[slot], sem.at[slot], E_BM, wait=True)
        o_ref[...] = jnp.zeros(o_ref.shape, F32)


def _moe_call(h2, row_tok, block_e, nused, w1, w3, w2, layer):
    n_rows = row_tok.shape[0]
    n_blocks = n_rows // E_BM
    wspec = lambda shp: pl.BlockSpec((1, 1) + shp, lambda i, be, nu: (layer, be[i], 0, 0))
    grid_spec = pltpu.PrefetchScalarGridSpec(
        num_scalar_prefetch=2,
        grid=(n_blocks,),
        in_specs=[pl.BlockSpec((1, 1, E_BM), lambda i, be, nu: (i, 0, 0), memory_space=pltpu.SMEM),
                  pl.BlockSpec((1, 1, E_BM), lambda i, be, nu: (jnp.minimum(i + 1, n_blocks - 1), 0, 0),
                               memory_space=pltpu.SMEM),
                  pl.BlockSpec(memory_space=pl.ANY),
                  wspec((D_MODEL, F_EXPERT)), wspec((D_MODEL, F_EXPERT)), wspec((F_EXPERT, D_MODEL))],
        out_specs=pl.BlockSpec((E_BM, D_MODEL), lambda i, be, nu: (i, 0)),
        scratch_shapes=[pltpu.VMEM((2, E_BM, D_MODEL), F32),
                        pltpu.VMEM((D_MODEL, F_EXPERT), BF16),
                        pltpu.VMEM((D_MODEL, F_EXPERT), BF16),
                        pltpu.VMEM((F_EXPERT, D_MODEL), BF16),
                        pltpu.SemaphoreType.DMA((2,))],
    )
    tok3 = row_tok.reshape(n_blocks, 1, E_BM)
    return pl.pallas_call(
        _moe_kernel,
        out_shape=jax.ShapeDtypeStruct((n_rows, D_MODEL), F32),
        grid_spec=grid_spec,
        compiler_params=_cparams(("arbitrary",)),
        name="moe_experts",
    )(block_e, nused, tok3, tok3, h2, w1, w3, w2)


CB_TM = 128


def _combine_kernel(final, pos_ref, pos_next_ref, x_ref, rt_ref, ys_hbm, gfin_ref, o_ref, buf, sem):
    i = pl.program_id(0)
    slot = i % 2
    n = 2 * CB_TM

    @pl.when(i == 0)
    def _():
        _gather_rows(ys_hbm, pos_ref, buf.at[0], sem.at[0], n, wait=False)

    @pl.when(i + 1 < pl.num_programs(0))
    def _():
        _gather_rows(ys_hbm, pos_next_ref, buf.at[1 - slot], sem.at[1 - slot], n, wait=False)
    _gather_rows(ys_hbm, pos_ref, buf.at[slot], sem.at[slot], n, wait=True)

    rt = rt_ref[...]
    y = (buf[slot, 0:CB_TM, :] * rt[:, RT_G1:RT_G1 + 1]
         + buf[slot, CB_TM:2 * CB_TM, :] * rt[:, RT_G2:RT_G2 + 1])
    xn = x_ref[...] + y
    if final:
        xn = _rms(xn, gfin_ref[...])
    o_ref[...] = xn


def _combine_call(x2, rt, ys, pos, g_final, final):
    s = x2.shape[0]
    nb = s // CB_TM
    pos_b = pos.reshape(nb, CB_TM, 2).transpose(0, 2, 1).reshape(nb, 1, 2 * CB_TM)
    return pl.pallas_call(
        functools.partial(_combine_kernel, final),
        out_shape=jax.ShapeDtypeStruct((s, D_MODEL), F32),
        grid=(nb,),
        in_specs=[pl.BlockSpec((1, 1, 2 * CB_TM), lambda i: (i, 0, 0), memory_space=pltpu.SMEM),
                  pl.BlockSpec((1, 1, 2 * CB_TM), lambda i: (jnp.minimum(i + 1, nb - 1), 0, 0),
                               memory_space=pltpu.SMEM),
                  pl.BlockSpec((CB_TM, D_MODEL), lambda i: (i, 0)),
                  pl.BlockSpec((CB_TM, LANES), lambda i: (i, 0)),
                  pl.BlockSpec(memory_space=pl.ANY),
                  _whole((1, D_MODEL))],
        out_specs=pl.BlockSpec((CB_TM, D_MODEL), lambda i: (i, 0)),
        scratch_shapes=[pltpu.VMEM((2, 2 * CB_TM, D_MODEL), F32), pltpu.SemaphoreType.DMA((2,))],
        compiler_params=_cparams(("arbitrary",)),
        name="moe_combine",
    )(pos_b, pos_b, x2, rt, ys, g_final.reshape(1, D_MODEL))


PLAN_BLOCK = 256


def _dispatch_plan(rt):
    t = rt.shape[0]
    n_assign = t * TOP_K_INNER
    eid = rt[:, RT_E1:RT_E2 + 1].astype(I32).reshape(n_assign)
    onehot = (eid[:, None] == jnp.arange(N_EXPERTS, dtype=I32)[None, :]).astype(F32)
    pb = PLAN_BLOCK
    oh3 = onehot.reshape(n_assign // pb, pb, N_EXPERTS)
    tri = jnp.tril(jnp.ones((pb, pb), F32), k=-1)
    within = jnp.einsum('ij,bjk->bik', tri, oh3)
    block_tot = jnp.sum(oh3, axis=1)
    block_off = jnp.cumsum(block_tot, axis=0) - block_tot
    counts = (block_off[-1] + block_tot[-1]).astype(I32)
    rank = jnp.sum(oh3 * (within + block_off[:, None, :]), axis=-1).reshape(n_assign)
    padded = (counts + E_BM - 1) // E_BM * E_BM
    ends_pad = jnp.cumsum(padded)
    starts_pad = ends_pad - padded
    dest = (jnp.sum(onehot * starts_pad.astype(F32)[None, :], axis=-1) + rank).astype(I32)
    n_rows = (-(-(n_assign + N_EXPERTS * (E_BM - 1)) // E_BM) + 1) * E_BM
    n_blocks = n_rows // E_BM
    tok = jnp.repeat(jnp.arange(t, dtype=I32), TOP_K_INNER)
    row_tok = jnp.zeros((n_rows,), I32).at[dest].set(tok)
    block_start = jnp.arange(n_blocks, dtype=I32) * E_BM
    block_e = jnp.minimum(jnp.sum((ends_pad[None, :] <= block_start[:, None]).astype(I32), axis=1),
                          N_EXPERTS - 1)
    nused = (ends_pad[-1] // E_BM).astype(I32).reshape(1)
    return row_tok, block_e, nused, dest.reshape(t, TOP_K_INNER)


def _layer(x2, layer_idx, p, experts, final_g, final):
    (norm_mix_g, w_in, conv_w, conv_b, i_bias, f_bias, mnorm_g, rel_bias, lam_p, dnorm_g,
     w_branch, w_gate, w_out, norm_ffn_g, rgw, rgb, rew, reb) = p
    o = _k1_call(x2, norm_mix_g, *_rearrange_w_in(w_in))

    ya = _dsa_call(o["qaT"], o["qiT"], o["wiT"], o["ka"], o["vaT"], o["misc_bf"])
    kcT_pad = jnp.pad(o["kcT"], ((0, 0), (C_PAD, 0)))
    vc_pad = jnp.pad(o["vc"], ((C_PAD, 0), (0, 0)))
    yb, yc = _mlstm_band_call(o["qkb"], o["vb"], o["ob"], o["misc"], conv_w, conv_b, i_bias, f_bias, mnorm_g,
                              o["qc"], kcT_pad, vc_pad, _band_bias_line(rel_bias))
    lam_init = 0.8 - 0.6 * math.exp(-0.3 * layer_idx)
    yd = _diff_call(o["qdT"], o["kd"], o["vdT"], lam_p, dnorm_g, lam_init)

    w_route = jnp.concatenate([rgw, rew, jnp.zeros((D_MODEL, LANES - N_GROUPS - N_EXPERTS), F32)],
                              axis=1).astype(BF16)
    b_route = jnp.concatenate([rgb, reb, jnp.zeros((LANES - N_GROUPS - N_EXPERTS,), F32)]).reshape(1, LANES)
    xn, h2, rt = _merge_call(x2, ya, yb, yc, yd, norm_mix_g, w_gate.astype(BF16), w_branch.astype(BF16),
                             w_out.astype(BF16), norm_ffn_g, w_route, b_route)
    row_tok, block_e, nused, pos = _dispatch_plan(rt)
    ys = _moe_call(h2, row_tok, block_e, nused, *experts, layer_idx)
    return _combine_call(xn, rt, ys, pos, final_g, final)


def kernel(x, norm_mix_g, w_in, conv_w, conv_b, mlstm_i_bias, mlstm_f_bias, mlstm_norm_g, relpos_bias, diff_lambda, diff_norm_g, w_branch, w_gate, w_out, norm_ffn_g, router_group_w, router_group_b, router_expert_w, router_expert_b, expert_w1, expert_w3, expert_w2, final_norm_g):
    assert x.shape[0] == 1 and x.shape[2] == D_MODEL
    params = (norm_mix_g, w_in, conv_w, conv_b, mlstm_i_bias, mlstm_f_bias, mlstm_norm_g, relpos_bias,
              diff_lambda, diff_norm_g, w_branch, w_gate, w_out, norm_ffn_g, router_group_w,
              router_group_b, router_expert_w, router_expert_b)
    experts = (expert_w1, expert_w3, expert_w2)
    depth = norm_mix_g.shape[0]
    x2 = x[0]
    for l in range(depth):
        x2 = _layer(x2, l, tuple(a[l] for a in params), experts, final_norm_g, l == depth - 1)
    return x2[None]
```

```python
import functools
import math

import jax
import jax.numpy as jnp
from jax import lax
from jax.experimental import pallas as pl
from jax.experimental.pallas import tpu as pltpu

F32 = jnp.float32
BF16 = jnp.bfloat16
I32 = jnp.int32

D_MODEL = 1024
CHUNK = 64
HEAD_DIM = 64
NEG_INF = -1e30
H_A = 4
H_IDX = 4
D_IDX = 32
TOPK_MAX = 256
H_B = 4
CONV_K = 4
H_C = 4
N_PREV_CHUNKS = 8
MAX_REL_PAST = 128
H_D = 4
DQ_D = 32
W_BRANCH = 256
N_BRANCH = 4
N_GROUPS = 4
EXPERTS_PER_GROUP = 8
N_EXPERTS = 32
TOP_K_INNER = 2
F_EXPERT = 512
EPS = 1e-6

VMEM_LIMIT_BYTES = 52 * 1024 * 1024
LANES = 128

INT_MIN = -(2 ** 31)
I16_MIN = -(2 ** 15)
I16 = jnp.int16
M_INIT = -5e29

_COL_SIZES = (256, 256, 256, 128, 32, 4, 256, 256, 256, 4, 4, 256, 256, 256, 256, 256, 256, 256)
_COL_NAMES = ("qa", "ka", "va", "qi", "ki", "wi", "qb", "kb", "vb", "ib", "fb", "ob",
              "qc", "kc", "vc", "qd", "kd", "vd")
_COL_OFF = {}
_o = 0
for _n, _s in zip(_COL_NAMES, _COL_SIZES):
    _COL_OFF[_n] = (_o, _s)
    _o += _s
C_IN = _o
MISC_KI = 0
MISC_WI = 32
MISC_IB = 36
MISC_FB = 40


def _cparams(sem):
    return pltpu.CompilerParams(dimension_semantics=sem, vmem_limit_bytes=VMEM_LIMIT_BYTES)


def _whole(shape):
    nd = len(shape)
    return pl.BlockSpec(shape, lambda *_: (0,) * nd)


_K1_NAT = (("ka", 256, BF16), ("misc", 128, F32), ("qkb", 512, F32), ("vb", 256, F32), ("ob", 256, F32),
           ("qc", 256, BF16), ("vc", 256, BF16), ("kd", 256, BF16))
_K1_TR = (("qaT", 256, BF16), ("qiT", 128, BF16), ("vaT", 256, BF16), ("kcT", 256, BF16),
          ("qdT", 256, BF16), ("vdT", 256, BF16), ("wiT", 8, F32))
K1_NAT_WIDTH = sum(w for _, w, _ in _K1_NAT)
K1_TR_WIDTH = sum(w for _, w, _ in _K1_TR)
K1_TM = 512
V_AUG = 80
_K1_VAUG = ("vaT", "vdT")


def _k1_tr_rows(name, width):
    return (width // HEAD_DIM) * V_AUG if name in _K1_VAUG else width


def _rearrange_w_in(w_in):
    def cols(name):
        o, s = _COL_OFF[name]
        return w_in[:, o:o + s]
    d = w_in.shape[0]
    misc = jnp.concatenate([cols("ki"), cols("wi"), cols("ib"), cols("fb"),
                            jnp.zeros((d, LANES - 44), w_in.dtype)], axis=1)
    nat = [cols("ka"), misc, cols("qb"), cols("kb"), cols("vb"), cols("ob"), cols("qc"), cols("vc"),
           cols("kd")]
    tr = [cols("qa"), cols("qi"), cols("va"), cols("kc"), cols("qd"), cols("vd"), cols("wi"),
          jnp.zeros((d, 8 - H_IDX), w_in.dtype)]
    return (jnp.concatenate(nat, axis=1).astype(BF16), jnp.concatenate(tr, axis=1).T.astype(BF16))


def _rms(x, g):
    ms = jnp.mean(x * x, axis=-1, keepdims=True)
    return x * lax.rsqrt(ms + EPS) * g


def _k1_kernel(x_ref, g_ref, wn_ref, wt_ref, *out_refs):
    h = _rms(x_ref[...], g_ref[...]).astype(BF16)
    refs = dict(zip([n for n, _, _ in _K1_NAT] + ["misc_bf"] + [n for n, _, _ in _K1_TR], out_refs))
    off = 0
    for name, width, _ in _K1_NAT:
        r = jnp.dot(h, wn_ref[:, off:off + width], preferred_element_type=F32)
        refs[name][...] = r.astype(refs[name].dtype)
        if name == "misc":
            refs["misc_bf"][...] = r.astype(BF16)
        off += width
    off = 0
    for name, width, _ in _K1_TR:
        r = lax.dot_general(wt_ref[off:off + width, :], h, (((1,), (1,)), ((), ())),
                            preferred_element_type=F32)
        if name in _K1_VAUG:
            tail = (lax.broadcasted_iota(I32, (V_AUG - HEAD_DIM, r.shape[1]), 0) == 0)
            tail = jnp.where(tail, 1.0, 0.0).astype(BF16)
            for hh in range(width // HEAD_DIM):
                refs[name][hh * V_AUG:hh * V_AUG + HEAD_DIM, :] = \
                    r[hh * HEAD_DIM:(hh + 1) * HEAD_DIM, :].astype(BF16)
                refs[name][hh * V_AUG + HEAD_DIM:(hh + 1) * V_AUG, :] = tail
        else:
            refs[name][...] = r.astype(refs[name].dtype)
        off += width


def _k1_call(x2, g, w_nat, w_trT):
    s = x2.shape[0]
    tm = K1_TM
    outs = (tuple(jax.ShapeDtypeStruct((s, w), dt) for _, w, dt in _K1_NAT)
            + (jax.ShapeDtypeStruct((s, LANES), BF16),)
            + tuple(jax.ShapeDtypeStruct((_k1_tr_rows(n, w), s), dt) for n, w, dt in _K1_TR))
    out_specs = (tuple(pl.BlockSpec((tm, w), lambda i: (i, 0)) for _, w, _ in _K1_NAT)
                 + (pl.BlockSpec((tm, LANES), lambda i: (i, 0)),)
                 + tuple(pl.BlockSpec((_k1_tr_rows(n, w), tm), lambda i: (0, i)) for n, w, _ in _K1_TR))
    res = pl.pallas_call(
        _k1_kernel,
        out_shape=outs,
        grid=(s // tm,),
        in_specs=[pl.BlockSpec((tm, D_MODEL), lambda i: (i, 0)),
                  _whole((1, D_MODEL)),
                  _whole((D_MODEL, K1_NAT_WIDTH)),
                  _whole((K1_TR_WIDTH, D_MODEL))],
        out_specs=out_specs,
        compiler_params=_cparams(("parallel",)),
        name="k1_norm_proj",
    )(x2, g.reshape(1, D_MODEL), w_nat, w_trT)
    return dict(zip([n for n, _, _ in _K1_NAT] + ["misc_bf"] + [n for n, _, _ in _K1_TR], res))


A_TQ = 256
A_KT = 512
A_SLAB = 256
A_SWEEP_SLABS = 8
LOG2E = 1.4426950408889634
QK_AHEAD = 4
BF16_ROWS = 16


def _flash_step(s, i, v_t, m_ref, acc_ref):
    kt, tq = s.shape
    parts = s.reshape(kt // BF16_ROWS, BF16_ROWS, tq)
    parts = [parts[g] for g in range(kt // BF16_ROWS)]
    while len(parts) > 1:
        parts = [jnp.maximum(parts[g], parts[g + 1]) for g in range(0, len(parts), 2)]
    m_old = m_ref[i]
    m_new = jnp.maximum(m_old, jnp.max(parts[0].astype(F32), axis=0, keepdims=True))
    alpha = jnp.exp2(m_old - m_new)
    p = jnp.exp2(s - m_new.astype(BF16))
    acc_ref[i] = alpha * acc_ref[i] + jnp.dot(v_t, p, preferred_element_type=F32)
    m_ref[i] = m_new


def _flash_result(acc_ref, i):
    return acc_ref[i, 0:HEAD_DIM, :] / acc_ref[i, HEAD_DIM:HEAD_DIM + 1, :]


def _logits(k_t, q_pad):
    return jnp.dot(k_t, q_pad, preferred_element_type=F32).astype(BF16)


def _qk_prologue(k_t, qpad_ref, n, s_ref):
    for i in range(min(QK_AHEAD, n)):
        s_ref[i] = _logits(k_t, qpad_ref[i])


def _flash_tile(k_t, qpad_ref, n, v_tile, bias, m_ref, acc_ref, s_ref=None, k_next=None):
    a = min(QK_AHEAD, n)
    if s_ref is None:
        pend = [_logits(k_t, qpad_ref[i]) for i in range(a)]
    else:
        pend = [s_ref[i] for i in range(a)]
    for i in range(n):
        s = pend.pop(0)
        if i + a < n:
            pend.append(_logits(k_t, qpad_ref[i + a]))
        elif s_ref is not None:
            s_ref[i + a - n] = _logits(k_next, qpad_ref[i + a - n])
        if bias is not None:
            s = s + bias
        _flash_step(s, i, v_tile(i), m_ref, acc_ref)


def _bit_transpose32(words):
    a = list(words)
    j, m = 16, 0x0000FFFF
    while j:
        k = 0
        while k < 32:
            t = (a[k] ^ lax.shift_right_logical(a[k + j], j)) & m
            a[k] = a[k] ^ t
            a[k + j] = a[k + j] ^ lax.shift_left(t, j)
            k = (k + j + 1) & ~j
        j >>= 1
        m = (m ^ (m << j)) & 0xFFFFFFFF
    return a


def _dsa_kernel(topk, qaT_ref, qiT_ref, wiT_ref, qiTn_ref, wiTn_ref, ka_ref, vaT_ref, mb_ref, tri_ref,
                o_ref, planes_ref, cand_ref, above_ref, qpad_ref, qipad_ref, acc_ref, m_ref, carry_ref):
    tq, kt = A_TQ, A_KT
    b = pl.program_id(0)
    nb = pl.num_programs(0)
    ntiles = ((b + 1) * tq + kt - 1) // kt

    def vis_end_of(blk):
        q_pos = blk * tq + lax.broadcasted_iota(I32, (1, tq), 1)
        return (q_pos // CHUNK + 1) * CHUNK

    def set_qipad(src_ref):
        qiT = src_ref[...]
        for h in range(H_IDX):
            qipad_ref[h, 0:D_IDX, :] = qiT[h * D_IDX:(h + 1) * D_IDX, :]
            qipad_ref[h, D_IDX:LANES, :] = jnp.zeros((LANES - D_IDX, tq), BF16)

    def p1(j, w_ref, vis_end):
        s0 = pl.multiple_of(j * kt, kt)
        mb = mb_ref[pl.ds(s0, kt), :]
        score = jnp.zeros((kt, tq), F32)
        for h in range(H_IDX):
            r = jnp.dot(mb, qipad_ref[h], preferred_element_type=F32)
            score = score + jnp.maximum(r, 0.0) * w_ref[h:h + 1, :]
        bits = lax.bitcast_convert_type(score, I32)
        ukey = bits ^ (lax.shift_right_arithmetic(bits, 31) | INT_MIN)
        if vis_end is not None:
            s_pos = s0 + lax.broadcasted_iota(I32, (kt, 1), 0)
            ukey = jnp.where(s_pos < vis_end, ukey, 0)
        u4 = ukey.reshape(kt // A_SLAB, 32, 8, tq)
        for s2 in range(kt // A_SLAB):
            planes = _bit_transpose32([u4[s2, v] for v in range(32)])
            for r in range(32):
                planes_ref[j * (kt // A_SLAB) + s2, r] = planes[r]

    @pl.when(b == 0)
    def _():
        set_qipad(qiT_ref)
        vis0 = vis_end_of(0)

        def first_block(j, carry):
            p1(j, wiT_ref, vis0)
            return carry
        lax.fori_loop(0, ntiles, first_block, 0)

    kf = float(topk)
    nslab = ntiles * (kt // A_SLAB)
    nstep = (nslab + A_SWEEP_SLABS - 1) // A_SWEEP_SLABS

    def init_sets(sl, carry):
        cand_ref[sl] = jnp.full((8, tq), -1, I32)
        above_ref[sl] = jnp.zeros((8, tq), I32)
        return carry
    lax.fori_loop(0, nslab, init_sets, 0)

    def init_pad(sl, carry):
        cand_ref[sl] = jnp.zeros((8, tq), I32)
        above_ref[sl] = jnp.zeros((8, tq), I32)
        planes_ref[sl] = jnp.zeros((32, 8, tq), I32)
        return carry
    lax.fori_loop(nslab, nstep * A_SWEEP_SLABS, init_pad, 0)

    def apply_decision(sl, prev_plane, took_one):
        cand = cand_ref[sl]
        ones = cand & prev_plane
        cand = jnp.where(took_one, ones, cand ^ ones)
        above_ref[sl] = jnp.where(took_one, above_ref[sl], above_ref[sl] | ones)
        cand_ref[sl] = cand
        return cand

    def sweep(i, carry):
        n_above, took, tau = carry
        took_one = jnp.broadcast_to(took, (8, tq)) != 0
        first = i == 0

        def step(jj, acc):
            for s2 in range(A_SWEEP_SLABS):
                sl = jj * A_SWEEP_SLABS + s2
                prev_plane = jnp.where(first, -1, planes_ref[sl, jnp.maximum(i - 1, 0)])
                cand = apply_decision(sl, prev_plane, took_one)
                acc = acc + lax.population_count(cand & planes_ref[sl, i])
            return acc
        acc = lax.fori_loop(0, nstep, step, jnp.zeros((8, tq), I32))
        n_one = jnp.sum(acc.astype(F32), axis=0, keepdims=True)
        take = (n_above + n_one) >= kf
        n_above = jnp.where(take, n_above, n_above + n_one)
        tau = jnp.where(take, tau | lax.shift_left(jnp.int32(1), 31 - i), tau)
        return n_above, jnp.where(take, 1, 0), tau
    n_above, took, tau = lax.fori_loop(
        0, 32, sweep, (jnp.zeros((1, tq), F32), jnp.ones((1, tq), I32), jnp.zeros((1, tq), I32)))

    def last_decision(sl, carry):
        apply_decision(sl, planes_ref[sl, 31], jnp.broadcast_to(took, (8, tq)) != 0)
        return carry
    lax.fori_loop(0, nslab, last_decision, 0)
    need = jnp.where(tau == 0, 0.0, kf - n_above)

    m_ref[...] = jnp.full(m_ref.shape, M_INIT, F32)
    acc_ref[...] = jnp.zeros(acc_ref.shape, F32)
    carry_ref[...] = jnp.zeros(carry_ref.shape, F32)
    qaT = (qaT_ref[...].astype(F32) * (HEAD_DIM ** -0.5 * LOG2E)).astype(BF16)
    row = lax.broadcasted_iota(I32, qaT.shape, 0)
    for h in range(H_A):
        qpad_ref[h] = jnp.where((row >= h * HEAD_DIM) & (row < (h + 1) * HEAD_DIM), qaT,
                                jnp.zeros_like(qaT))

    def slab_rows(ref, sl):
        word = ref[sl]
        return jnp.concatenate([lax.shift_right_logical(word, 31 - v) & 1 for v in range(32)], axis=0)

    def p3(j):
        s0 = pl.multiple_of(j * kt, kt)
        slabs = [j * (kt // A_SLAB) + s2 for s2 in range(kt // A_SLAB)]
        gt = jnp.concatenate([slab_rows(above_ref, sl) for sl in slabs], axis=0) != 0
        eq_i = jnp.concatenate([slab_rows(cand_ref, sl) for sl in slabs], axis=0)
        eq = eq_i != 0
        eqf = eq_i.astype(F32)
        pref = jnp.dot(tri_ref[...], eqf.astype(BF16), preferred_element_type=F32)
        seen = carry_ref[...]
        sel = gt | (eq & (pref < need - seen))
        bias = jnp.where(sel, 0.0, NEG_INF).astype(BF16)
        carry_ref[...] = seen + pref[kt - 1:kt, :] + eqf[kt - 1:kt, :]
        k_t = ka_ref[pl.ds(s0, kt), :]
        _flash_tile(k_t, qpad_ref, H_A, lambda i: vaT_ref[i * V_AUG:(i + 1) * V_AUG, pl.ds(s0, kt)],
                    bias, m_ref, acc_ref)

    def p3_only(j, carry):
        p3(j)
        return carry

    @pl.when(b + 1 < nb)
    def _():
        set_qipad(qiTn_ref)
        ntiles_next = ((b + 2) * tq + kt - 1) // kt
        nfull_next = ((b + 1) * tq) // kt
        vis_next = vis_end_of(b + 1)

        def both(j, carry):
            p1(j, wiTn_ref, None)
            p3(j)
            return carry
        lax.fori_loop(0, nfull_next, both, 0)
        lax.fori_loop(nfull_next, ntiles, p3_only, 0)

        def next_masked(j, carry):
            p1(j, wiTn_ref, vis_next)
            return carry
        lax.fori_loop(nfull_next, ntiles_next, next_masked, 0)

    @pl.when(b + 1 == nb)
    def _():
        lax.fori_loop(0, ntiles, p3_only, 0)

    ys = [_flash_result(acc_ref, h) for h in range(H_A)]
    o_ref[...] = jnp.concatenate(ys, axis=0).T.astype(o_ref.dtype)


def _dsa_call(qaT, qiT, wiT, ka, vaT, misc_bf):
    s = ka.shape[0]
    topk = min(TOPK_MAX, s // 4)
    tri = jnp.tril(jnp.ones((A_KT, A_KT), F32), k=-1).astype(BF16)
    vm = pl.BlockSpec(memory_space=pltpu.VMEM)
    assert s % (A_SLAB * A_SWEEP_SLABS) == 0 and s % A_KT == 0, "sequence length not supported"
    nb = s // A_TQ
    nxt = lambda i: (0, jnp.minimum(i + 1, nb - 1))
    return pl.pallas_call(
        functools.partial(_dsa_kernel, topk),
        out_shape=jax.ShapeDtypeStruct((s, W_BRANCH), BF16),
        grid=(nb,),
        in_specs=[pl.BlockSpec((256, A_TQ), lambda i: (0, i)),
                  pl.BlockSpec((LANES, A_TQ), lambda i: (0, i)),
                  pl.BlockSpec((8, A_TQ), lambda i: (0, i)),
                  pl.BlockSpec((LANES, A_TQ), nxt),
                  pl.BlockSpec((8, A_TQ), nxt),
                  vm, vm, vm, vm],
        out_specs=pl.BlockSpec((A_TQ, 256), lambda i: (i, 0)),
        scratch_shapes=[pltpu.VMEM((s // A_SLAB, 32, 8, A_TQ), I32),
                        pltpu.VMEM((s // A_SLAB, 8, A_TQ), I32),
                        pltpu.VMEM((s // A_SLAB, 8, A_TQ), I32),
                        pltpu.VMEM((H_A, 256, A_TQ), BF16),
                        pltpu.VMEM((H_IDX, LANES, A_TQ), BF16),
                        pltpu.VMEM((H_A, V_AUG, A_TQ), F32),
                        pltpu.VMEM((H_A, 1, A_TQ), F32),
                        pltpu.VMEM((1, A_TQ), F32)],
        compiler_params=_cparams(("arbitrary",)),
        name="dsa_mixer",
    )(qaT, qiT, wiT, qiT, wiT, ka, vaT, misc_bf, tri)


def _pair_select(lo, hi):
    lane = lax.broadcasted_iota(I32, lo.shape, 1)
    return jnp.where(lane < HEAD_DIM, lo, hi)


def _pair_head_rms(o, g):
    lane = lax.broadcasted_iota(I32, o.shape, 1)
    low = lane < HEAD_DIM
    sq = o * o
    ms_lo = jnp.sum(jnp.where(low, sq, 0.0), axis=1, keepdims=True) * (1.0 / HEAD_DIM)
    ms_hi = jnp.sum(jnp.where(low, 0.0, sq), axis=1, keepdims=True) * (1.0 / HEAD_DIM)
    ms = jnp.where(low, ms_lo, ms_hi)
    return o * lax.rsqrt(ms + EPS) * g


D_TQ = 256
D_KT = 1024


def _diff_kernel(lam_init, qT_ref, kd_ref, vT_ref, lam_ref, g_ref, o_ref, qpad_ref, acc_ref, m_ref, s_ref):
    tq = D_TQ
    b = pl.program_id(0)
    q_pos = b * tq + lax.broadcasted_iota(I32, (1, tq), 1)
    vis_end = (q_pos // CHUNK + 1) * CHUNK
    m_ref[...] = jnp.full(m_ref.shape, M_INIT, F32)
    acc_ref[...] = jnp.zeros(acc_ref.shape, F32)
    qT = (qT_ref[...].astype(F32) * (DQ_D ** -0.5 * LOG2E)).astype(BF16)
    row = lax.broadcasted_iota(I32, qT.shape, 0)
    for i in range(2 * H_D):
        qpad_ref[i] = jnp.where((row >= i * DQ_D) & (row < (i + 1) * DQ_D), qT, jnp.zeros_like(qT))

    def tile(s0, kt, masked, s_ref=None, s0_next=None):
        k_t = kd_ref[pl.ds(s0, kt), :]
        k_next = None if s0_next is None else kd_ref[pl.ds(s0_next, kt), :]
        bias = None
        if masked:
            vis = (s0 + lax.broadcasted_iota(I32, (kt, 1), 0)) < vis_end
            bias = jnp.where(vis, 0.0, NEG_INF).astype(BF16)
        _flash_tile(k_t, qpad_ref, 2 * H_D, lambda i: vT_ref[(i // 2) * V_AUG:(i // 2 + 1) * V_AUG, pl.ds(s0, kt)],
                    bias, m_ref, acc_ref, s_ref, k_next)

    n_big = (b * tq) // D_KT

    @pl.when(n_big > 0)
    def _():
        _qk_prologue(kd_ref[pl.ds(0, D_KT), :], qpad_ref, 2 * H_D, s_ref)

    def full_tile(j, carry):
        j_next = jnp.minimum(j + 1, n_big - 1)
        tile(pl.multiple_of(j * D_KT, D_KT), D_KT, False, s_ref, pl.multiple_of(j_next * D_KT, D_KT))
        return carry
    lax.fori_loop(0, n_big, full_tile, 0)

    def small_tile(j, carry):
        tile(pl.multiple_of(j * tq, tq), tq, False)
        return carry
    lax.fori_loop(n_big * (D_KT // tq), b, small_tile, 0)
    tile(pl.multiple_of(b * tq, tq), tq, True)

    lp = lam_ref[...]
    lam = (jnp.exp(jnp.sum(lp[0:1] * lp[1:2], axis=1, keepdims=True))
           - jnp.exp(jnp.sum(lp[2:3] * lp[3:4], axis=1, keepdims=True)) + lam_init)
    ys = []
    for h in range(H_D):
        o = _flash_result(acc_ref, 2 * h) - lam * _flash_result(acc_ref, 2 * h + 1)
        ms = jnp.mean(o * o, axis=0, keepdims=True)
        ys.append(o * lax.rsqrt(ms + EPS) * g_ref[h * HEAD_DIM:(h + 1) * HEAD_DIM, :] * (1.0 - lam_init))
    o_ref[...] = jnp.concatenate(ys, axis=0).T.astype(o_ref.dtype)


def _diff_call(qdT, kd, vdT, lam_p, dnorm_g, lam_init):
    s = kd.shape[0]
    return pl.pallas_call(
        functools.partial(_diff_kernel, lam_init),
        out_shape=jax.ShapeDtypeStruct((s, W_BRANCH), BF16),
        grid=(s // D_TQ,),
        in_specs=[pl.BlockSpec((256, D_TQ), lambda i: (0, i)),
                  pl.BlockSpec(memory_space=pltpu.VMEM),
                  pl.BlockSpec(memory_space=pltpu.VMEM),
                  _whole((4, DQ_D)),
                  _whole((256, 1))],
        out_specs=pl.BlockSpec((D_TQ, 256), lambda i: (i, 0)),
        scratch_shapes=[pltpu.VMEM((2 * H_D, 256, D_TQ), BF16),
                        pltpu.VMEM((2 * H_D, V_AUG, D_TQ), F32),
                        pltpu.VMEM((2 * H_D, 1, D_TQ), F32),
                        pltpu.VMEM((QK_AHEAD, D_KT, D_TQ), BF16)],
        compiler_params=_cparams(("parallel",)),
        name="diff_attention",
    )(qdT, kd, vdT, lam_p, dnorm_g.reshape(256, 1))


C_TQ = 128
C_PAD = N_PREV_CHUNKS * CHUNK
C_WIN = C_PAD + C_TQ
C_LINE = C_TQ + C_WIN


def _band_bias_line(rel_bias):
    dist_desc = jnp.arange(C_TQ - 1 + C_PAD, C_PAD - C_WIN, -1)
    line = rel_bias[:, jnp.clip(dist_desc, -(CHUNK - 1), MAX_REL_PAST) + (CHUNK - 1)].astype(F32)
    return jnp.pad(line, ((0, 0), (0, C_LINE - line.shape[1])))


def _band_table_init(line_ref, bias_ref):
    tq = C_TQ

    @pl.when(pl.program_id(0) == 0)
    def _():
        i = lax.broadcasted_iota(I32, (tq, C_WIN), 0)
        w = lax.broadcasted_iota(I32, (tq, C_WIN), 1)
        in_band = (w // CHUNK >= i // CHUNK) & (w // CHUNK <= i // CHUNK + N_PREV_CHUNKS)
        for h in range(H_C):
            x = jnp.broadcast_to(line_ref[h:h + 1, :], (tq, C_LINE))
            y = pltpu.roll(x, 1, 1, stride=1, stride_axis=0)
            bias_ref[h] = jnp.where(in_band, y[:, C_LINE - C_WIN:], NEG_INF)


def _band_body(qc_ref, kcT_ref, vc_ref, o_ref, bias_ref):
    tq = C_TQ
    b = pl.program_id(0)
    w0 = pl.multiple_of(b * tq, tq)
    q_all = qc_ref[...] * 0.125
    key_abs = b * tq - C_PAD + lax.broadcasted_iota(I32, (1, C_WIN), 1)
    ok = key_abs >= 0
    logits = [jnp.dot(q_all[:, h * HEAD_DIM:(h + 1) * HEAD_DIM],
                      kcT_ref[h * HEAD_DIM:(h + 1) * HEAD_DIM, pl.ds(w0, C_WIN)],
                      preferred_element_type=F32) for h in range(H_C)]
    heads = []
    for h in range(H_C):
        s = jnp.where(ok, logits[h] + bias_ref[h], NEG_INF)
        m = jnp.max(s, axis=1, keepdims=True)
        p = jnp.exp(s - m)
        l = jnp.sum(p, axis=1, keepdims=True)
        pv = jnp.dot(p.astype(BF16), vc_ref[pl.ds(w0, C_WIN), (h // 2) * LANES:(h // 2 + 1) * LANES],
                     preferred_element_type=F32)
        heads.append(pv / l)
    for pr in range(H_C // 2):
        o_ref[:, pr * LANES:(pr + 1) * LANES] = _pair_select(heads[2 * pr], heads[2 * pr + 1]).astype(o_ref.dtype)


def _band_kernel(qc_ref, kcT_ref, vc_ref, line_ref, o_ref, bias_ref):
    _band_table_init(line_ref, bias_ref)
    _band_body(qc_ref, kcT_ref, vc_ref, o_ref, bias_ref)


def _band_call(qc, kcT_pad, vc_pad, bias_line):
    s = qc.shape[0]
    return pl.pallas_call(
        _band_kernel,
        out_shape=jax.ShapeDtypeStruct((s, W_BRANCH), BF16),
        grid=(s // C_TQ,),
        in_specs=[pl.BlockSpec((C_TQ, 256), lambda i: (i, 0)),
                  pl.BlockSpec(memory_space=pltpu.VMEM),
                  pl.BlockSpec(memory_space=pltpu.VMEM),
                  _whole((H_C, C_LINE))],
        out_specs=pl.BlockSpec((C_TQ, 256), lambda i: (i, 0)),
        scratch_shapes=[pltpu.VMEM((H_C, C_TQ, C_WIN), F32)],
        compiler_params=_cparams(("arbitrary",)),
        name="band_attention",
    )(qc, kcT_pad, vc_pad, bias_line)


B_NCH = 2
B_L = B_NCH * CHUNK
HIGHEST = lax.Precision.HIGHEST


def _mlstm_state_init(tail_ref, ct_ref, n_ref, m_ref):
    @pl.when(pl.program_id(0) == 0)
    def _():
        tail_ref[...] = jnp.zeros(tail_ref.shape, F32)
        ct_ref[...] = jnp.zeros(ct_ref.shape, F32)
        n_ref[...] = jnp.zeros(n_ref.shape, F32)
        m_ref[...] = jnp.zeros(m_ref.shape, F32)


def _mlstm_kernel(qk_ref, vb_ref, ob_ref, misc_ref, cw_ref, cb_ref, gb_ref, ng_ref, o_ref,
                  tail_ref, ct_ref, n_ref, m_ref):
    _mlstm_state_init(tail_ref, ct_ref, n_ref, m_ref)
    _mlstm_body(qk_ref, vb_ref, ob_ref, misc_ref, cw_ref, cb_ref, gb_ref, ng_ref, o_ref,
                tail_ref, ct_ref, n_ref, m_ref)


def _mlstm_body(qk_ref, vb_ref, ob_ref, misc_ref, cw_ref, cb_ref, gb_ref, ng_ref, o_ref,
                tail_ref, ct_ref, n_ref, m_ref):
    L = CHUNK
    T = B_L
    hd = HEAD_DIM

    x = qk_ref[...]
    xx = jnp.concatenate([tail_ref[...], x], axis=0)
    y = jnp.broadcast_to(cb_ref[...], x.shape)
    for j in range(CONV_K):
        y = y + cw_ref[j:j + 1, :] * xx[8 - (CONV_K - 1) + j:8 - (CONV_K - 1) + j + T, :]
    tail_ref[...] = x[T - 8:T, :]
    qk = y * jax.nn.sigmoid(y)
    q_all = qk[:, :W_BRANCH]
    k_all = qk[:, W_BRANCH:] * 0.125
    v_all = vb_ref[...]
    o_gate = jax.nn.sigmoid(ob_ref[...])

    gts = misc_ref[...] + gb_ref[...]
    lf = jnp.minimum(gts, 0.0) - jnp.log1p(jnp.exp(-jnp.abs(gts)))
    r_t = lax.broadcasted_iota(I32, (T, T), 0)
    c_t = lax.broadcasted_iota(I32, (T, T), 1)
    ltri = jnp.where((c_t <= r_t) & (c_t // L == r_t // L), 1.0, 0.0)
    cum = jnp.dot(ltri, lf, precision=HIGHEST, preferred_element_type=F32)
    lane = lax.broadcasted_iota(I32, (T, LANES), 1)
    mixed = jnp.where(lane < MISC_FB, gts, cum)
    sel_r = lax.broadcasted_iota(I32, (8, LANES), 0)
    sel_c = lax.broadcasted_iota(I32, (8, LANES), 1)
    sel = jnp.where(sel_c == sel_r + MISC_IB, 1.0, 0.0)
    rows = lax.dot_general(sel, mixed, (((1,), (1,)), ((), ())), precision=HIGHEST,
                           preferred_element_type=F32)
    causal = lax.broadcasted_iota(I32, (L, L), 1) <= lax.broadcasted_iota(I32, (L, L), 0)

    heads = range(H_B)
    sl = lambda a, t0, h: a[t0:t0 + L, h * hd:(h + 1) * hd]
    pre = []
    for ci in range(B_NCH):
        t0 = ci * L
        per_head = []
        for h in heads:
            q, k, v = sl(q_all, t0, h), sl(k_all, t0, h), sl(v_all, t0, h)
            qb, kb, vbf = q.astype(BF16), k.astype(BF16), v.astype(BF16)
            qkt = lax.dot_general(qb, kb, (((1,), (1,)), ((), ())), preferred_element_type=F32)
            it_r = rows[h:h + 1, t0:t0 + L]
            cum_r = rows[H_B + h:H_B + h + 1, t0:t0 + L]
            it_c = gts[t0:t0 + L, MISC_IB + h:MISC_IB + h + 1]
            cum_c = cum[t0:t0 + L, MISC_FB + h:MISC_FB + h + 1]
            dmat = jnp.where(causal, cum_c - cum_r + it_r, NEG_INF)
            dmax = jnp.max(dmat, axis=1, keepdims=True)
            cum_end = cum_c[L - 1:L, :]
            g = cum_end - cum_c + it_c
            gmax = jnp.max(g, axis=0, keepdims=True)
            per_head.append((q, k, qb, vbf, qkt, dmat, dmax, cum_c, cum_end, g, gmax))
        pre.append(per_head)

    state = [(ct_ref[h], n_ref[h], m_ref[h]) for h in heads]
    out_chunks = []
    for ci in range(B_NCH):
        t0 = ci * L
        qc = [jnp.dot(pre[ci][h][2], state[h][0].astype(BF16), preferred_element_type=F32) for h in heads]
        mid = []
        for h in heads:
            q, k, qb, vbf, qkt, dmat, dmax, cum_c, cum_end, g, gmax = pre[ci][h]
            ct, n_row, m_prev = state[h]
            m_inter = cum_c + m_prev
            m_t = jnp.maximum(m_inter, dmax)
            w = jnp.exp(dmat - m_t) * qkt
            inter = jnp.exp(m_inter - m_t)
            m_new = jnp.maximum(cum_end + m_prev, gmax)
            carry_scale = jnp.exp(cum_end + m_prev - m_new)
            src_k = jnp.exp(g - m_new) * k
            mid.append((w, inter, m_t, m_new, carry_scale, src_k))
        wv = [jnp.dot(mid[h][0].astype(BF16), pre[ci][h][3], preferred_element_type=F32) for h in heads]
        upd = [lax.dot_general(mid[h][5].astype(BF16), pre[ci][h][3], (((0,), (0,)), ((), ())),
                               preferred_element_type=F32) for h in heads]
        outs = []
        for h in heads:
            q = pre[ci][h][0]
            w, inter, m_t, m_new, carry_scale, src_k = mid[h]
            ct, n_row, _ = state[h]
            num = inter * qc[h] + wv[h]
            den = inter * jnp.sum(q * n_row, axis=1, keepdims=True) + jnp.sum(w, axis=1, keepdims=True)
            h_t = num / jnp.maximum(jnp.abs(den), jnp.exp(-m_t))
            state[h] = (carry_scale * ct + upd[h],
                        carry_scale * n_row + jnp.sum(src_k, axis=0, keepdims=True), m_new)
            ms = jnp.mean(h_t * h_t, axis=1, keepdims=True)
            hn = h_t * lax.rsqrt(ms + EPS) * ng_ref[:, h * hd:(h + 1) * hd]
            outs.append(sl(o_gate, t0, h) * hn)
        out_chunks.append(jnp.concatenate(outs, axis=1))
    for h in heads:
        ct_ref[h], n_ref[h], m_ref[h] = state[h]
    o_ref[...] = jnp.concatenate(out_chunks, axis=0).astype(o_ref.dtype)


def _mlstm_call(qkb, vb, ob, misc, conv_w, conv_b, i_bias, f_bias, norm_g):
    s = qkb.shape[0]
    gbias = jnp.zeros((1, LANES), F32)
    gbias = gbias.at[0, MISC_IB:MISC_IB + H_B].set(i_bias).at[0, MISC_FB:MISC_FB + H_B].set(f_bias)
    row = lambda w: pl.BlockSpec((B_L, w), lambda i: (i, 0))
    return pl.pallas_call(
        _mlstm_kernel,
        out_shape=jax.ShapeDtypeStruct((s, W_BRANCH), BF16),
        grid=(s // B_L,),
        in_specs=[row(512), row(256), row(256), row(128),
                  _whole((CONV_K, 512)), _whole((1, 512)), _whole((1, LANES)), _whole((1, 256))],
        out_specs=row(256),
        scratch_shapes=[pltpu.VMEM((8, 512), F32),
                        pltpu.VMEM((H_B, HEAD_DIM, HEAD_DIM), F32),
                        pltpu.VMEM((H_B, 1, HEAD_DIM), F32),
                        pltpu.VMEM((H_B, 1, 1), F32)],
        compiler_params=_cparams(("arbitrary",)),
        name="mlstm_mixer",
    )(qkb, vb, ob, misc, conv_w, conv_b.reshape(1, 512), gbias, norm_g.reshape(1, 256))


N_B_IN, N_C_IN = 8, 4


def _mlstm_band_kernel(*refs):
    b_in = refs[:N_B_IN]
    c_in = refs[N_B_IN:N_B_IN + N_C_IN]
    yb_ref, yc_ref = refs[N_B_IN + N_C_IN:N_B_IN + N_C_IN + 2]
    b_scr = refs[N_B_IN + N_C_IN + 2:N_B_IN + N_C_IN + 6]
    (bias_ref,) = refs[N_B_IN + N_C_IN + 6:]
    qc_ref, kcT_ref, vc_ref, line_ref = c_in
    _mlstm_state_init(*b_scr)
    _band_table_init(line_ref, bias_ref)
    _mlstm_body(*b_in, yb_ref, *b_scr)
    _band_body(qc_ref, kcT_ref, vc_ref, yc_ref, bias_ref)


def _mlstm_band_call(qkb, vb, ob, misc, conv_w, conv_b, i_bias, f_bias, norm_g,
                     qc, kcT_pad, vc_pad, bias_line):
    assert B_L == C_TQ
    s = qkb.shape[0]
    gbias = jnp.zeros((1, LANES), F32)
    gbias = gbias.at[0, MISC_IB:MISC_IB + H_B].set(i_bias).at[0, MISC_FB:MISC_FB + H_B].set(f_bias)
    row = lambda w: pl.BlockSpec((B_L, w), lambda i: (i, 0))
    vm = pl.BlockSpec(memory_space=pltpu.VMEM)
    return pl.pallas_call(
        _mlstm_band_kernel,
        out_shape=(jax.ShapeDtypeStruct((s, W_BRANCH), BF16), jax.ShapeDtypeStruct((s, W_BRANCH), BF16)),
        grid=(s // B_L,),
        in_specs=[row(512), row(256), row(256), row(128),
                  _whole((CONV_K, 512)), _whole((1, 512)), _whole((1, LANES)), _whole((1, 256)),
                  row(256), vm, vm, _whole((H_C, C_LINE))],
        out_specs=(row(256), row(256)),
        scratch_shapes=[pltpu.VMEM((8, 512), F32),
                        pltpu.VMEM((H_B, HEAD_DIM, HEAD_DIM), F32),
                        pltpu.VMEM((H_B, 1, HEAD_DIM), F32),
                        pltpu.VMEM((H_B, 1, 1), F32),
                        pltpu.VMEM((H_C, C_TQ, C_WIN), F32)],
        compiler_params=_cparams(("arbitrary",)),
        name="mlstm_band",
    )(qkb, vb, ob, misc, conv_w, conv_b.reshape(1, 512), gbias, norm_g.reshape(1, 256),
      qc, kcT_pad, vc_pad, bias_line)


M_TM = 256
ROUTE_LOGIT0 = N_GROUPS
RT_E1, RT_E2, RT_G1, RT_G2 = 0, 1, 2, 3


def _lane_argmax(vals, lane):
    v = jnp.max(vals, axis=1, keepdims=True)
    idx = jnp.min(jnp.where(vals == v, lane, float(LANES)), axis=1, keepdims=True)
    return v, idx


def _merge_kernel(x_ref, ya_ref, yb_ref, yc_ref, yd_ref, gm_ref, wg_ref, wb_ref, wo_ref, gf_ref,
                  wr_ref, rb_ref, xo_ref, h2_ref, rt_ref):
    x = x_ref[...]
    h = _rms(x, gm_ref[...]).astype(BF16)
    mixed = jnp.zeros(x.shape, F32)
    for n, y_ref in enumerate((ya_ref, yb_ref, yc_ref, yd_ref)):
        gate = jax.nn.sigmoid(jnp.dot(h, wg_ref[:, n * D_MODEL:(n + 1) * D_MODEL],
                                      preferred_element_type=F32))
        up = jnp.dot(y_ref[...], wb_ref[n], preferred_element_type=F32)
        mixed = mixed + gate * up
    xn = x + jnp.dot(mixed.astype(BF16), wo_ref[...], preferred_element_type=F32)
    xo_ref[...] = xn
    h2 = _rms(xn, gf_ref[...])
    h2_ref[...] = h2
    logits = jnp.dot(h2.astype(BF16), wr_ref[...], preferred_element_type=F32) + rb_ref[...]

    lane = lax.broadcasted_iota(I32, logits.shape, 1).astype(F32)
    neg = -jnp.inf
    gmask = lane < N_GROUPS
    gmax, g_sel = _lane_argmax(jnp.where(gmask, logits, neg), lane)
    g_gate = 1.0 / jnp.sum(jnp.where(gmask, jnp.exp(logits - gmax), 0.0), axis=1, keepdims=True)
    e_lo = ROUTE_LOGIT0 + EXPERTS_PER_GROUP * g_sel
    el = jnp.where((lane >= e_lo) & (lane < e_lo + EXPERTS_PER_GROUP), logits, neg)
    v1, i1 = _lane_argmax(el, lane)
    v2, i2 = _lane_argmax(jnp.where(lane == i1, neg, el), lane)
    e = jnp.exp(v2 - v1)
    p1 = 1.0 / (1.0 + e)
    p2 = e / (1.0 + e)
    rt = jnp.where(lane == RT_E1, i1 - ROUTE_LOGIT0,
                   jnp.where(lane == RT_E2, i2 - ROUTE_LOGIT0,
                             jnp.where(lane == RT_G1, p1 * g_gate,
                                       jnp.where(lane == RT_G2, p2 * g_gate, 0.0))))
    rt_ref[...] = rt


def _merge_call(x2, ya, yb, yc, yd, g_mix, w_gate, w_branch, w_out, g_ffn, w_route, b_route):
    s = x2.shape[0]
    row = lambda w: pl.BlockSpec((M_TM, w), lambda i: (i, 0))
    vm = pl.BlockSpec(memory_space=pltpu.VMEM)
    return pl.pallas_call(
        _merge_kernel,
        out_shape=(jax.ShapeDtypeStruct((s, D_MODEL), F32), jax.ShapeDtypeStruct((s, D_MODEL), F32),
                   jax.ShapeDtypeStruct((s, LANES), F32)),
        grid=(s // M_TM,),
        in_specs=[row(D_MODEL), row(256), row(256), row(256), row(256),
                  vm, vm, vm, vm, vm, vm, vm],
        out_specs=(row(D_MODEL), row(D_MODEL), row(LANES)),
        compiler_params=_cparams(("parallel",)),
        name="merge_route",
    )(x2, ya, yb, yc, yd, g_mix.reshape(1, D_MODEL), w_gate, w_branch, w_out,
      g_ffn.reshape(1, D_MODEL), w_route, b_route)


E_BM = 256
GATHER_UNROLL = 8


def _gather_rows(src_hbm, idx_ref, dst, sem, n, wait, inline=False):
    def one(r, parity):
        cp = pltpu.make_async_copy(src_hbm.at[pl.ds(idx_ref[0, 0, r], 1)], dst.at[pl.ds(r, 1)], sem)
        if wait:
            cp.wait()
        else:
            cp.start(priority=parity)

    if inline:
        for r in range(n):
            one(r, r % 2)
    else:
        def group(g, c):
            for u in range(GATHER_UNROLL):
                one(g * GATHER_UNROLL + u, u % 2)
            return c
        lax.fori_loop(0, n // GATHER_UNROLL, group, 0)


def _moe_kernel(be_ref, nused_ref, tok_ref, tok_next_ref, h2_hbm, w1_ref, w3_ref, w2_ref, o_ref,
                xbuf, w1b, w3b, w2b, sem):
    i = pl.program_id(0)
    nused = nused_ref[0]
    used = i < nused
    slot = i % 2

    @pl.when((i == 0) & used)
    def _():
        _gather_rows(h2_hbm, tok_ref, xbuf.at[0], sem.at[0], E_BM, wait=False)

    @pl.when(used)
    def _():
        prev = be_ref[jnp.maximum(i - 1, 0)]

        @pl.when((i == 0) | (be_ref[i] != prev))
        def _():
            w1b[...] = w1_ref[0, 0].astype(BF16)
            w3b[...] = w3_ref[0, 0].astype(BF16)
            w2b[...] = w2_ref[0, 0].astype(BF16)

        _gather_rows(h2_hbm, tok_ref, xbuf.at[slot], sem.at[slot], E_BM, wait=True)
        _gather_rows(h2_hbm, tok_next_ref, xbuf.at[1 - slot], sem.at[1 - slot], E_BM, wait=False, inline=True)
        xb = xbuf[slot].astype(BF16)
        a = jnp.dot(xb, w1b[...], preferred_element_type=F32)
        g = jnp.dot(xb, w3b[...], preferred_element_type=F32)
        hid = (a * jax.nn.sigmoid(a) * g).astype(BF16)
        o_ref[...] = jnp.dot(hid, w2b[...], preferred_element_type=F32)

    @pl.when(jnp.logical_not(used))
    def _():
        @pl.when(i == nused)
        def _():
            _gather_rows(h2_hbm, tok_ref, xbuf.at[slot], sem.at[slot], E_BM, wait=True)
        o_ref[...] = jnp.zeros(o_ref.shape, F32)


def _moe_call(h2, row_tok, block_e, nused, w1, w3, w2, layer):
    n_rows = row_tok.shape[0]
    n_blocks = n_rows // E_BM
    wspec = lambda shp: pl.BlockSpec((1, 1) + shp, lambda i, be, nu: (layer, be[i], 0, 0))
    grid_spec = pltpu.PrefetchScalarGridSpec(
        num_scalar_prefetch=2,
        grid=(n_blocks,),
        in_specs=[pl.BlockSpec((1, 1, E_BM), lambda i, be, nu: (i, 0, 0), memory_space=pltpu.SMEM),
                  pl.BlockSpec((1, 1, E_BM), lambda i, be, nu: (jnp.minimum(i + 1, n_blocks - 1), 0, 0),
                               memory_space=pltpu.SMEM),
                  pl.BlockSpec(memory_space=pl.ANY),
                  wspec((D_MODEL, F_EXPERT)), wspec((D_MODEL, F_EXPERT)), wspec((F_EXPERT, D_MODEL))],
        out_specs=pl.BlockSpec((E_BM, D_MODEL), lambda i, be, nu: (i, 0)),
        scratch_shapes=[pltpu.VMEM((2, E_BM, D_MODEL), F32),
                        pltpu.VMEM((D_MODEL, F_EXPERT), BF16),
                        pltpu.VMEM((D_MODEL, F_EXPERT), BF16),
                        pltpu.VMEM((F_EXPERT, D_MODEL), BF16),
                        pltpu.SemaphoreType.DMA((2,))],
    )
    tok3 = row_tok.reshape(n_blocks, 1, E_BM)
    return pl.pallas_call(
        _moe_kernel,
        out_shape=jax.ShapeDtypeStruct((n_rows, D_MODEL), F32),
        grid_spec=grid_spec,
        compiler_params=_cparams(("arbitrary",)),
        name="moe_experts",
    )(block_e, nused, tok3, tok3, h2, w1, w3, w2)


CB_TM = 256


def _combine_kernel(final, pos_ref, pos_next_ref, x_ref, rt_ref, ys_hbm, gfin_ref, o_ref, buf, sem):
    i = pl.program_id(0)
    slot = i % 2
    n = 2 * CB_TM

    @pl.when(i == 0)
    def _():
        _gather_rows(ys_hbm, pos_ref, buf.at[0], sem.at[0], n, wait=False)

    @pl.when(i + 1 < pl.num_programs(0))
    def _():
        _gather_rows(ys_hbm, pos_next_ref, buf.at[1 - slot], sem.at[1 - slot], n, wait=False)
    _gather_rows(ys_hbm, pos_ref, buf.at[slot], sem.at[slot], n, wait=True)

    rt = rt_ref[...]
    y = (buf[slot, 0:CB_TM, :] * rt[:, RT_G1:RT_G1 + 1]
         + buf[slot, CB_TM:2 * CB_TM, :] * rt[:, RT_G2:RT_G2 + 1])
    xn = x_ref[...] + y
    if final:
        xn = _rms(xn, gfin_ref[...])
    o_ref[...] = xn


def _combine_call(x2, rt, ys, pos, g_final, final):
    s = x2.shape[0]
    nb = s // CB_TM
    pos_b = pos.reshape(nb, CB_TM, 2).transpose(0, 2, 1).reshape(nb, 1, 2 * CB_TM)
    return pl.pallas_call(
        functools.partial(_combine_kernel, final),
        out_shape=jax.ShapeDtypeStruct((s, D_MODEL), F32),
        grid=(nb,),
        in_specs=[pl.BlockSpec((1, 1, 2 * CB_TM), lambda i: (i, 0, 0), memory_space=pltpu.SMEM),
                  pl.BlockSpec((1, 1, 2 * CB_TM), lambda i: (jnp.minimum(i + 1, nb - 1), 0, 0),
                               memory_space=pltpu.SMEM),
                  pl.BlockSpec((CB_TM, D_MODEL), lambda i: (i, 0)),
                  pl.BlockSpec((CB_TM, LANES), lambda i: (i, 0)),
                  pl.BlockSpec(memory_space=pl.ANY),
                  _whole((1, D_MODEL))],
        out_specs=pl.BlockSpec((CB_TM, D_MODEL), lambda i: (i, 0)),
        scratch_shapes=[pltpu.VMEM((2, 2 * CB_TM, D_MODEL), F32), pltpu.SemaphoreType.DMA((2,))],
        compiler_params=_cparams(("arbitrary",)),
        name="moe_combine",
    )(pos_b, pos_b, x2, rt, ys, g_final.reshape(1, D_MODEL))


PLAN_BLOCK = 256


def _dispatch_plan(rt):
    t = rt.shape[0]
    n_assign = t * TOP_K_INNER
    eid = rt[:, RT_E1:RT_E2 + 1].astype(I32).reshape(n_assign)
    onehot = (eid[:, None] == jnp.arange(N_EXPERTS, dtype=I32)[None, :]).astype(F32)
    pb = PLAN_BLOCK
    oh3 = onehot.reshape(n_assign // pb, pb, N_EXPERTS)
    tri = jnp.tril(jnp.ones((pb, pb), F32), k=-1)
    within = jnp.einsum('ij,bjk->bik', tri, oh3)
    block_tot = jnp.sum(oh3, axis=1)
    block_off = jnp.cumsum(block_tot, axis=0) - block_tot
    counts = (block_off[-1] + block_tot[-1]).astype(I32)
    rank = jnp.sum(oh3 * (within + block_off[:, None, :]), axis=-1).reshape(n_assign)
    padded = (counts + E_BM - 1) // E_BM * E_BM
    ends_pad = jnp.cumsum(padded)
    starts_pad = ends_pad - padded
    dest = (jnp.sum(onehot * starts_pad.astype(F32)[None, :], axis=-1) + rank).astype(I32)
    n_rows = (-(-(n_assign + N_EXPERTS * (E_BM - 1)) // E_BM) + 1) * E_BM
    n_blocks = n_rows // E_BM
    tok = jnp.repeat(jnp.arange(t, dtype=I32), TOP_K_INNER)
    row_tok = jnp.zeros((n_rows,), I32).at[dest].set(tok)
    block_start = jnp.arange(n_blocks, dtype=I32) * E_BM
    block_e = jnp.minimum(jnp.sum((ends_pad[None, :] <= block_start[:, None]).astype(I32), axis=1),
                          N_EXPERTS - 1)
    nused = (ends_pad[-1] // E_BM).astype(I32).reshape(1)
    return row_tok, block_e, nused, dest.reshape(t, TOP_K_INNER)


def _layer(x2, layer_idx, p, experts, final_g, final):
    (norm_mix_g, w_in, conv_w, conv_b, i_bias, f_bias, mnorm_g, rel_bias, lam_p, dnorm_g,
     w_branch, w_gate, w_out, norm_ffn_g, rgw, rgb, rew, reb) = p
    o = _k1_call(x2, norm_mix_g, *_rearrange_w_in(w_in))

    ya = _dsa_call(o["qaT"], o["qiT"], o["wiT"], o["ka"], o["vaT"], o["misc_bf"])
    kcT_pad = jnp.pad(o["kcT"], ((0, 0), (C_PAD, 0)))
    vc_pad = jnp.pad(o["vc"], ((C_PAD, 0), (0, 0)))
    yb, yc = _mlstm_band_call(o["qkb"], o["vb"], o["ob"], o["misc"], conv_w, conv_b, i_bias, f_bias, mnorm_g,
                              o["qc"], kcT_pad, vc_pad, _band_bias_line(rel_bias))
    lam_init = 0.8 - 0.6 * math.exp(-0.3 * layer_idx)
    yd = _diff_call(o["qdT"], o["kd"], o["vdT"], lam_p, dnorm_g, lam_init)

    w_route = jnp.concatenate([rgw, rew, jnp.zeros((D_MODEL, LANES - N_GROUPS - N_EXPERTS), F32)],
                              axis=1).astype(BF16)
    b_route = jnp.concatenate([rgb, reb, jnp.zeros((LANES - N_GROUPS - N_EXPERTS,), F32)]).reshape(1, LANES)
    xn, h2, rt = _merge_call(x2, ya, yb, yc, yd, norm_mix_g, w_gate.astype(BF16), w_branch.astype(BF16),
                             w_out.astype(BF16), norm_ffn_g, w_route, b_route)
    row_tok, block_e, nused, pos = _dispatch_plan(rt)
    ys = _moe_call(h2, row_tok, block_e, nused, *experts, layer_idx)
    return _combine_call(xn, rt, ys, pos, final_g, final)


def kernel(x, norm_mix_g, w_in, conv_w, conv_b, mlstm_i_bias, mlstm_f_bias, mlstm_norm_g, relpos_bias, diff_lambda, diff_norm_g, w_branch, w_gate, w_out, norm_ffn_g, router_group_w, router_group_b, router_expert_w, router_expert_b, expert_w1, expert_w3, expert_w2, final_norm_g):
    assert x.shape[0] == 1 and x.shape[2] == D_MODEL
    params = (norm_mix_g, w_in, conv_w, conv_b, mlstm_i_bias, mlstm_f_bias, mlstm_norm_g, relpos_bias,
              diff_lambda, diff_norm_g, w_branch, w_gate, w_out, norm_ffn_g, router_group_w,
              router_group_b, router_expert_w, router_expert_b)
    experts = (expert_w1, expert_w3, expert_w2)
    depth = norm_mix_g.shape[0]
    x2 = x[0]
    for l in range(depth):
        x2 = _layer(x2, l, tuple(a[l] for a in params), experts, final_norm_g, l == depth - 1)
    return x2[None]
```

```python
import functools
import math

import jax
import jax.numpy as jnp
from jax import lax
from jax.experimental import pallas as pl
from jax.experimental.pallas import tpu as pltpu

F32 = jnp.float32
BF16 = jnp.bfloat16
I32 = jnp.int32

D_MODEL = 1024
CHUNK = 64
HEAD_DIM = 64
NEG_INF = -1e30
H_A = 4
H_IDX = 4
D_IDX = 32
TOPK_MAX = 256
H_B = 4
CONV_K = 4
H_C = 4
N_PREV_CHUNKS = 8
MAX_REL_PAST = 128
H_D = 4
DQ_D = 32
W_BRANCH = 256
N_BRANCH = 4
N_GROUPS = 4
EXPERTS_PER_GROUP = 8
N_EXPERTS = 32
TOP_K_INNER = 2
F_EXPERT = 512
EPS = 1e-6

VMEM_LIMIT_BYTES = 52 * 1024 * 1024
LANES = 128

INT_MIN = -(2 ** 31)
I16_MIN = -(2 ** 15)
I16 = jnp.int16
M_INIT = -5e29

_COL_SIZES = (256, 256, 256, 128, 32, 4, 256, 256, 256, 4, 4, 256, 256, 256, 256, 256, 256, 256)
_COL_NAMES = ("qa", "ka", "va", "qi", "ki", "wi", "qb", "kb", "vb", "ib", "fb", "ob",
              "qc", "kc", "vc", "qd", "kd", "vd")
_COL_OFF = {}
_o = 0
for _n, _s in zip(_COL_NAMES, _COL_SIZES):
    _COL_OFF[_n] = (_o, _s)
    _o += _s
C_IN = _o
MISC_KI = 0
MISC_WI = 32
MISC_IB = 36
MISC_FB = 40


def _cparams(sem):
    return pltpu.CompilerParams(dimension_semantics=sem, vmem_limit_bytes=VMEM_LIMIT_BYTES)


def _whole(shape):
    nd = len(shape)
    return pl.BlockSpec(shape, lambda *_: (0,) * nd)


_K1_NAT = (("ka", 256, BF16), ("misc", 128, F32), ("qkb", 512, F32), ("vb", 256, F32), ("ob", 256, F32),
           ("qc", 256, BF16), ("vc", 256, BF16), ("kd", 256, BF16))
_K1_TR = (("qaT", 256, BF16), ("qiT", 128, BF16), ("vaT", 256, BF16), ("kcT", 256, BF16),
          ("qdT", 256, BF16), ("vdT", 256, BF16), ("wiT", 8, F32))
K1_NAT_WIDTH = sum(w for _, w, _ in _K1_NAT)
K1_TR_WIDTH = sum(w for _, w, _ in _K1_TR)
K1_TM = 512
V_AUG = 80
_K1_VAUG = ("vaT", "vdT")


def _k1_tr_rows(name, width):
    return (width // HEAD_DIM) * V_AUG if name in _K1_VAUG else width


def _rearrange_w_in(w_in):
    def cols(name):
        o, s = _COL_OFF[name]
        return w_in[:, o:o + s]
    d = w_in.shape[0]
    misc = jnp.concatenate([cols("ki"), cols("wi"), cols("ib"), cols("fb"),
                            jnp.zeros((d, LANES - 44), w_in.dtype)], axis=1)
    nat = [cols("ka"), misc, cols("qb"), cols("kb"), cols("vb"), cols("ob"), cols("qc"), cols("vc"),
           cols("kd")]
    tr = [cols("qa"), cols("qi"), cols("va"), cols("kc"), cols("qd"), cols("vd"), cols("wi"),
          jnp.zeros((d, 8 - H_IDX), w_in.dtype)]
    return (jnp.concatenate(nat, axis=1).astype(BF16), jnp.concatenate(tr, axis=1).T.astype(BF16))


def _rms(x, g):
    ms = jnp.mean(x * x, axis=-1, keepdims=True)
    return x * lax.rsqrt(ms + EPS) * g


def _k1_kernel(x_ref, g_ref, wn_ref, wt_ref, *out_refs):
    h = _rms(x_ref[...], g_ref[...]).astype(BF16)
    refs = dict(zip([n for n, _, _ in _K1_NAT] + ["misc_bf"] + [n for n, _, _ in _K1_TR], out_refs))
    off = 0
    for name, width, _ in _K1_NAT:
        r = jnp.dot(h, wn_ref[:, off:off + width], preferred_element_type=F32)
        refs[name][...] = r.astype(refs[name].dtype)
        if name == "misc":
            refs["misc_bf"][...] = r.astype(BF16)
        off += width
    off = 0
    for name, width, _ in _K1_TR:
        r = lax.dot_general(wt_ref[off:off + width, :], h, (((1,), (1,)), ((), ())),
                            preferred_element_type=F32)
        if name in _K1_VAUG:
            tail = (lax.broadcasted_iota(I32, (V_AUG - HEAD_DIM, r.shape[1]), 0) == 0)
            tail = jnp.where(tail, 1.0, 0.0).astype(BF16)
            for hh in range(width // HEAD_DIM):
                refs[name][hh * V_AUG:hh * V_AUG + HEAD_DIM, :] = \
                    r[hh * HEAD_DIM:(hh + 1) * HEAD_DIM, :].astype(BF16)
                refs[name][hh * V_AUG + HEAD_DIM:(hh + 1) * V_AUG, :] = tail
        else:
            refs[name][...] = r.astype(refs[name].dtype)
        off += width


def _k1_call(x2, g, w_nat, w_trT):
    s = x2.shape[0]
    tm = K1_TM
    outs = (tuple(jax.ShapeDtypeStruct((s, w), dt) for _, w, dt in _K1_NAT)
            + (jax.ShapeDtypeStruct((s, LANES), BF16),)
            + tuple(jax.ShapeDtypeStruct((_k1_tr_rows(n, w), s), dt) for n, w, dt in _K1_TR))
    out_specs = (tuple(pl.BlockSpec((tm, w), lambda i: (i, 0)) for _, w, _ in _K1_NAT)
                 + (pl.BlockSpec((tm, LANES), lambda i: (i, 0)),)
                 + tuple(pl.BlockSpec((_k1_tr_rows(n, w), tm), lambda i: (0, i)) for n, w, _ in _K1_TR))
    res = pl.pallas_call(
        _k1_kernel,
        out_shape=outs,
        grid=(s // tm,),
        in_specs=[pl.BlockSpec((tm, D_MODEL), lambda i: (i, 0)),
                  _whole((1, D_MODEL)),
                  _whole((D_MODEL, K1_NAT_WIDTH)),
                  _whole((K1_TR_WIDTH, D_MODEL))],
        out_specs=out_specs,
        compiler_params=_cparams(("parallel",)),
        name="k1_norm_proj",
    )(x2, g.reshape(1, D_MODEL), w_nat, w_trT)
    return dict(zip([n for n, _, _ in _K1_NAT] + ["misc_bf"] + [n for n, _, _ in _K1_TR], res))


A_TQ = 256
A_KT = 512
A_SLAB = 256
A_SWEEP_SLABS = 8
LOG2E = 1.4426950408889634
QK_AHEAD = 4
BF16_ROWS = 16


def _flash_step(s, i, v_t, m_ref, acc_ref):
    kt, tq = s.shape
    parts = s.reshape(kt // BF16_ROWS, BF16_ROWS, tq)
    parts = [parts[g] for g in range(kt // BF16_ROWS)]
    while len(parts) > 1:
        parts = [jnp.maximum(parts[g], parts[g + 1]) for g in range(0, len(parts), 2)]
    m_old = m_ref[i]
    m_new = jnp.maximum(m_old, jnp.max(parts[0].astype(F32), axis=0, keepdims=True))
    alpha = jnp.exp2(m_old - m_new)
    p = jnp.exp2(s - m_new.astype(BF16))
    acc_ref[i] = alpha * acc_ref[i] + jnp.dot(v_t, p, preferred_element_type=F32)
    m_ref[i] = m_new


def _flash_result(acc_ref, i):
    return acc_ref[i, 0:HEAD_DIM, :] / acc_ref[i, HEAD_DIM:HEAD_DIM + 1, :]


def _logits(k_t, q_pad):
    return jnp.dot(k_t, q_pad, preferred_element_type=F32).astype(BF16)


def _qk_prologue(k_t, qpad_ref, n, s_ref):
    for i in range(min(QK_AHEAD, n)):
        s_ref[i] = _logits(k_t, qpad_ref[i])


def _flash_tile(k_t, qpad_ref, n, v_tile, bias, m_ref, acc_ref, s_ref=None, k_next=None):
    a = min(QK_AHEAD, n)
    if s_ref is None:
        pend = [_logits(k_t, qpad_ref[i]) for i in range(a)]
    else:
        pend = [s_ref[i] for i in range(a)]
    for i in range(n):
        s = pend.pop(0)
        if i + a < n:
            pend.append(_logits(k_t, qpad_ref[i + a]))
        elif s_ref is not None:
            s_ref[i + a - n] = _logits(k_next, qpad_ref[i + a - n])
        if bias is not None:
            s = s + bias
        _flash_step(s, i, v_tile(i), m_ref, acc_ref)


def _bit_transpose32(words):
    a = list(words)
    j, m = 16, 0x0000FFFF
    while j:
        k = 0
        while k < 32:
            t = (a[k] ^ lax.shift_right_logical(a[k + j], j)) & m
            a[k] = a[k] ^ t
            a[k + j] = a[k + j] ^ lax.shift_left(t, j)
            k = (k + j + 1) & ~j
        j >>= 1
        m = (m ^ (m << j)) & 0xFFFFFFFF
    return a


def _dsa_kernel(topk, qaT_ref, qiT_ref, wiT_ref, qiTn_ref, wiTn_ref, ka_ref, vaT_ref, mb_ref, tri_ref,
                o_ref, planes_ref, cand_ref, above_ref, qpad_ref, qipad_ref, acc_ref, m_ref, carry_ref):
    tq, kt = A_TQ, A_KT
    b = pl.program_id(0)
    nb = pl.num_programs(0)
    ntiles = ((b + 1) * tq + kt - 1) // kt

    def vis_end_of(blk):
        q_pos = blk * tq + lax.broadcasted_iota(I32, (1, tq), 1)
        return (q_pos // CHUNK + 1) * CHUNK

    def set_qipad(src_ref):
        qiT = src_ref[...]
        for h in range(H_IDX):
            qipad_ref[h, 0:D_IDX, :] = qiT[h * D_IDX:(h + 1) * D_IDX, :]
            qipad_ref[h, D_IDX:LANES, :] = jnp.zeros((LANES - D_IDX, tq), BF16)

    def p1(j, w_ref, vis_end):
        s0 = pl.multiple_of(j * kt, kt)
        mb = mb_ref[pl.ds(s0, kt), :]
        score = jnp.zeros((kt, tq), F32)
        for h in range(H_IDX):
            r = jnp.dot(mb, qipad_ref[h], preferred_element_type=F32)
            score = score + jnp.maximum(r, 0.0) * w_ref[h:h + 1, :]
        bits = lax.bitcast_convert_type(score, I32)
        ukey = bits ^ (lax.shift_right_arithmetic(bits, 31) | INT_MIN)
        if vis_end is not None:
            s_pos = s0 + lax.broadcasted_iota(I32, (kt, 1), 0)
            ukey = jnp.where(s_pos < vis_end, ukey, 0)
        u4 = ukey.reshape(kt // A_SLAB, 32, 8, tq)
        for s2 in range(kt // A_SLAB):
            planes = _bit_transpose32([u4[s2, v] for v in range(32)])
            for r in range(32):
                planes_ref[j * (kt // A_SLAB) + s2, r] = planes[r]

    @pl.when(b == 0)
    def _():
        set_qipad(qiT_ref)
        vis0 = vis_end_of(0)

        def first_block(j, carry):
            p1(j, wiT_ref, vis0)
            return carry
        lax.fori_loop(0, ntiles, first_block, 0)

    kf = float(topk)
    nslab = ntiles * (kt // A_SLAB)
    nstep = (nslab + A_SWEEP_SLABS - 1) // A_SWEEP_SLABS

    def init_sets(sl, carry):
        cand_ref[sl] = jnp.full((8, tq), -1, I32)
        above_ref[sl] = jnp.zeros((8, tq), I32)
        return carry
    lax.fori_loop(0, nslab, init_sets, 0)

    def init_pad(sl, carry):
        cand_ref[sl] = jnp.zeros((8, tq), I32)
        above_ref[sl] = jnp.zeros((8, tq), I32)
        planes_ref[sl] = jnp.zeros((32, 8, tq), I32)
        return carry
    lax.fori_loop(nslab, nstep * A_SWEEP_SLABS, init_pad, 0)

    def apply_decision(sl, prev_plane, took_one):
        cand = cand_ref[sl]
        ones = cand & prev_plane
        cand = jnp.where(took_one, ones, cand ^ ones)
        above_ref[sl] = jnp.where(took_one, above_ref[sl], above_ref[sl] | ones)
        cand_ref[sl] = cand
        return cand

    def sweep(i, carry):
        n_above, took, tau = carry
        took_one = jnp.broadcast_to(took, (8, tq)) != 0
        first = i == 0

        def step(jj, acc):
            for s2 in range(A_SWEEP_SLABS):
                sl = jj * A_SWEEP_SLABS + s2
                prev_plane = jnp.where(first, -1, planes_ref[sl, jnp.maximum(i - 1, 0)])
                cand = apply_decision(sl, prev_plane, took_one)
                acc = acc + lax.population_count(cand & planes_ref[sl, i])
            return acc
        acc = lax.fori_loop(0, nstep, step, jnp.zeros((8, tq), I32))
        n_one = jnp.sum(acc.astype(F32), axis=0, keepdims=True)
        take = (n_above + n_one) >= kf
        n_above = jnp.where(take, n_above, n_above + n_one)
        tau = jnp.where(take, tau | lax.shift_left(jnp.int32(1), 31 - i), tau)
        return n_above, jnp.where(take, 1, 0), tau
    n_above, took, tau = lax.fori_loop(
        0, 32, sweep, (jnp.zeros((1, tq), F32), jnp.ones((1, tq), I32), jnp.zeros((1, tq), I32)))

    def last_decision(sl, carry):
        apply_decision(sl, planes_ref[sl, 31], jnp.broadcast_to(took, (8, tq)) != 0)
        return carry
    lax.fori_loop(0, nslab, last_decision, 0)
    need = jnp.where(tau == 0, 0.0, kf - n_above)

    m_ref[...] = jnp.full(m_ref.shape, M_INIT, F32)
    acc_ref[...] = jnp.zeros(acc_ref.shape, F32)
    carry_ref[...] = jnp.zeros(carry_ref.shape, F32)
    qaT = (qaT_ref[...].astype(F32) * (HEAD_DIM ** -0.5 * LOG2E)).astype(BF16)
    row = lax.broadcasted_iota(I32, qaT.shape, 0)
    for h in range(H_A):
        qpad_ref[h] = jnp.where((row >= h * HEAD_DIM) & (row < (h + 1) * HEAD_DIM), qaT,
                                jnp.zeros_like(qaT))

    def slab_rows(ref, sl):
        word = ref[sl]
        return jnp.concatenate([lax.shift_right_logical(word, 31 - v) & 1 for v in range(32)], axis=0)

    def p3(j):
        s0 = pl.multiple_of(j * kt, kt)
        slabs = [j * (kt // A_SLAB) + s2 for s2 in range(kt // A_SLAB)]
        gt = jnp.concatenate([slab_rows(above_ref, sl) for sl in slabs], axis=0) != 0
        eq_i = jnp.concatenate([slab_rows(cand_ref, sl) for sl in slabs], axis=0)
        eq = eq_i != 0
        eqf = eq_i.astype(F32)
        pref = jnp.dot(tri_ref[...], eqf.astype(BF16), preferred_element_type=F32)
        seen = carry_ref[...]
        sel = gt | (eq & (pref < need - seen))
        bias = jnp.where(sel, 0.0, NEG_INF).astype(BF16)
        carry_ref[...] = seen + pref[kt - 1:kt, :] + eqf[kt - 1:kt, :]
        k_t = ka_ref[pl.ds(s0, kt), :]
        _flash_tile(k_t, qpad_ref, H_A, lambda i: vaT_ref[i * V_AUG:(i + 1) * V_AUG, pl.ds(s0, kt)],
                    bias, m_ref, acc_ref)

    def p3_only(j, carry):
        p3(j)
        return carry

    @pl.when(b + 1 < nb)
    def _():
        set_qipad(qiTn_ref)
        ntiles_next = ((b + 2) * tq + kt - 1) // kt
        nfull_next = ((b + 1) * tq) // kt
        vis_next = vis_end_of(b + 1)

        def both(j, carry):
            p1(j, wiTn_ref, None)
            p3(j)
            return carry
        lax.fori_loop(0, nfull_next, both, 0)
        lax.fori_loop(nfull_next, ntiles, p3_only, 0)

        def next_masked(j, carry):
            p1(j, wiTn_ref, vis_next)
            return carry
        lax.fori_loop(nfull_next, ntiles_next, next_masked, 0)

    @pl.when(b + 1 == nb)
    def _():
        lax.fori_loop(0, ntiles, p3_only, 0)

    ys = [_flash_result(acc_ref, h) for h in range(H_A)]
    o_ref[...] = jnp.concatenate(ys, axis=0).T.astype(o_ref.dtype)


def _dsa_call(qaT, qiT, wiT, ka, vaT, misc_bf):
    s = ka.shape[0]
    topk = min(TOPK_MAX, s // 4)
    tri = jnp.tril(jnp.ones((A_KT, A_KT), F32), k=-1).astype(BF16)
    vm = pl.BlockSpec(memory_space=pltpu.VMEM)
    assert s % (A_SLAB * A_SWEEP_SLABS) == 0 and s % A_KT == 0, "sequence length not supported"
    nb = s // A_TQ
    nxt = lambda i: (0, jnp.minimum(i + 1, nb - 1))
    return pl.pallas_call(
        functools.partial(_dsa_kernel, topk),
        out_shape=jax.ShapeDtypeStruct((s, W_BRANCH), BF16),
        grid=(nb,),
        in_specs=[pl.BlockSpec((256, A_TQ), lambda i: (0, i)),
                  pl.BlockSpec((LANES, A_TQ), lambda i: (0, i)),
                  pl.BlockSpec((8, A_TQ), lambda i: (0, i)),
                  pl.BlockSpec((LANES, A_TQ), nxt),
                  pl.BlockSpec((8, A_TQ), nxt),
                  vm, vm, vm, vm],
        out_specs=pl.BlockSpec((A_TQ, 256), lambda i: (i, 0)),
        scratch_shapes=[pltpu.VMEM((s // A_SLAB, 32, 8, A_TQ), I32),
                        pltpu.VMEM((s // A_SLAB, 8, A_TQ), I32),
                        pltpu.VMEM((s // A_SLAB, 8, A_TQ), I32),
                        pltpu.VMEM((H_A, 256, A_TQ), BF16),
                        pltpu.VMEM((H_IDX, LANES, A_TQ), BF16),
                        pltpu.VMEM((H_A, V_AUG, A_TQ), F32),
                        pltpu.VMEM((H_A, 1, A_TQ), F32),
                        pltpu.VMEM((1, A_TQ), F32)],
        compiler_params=_cparams(("arbitrary",)),
        name="dsa_mixer",
    )(qaT, qiT, wiT, qiT, wiT, ka, vaT, misc_bf, tri)


def _pair_select(lo, hi):
    lane = lax.broadcasted_iota(I32, lo.shape, 1)
    return jnp.where(lane < HEAD_DIM, lo, hi)


def _pair_head_rms(o, g):
    lane = lax.broadcasted_iota(I32, o.shape, 1)
    low = lane < HEAD_DIM
    sq = o * o
    ms_lo = jnp.sum(jnp.where(low, sq, 0.0), axis=1, keepdims=True) * (1.0 / HEAD_DIM)
    ms_hi = jnp.sum(jnp.where(low, 0.0, sq), axis=1, keepdims=True) * (1.0 / HEAD_DIM)
    ms = jnp.where(low, ms_lo, ms_hi)
    return o * lax.rsqrt(ms + EPS) * g


D_TQ = 256
D_KT = 1024


def _diff_kernel(lam_init, qT_ref, kd_ref, vT_ref, lam_ref, g_ref, o_ref, qpad_ref, acc_ref, m_ref, s_ref):
    tq = D_TQ
    b = pl.program_id(0)
    q_pos = b * tq + lax.broadcasted_iota(I32, (1, tq), 1)
    vis_end = (q_pos // CHUNK + 1) * CHUNK
    m_ref[...] = jnp.full(m_ref.shape, M_INIT, F32)
    acc_ref[...] = jnp.zeros(acc_ref.shape, F32)
    qT = (qT_ref[...].astype(F32) * (DQ_D ** -0.5 * LOG2E)).astype(BF16)
    row = lax.broadcasted_iota(I32, qT.shape, 0)
    for i in range(2 * H_D):
        qpad_ref[i] = jnp.where((row >= i * DQ_D) & (row < (i + 1) * DQ_D), qT, jnp.zeros_like(qT))

    def tile(s0, kt, masked, s_ref=None, s0_next=None):
        k_t = kd_ref[pl.ds(s0, kt), :]
        k_next = None if s0_next is None else kd_ref[pl.ds(s0_next, kt), :]
        bias = None
        if masked:
            vis = (s0 + lax.broadcasted_iota(I32, (kt, 1), 0)) < vis_end
            bias = jnp.where(vis, 0.0, NEG_INF).astype(BF16)
        _flash_tile(k_t, qpad_ref, 2 * H_D, lambda i: vT_ref[(i // 2) * V_AUG:(i // 2 + 1) * V_AUG, pl.ds(s0, kt)],
                    bias, m_ref, acc_ref, s_ref, k_next)

    n_big = (b * tq) // D_KT

    @pl.when(n_big > 0)
    def _():
        _qk_prologue(kd_ref[pl.ds(0, D_KT), :], qpad_ref, 2 * H_D, s_ref)

    def full_tile(j, carry):
        j_next = jnp.minimum(j + 1, n_big - 1)
        tile(pl.multiple_of(j * D_KT, D_KT), D_KT, False, s_ref, pl.multiple_of(j_next * D_KT, D_KT))
        return carry
    lax.fori_loop(0, n_big, full_tile, 0)

    def small_tile(j, carry):
        tile(pl.multiple_of(j * tq, tq), tq, False)
        return carry
    lax.fori_loop(n_big * (D_KT // tq), b, small_tile, 0)
    tile(pl.multiple_of(b * tq, tq), tq, True)

    lp = lam_ref[...]
    lam = (jnp.exp(jnp.sum(lp[0:1] * lp[1:2], axis=1, keepdims=True))
           - jnp.exp(jnp.sum(lp[2:3] * lp[3:4], axis=1, keepdims=True)) + lam_init)
    ys = []
    for h in range(H_D):
        o = _flash_result(acc_ref, 2 * h) - lam * _flash_result(acc_ref, 2 * h + 1)
        ms = jnp.mean(o * o, axis=0, keepdims=True)
        ys.append(o * lax.rsqrt(ms + EPS) * g_ref[h * HEAD_DIM:(h + 1) * HEAD_DIM, :] * (1.0 - lam_init))
    o_ref[...] = jnp.concatenate(ys, axis=0).T.astype(o_ref.dtype)


def _diff_call(qdT, kd, vdT, lam_p, dnorm_g, lam_init):
    s = kd.shape[0]
    return pl.pallas_call(
        functools.partial(_diff_kernel, lam_init),
        out_shape=jax.ShapeDtypeStruct((s, W_BRANCH), BF16),
        grid=(s // D_TQ,),
        in_specs=[pl.BlockSpec((256, D_TQ), lambda i: (0, i)),
                  pl.BlockSpec(memory_space=pltpu.VMEM),
                  pl.BlockSpec(memory_space=pltpu.VMEM),
                  _whole((4, DQ_D)),
                  _whole((256, 1))],
        out_specs=pl.BlockSpec((D_TQ, 256), lambda i: (i, 0)),
        scratch_shapes=[pltpu.VMEM((2 * H_D, 256, D_TQ), BF16),
                        pltpu.VMEM((2 * H_D, V_AUG, D_TQ), F32),
                        pltpu.VMEM((2 * H_D, 1, D_TQ), F32),
                        pltpu.VMEM((QK_AHEAD, D_KT, D_TQ), BF16)],
        compiler_params=_cparams(("parallel",)),
        name="diff_attention",
    )(qdT, kd, vdT, lam_p, dnorm_g.reshape(256, 1))


C_TQ = 128
C_PAD = N_PREV_CHUNKS * CHUNK
C_WIN = C_PAD + C_TQ
C_LINE = C_TQ + C_WIN


def _band_bias_line(rel_bias):
    dist_desc = jnp.arange(C_TQ - 1 + C_PAD, C_PAD - C_WIN, -1)
    line = rel_bias[:, jnp.clip(dist_desc, -(CHUNK - 1), MAX_REL_PAST) + (CHUNK - 1)].astype(F32)
    return jnp.pad(line, ((0, 0), (0, C_LINE - line.shape[1])))


def _band_table_init(line_ref, bias_ref):
    tq = C_TQ

    @pl.when(pl.program_id(0) == 0)
    def _():
        i = lax.broadcasted_iota(I32, (tq, C_WIN), 0)
        w = lax.broadcasted_iota(I32, (tq, C_WIN), 1)
        in_band = (w // CHUNK >= i // CHUNK) & (w // CHUNK <= i // CHUNK + N_PREV_CHUNKS)
        for h in range(H_C):
            x = jnp.broadcast_to(line_ref[h:h + 1, :], (tq, C_LINE))
            y = pltpu.roll(x, 1, 1, stride=1, stride_axis=0)
            bias_ref[h] = jnp.where(in_band, y[:, C_LINE - C_WIN:], NEG_INF)


def _band_body(qc_ref, kcT_ref, vc_ref, o_ref, bias_ref):
    tq = C_TQ
    b = pl.program_id(0)
    w0 = pl.multiple_of(b * tq, tq)
    q_all = qc_ref[...] * 0.125
    key_abs = b * tq - C_PAD + lax.broadcasted_iota(I32, (1, C_WIN), 1)
    ok = key_abs >= 0
    logits = [jnp.dot(q_all[:, h * HEAD_DIM:(h + 1) * HEAD_DIM],
                      kcT_ref[h * HEAD_DIM:(h + 1) * HEAD_DIM, pl.ds(w0, C_WIN)],
                      preferred_element_type=F32) for h in range(H_C)]
    heads = []
    for h in range(H_C):
        s = jnp.where(ok, logits[h] + bias_ref[h], NEG_INF)
        m = jnp.max(s, axis=1, keepdims=True)
        p = jnp.exp(s - m)
        l = jnp.sum(p, axis=1, keepdims=True)
        pv = jnp.dot(p.astype(BF16), vc_ref[pl.ds(w0, C_WIN), (h // 2) * LANES:(h // 2 + 1) * LANES],
                     preferred_element_type=F32)
        heads.append(pv / l)
    for pr in range(H_C // 2):
        o_ref[:, pr * LANES:(pr + 1) * LANES] = _pair_select(heads[2 * pr], heads[2 * pr + 1]).astype(o_ref.dtype)


def _band_kernel(qc_ref, kcT_ref, vc_ref, line_ref, o_ref, bias_ref):
    _band_table_init(line_ref, bias_ref)
    _band_body(qc_ref, kcT_ref, vc_ref, o_ref, bias_ref)


def _band_call(qc, kcT_pad, vc_pad, bias_line):
    s = qc.shape[0]
    return pl.pallas_call(
        _band_kernel,
        out_shape=jax.ShapeDtypeStruct((s, W_BRANCH), BF16),
        grid=(s // C_TQ,),
        in_specs=[pl.BlockSpec((C_TQ, 256), lambda i: (i, 0)),
                  pl.BlockSpec(memory_space=pltpu.VMEM),
                  pl.BlockSpec(memory_space=pltpu.VMEM),
                  _whole((H_C, C_LINE))],
        out_specs=pl.BlockSpec((C_TQ, 256), lambda i: (i, 0)),
        scratch_shapes=[pltpu.VMEM((H_C, C_TQ, C_WIN), F32)],
        compiler_params=_cparams(("arbitrary",)),
        name="band_attention",
    )(qc, kcT_pad, vc_pad, bias_line)


B_NCH = 2
B_L = B_NCH * CHUNK
HIGHEST = lax.Precision.HIGHEST


def _mlstm_state_init(tail_ref, ct_ref, n_ref, m_ref):
    @pl.when(pl.program_id(0) == 0)
    def _():
        tail_ref[...] = jnp.zeros(tail_ref.shape, F32)
        ct_ref[...] = jnp.zeros(ct_ref.shape, F32)
        n_ref[...] = jnp.zeros(n_ref.shape, F32)
        m_ref[...] = jnp.zeros(m_ref.shape, F32)


def _mlstm_kernel(qk_ref, vb_ref, ob_ref, misc_ref, cw_ref, cb_ref, gb_ref, ng_ref, o_ref,
                  tail_ref, ct_ref, n_ref, m_ref):
    _mlstm_state_init(tail_ref, ct_ref, n_ref, m_ref)
    _mlstm_body(qk_ref, vb_ref, ob_ref, misc_ref, cw_ref, cb_ref, gb_ref, ng_ref, o_ref,
                tail_ref, ct_ref, n_ref, m_ref)


def _mlstm_body(qk_ref, vb_ref, ob_ref, misc_ref, cw_ref, cb_ref, gb_ref, ng_ref, o_ref,
                tail_ref, ct_ref, n_ref, m_ref):
    L = CHUNK
    T = B_L
    hd = HEAD_DIM

    x = qk_ref[...]
    xx = jnp.concatenate([tail_ref[...], x], axis=0)
    y = jnp.broadcast_to(cb_ref[...], x.shape)
    for j in range(CONV_K):
        y = y + cw_ref[j:j + 1, :] * xx[8 - (CONV_K - 1) + j:8 - (CONV_K - 1) + j + T, :]
    tail_ref[...] = x[T - 8:T, :]
    qk = y * jax.nn.sigmoid(y)
    q_all = qk[:, :W_BRANCH]
    k_all = qk[:, W_BRANCH:] * 0.125
    v_all = vb_ref[...]
    o_gate = jax.nn.sigmoid(ob_ref[...])

    gts = misc_ref[...] + gb_ref[...]
    lf = jnp.minimum(gts, 0.0) - jnp.log1p(jnp.exp(-jnp.abs(gts)))
    r_t = lax.broadcasted_iota(I32, (T, T), 0)
    c_t = lax.broadcasted_iota(I32, (T, T), 1)
    ltri = jnp.where((c_t <= r_t) & (c_t // L == r_t // L), 1.0, 0.0)
    cum = jnp.dot(ltri, lf, precision=HIGHEST, preferred_element_type=F32)
    lane = lax.broadcasted_iota(I32, (T, LANES), 1)
    mixed = jnp.where(lane < MISC_FB, gts, cum)
    sel_r = lax.broadcasted_iota(I32, (8, LANES), 0)
    sel_c = lax.broadcasted_iota(I32, (8, LANES), 1)
    sel = jnp.where(sel_c == sel_r + MISC_IB, 1.0, 0.0)
    rows = lax.dot_general(sel, mixed, (((1,), (1,)), ((), ())), precision=HIGHEST,
                           preferred_element_type=F32)
    causal = lax.broadcasted_iota(I32, (L, L), 1) <= lax.broadcasted_iota(I32, (L, L), 0)

    heads = range(H_B)
    sl = lambda a, t0, h: a[t0:t0 + L, h * hd:(h + 1) * hd]
    pre = []
    for ci in range(B_NCH):
        t0 = ci * L
        per_head = []
        for h in heads:
            q, k, v = sl(q_all, t0, h), sl(k_all, t0, h), sl(v_all, t0, h)
            qb, kb, vbf = q.astype(BF16), k.astype(BF16), v.astype(BF16)
            qkt = lax.dot_general(qb, kb, (((1,), (1,)), ((), ())), preferred_element_type=F32)
            it_r = rows[h:h + 1, t0:t0 + L]
            cum_r = rows[H_B + h:H_B + h + 1, t0:t0 + L]
            it_c = gts[t0:t0 + L, MISC_IB + h:MISC_IB + h + 1]
            cum_c = cum[t0:t0 + L, MISC_FB + h:MISC_FB + h + 1]
            dmat = jnp.where(causal, cum_c - cum_r + it_r, NEG_INF)
            dmax = jnp.max(dmat, axis=1, keepdims=True)
            cum_end = cum_c[L - 1:L, :]
            g = cum_end - cum_c + it_c
            gmax = jnp.max(g, axis=0, keepdims=True)
            per_head.append((q, k, qb, vbf, qkt, dmat, dmax, cum_c, cum_end, g, gmax))
        pre.append(per_head)

    state = [(ct_ref[h], n_ref[h], m_ref[h]) for h in heads]
    out_chunks = []
    for ci in range(B_NCH):
        t0 = ci * L
        qc = [jnp.dot(pre[ci][h][2], state[h][0].astype(BF16), preferred_element_type=F32) for h in heads]
        mid = []
        for h in heads:
            q, k, qb, vbf, qkt, dmat, dmax, cum_c, cum_end, g, gmax = pre[ci][h]
            ct, n_row, m_prev = state[h]
            m_inter = cum_c + m_prev
            m_t = jnp.maximum(m_inter, dmax)
            w = jnp.exp(dmat - m_t) * qkt
            inter = jnp.exp(m_inter - m_t)
            m_new = jnp.maximum(cum_end + m_prev, gmax)
            carry_scale = jnp.exp(cum_end + m_prev - m_new)
            src_k = jnp.exp(g - m_new) * k
            mid.append((w, inter, m_t, m_new, carry_scale, src_k))
        wv = [jnp.dot(mid[h][0].astype(BF16), pre[ci][h][3], preferred_element_type=F32) for h in heads]
        upd = [lax.dot_general(mid[h][5].astype(BF16), pre[ci][h][3], (((0,), (0,)), ((), ())),
                               preferred_element_type=F32) for h in heads]
        outs = []
        for h in heads:
            q = pre[ci][h][0]
            w, inter, m_t, m_new, carry_scale, src_k = mid[h]
            ct, n_row, _ = state[h]
            num = inter * qc[h] + wv[h]
            den = inter * jnp.sum(q * n_row, axis=1, keepdims=True) + jnp.sum(w, axis=1, keepdims=True)
            h_t = num / jnp.maximum(jnp.abs(den), jnp.exp(-m_t))
            state[h] = (carry_scale * ct + upd[h],
                        carry_scale * n_row + jnp.sum(src_k, axis=0, keepdims=True), m_new)
            ms = jnp.mean(h_t * h_t, axis=1, keepdims=True)
            hn = h_t * lax.rsqrt(ms + EPS) * ng_ref[:, h * hd:(h + 1) * hd]
            outs.append(sl(o_gate, t0, h) * hn)
        out_chunks.append(jnp.concatenate(outs, axis=1))
    for h in heads:
        ct_ref[h], n_ref[h], m_ref[h] = state[h]
    o_ref[...] = jnp.concatenate(out_chunks, axis=0).astype(o_ref.dtype)


def _mlstm_call(qkb, vb, ob, misc, conv_w, conv_b, i_bias, f_bias, norm_g):
    s = qkb.shape[0]
    gbias = jnp.zeros((1, LANES), F32)
    gbias = gbias.at[0, MISC_IB:MISC_IB + H_B].set(i_bias).at[0, MISC_FB:MISC_FB + H_B].set(f_bias)
    row = lambda w: pl.BlockSpec((B_L, w), lambda i: (i, 0))
    return pl.pallas_call(
        _mlstm_kernel,
        out_shape=jax.ShapeDtypeStruct((s, W_BRANCH), BF16),
        grid=(s // B_L,),
        in_specs=[row(512), row(256), row(256), row(128),
                  _whole((CONV_K, 512)), _whole((1, 512)), _whole((1, LANES)), _whole((1, 256))],
        out_specs=row(256),
        scratch_shapes=[pltpu.VMEM((8, 512), F32),
                        pltpu.VMEM((H_B, HEAD_DIM, HEAD_DIM), F32),
                        pltpu.VMEM((H_B, 1, HEAD_DIM), F32),
                        pltpu.VMEM((H_B, 1, 1), F32)],
        compiler_params=_cparams(("arbitrary",)),
        name="mlstm_mixer",
    )(qkb, vb, ob, misc, conv_w, conv_b.reshape(1, 512), gbias, norm_g.reshape(1, 256))


N_B_IN, N_C_IN = 8, 4


def _mlstm_band_kernel(*refs):
    b_in = refs[:N_B_IN]
    c_in = refs[N_B_IN:N_B_IN + N_C_IN]
    yb_ref, yc_ref = refs[N_B_IN + N_C_IN:N_B_IN + N_C_IN + 2]
    b_scr = refs[N_B_IN + N_C_IN + 2:N_B_IN + N_C_IN + 6]
    (bias_ref,) = refs[N_B_IN + N_C_IN + 6:]
    qc_ref, kcT_ref, vc_ref, line_ref = c_in
    _mlstm_state_init(*b_scr)
    _band_table_init(line_ref, bias_ref)
    _mlstm_body(*b_in, yb_ref, *b_scr)
    _band_body(qc_ref, kcT_ref, vc_ref, yc_ref, bias_ref)


def _mlstm_band_call(qkb, vb, ob, misc, conv_w, conv_b, i_bias, f_bias, norm_g,
                     qc, kcT_pad, vc_pad, bias_line):
    assert B_L == C_TQ
    s = qkb.shape[0]
    gbias = jnp.zeros((1, LANES), F32)
    gbias = gbias.at[0, MISC_IB:MISC_IB + H_B].set(i_bias).at[0, MISC_FB:MISC_FB + H_B].set(f_bias)
    row = lambda w: pl.BlockSpec((B_L, w), lambda i: (i, 0))
    vm = pl.BlockSpec(memory_space=pltpu.VMEM)
    return pl.pallas_call(
        _mlstm_band_kernel,
        out_shape=(jax.ShapeDtypeStruct((s, W_BRANCH), BF16), jax.ShapeDtypeStruct((s, W_BRANCH), BF16)),
        grid=(s // B_L,),
        in_specs=[row(512), row(256), row(256), row(128),
                  _whole((CONV_K, 512)), _whole((1, 512)), _whole((1, LANES)), _whole((1, 256)),
                  row(256), vm, vm, _whole((H_C, C_LINE))],
        out_specs=(row(256), row(256)),
        scratch_shapes=[pltpu.VMEM((8, 512), F32),
                        pltpu.VMEM((H_B, HEAD_DIM, HEAD_DIM), F32),
                        pltpu.VMEM((H_B, 1, HEAD_DIM), F32),
                        pltpu.VMEM((H_B, 1, 1), F32),
                        pltpu.VMEM((H_C, C_TQ, C_WIN), F32)],
        compiler_params=_cparams(("arbitrary",)),
        name="mlstm_band",
    )(qkb, vb, ob, misc, conv_w, conv_b.reshape(1, 512), gbias, norm_g.reshape(1, 256),
      qc, kcT_pad, vc_pad, bias_line)


M_TM = 256
ROUTE_LOGIT0 = N_GROUPS
RT_E1, RT_E2, RT_G1, RT_G2 = 0, 1, 2, 3


def _lane_argmax(vals, lane):
    v = jnp.max(vals, axis=1, keepdims=True)
    idx = jnp.min(jnp.where(vals == v, lane, float(LANES)), axis=1, keepdims=True)
    return v, idx


def _merge_kernel(x_ref, ya_ref, yb_ref, yc_ref, yd_ref, gm_ref, wg_ref, wb_ref, wo_ref, gf_ref,
                  wr_ref, rb_ref, xo_ref, h2_ref, rt_ref):
    x = x_ref[...]
    h = _rms(x, gm_ref[...]).astype(BF16)
    mixed = jnp.zeros(x.shape, F32)
    for n, y_ref in enumerate((ya_ref, yb_ref, yc_ref, yd_ref)):
        gate = jax.nn.sigmoid(jnp.dot(h, wg_ref[:, n * D_MODEL:(n + 1) * D_MODEL],
                                      preferred_element_type=F32))
        up = jnp.dot(y_ref[...], wb_ref[n], preferred_element_type=F32)
        mixed = mixed + gate * up
    xn = x + jnp.dot(mixed.astype(BF16), wo_ref[...], preferred_element_type=F32)
    xo_ref[...] = xn
    h2 = _rms(xn, gf_ref[...])
    _store_row_tiled(h2_ref, h2)
    logits = jnp.dot(h2.astype(BF16), wr_ref[...], preferred_element_type=F32) + rb_ref[...]

    lane = lax.broadcasted_iota(I32, logits.shape, 1).astype(F32)
    neg = -jnp.inf
    gmask = lane < N_GROUPS
    gmax, g_sel = _lane_argmax(jnp.where(gmask, logits, neg), lane)
    g_gate = 1.0 / jnp.sum(jnp.where(gmask, jnp.exp(logits - gmax), 0.0), axis=1, keepdims=True)
    e_lo = ROUTE_LOGIT0 + EXPERTS_PER_GROUP * g_sel
    el = jnp.where((lane >= e_lo) & (lane < e_lo + EXPERTS_PER_GROUP), logits, neg)
    v1, i1 = _lane_argmax(el, lane)
    v2, i2 = _lane_argmax(jnp.where(lane == i1, neg, el), lane)
    e = jnp.exp(v2 - v1)
    p1 = 1.0 / (1.0 + e)
    p2 = e / (1.0 + e)
    rt = jnp.where(lane == RT_E1, i1 - ROUTE_LOGIT0,
                   jnp.where(lane == RT_E2, i2 - ROUTE_LOGIT0,
                             jnp.where(lane == RT_G1, p1 * g_gate,
                                       jnp.where(lane == RT_G2, p2 * g_gate, 0.0))))
    rt_ref[...] = rt


def _merge_call(x2, ya, yb, yc, yd, g_mix, w_gate, w_branch, w_out, g_ffn, w_route, b_route):
    s = x2.shape[0]
    row = lambda w: pl.BlockSpec((M_TM, w), lambda i: (i, 0))
    vm = pl.BlockSpec(memory_space=pltpu.VMEM)
    return pl.pallas_call(
        _merge_kernel,
        out_shape=(jax.ShapeDtypeStruct((s, D_MODEL), F32),
                   jax.ShapeDtypeStruct((s * ROW_TILES, LANES), F32),
                   jax.ShapeDtypeStruct((s, LANES), F32)),
        grid=(s // M_TM,),
        in_specs=[row(D_MODEL), row(256), row(256), row(256), row(256),
                  vm, vm, vm, vm, vm, vm, vm],
        out_specs=(row(D_MODEL), pl.BlockSpec((M_TM * ROW_TILES, LANES), lambda i: (i, 0)), row(LANES)),
        compiler_params=_cparams(("parallel",)),
        name="merge_route",
    )(x2, ya, yb, yc, yd, g_mix.reshape(1, D_MODEL), w_gate, w_branch, w_out,
      g_ffn.reshape(1, D_MODEL), w_route, b_route)


E_BM = 256
GATHER_UNROLL = 8


ROW_TILES = D_MODEL // LANES


def _store_row_tiled(ref, x):
    m = x.shape[0]
    for c in range(ROW_TILES):
        ref[pl.ds(c, m, stride=ROW_TILES), :] = x[:, c * LANES:(c + 1) * LANES]


def _load_row_tiled(ref, m, row0=0):
    return jnp.concatenate([ref[pl.ds(row0 * ROW_TILES + c, m, stride=ROW_TILES), :]
                            for c in range(ROW_TILES)], axis=1)


def _gather_rows(src_hbm, idx_ref, dst, sem, n, wait, inline=False):
    def one(r, parity):
        src_row = pl.multiple_of(idx_ref[0, 0, r] * ROW_TILES, ROW_TILES)
        dst_row = r * ROW_TILES if isinstance(r, int) else pl.multiple_of(r * ROW_TILES, ROW_TILES)
        cp = pltpu.make_async_copy(src_hbm.at[pl.ds(src_row, ROW_TILES)],
                                   dst.at[pl.ds(dst_row, ROW_TILES)], sem)
        if wait:
            cp.wait()
        else:
            cp.start(priority=parity)

    if inline:
        for r in range(n):
            one(r, r % 2)
    else:
        def group(g, c):
            for u in range(GATHER_UNROLL):
                one(g * GATHER_UNROLL + u, u % 2)
            return c
        lax.fori_loop(0, n // GATHER_UNROLL, group, 0)


def _moe_kernel(be_ref, nused_ref, tok_ref, tok_next_ref, h2_hbm, w1_ref, w3_ref, w2_ref, o_ref,
                xbuf, w1b, w3b, w2b, sem):
    i = pl.program_id(0)
    nused = nused_ref[0]
    used = i < nused
    slot = i % 2

    @pl.when((i == 0) & used)
    def _():
        _gather_rows(h2_hbm, tok_ref, xbuf.at[0], sem.at[0], E_BM, wait=False)

    @pl.when(used)
    def _():
        prev = be_ref[jnp.maximum(i - 1, 0)]

        @pl.when((i == 0) | (be_ref[i] != prev))
        def _():
            w1b[...] = w1_ref[0, 0].astype(BF16)
            w3b[...] = w3_ref[0, 0].astype(BF16)
            w2b[...] = w2_ref[0, 0].astype(BF16)

        _gather_rows(h2_hbm, tok_ref, xbuf.at[slot], sem.at[slot], E_BM, wait=True)
        _gather_rows(h2_hbm, tok_next_ref, xbuf.at[1 - slot], sem.at[1 - slot], E_BM, wait=False, inline=True)
        xb = _load_row_tiled(xbuf.at[slot], E_BM).astype(BF16)
        a = jnp.dot(xb, w1b[...], preferred_element_type=F32)
        g = jnp.dot(xb, w3b[...], preferred_element_type=F32)
        hid = (a * jax.nn.sigmoid(a) * g).astype(BF16)
        _store_row_tiled(o_ref, jnp.dot(hid, w2b[...], preferred_element_type=F32))

    @pl.when(jnp.logical_not(used))
    def _():
        @pl.when(i == nused)
        def _():
            _gather_rows(h2_hbm, tok_ref, xbuf.at[slot], sem.at[slot], E_BM, wait=True)
        o_ref[...] = jnp.zeros(o_ref.shape, F32)


def _moe_call(h2, row_tok, block_e, nused, w1, w3, w2, layer):
    n_rows = row_tok.shape[0]
    n_blocks = n_rows // E_BM
    wspec = lambda shp: pl.BlockSpec((1, 1) + shp, lambda i, be, nu: (layer, be[i], 0, 0))
    grid_spec = pltpu.PrefetchScalarGridSpec(
        num_scalar_prefetch=2,
        grid=(n_blocks,),
        in_specs=[pl.BlockSpec((1, 1, E_BM), lambda i, be, nu: (i, 0, 0), memory_space=pltpu.SMEM),
                  pl.BlockSpec((1, 1, E_BM), lambda i, be, nu: (jnp.minimum(i + 1, n_blocks - 1), 0, 0),
                               memory_space=pltpu.SMEM),
                  pl.BlockSpec(memory_space=pl.ANY),
                  wspec((D_MODEL, F_EXPERT)), wspec((D_MODEL, F_EXPERT)), wspec((F_EXPERT, D_MODEL))],
        out_specs=pl.BlockSpec((E_BM * ROW_TILES, LANES), lambda i, be, nu: (i, 0)),
        scratch_shapes=[pltpu.VMEM((2, E_BM * ROW_TILES, LANES), F32),
                        pltpu.VMEM((D_MODEL, F_EXPERT), BF16),
                        pltpu.VMEM((D_MODEL, F_EXPERT), BF16),
                        pltpu.VMEM((F_EXPERT, D_MODEL), BF16),
                        pltpu.SemaphoreType.DMA((2,))],
    )
    tok3 = row_tok.reshape(n_blocks, 1, E_BM)
    return pl.pallas_call(
        _moe_kernel,
        out_shape=jax.ShapeDtypeStruct((n_rows * ROW_TILES, LANES), F32),
        grid_spec=grid_spec,
        compiler_params=_cparams(("arbitrary",)),
        name="moe_experts",
    )(block_e, nused, tok3, tok3, h2, w1, w3, w2)


CB_TM = 256


def _combine_kernel(final, pos_ref, pos_next_ref, x_ref, rt_ref, ys_hbm, gfin_ref, o_ref, buf, sem):
    i = pl.program_id(0)
    slot = i % 2
    n = 2 * CB_TM

    @pl.when(i == 0)
    def _():
        _gather_rows(ys_hbm, pos_ref, buf.at[0], sem.at[0], n, wait=False)

    @pl.when(i + 1 < pl.num_programs(0))
    def _():
        _gather_rows(ys_hbm, pos_next_ref, buf.at[1 - slot], sem.at[1 - slot], n, wait=False)
    _gather_rows(ys_hbm, pos_ref, buf.at[slot], sem.at[slot], n, wait=True)

    rt = rt_ref[...]
    y = (_load_row_tiled(buf.at[slot], CB_TM) * rt[:, RT_G1:RT_G1 + 1]
         + _load_row_tiled(buf.at[slot], CB_TM, CB_TM) * rt[:, RT_G2:RT_G2 + 1])
    xn = x_ref[...] + y
    if final:
        xn = _rms(xn, gfin_ref[...])
    o_ref[...] = xn


def _combine_call(x2, rt, ys, pos, g_final, final):
    s = x2.shape[0]
    nb = s // CB_TM
    pos_b = pos.reshape(nb, CB_TM, 2).transpose(0, 2, 1).reshape(nb, 1, 2 * CB_TM)
    return pl.pallas_call(
        functools.partial(_combine_kernel, final),
        out_shape=jax.ShapeDtypeStruct((s, D_MODEL), F32),
        grid=(nb,),
        in_specs=[pl.BlockSpec((1, 1, 2 * CB_TM), lambda i: (i, 0, 0), memory_space=pltpu.SMEM),
                  pl.BlockSpec((1, 1, 2 * CB_TM), lambda i: (jnp.minimum(i + 1, nb - 1), 0, 0),
                               memory_space=pltpu.SMEM),
                  pl.BlockSpec((CB_TM, D_MODEL), lambda i: (i, 0)),
                  pl.BlockSpec((CB_TM, LANES), lambda i: (i, 0)),
                  pl.BlockSpec(memory_space=pl.ANY),
                  _whole((1, D_MODEL))],
        out_specs=pl.BlockSpec((CB_TM, D_MODEL), lambda i: (i, 0)),
        scratch_shapes=[pltpu.VMEM((2, 2 * CB_TM * ROW_TILES, LANES), F32), pltpu.SemaphoreType.DMA((2,))],
        compiler_params=_cparams(("arbitrary",)),
        name="moe_combine",
    )(pos_b, pos_b, x2, rt, ys, g_final.reshape(1, D_MODEL))


PLAN_BLOCK = 256


def _dispatch_plan(rt):
    t = rt.shape[0]
    n_assign = t * TOP_K_INNER
    eid = rt[:, RT_E1:RT_E2 + 1].astype(I32).reshape(n_assign)
    onehot = (eid[:, None] == jnp.arange(N_EXPERTS, dtype=I32)[None, :]).astype(F32)
    pb = PLAN_BLOCK
    oh3 = onehot.reshape(n_assign // pb, pb, N_EXPERTS)
    tri = jnp.tril(jnp.ones((pb, pb), F32), k=-1)
    within = jnp.einsum('ij,bjk->bik', tri, oh3)
    block_tot = jnp.sum(oh3, axis=1)
    block_off = jnp.cumsum(block_tot, axis=0) - block_tot
    counts = (block_off[-1] + block_tot[-1]).astype(I32)
    rank = jnp.sum(oh3 * (within + block_off[:, None, :]), axis=-1).reshape(n_assign)
    padded = (counts + E_BM - 1) // E_BM * E_BM
    ends_pad = jnp.cumsum(padded)
    starts_pad = ends_pad - padded
    dest = (jnp.sum(onehot * starts_pad.astype(F32)[None, :], axis=-1) + rank).astype(I32)
    n_rows = (-(-(n_assign + N_EXPERTS * (E_BM - 1)) // E_BM) + 1) * E_BM
    n_blocks = n_rows // E_BM
    tok = jnp.repeat(jnp.arange(t, dtype=I32), TOP_K_INNER)
    row_tok = jnp.zeros((n_rows,), I32).at[dest].set(tok)
    block_start = jnp.arange(n_blocks, dtype=I32) * E_BM
    block_e = jnp.minimum(jnp.sum((ends_pad[None, :] <= block_start[:, None]).astype(I32), axis=1),
                          N_EXPERTS - 1)
    nused = (ends_pad[-1] // E_BM).astype(I32).reshape(1)
    return row_tok, block_e, nused, dest.reshape(t, TOP_K_INNER)


def _layer(x2, layer_idx, p, experts, final_g, final):
    (norm_mix_g, w_in, conv_w, conv_b, i_bias, f_bias, mnorm_g, rel_bias, lam_p, dnorm_g,
     w_branch, w_gate, w_out, norm_ffn_g, rgw, rgb, rew, reb) = p
    o = _k1_call(x2, norm_mix_g, *_rearrange_w_in(w_in))

    ya = _dsa_call(o["qaT"], o["qiT"], o["wiT"], o["ka"], o["vaT"], o["misc_bf"])
    kcT_pad = jnp.pad(o["kcT"], ((0, 0), (C_PAD, 0)))
    vc_pad = jnp.pad(o["vc"], ((C_PAD, 0), (0, 0)))
    yb, yc = _mlstm_band_call(o["qkb"], o["vb"], o["ob"], o["misc"], conv_w, conv_b, i_bias, f_bias, mnorm_g,
                              o["qc"], kcT_pad, vc_pad, _band_bias_line(rel_bias))
    lam_init = 0.8 - 0.6 * math.exp(-0.3 * layer_idx)
    yd = _diff_call(o["qdT"], o["kd"], o["vdT"], lam_p, dnorm_g, lam_init)

    w_route = jnp.concatenate([rgw, rew, jnp.zeros((D_MODEL, LANES - N_GROUPS - N_EXPERTS), F32)],
                              axis=1).astype(BF16)
    b_route = jnp.concatenate([rgb, reb, jnp.zeros((LANES - N_GROUPS - N_EXPERTS,), F32)]).reshape(1, LANES)
    xn, h2, rt = _merge_call(x2, ya, yb, yc, yd, norm_mix_g, w_gate.astype(BF16), w_branch.astype(BF16),
                             w_out.astype(BF16), norm_ffn_g, w_route, b_route)
    row_tok, block_e, nused, pos = _dispatch_plan(rt)
    ys = _moe_call(h2, row_tok, block_e, nused, *experts, layer_idx)
    return _combine_call(xn, rt, ys, pos, final_g, final)


def kernel(x, norm_mix_g, w_in, conv_w, conv_b, mlstm_i_bias, mlstm_f_bias, mlstm_norm_g, relpos_bias, diff_lambda, diff_norm_g, w_branch, w_gate, w_out, norm_ffn_g, router_group_w, router_group_b, router_expert_w, router_expert_b, expert_w1, expert_w3, expert_w2, final_norm_g):
    assert x.shape[0] == 1 and x.shape[2] == D_MODEL
    params = (norm_mix_g, w_in, conv_w, conv_b, mlstm_i_bias, mlstm_f_bias, mlstm_norm_g, relpos_bias,
              diff_lambda, diff_norm_g, w_branch, w_gate, w_out, norm_ffn_g, router_group_w,
              router_group_b, router_expert_w, router_expert_b)
    experts = (expert_w1, expert_w3, expert_w2)
    depth = norm_mix_g.shape[0]
    x2 = x[0]
    for l in range(depth):
        x2 = _layer(x2, l, tuple(a[l] for a in params), experts, final_norm_g, l == depth - 1)
    return x2[None]
```

```python
import functools
import math

import jax
import jax.numpy as jnp
from jax import lax
from jax.experimental import pallas as pl
from jax.experimental.pallas import tpu as pltpu

F32 = jnp.float32
BF16 = jnp.bfloat16
I32 = jnp.int32

D_MODEL = 1024
CHUNK = 64
HEAD_DIM = 64
NEG_INF = -1e30
H_A = 4
H_IDX = 4
D_IDX = 32
TOPK_MAX = 256
H_B = 4
CONV_K = 4
H_C = 4
N_PREV_CHUNKS = 8
MAX_REL_PAST = 128
H_D = 4
DQ_D = 32
W_BRANCH = 256
N_BRANCH = 4
N_GROUPS = 4
EXPERTS_PER_GROUP = 8
N_EXPERTS = 32
TOP_K_INNER = 2
F_EXPERT = 512
EPS = 1e-6

VMEM_LIMIT_BYTES = 52 * 1024 * 1024
LANES = 128

INT_MIN = -(2 ** 31)
I16_MIN = -(2 ** 15)
I16 = jnp.int16
M_INIT = -5e29

_COL_SIZES = (256, 256, 256, 128, 32, 4, 256, 256, 256, 4, 4, 256, 256, 256, 256, 256, 256, 256)
_COL_NAMES = ("qa", "ka", "va", "qi", "ki", "wi", "qb", "kb", "vb", "ib", "fb", "ob",
              "qc", "kc", "vc", "qd", "kd", "vd")
_COL_OFF = {}
_o = 0
for _n, _s in zip(_COL_NAMES, _COL_SIZES):
    _COL_OFF[_n] = (_o, _s)
    _o += _s
C_IN = _o
MISC_KI = 0
MISC_WI = 32
MISC_IB = 36
MISC_FB = 40


def _cparams(sem):
    return pltpu.CompilerParams(dimension_semantics=sem, vmem_limit_bytes=VMEM_LIMIT_BYTES)


def _whole(shape):
    nd = len(shape)
    return pl.BlockSpec(shape, lambda *_: (0,) * nd)


_K1_NAT = (("ka", 256, BF16), ("misc", 128, F32), ("qkb", 512, F32), ("vb", 256, F32), ("ob", 256, F32),
           ("qc", 256, BF16), ("vc", 256, BF16), ("kd", 256, BF16))
_K1_TR = (("qaT", 256, BF16), ("qiT", 128, BF16), ("vaT", 256, BF16), ("kcT", 256, BF16),
          ("qdT", 256, BF16), ("vdT", 256, BF16), ("wiT", 8, F32))
K1_NAT_WIDTH = sum(w for _, w, _ in _K1_NAT)
K1_TR_WIDTH = sum(w for _, w, _ in _K1_TR)
K1_TM = 512
V_AUG = 80
_K1_VAUG = ("vaT", "vdT")


def _k1_tr_rows(name, width):
    return (width // HEAD_DIM) * V_AUG if name in _K1_VAUG else width


def _rearrange_w_in(w_in):
    def cols(name):
        o, s = _COL_OFF[name]
        return w_in[:, o:o + s]
    d = w_in.shape[0]
    misc = jnp.concatenate([cols("ki"), cols("wi"), cols("ib"), cols("fb"),
                            jnp.zeros((d, LANES - 44), w_in.dtype)], axis=1)
    nat = [cols("ka"), misc, cols("qb"), cols("kb"), cols("vb"), cols("ob"), cols("qc"), cols("vc"),
           cols("kd")]
    tr = [cols("qa"), cols("qi"), cols("va"), cols("kc"), cols("qd"), cols("vd"), cols("wi"),
          jnp.zeros((d, 8 - H_IDX), w_in.dtype)]
    return (jnp.concatenate(nat, axis=1).astype(BF16), jnp.concatenate(tr, axis=1).T.astype(BF16))


def _rms(x, g):
    ms = jnp.mean(x * x, axis=-1, keepdims=True)
    return x * lax.rsqrt(ms + EPS) * g


def _k1_kernel(x_ref, g_ref, wn_ref, wt_ref, *out_refs):
    h = _rms(x_ref[...], g_ref[...]).astype(BF16)
    refs = dict(zip([n for n, _, _ in _K1_NAT] + ["misc_bf"] + [n for n, _, _ in _K1_TR], out_refs))
    off = 0
    for name, width, _ in _K1_NAT:
        r = jnp.dot(h, wn_ref[:, off:off + width], preferred_element_type=F32)
        refs[name][...] = r.astype(refs[name].dtype)
        if name == "misc":
            refs["misc_bf"][...] = r.astype(BF16)
        off += width
    off = 0
    for name, width, _ in _K1_TR:
        r = lax.dot_general(wt_ref[off:off + width, :], h, (((1,), (1,)), ((), ())),
                            preferred_element_type=F32)
        if name in _K1_VAUG:
            tail = (lax.broadcasted_iota(I32, (V_AUG - HEAD_DIM, r.shape[1]), 0) == 0)
            tail = jnp.where(tail, 1.0, 0.0).astype(BF16)
            for hh in range(width // HEAD_DIM):
                refs[name][hh * V_AUG:hh * V_AUG + HEAD_DIM, :] = \
                    r[hh * HEAD_DIM:(hh + 1) * HEAD_DIM, :].astype(BF16)
                refs[name][hh * V_AUG + HEAD_DIM:(hh + 1) * V_AUG, :] = tail
        else:
            refs[name][...] = r.astype(refs[name].dtype)
        off += width


def _k1_call(x2, g, w_nat, w_trT):
    s = x2.shape[0]
    tm = K1_TM
    outs = (tuple(jax.ShapeDtypeStruct((s, w), dt) for _, w, dt in _K1_NAT)
            + (jax.ShapeDtypeStruct((s, LANES), BF16),)
            + tuple(jax.ShapeDtypeStruct((_k1_tr_rows(n, w), s), dt) for n, w, dt in _K1_TR))
    out_specs = (tuple(pl.BlockSpec((tm, w), lambda i: (i, 0)) for _, w, _ in _K1_NAT)
                 + (pl.BlockSpec((tm, LANES), lambda i: (i, 0)),)
                 + tuple(pl.BlockSpec((_k1_tr_rows(n, w), tm), lambda i: (0, i)) for n, w, _ in _K1_TR))
    res = pl.pallas_call(
        _k1_kernel,
        out_shape=outs,
        grid=(s // tm,),
        in_specs=[pl.BlockSpec((tm, D_MODEL), lambda i: (i, 0)),
                  _whole((1, D_MODEL)),
                  _whole((D_MODEL, K1_NAT_WIDTH)),
                  _whole((K1_TR_WIDTH, D_MODEL))],
        out_specs=out_specs,
        compiler_params=_cparams(("parallel",)),
        name="k1_norm_proj",
    )(x2, g.reshape(1, D_MODEL), w_nat, w_trT)
    return dict(zip([n for n, _, _ in _K1_NAT] + ["misc_bf"] + [n for n, _, _ in _K1_TR], res))


A_TQ = 256
A_KT = 512
A_SLAB = 256
A_SWEEP_SLABS = 8
LOG2E = 1.4426950408889634
QK_AHEAD = 4
BF16_ROWS = 16


def _flash_step(s, i, v_t, m_ref, acc_ref):
    kt, tq = s.shape
    parts = s.reshape(kt // BF16_ROWS, BF16_ROWS, tq)
    parts = [parts[g] for g in range(kt // BF16_ROWS)]
    while len(parts) > 1:
        parts = [jnp.maximum(parts[g], parts[g + 1]) for g in range(0, len(parts), 2)]
    m_old = m_ref[i]
    m_new = jnp.maximum(m_old, jnp.max(parts[0].astype(F32), axis=0, keepdims=True))
    alpha = jnp.exp2(m_old - m_new)
    p = jnp.exp2(s - m_new.astype(BF16))
    acc_ref[i] = alpha * acc_ref[i] + jnp.dot(v_t, p, preferred_element_type=F32)
    m_ref[i] = m_new


def _flash_result(acc_ref, i):
    return acc_ref[i, 0:HEAD_DIM, :] / acc_ref[i, HEAD_DIM:HEAD_DIM + 1, :]


def _logits(k_t, q_pad):
    return jnp.dot(k_t, q_pad, preferred_element_type=F32).astype(BF16)


def _qk_prologue(k_t, qpad_ref, n, s_ref):
    for i in range(min(QK_AHEAD, n)):
        s_ref[i] = _logits(k_t, qpad_ref[i])


def _flash_tile(k_t, qpad_ref, n, v_tile, bias, m_ref, acc_ref, s_ref=None, k_next=None):
    a = min(QK_AHEAD, n)
    if s_ref is None:
        pend = [_logits(k_t, qpad_ref[i]) for i in range(a)]
    else:
        pend = [s_ref[i] for i in range(a)]
    for i in range(n):
        s = pend.pop(0)
        if i + a < n:
            pend.append(_logits(k_t, qpad_ref[i + a]))
        elif s_ref is not None:
            s_ref[i + a - n] = _logits(k_next, qpad_ref[i + a - n])
        if bias is not None:
            s = s + bias
        _flash_step(s, i, v_tile(i), m_ref, acc_ref)


def _bit_transpose32(words):
    a = list(words)
    j, m = 16, 0x0000FFFF
    while j:
        k = 0
        while k < 32:
            t = (a[k] ^ lax.shift_right_logical(a[k + j], j)) & m
            a[k] = a[k] ^ t
            a[k + j] = a[k + j] ^ lax.shift_left(t, j)
            k = (k + j + 1) & ~j
        j >>= 1
        m = (m ^ (m << j)) & 0xFFFFFFFF
    return a


def _dsa_kernel(topk, qaT_ref, qiT_ref, wiT_ref, qiTn_ref, wiTn_ref, ka_ref, vaT_ref, mb_ref, tri_ref,
                o_ref, planes_ref, cand_ref, above_ref, qpad_ref, qipad_ref, acc_ref, m_ref, carry_ref):
    tq, kt = A_TQ, A_KT
    b = pl.program_id(0)
    nb = pl.num_programs(0)
    ntiles = ((b + 1) * tq + kt - 1) // kt

    def vis_end_of(blk):
        q_pos = blk * tq + lax.broadcasted_iota(I32, (1, tq), 1)
        return (q_pos // CHUNK + 1) * CHUNK

    def set_qipad(src_ref):
        qiT = src_ref[...]
        for h in range(H_IDX):
            qipad_ref[h, 0:D_IDX, :] = qiT[h * D_IDX:(h + 1) * D_IDX, :]
            qipad_ref[h, D_IDX:LANES, :] = jnp.zeros((LANES - D_IDX, tq), BF16)

    def p1(j, w_ref, vis_end):
        s0 = pl.multiple_of(j * kt, kt)
        mb = mb_ref[pl.ds(s0, kt), :]
        score = jnp.zeros((kt, tq), F32)
        for h in range(H_IDX):
            r = jnp.dot(mb, qipad_ref[h], preferred_element_type=F32)
            score = score + jnp.maximum(r, 0.0) * w_ref[h:h + 1, :]
        bits = lax.bitcast_convert_type(score, I32)
        ukey = bits ^ (lax.shift_right_arithmetic(bits, 31) | INT_MIN)
        if vis_end is not None:
            s_pos = s0 + lax.broadcasted_iota(I32, (kt, 1), 0)
            ukey = jnp.where(s_pos < vis_end, ukey, 0)
        u4 = ukey.reshape(kt // A_SLAB, 32, 8, tq)
        for s2 in range(kt // A_SLAB):
            planes = _bit_transpose32([u4[s2, v] for v in range(32)])
            for r in range(32):
                planes_ref[j * (kt // A_SLAB) + s2, r] = planes[r]

    @pl.when(b == 0)
    def _():
        set_qipad(qiT_ref)
        vis0 = vis_end_of(0)

        def first_block(j, carry):
            p1(j, wiT_ref, vis0)
            return carry
        lax.fori_loop(0, ntiles, first_block, 0)

    kf = float(topk)
    nslab = ntiles * (kt // A_SLAB)
    nstep = (nslab + A_SWEEP_SLABS - 1) // A_SWEEP_SLABS

    def init_sets(sl, carry):
        cand_ref[sl] = jnp.full((8, tq), -1, I32)
        above_ref[sl] = jnp.zeros((8, tq), I32)
        return carry
    lax.fori_loop(0, nslab, init_sets, 0)

    def init_pad(sl, carry):
        cand_ref[sl] = jnp.zeros((8, tq), I32)
        above_ref[sl] = jnp.zeros((8, tq), I32)
        planes_ref[sl] = jnp.zeros((32, 8, tq), I32)
        return carry
    lax.fori_loop(nslab, nstep * A_SWEEP_SLABS, init_pad, 0)

    def apply_decision(sl, prev_plane, took_one):
        cand = cand_ref[sl]
        ones = cand & prev_plane
        cand = jnp.where(took_one, ones, cand ^ ones)
        above_ref[sl] = jnp.where(took_one, above_ref[sl], above_ref[sl] | ones)
        cand_ref[sl] = cand
        return cand

    def sweep(i, carry):
        n_above, took, tau = carry
        took_one = jnp.broadcast_to(took, (8, tq)) != 0
        first = i == 0

        def step(jj, acc):
            for s2 in range(A_SWEEP_SLABS):
                sl = jj * A_SWEEP_SLABS + s2
                prev_plane = jnp.where(first, -1, planes_ref[sl, jnp.maximum(i - 1, 0)])
                cand = apply_decision(sl, prev_plane, took_one)
                acc = acc + lax.population_count(cand & planes_ref[sl, i])
            return acc
        acc = lax.fori_loop(0, nstep, step, jnp.zeros((8, tq), I32))
        n_one = jnp.sum(acc.astype(F32), axis=0, keepdims=True)
        take = (n_above + n_one) >= kf
        n_above = jnp.where(take, n_above, n_above + n_one)
        tau = jnp.where(take, tau | lax.shift_left(jnp.int32(1), 31 - i), tau)
        return n_above, jnp.where(take, 1, 0), tau
    n_above, took, tau = lax.fori_loop(
        0, 32, sweep, (jnp.zeros((1, tq), F32), jnp.ones((1, tq), I32), jnp.zeros((1, tq), I32)))

    def last_decision(sl, carry):
        apply_decision(sl, planes_ref[sl, 31], jnp.broadcast_to(took, (8, tq)) != 0)
        return carry
    lax.fori_loop(0, nslab, last_decision, 0)
    need = jnp.where(tau == 0, 0.0, kf - n_above)

    m_ref[...] = jnp.full(m_ref.shape, M_INIT, F32)
    acc_ref[...] = jnp.zeros(acc_ref.shape, F32)
    carry_ref[...] = jnp.zeros(carry_ref.shape, F32)
    qaT = (qaT_ref[...].astype(F32) * (HEAD_DIM ** -0.5 * LOG2E)).astype(BF16)
    row = lax.broadcasted_iota(I32, qaT.shape, 0)
    for h in range(H_A):
        qpad_ref[h] = jnp.where((row >= h * HEAD_DIM) & (row < (h + 1) * HEAD_DIM), qaT,
                                jnp.zeros_like(qaT))

    def slab_rows(ref, sl):
        word = ref[sl]
        return jnp.concatenate([lax.shift_right_logical(word, 31 - v) & 1 for v in range(32)], axis=0)

    def p3(j):
        s0 = pl.multiple_of(j * kt, kt)
        slabs = [j * (kt // A_SLAB) + s2 for s2 in range(kt // A_SLAB)]
        gt = jnp.concatenate([slab_rows(above_ref, sl) for sl in slabs], axis=0) != 0
        eq_i = jnp.concatenate([slab_rows(cand_ref, sl) for sl in slabs], axis=0)
        eq = eq_i != 0
        eqf = eq_i.astype(F32)
        pref = jnp.dot(tri_ref[...], eqf.astype(BF16), preferred_element_type=F32)
        seen = carry_ref[...]
        sel = gt | (eq & (pref < need - seen))
        bias = jnp.where(sel, 0.0, NEG_INF).astype(BF16)
        carry_ref[...] = seen + pref[kt - 1:kt, :] + eqf[kt - 1:kt, :]
        k_t = ka_ref[pl.ds(s0, kt), :]
        _flash_tile(k_t, qpad_ref, H_A, lambda i: vaT_ref[i * V_AUG:(i + 1) * V_AUG, pl.ds(s0, kt)],
                    bias, m_ref, acc_ref)

    def p3_only(j, carry):
        p3(j)
        return carry

    @pl.when(b + 1 < nb)
    def _():
        set_qipad(qiTn_ref)
        ntiles_next = ((b + 2) * tq + kt - 1) // kt
        nfull_next = ((b + 1) * tq) // kt
        vis_next = vis_end_of(b + 1)

        def both(j, carry):
            p1(j, wiTn_ref, None)
            p3(j)
            return carry
        lax.fori_loop(0, nfull_next, both, 0)
        lax.fori_loop(nfull_next, ntiles, p3_only, 0)

        def next_masked(j, carry):
            p1(j, wiTn_ref, vis_next)
            return carry
        lax.fori_loop(nfull_next, ntiles_next, next_masked, 0)

    @pl.when(b + 1 == nb)
    def _():
        lax.fori_loop(0, ntiles, p3_only, 0)

    ys = [_flash_result(acc_ref, h) for h in range(H_A)]
    o_ref[...] = jnp.concatenate(ys, axis=0).T.astype(o_ref.dtype)


def _dsa_call(qaT, qiT, wiT, ka, vaT, misc_bf):
    s = ka.shape[0]
    topk = min(TOPK_MAX, s // 4)
    tri = jnp.tril(jnp.ones((A_KT, A_KT), F32), k=-1).astype(BF16)
    vm = pl.BlockSpec(memory_space=pltpu.VMEM)
    assert s % (A_SLAB * A_SWEEP_SLABS) == 0 and s % A_KT == 0, "sequence length not supported"
    nb = s // A_TQ
    nxt = lambda i: (0, jnp.minimum(i + 1, nb - 1))
    return pl.pallas_call(
        functools.partial(_dsa_kernel, topk),
        out_shape=jax.ShapeDtypeStruct((s, W_BRANCH), BF16),
        grid=(nb,),
        in_specs=[pl.BlockSpec((256, A_TQ), lambda i: (0, i)),
                  pl.BlockSpec((LANES, A_TQ), lambda i: (0, i)),
                  pl.BlockSpec((8, A_TQ), lambda i: (0, i)),
                  pl.BlockSpec((LANES, A_TQ), nxt),
                  pl.BlockSpec((8, A_TQ), nxt),
                  vm, vm, vm, vm],
        out_specs=pl.BlockSpec((A_TQ, 256), lambda i: (i, 0)),
        scratch_shapes=[pltpu.VMEM((s // A_SLAB, 32, 8, A_TQ), I32),
                        pltpu.VMEM((s // A_SLAB, 8, A_TQ), I32),
                        pltpu.VMEM((s // A_SLAB, 8, A_TQ), I32),
                        pltpu.VMEM((H_A, 256, A_TQ), BF16),
                        pltpu.VMEM((H_IDX, LANES, A_TQ), BF16),
                        pltpu.VMEM((H_A, V_AUG, A_TQ), F32),
                        pltpu.VMEM((H_A, 1, A_TQ), F32),
                        pltpu.VMEM((1, A_TQ), F32)],
        compiler_params=_cparams(("arbitrary",)),
        name="dsa_mixer",
    )(qaT, qiT, wiT, qiT, wiT, ka, vaT, misc_bf, tri)


def _pair_select(lo, hi):
    lane = lax.broadcasted_iota(I32, lo.shape, 1)
    return jnp.where(lane < HEAD_DIM, lo, hi)


def _pair_head_rms(o, g):
    lane = lax.broadcasted_iota(I32, o.shape, 1)
    low = lane < HEAD_DIM
    sq = o * o
    ms_lo = jnp.sum(jnp.where(low, sq, 0.0), axis=1, keepdims=True) * (1.0 / HEAD_DIM)
    ms_hi = jnp.sum(jnp.where(low, 0.0, sq), axis=1, keepdims=True) * (1.0 / HEAD_DIM)
    ms = jnp.where(low, ms_lo, ms_hi)
    return o * lax.rsqrt(ms + EPS) * g


D_TQ = 256
D_KT = 1024


def _diff_kernel(lam_init, qT_ref, kd_ref, vT_ref, lam_ref, g_ref, o_ref, qpad_ref, acc_ref, m_ref, s_ref):
    tq = D_TQ
    b = pl.program_id(0)
    q_pos = b * tq + lax.broadcasted_iota(I32, (1, tq), 1)
    vis_end = (q_pos // CHUNK + 1) * CHUNK
    m_ref[...] = jnp.full(m_ref.shape, M_INIT, F32)
    acc_ref[...] = jnp.zeros(acc_ref.shape, F32)
    qT = (qT_ref[...].astype(F32) * (DQ_D ** -0.5 * LOG2E)).astype(BF16)
    row = lax.broadcasted_iota(I32, qT.shape, 0)
    for i in range(2 * H_D):
        qpad_ref[i] = jnp.where((row >= i * DQ_D) & (row < (i + 1) * DQ_D), qT, jnp.zeros_like(qT))

    def tile(s0, kt, masked, s_ref=None, s0_next=None):
        k_t = kd_ref[pl.ds(s0, kt), :]
        k_next = None if s0_next is None else kd_ref[pl.ds(s0_next, kt), :]
        bias = None
        if masked:
            vis = (s0 + lax.broadcasted_iota(I32, (kt, 1), 0)) < vis_end
            bias = jnp.where(vis, 0.0, NEG_INF).astype(BF16)
        _flash_tile(k_t, qpad_ref, 2 * H_D, lambda i: vT_ref[(i // 2) * V_AUG:(i // 2 + 1) * V_AUG, pl.ds(s0, kt)],
                    bias, m_ref, acc_ref, s_ref, k_next)

    n_big = (b * tq) // D_KT

    @pl.when(n_big > 0)
    def _():
        _qk_prologue(kd_ref[pl.ds(0, D_KT), :], qpad_ref, 2 * H_D, s_ref)

    def full_tile(j, carry):
        j_next = jnp.minimum(j + 1, n_big - 1)
        tile(pl.multiple_of(j * D_KT, D_KT), D_KT, False, s_ref, pl.multiple_of(j_next * D_KT, D_KT))
        return carry
    lax.fori_loop(0, n_big, full_tile, 0)

    def small_tile(j, carry):
        tile(pl.multiple_of(j * tq, tq), tq, False)
        return carry
    lax.fori_loop(n_big * (D_KT // tq), b, small_tile, 0)
    tile(pl.multiple_of(b * tq, tq), tq, True)

    lp = lam_ref[...]
    lam = (jnp.exp(jnp.sum(lp[0:1] * lp[1:2], axis=1, keepdims=True))
           - jnp.exp(jnp.sum(lp[2:3] * lp[3:4], axis=1, keepdims=True)) + lam_init)
    ys = []
    for h in range(H_D):
        o = _flash_result(acc_ref, 2 * h) - lam * _flash_result(acc_ref, 2 * h + 1)
        ms = jnp.mean(o * o, axis=0, keepdims=True)
        ys.append(o * lax.rsqrt(ms + EPS) * g_ref[h * HEAD_DIM:(h + 1) * HEAD_DIM, :] * (1.0 - lam_init))
    o_ref[...] = jnp.concatenate(ys, axis=0).T.astype(o_ref.dtype)


def _diff_call(qdT, kd, vdT, lam_p, dnorm_g, lam_init):
    s = kd.shape[0]
    return pl.pallas_call(
        functools.partial(_diff_kernel, lam_init),
        out_shape=jax.ShapeDtypeStruct((s, W_BRANCH), BF16),
        grid=(s // D_TQ,),
        in_specs=[pl.BlockSpec((256, D_TQ), lambda i: (0, i)),
                  pl.BlockSpec(memory_space=pltpu.VMEM),
                  pl.BlockSpec(memory_space=pltpu.VMEM),
                  _whole((4, DQ_D)),
                  _whole((256, 1))],
        out_specs=pl.BlockSpec((D_TQ, 256), lambda i: (i, 0)),
        scratch_shapes=[pltpu.VMEM((2 * H_D, 256, D_TQ), BF16),
                        pltpu.VMEM((2 * H_D, V_AUG, D_TQ), F32),
                        pltpu.VMEM((2 * H_D, 1, D_TQ), F32),
                        pltpu.VMEM((QK_AHEAD, D_KT, D_TQ), BF16)],
        compiler_params=_cparams(("parallel",)),
        name="diff_attention",
    )(qdT, kd, vdT, lam_p, dnorm_g.reshape(256, 1))


C_TQ = 128
C_PAD = N_PREV_CHUNKS * CHUNK
C_WIN = C_PAD + C_TQ
C_LINE = C_TQ + C_WIN


def _band_bias_line(rel_bias):
    dist_desc = jnp.arange(C_TQ - 1 + C_PAD, C_PAD - C_WIN, -1)
    line = rel_bias[:, jnp.clip(dist_desc, -(CHUNK - 1), MAX_REL_PAST) + (CHUNK - 1)].astype(F32)
    return jnp.pad(line, ((0, 0), (0, C_LINE - line.shape[1])))


def _band_table_init(line_ref, bias_ref):
    tq = C_TQ

    @pl.when(pl.program_id(0) == 0)
    def _():
        i = lax.broadcasted_iota(I32, (tq, C_WIN), 0)
        w = lax.broadcasted_iota(I32, (tq, C_WIN), 1)
        in_band = (w // CHUNK >= i // CHUNK) & (w // CHUNK <= i // CHUNK + N_PREV_CHUNKS)
        for h in range(H_C):
            x = jnp.broadcast_to(line_ref[h:h + 1, :], (tq, C_LINE))
            y = pltpu.roll(x, 1, 1, stride=1, stride_axis=0)
            bias_ref[h] = jnp.where(in_band, y[:, C_LINE - C_WIN:], NEG_INF)


def _band_body(qc_ref, kcT_ref, vc_ref, o_ref, bias_ref):
    tq = C_TQ
    b = pl.program_id(0)
    w0 = pl.multiple_of(b * tq, tq)
    q_all = qc_ref[...] * 0.125
    key_abs = b * tq - C_PAD + lax.broadcasted_iota(I32, (1, C_WIN), 1)
    ok = key_abs >= 0
    logits = [jnp.dot(q_all[:, h * HEAD_DIM:(h + 1) * HEAD_DIM],
                      kcT_ref[h * HEAD_DIM:(h + 1) * HEAD_DIM, pl.ds(w0, C_WIN)],
                      preferred_element_type=F32) for h in range(H_C)]
    heads = []
    for h in range(H_C):
        s = jnp.where(ok, logits[h] + bias_ref[h], NEG_INF)
        m = jnp.max(s, axis=1, keepdims=True)
        p = jnp.exp(s - m)
        l = jnp.sum(p, axis=1, keepdims=True)
        pv = jnp.dot(p.astype(BF16), vc_ref[pl.ds(w0, C_WIN), (h // 2) * LANES:(h // 2 + 1) * LANES],
                     preferred_element_type=F32)
        heads.append(pv / l)
    for pr in range(H_C // 2):
        o_ref[:, pr * LANES:(pr + 1) * LANES] = _pair_select(heads[2 * pr], heads[2 * pr + 1]).astype(o_ref.dtype)


def _band_kernel(qc_ref, kcT_ref, vc_ref, line_ref, o_ref, bias_ref):
    _band_table_init(line_ref, bias_ref)
    _band_body(qc_ref, kcT_ref, vc_ref, o_ref, bias_ref)


def _band_call(qc, kcT_pad, vc_pad, bias_line):
    s = qc.shape[0]
    return pl.pallas_call(
        _band_kernel,
        out_shape=jax.ShapeDtypeStruct((s, W_BRANCH), BF16),
        grid=(s // C_TQ,),
        in_specs=[pl.BlockSpec((C_TQ, 256), lambda i: (i, 0)),
                  pl.BlockSpec(memory_space=pltpu.VMEM),
                  pl.BlockSpec(memory_space=pltpu.VMEM),
                  _whole((H_C, C_LINE))],
        out_specs=pl.BlockSpec((C_TQ, 256), lambda i: (i, 0)),
        scratch_shapes=[pltpu.VMEM((H_C, C_TQ, C_WIN), F32)],
        compiler_params=_cparams(("arbitrary",)),
        name="band_attention",
    )(qc, kcT_pad, vc_pad, bias_line)


B_NCH = 2
B_L = B_NCH * CHUNK
HIGHEST = lax.Precision.HIGHEST


def _mlstm_state_init(tail_ref, ct_ref, n_ref, m_ref):
    @pl.when(pl.program_id(0) == 0)
    def _():
        tail_ref[...] = jnp.zeros(tail_ref.shape, F32)
        ct_ref[...] = jnp.zeros(ct_ref.shape, F32)
        n_ref[...] = jnp.zeros(n_ref.shape, F32)
        m_ref[...] = jnp.zeros(m_ref.shape, F32)


def _mlstm_kernel(qk_ref, vb_ref, ob_ref, misc_ref, cw_ref, cb_ref, gb_ref, ng_ref, o_ref,
                  tail_ref, ct_ref, n_ref, m_ref):
    _mlstm_state_init(tail_ref, ct_ref, n_ref, m_ref)
    _mlstm_body(qk_ref, vb_ref, ob_ref, misc_ref, cw_ref, cb_ref, gb_ref, ng_ref, o_ref,
                tail_ref, ct_ref, n_ref, m_ref)


def _mlstm_body(qk_ref, vb_ref, ob_ref, misc_ref, cw_ref, cb_ref, gb_ref, ng_ref, o_ref,
                tail_ref, ct_ref, n_ref, m_ref):
    L = CHUNK
    T = B_L
    hd = HEAD_DIM

    x = qk_ref[...]
    xx = jnp.concatenate([tail_ref[...], x], axis=0)
    y = jnp.broadcast_to(cb_ref[...], x.shape)
    for j in range(CONV_K):
        y = y + cw_ref[j:j + 1, :] * xx[8 - (CONV_K - 1) + j:8 - (CONV_K - 1) + j + T, :]
    tail_ref[...] = x[T - 8:T, :]
    qk = y * jax.nn.sigmoid(y)
    q_all = qk[:, :W_BRANCH]
    k_all = qk[:, W_BRANCH:] * 0.125
    v_all = vb_ref[...]
    o_gate = jax.nn.sigmoid(ob_ref[...])

    gts = misc_ref[...] + gb_ref[...]
    lf = jnp.minimum(gts, 0.0) - jnp.log1p(jnp.exp(-jnp.abs(gts)))
    r_t = lax.broadcasted_iota(I32, (T, T), 0)
    c_t = lax.broadcasted_iota(I32, (T, T), 1)
    ltri = jnp.where((c_t <= r_t) & (c_t // L == r_t // L), 1.0, 0.0)
    cum = jnp.dot(ltri, lf, precision=HIGHEST, preferred_element_type=F32)
    lane = lax.broadcasted_iota(I32, (T, LANES), 1)
    mixed = jnp.where(lane < MISC_FB, gts, cum)
    sel_r = lax.broadcasted_iota(I32, (8, LANES), 0)
    sel_c = lax.broadcasted_iota(I32, (8, LANES), 1)
    sel = jnp.where(sel_c == sel_r + MISC_IB, 1.0, 0.0)
    rows = lax.dot_general(sel, mixed, (((1,), (1,)), ((), ())), precision=HIGHEST,
                           preferred_element_type=F32)
    causal = lax.broadcasted_iota(I32, (L, L), 1) <= lax.broadcasted_iota(I32, (L, L), 0)

    heads = range(H_B)
    sl = lambda a, t0, h: a[t0:t0 + L, h * hd:(h + 1) * hd]
    pre = []
    for ci in range(B_NCH):
        t0 = ci * L
        per_head = []
        for h in heads:
            q, k, v = sl(q_all, t0, h), sl(k_all, t0, h), sl(v_all, t0, h)
            qb, kb, vbf = q.astype(BF16), k.astype(BF16), v.astype(BF16)
            qkt = lax.dot_general(qb, kb, (((1,), (1,)), ((), ())), preferred_element_type=F32)
            it_r = rows[h:h + 1, t0:t0 + L]
            cum_r = rows[H_B + h:H_B + h + 1, t0:t0 + L]
            it_c = gts[t0:t0 + L, MISC_IB + h:MISC_IB + h + 1]
            cum_c = cum[t0:t0 + L, MISC_FB + h:MISC_FB + h + 1]
            dmat = jnp.where(causal, cum_c - cum_r + it_r, NEG_INF)
            dmax = jnp.max(dmat, axis=1, keepdims=True)
            cum_end = cum_c[L - 1:L, :]
            g = cum_end - cum_c + it_c
            gmax = jnp.max(g, axis=0, keepdims=True)
            per_head.append((q, k, qb, vbf, qkt, dmat, dmax, cum_c, cum_end, g, gmax))
        pre.append(per_head)

    state = [(ct_ref[h], n_ref[h], m_ref[h]) for h in heads]
    out_chunks = []
    for ci in range(B_NCH):
        t0 = ci * L
        qc = [jnp.dot(pre[ci][h][2], state[h][0].astype(BF16), preferred_element_type=F32) for h in heads]
        mid = []
        for h in heads:
            q, k, qb, vbf, qkt, dmat, dmax, cum_c, cum_end, g, gmax = pre[ci][h]
            ct, n_row, m_prev = state[h]
            m_inter = cum_c + m_prev
            m_t = jnp.maximum(m_inter, dmax)
            w = jnp.exp(dmat - m_t) * qkt
            inter = jnp.exp(m_inter - m_t)
            m_new = jnp.maximum(cum_end + m_prev, gmax)
            carry_scale = jnp.exp(cum_end + m_prev - m_new)
            src_k = jnp.exp(g - m_new) * k
            mid.append((w, inter, m_t, m_new, carry_scale, src_k))
        wv = [jnp.dot(mid[h][0].astype(BF16), pre[ci][h][3], preferred_element_type=F32) for h in heads]
        upd = [lax.dot_general(mid[h][5].astype(BF16), pre[ci][h][3], (((0,), (0,)), ((), ())),
                               preferred_element_type=F32) for h in heads]
        outs = []
        for h in heads:
            q = pre[ci][h][0]
            w, inter, m_t, m_new, carry_scale, src_k = mid[h]
            ct, n_row, _ = state[h]
            num = inter * qc[h] + wv[h]
            den = inter * jnp.sum(q * n_row, axis=1, keepdims=True) + jnp.sum(w, axis=1, keepdims=True)
            h_t = num / jnp.maximum(jnp.abs(den), jnp.exp(-m_t))
            state[h] = (carry_scale * ct + upd[h],
                        carry_scale * n_row + jnp.sum(src_k, axis=0, keepdims=True), m_new)
            ms = jnp.mean(h_t * h_t, axis=1, keepdims=True)
            hn = h_t * lax.rsqrt(ms + EPS) * ng_ref[:, h * hd:(h + 1) * hd]
            outs.append(sl(o_gate, t0, h) * hn)
        out_chunks.append(jnp.concatenate(outs, axis=1))
    for h in heads:
        ct_ref[h], n_ref[h], m_ref[h] = state[h]
    o_ref[...] = jnp.concatenate(out_chunks, axis=0).astype(o_ref.dtype)


def _mlstm_call(qkb, vb, ob, misc, conv_w, conv_b, i_bias, f_bias, norm_g):
    s = qkb.shape[0]
    gbias = jnp.zeros((1, LANES), F32)
    gbias = gbias.at[0, MISC_IB:MISC_IB + H_B].set(i_bias).at[0, MISC_FB:MISC_FB + H_B].set(f_bias)
    row = lambda w: pl.BlockSpec((B_L, w), lambda i: (i, 0))
    return pl.pallas_call(
        _mlstm_kernel,
        out_shape=jax.ShapeDtypeStruct((s, W_BRANCH), BF16),
        grid=(s // B_L,),
        in_specs=[row(512), row(256), row(256), row(128),
                  _whole((CONV_K, 512)), _whole((1, 512)), _whole((1, LANES)), _whole((1, 256))],
        out_specs=row(256),
        scratch_shapes=[pltpu.VMEM((8, 512), F32),
                        pltpu.VMEM((H_B, HEAD_DIM, HEAD_DIM), F32),
                        pltpu.VMEM((H_B, 1, HEAD_DIM), F32),
                        pltpu.VMEM((H_B, 1, 1), F32)],
        compiler_params=_cparams(("arbitrary",)),
        name="mlstm_mixer",
    )(qkb, vb, ob, misc, conv_w, conv_b.reshape(1, 512), gbias, norm_g.reshape(1, 256))


N_B_IN, N_C_IN = 8, 4


def _mlstm_band_kernel(*refs):
    b_in = refs[:N_B_IN]
    c_in = refs[N_B_IN:N_B_IN + N_C_IN]
    yb_ref, yc_ref = refs[N_B_IN + N_C_IN:N_B_IN + N_C_IN + 2]
    b_scr = refs[N_B_IN + N_C_IN + 2:N_B_IN + N_C_IN + 6]
    (bias_ref,) = refs[N_B_IN + N_C_IN + 6:]
    qc_ref, kcT_ref, vc_ref, line_ref = c_in
    _mlstm_state_init(*b_scr)
    _band_table_init(line_ref, bias_ref)
    _mlstm_body(*b_in, yb_ref, *b_scr)
    _band_body(qc_ref, kcT_ref, vc_ref, yc_ref, bias_ref)


def _mlstm_band_call(qkb, vb, ob, misc, conv_w, conv_b, i_bias, f_bias, norm_g,
                     qc, kcT_pad, vc_pad, bias_line):
    assert B_L == C_TQ
    s = qkb.shape[0]
    gbias = jnp.zeros((1, LANES), F32)
    gbias = gbias.at[0, MISC_IB:MISC_IB + H_B].set(i_bias).at[0, MISC_FB:MISC_FB + H_B].set(f_bias)
    row = lambda w: pl.BlockSpec((B_L, w), lambda i: (i, 0))
    vm = pl.BlockSpec(memory_space=pltpu.VMEM)
    return pl.pallas_call(
        _mlstm_band_kernel,
        out_shape=(jax.ShapeDtypeStruct((s, W_BRANCH), BF16), jax.ShapeDtypeStruct((s, W_BRANCH), BF16)),
        grid=(s // B_L,),
        in_specs=[row(512), row(256), row(256), row(128),
                  _whole((CONV_K, 512)), _whole((1, 512)), _whole((1, LANES)), _whole((1, 256)),
                  row(256), vm, vm, _whole((H_C, C_LINE))],
        out_specs=(row(256), row(256)),
        scratch_shapes=[pltpu.VMEM((8, 512), F32),
                        pltpu.VMEM((H_B, HEAD_DIM, HEAD_DIM), F32),
                        pltpu.VMEM((H_B, 1, HEAD_DIM), F32),
                        pltpu.VMEM((H_B, 1, 1), F32),
                        pltpu.VMEM((H_C, C_TQ, C_WIN), F32)],
        compiler_params=_cparams(("arbitrary",)),
        name="mlstm_band",
    )(qkb, vb, ob, misc, conv_w, conv_b.reshape(1, 512), gbias, norm_g.reshape(1, 256),
      qc, kcT_pad, vc_pad, bias_line)


M_TM = 256
ROUTE_LOGIT0 = N_GROUPS
RT_E1, RT_E2, RT_G1, RT_G2 = 0, 1, 2, 3


def _lane_argmax(vals, lane):
    v = jnp.max(vals, axis=1, keepdims=True)
    idx = jnp.min(jnp.where(vals == v, lane, float(LANES)), axis=1, keepdims=True)
    return v, idx


def _merge_kernel(x_ref, ya_ref, yb_ref, yc_ref, yd_ref, gm_ref, wg_ref, wb_ref, wo_ref, gf_ref,
                  wr_ref, rb_ref, xo_ref, h2_ref, rt_ref):
    x = x_ref[...]
    h = _rms(x, gm_ref[...]).astype(BF16)
    mixed = jnp.zeros(x.shape, F32)
    for n, y_ref in enumerate((ya_ref, yb_ref, yc_ref, yd_ref)):
        gate = jax.nn.sigmoid(jnp.dot(h, wg_ref[:, n * D_MODEL:(n + 1) * D_MODEL],
                                      preferred_element_type=F32))
        up = jnp.dot(y_ref[...], wb_ref[n], preferred_element_type=F32)
        mixed = mixed + gate * up
    xn = x + jnp.dot(mixed.astype(BF16), wo_ref[...], preferred_element_type=F32)
    xo_ref[...] = xn
    h2 = _rms(xn, gf_ref[...])
    _store_row_tiled(h2_ref, h2)
    logits = jnp.dot(h2.astype(BF16), wr_ref[...], preferred_element_type=F32) + rb_ref[...]

    lane = lax.broadcasted_iota(I32, logits.shape, 1).astype(F32)
    neg = -jnp.inf
    gmask = lane < N_GROUPS
    gmax, g_sel = _lane_argmax(jnp.where(gmask, logits, neg), lane)
    g_gate = 1.0 / jnp.sum(jnp.where(gmask, jnp.exp(logits - gmax), 0.0), axis=1, keepdims=True)
    e_lo = ROUTE_LOGIT0 + EXPERTS_PER_GROUP * g_sel
    el = jnp.where((lane >= e_lo) & (lane < e_lo + EXPERTS_PER_GROUP), logits, neg)
    v1, i1 = _lane_argmax(el, lane)
    v2, i2 = _lane_argmax(jnp.where(lane == i1, neg, el), lane)
    e = jnp.exp(v2 - v1)
    p1 = 1.0 / (1.0 + e)
    p2 = e / (1.0 + e)
    rt = jnp.where(lane == RT_E1, i1 - ROUTE_LOGIT0,
                   jnp.where(lane == RT_E2, i2 - ROUTE_LOGIT0,
                             jnp.where(lane == RT_G1, p1 * g_gate,
                                       jnp.where(lane == RT_G2, p2 * g_gate, 0.0))))
    rt_ref[...] = rt


def _merge_call(x2, ya, yb, yc, yd, g_mix, w_gate, w_branch, w_out, g_ffn, w_route, b_route):
    s = x2.shape[0]
    row = lambda w: pl.BlockSpec((M_TM, w), lambda i: (i, 0))
    vm = pl.BlockSpec(memory_space=pltpu.VMEM)
    return pl.pallas_call(
        _merge_kernel,
        out_shape=(jax.ShapeDtypeStruct((s, D_MODEL), F32),
                   jax.ShapeDtypeStruct((s * ROW_TILES, LANES), F32),
                   jax.ShapeDtypeStruct((s, LANES), F32)),
        grid=(s // M_TM,),
        in_specs=[row(D_MODEL), row(256), row(256), row(256), row(256),
                  vm, vm, vm, vm, vm, vm, vm],
        out_specs=(row(D_MODEL), pl.BlockSpec((M_TM * ROW_TILES, LANES), lambda i: (i, 0)), row(LANES)),
        compiler_params=_cparams(("parallel",)),
        name="merge_route",
    )(x2, ya, yb, yc, yd, g_mix.reshape(1, D_MODEL), w_gate, w_branch, w_out,
      g_ffn.reshape(1, D_MODEL), w_route, b_route)


E_BM = 256
GATHER_UNROLL = 8


ROW_TILES = D_MODEL // LANES


def _store_row_tiled(ref, x):
    m = x.shape[0]
    for c in range(ROW_TILES):
        ref[pl.ds(c, m, stride=ROW_TILES), :] = x[:, c * LANES:(c + 1) * LANES]


def _load_row_tiled(ref, m, row0=0):
    return jnp.concatenate([ref[pl.ds(row0 * ROW_TILES + c, m, stride=ROW_TILES), :]
                            for c in range(ROW_TILES)], axis=1)


def _gather_rows(src_hbm, idx_ref, dst, sem, n, wait, inline=False):
    def one(r, parity):
        src_row = pl.multiple_of(idx_ref[0, 0, r] * ROW_TILES, ROW_TILES)
        dst_row = r * ROW_TILES if isinstance(r, int) else pl.multiple_of(r * ROW_TILES, ROW_TILES)
        cp = pltpu.make_async_copy(src_hbm.at[pl.ds(src_row, ROW_TILES)],
                                   dst.at[pl.ds(dst_row, ROW_TILES)], sem)
        if wait:
            cp.wait()
        else:
            cp.start(priority=parity)

    if inline:
        for r in range(n):
            one(r, r % 2)
    else:
        def group(g, c):
            for u in range(GATHER_UNROLL):
                one(g * GATHER_UNROLL + u, u % 2)
            return c
        lax.fori_loop(0, n // GATHER_UNROLL, group, 0)


E_AHEAD = 2
E_NBUF = E_AHEAD + 1


def _moe_kernel(be_ref, nused_ref, tok_ref, tok1_ref, tok_next_ref, h2_hbm, w1_ref, w3_ref, w2_ref, o_ref,
                xbuf, w1b, w3b, w2b, sem):
    i = pl.program_id(0)
    nused = nused_ref[0]
    used = i < nused
    slot = i % E_NBUF
    slot_next = (i + E_AHEAD) % E_NBUF

    @pl.when((i == 0) & used)
    def _():
        _gather_rows(h2_hbm, tok_ref, xbuf.at[0], sem.at[0], E_BM, wait=False)
        _gather_rows(h2_hbm, tok1_ref, xbuf.at[1], sem.at[1], E_BM, wait=False)

    @pl.when(used)
    def _():
        prev = be_ref[jnp.maximum(i - 1, 0)]

        @pl.when((i == 0) | (be_ref[i] != prev))
        def _():
            w1b[...] = w1_ref[0, 0].astype(BF16)
            w3b[...] = w3_ref[0, 0].astype(BF16)
            w2b[...] = w2_ref[0, 0].astype(BF16)

        _gather_rows(h2_hbm, tok_ref, xbuf.at[slot], sem.at[slot], E_BM, wait=True)
        _gather_rows(h2_hbm, tok_next_ref, xbuf.at[slot_next], sem.at[slot_next], E_BM, wait=False, inline=True)
        xb = _load_row_tiled(xbuf.at[slot], E_BM).astype(BF16)
        a = jnp.dot(xb, w1b[...], preferred_element_type=F32)
        g = jnp.dot(xb, w3b[...], preferred_element_type=F32)
        hid = (a * jax.nn.sigmoid(a) * g).astype(BF16)
        _store_row_tiled(o_ref, jnp.dot(hid, w2b[...], preferred_element_type=F32))

    @pl.when(jnp.logical_not(used))
    def _():
        @pl.when(i < nused + E_AHEAD)
        def _():
            _gather_rows(h2_hbm, tok_ref, xbuf.at[slot], sem.at[slot], E_BM, wait=True)
        o_ref[...] = jnp.zeros(o_ref.shape, F32)


def _moe_call(h2, row_tok, block_e, nused, w1, w3, w2, layer):
    n_rows = row_tok.shape[0]
    n_blocks = n_rows // E_BM
    wspec = lambda shp: pl.BlockSpec((1, 1) + shp, lambda i, be, nu: (layer, be[i], 0, 0))
    grid_spec = pltpu.PrefetchScalarGridSpec(
        num_scalar_prefetch=2,
        grid=(n_blocks,),
        in_specs=[pl.BlockSpec((1, 1, E_BM), lambda i, be, nu: (i, 0, 0), memory_space=pltpu.SMEM),
                  pl.BlockSpec((1, 1, E_BM), lambda i, be, nu: (jnp.minimum(i + 1, n_blocks - 1), 0, 0),
                               memory_space=pltpu.SMEM),
                  pl.BlockSpec((1, 1, E_BM), lambda i, be, nu: (jnp.minimum(i + E_AHEAD, n_blocks - 1), 0, 0),
                               memory_space=pltpu.SMEM),
                  pl.BlockSpec(memory_space=pl.ANY),
                  wspec((D_MODEL, F_EXPERT)), wspec((D_MODEL, F_EXPERT)), wspec((F_EXPERT, D_MODEL))],
        out_specs=pl.BlockSpec((E_BM * ROW_TILES, LANES), lambda i, be, nu: (i, 0)),
        scratch_shapes=[pltpu.VMEM((E_NBUF, E_BM * ROW_TILES, LANES), F32),
                        pltpu.VMEM((D_MODEL, F_EXPERT), BF16),
                        pltpu.VMEM((D_MODEL, F_EXPERT), BF16),
                        pltpu.VMEM((F_EXPERT, D_MODEL), BF16),
                        pltpu.SemaphoreType.DMA((E_NBUF,))],
    )
    tok3 = row_tok.reshape(n_blocks, 1, E_BM)
    return pl.pallas_call(
        _moe_kernel,
        out_shape=jax.ShapeDtypeStruct((n_rows * ROW_TILES, LANES), F32),
        grid_spec=grid_spec,
        compiler_params=_cparams(("arbitrary",)),
        name="moe_experts",
    )(block_e, nused, tok3, tok3, tok3, h2, w1, w3, w2)


CB_TM = 256


def _combine_kernel(final, pos_ref, pos_next_ref, x_ref, rt_ref, ys_hbm, gfin_ref, o_ref, buf, sem):
    i = pl.program_id(0)
    slot = i % 2
    n = 2 * CB_TM

    @pl.when(i == 0)
    def _():
        _gather_rows(ys_hbm, pos_ref, buf.at[0], sem.at[0], n, wait=False)

    @pl.when(i + 1 < pl.num_programs(0))
    def _():
        _gather_rows(ys_hbm, pos_next_ref, buf.at[1 - slot], sem.at[1 - slot], n, wait=False)
    _gather_rows(ys_hbm, pos_ref, buf.at[slot], sem.at[slot], n, wait=True)

    rt = rt_ref[...]
    y = (_load_row_tiled(buf.at[slot], CB_TM) * rt[:, RT_G1:RT_G1 + 1]
         + _load_row_tiled(buf.at[slot], CB_TM, CB_TM) * rt[:, RT_G2:RT_G2 + 1])
    xn = x_ref[...] + y
    if final:
        xn = _rms(xn, gfin_ref[...])
    o_ref[...] = xn


def _combine_call(x2, rt, ys, pos, g_final, final):
    s = x2.shape[0]
    nb = s // CB_TM
    pos_b = pos.reshape(nb, CB_TM, 2).transpose(0, 2, 1).reshape(nb, 1, 2 * CB_TM)
    return pl.pallas_call(
        functools.partial(_combine_kernel, final),
        out_shape=jax.ShapeDtypeStruct((s, D_MODEL), F32),
        grid=(nb,),
        in_specs=[pl.BlockSpec((1, 1, 2 * CB_TM), lambda i: (i, 0, 0), memory_space=pltpu.SMEM),
                  pl.BlockSpec((1, 1, 2 * CB_TM), lambda i: (jnp.minimum(i + 1, nb - 1), 0, 0),
                               memory_space=pltpu.SMEM),
                  pl.BlockSpec((CB_TM, D_MODEL), lambda i: (i, 0)),
                  pl.BlockSpec((CB_TM, LANES), lambda i: (i, 0)),
                  pl.BlockSpec(memory_space=pl.ANY),
                  _whole((1, D_MODEL))],
        out_specs=pl.BlockSpec((CB_TM, D_MODEL), lambda i: (i, 0)),
        scratch_shapes=[pltpu.VMEM((2, 2 * CB_TM * ROW_TILES, LANES), F32), pltpu.SemaphoreType.DMA((2,))],
        compiler_params=_cparams(("arbitrary",)),
        name="moe_combine",
    )(pos_b, pos_b, x2, rt, ys, g_final.reshape(1, D_MODEL))


PLAN_BLOCK = 256


def _dispatch_plan(rt):
    t = rt.shape[0]
    n_assign = t * TOP_K_INNER
    eid = rt[:, RT_E1:RT_E2 + 1].astype(I32).reshape(n_assign)
    onehot = (eid[:, None] == jnp.arange(N_EXPERTS, dtype=I32)[None, :]).astype(F32)
    pb = PLAN_BLOCK
    oh3 = onehot.reshape(n_assign // pb, pb, N_EXPERTS)
    tri = jnp.tril(jnp.ones((pb, pb), F32), k=-1)
    within = jnp.einsum('ij,bjk->bik', tri, oh3)
    block_tot = jnp.sum(oh3, axis=1)
    block_off = jnp.cumsum(block_tot, axis=0) - block_tot
    counts = (block_off[-1] + block_tot[-1]).astype(I32)
    rank = jnp.sum(oh3 * (within + block_off[:, None, :]), axis=-1).reshape(n_assign)
    padded = (counts + E_BM - 1) // E_BM * E_BM
    ends_pad = jnp.cumsum(padded)
    starts_pad = ends_pad - padded
    dest = (jnp.sum(onehot * starts_pad.astype(F32)[None, :], axis=-1) + rank).astype(I32)
    n_rows = (-(-(n_assign + N_EXPERTS * (E_BM - 1)) // E_BM) + E_AHEAD) * E_BM
    n_blocks = n_rows // E_BM
    tok = jnp.repeat(jnp.arange(t, dtype=I32), TOP_K_INNER)
    row_tok = jnp.zeros((n_rows,), I32).at[dest].set(tok)
    block_start = jnp.arange(n_blocks, dtype=I32) * E_BM
    block_e = jnp.minimum(jnp.sum((ends_pad[None, :] <= block_start[:, None]).astype(I32), axis=1),
                          N_EXPERTS - 1)
    nused = (ends_pad[-1] // E_BM).astype(I32).reshape(1)
    return row_tok, block_e, nused, dest.reshape(t, TOP_K_INNER)


def _layer(x2, layer_idx, p, experts, final_g, final):
    (norm_mix_g, w_in, conv_w, conv_b, i_bias, f_bias, mnorm_g, rel_bias, lam_p, dnorm_g,
     w_branch, w_gate, w_out, norm_ffn_g, rgw, rgb, rew, reb) = p
    o = _k1_call(x2, norm_mix_g, *_rearrange_w_in(w_in))

    ya = _dsa_call(o["qaT"], o["qiT"], o["wiT"], o["ka"], o["vaT"], o["misc_bf"])
    kcT_pad = jnp.pad(o["kcT"], ((0, 0), (C_PAD, 0)))
    vc_pad = jnp.pad(o["vc"], ((C_PAD, 0), (0, 0)))
    yb, yc = _mlstm_band_call(o["qkb"], o["vb"], o["ob"], o["misc"], conv_w, conv_b, i_bias, f_bias, mnorm_g,
                              o["qc"], kcT_pad, vc_pad, _band_bias_line(rel_bias))
    lam_init = 0.8 - 0.6 * math.exp(-0.3 * layer_idx)
    yd = _diff_call(o["qdT"], o["kd"], o["vdT"], lam_p, dnorm_g, lam_init)

    w_route = jnp.concatenate([rgw, rew, jnp.zeros((D_MODEL, LANES - N_GROUPS - N_EXPERTS), F32)],
                              axis=1).astype(BF16)
    b_route = jnp.concatenate([rgb, reb, jnp.zeros((LANES - N_GROUPS - N_EXPERTS,), F32)]).reshape(1, LANES)
    xn, h2, rt = _merge_call(x2, ya, yb, yc, yd, norm_mix_g, w_gate.astype(BF16), w_branch.astype(BF16),
                             w_out.astype(BF16), norm_ffn_g, w_route, b_route)
    row_tok, block_e, nused, pos = _dispatch_plan(rt)
    ys = _moe_call(h2, row_tok, block_e, nused, *experts, layer_idx)
    return _combine_call(xn, rt, ys, pos, final_g, final)


def kernel(x, norm_mix_g, w_in, conv_w, conv_b, mlstm_i_bias, mlstm_f_bias, mlstm_norm_g, relpos_bias, diff_lambda, diff_norm_g, w_branch, w_gate, w_out, norm_ffn_g, router_group_w, router_group_b, router_expert_w, router_expert_b, expert_w1, expert_w3, expert_w2, final_norm_g):
    assert x.shape[0] == 1 and x.shape[2] == D_MODEL
    params = (norm_mix_g, w_in, conv_w, conv_b, mlstm_i_bias, mlstm_f_bias, mlstm_norm_g, relpos_bias,
              diff_lambda, diff_norm_g, w_branch, w_gate, w_out, norm_ffn_g, router_group_w,
              router_group_b, router_expert_w, router_expert_b)
    experts = (expert_w1, expert_w3, expert_w2)
    depth = norm_mix_g.shape[0]
    x2 = x[0]
    for l in range(depth):
        x2 = _layer(x2, l, tuple(a[l] for a in params), experts, final_norm_g, l == depth - 1)
    return x2[None]
```

```python
import functools
import math

import jax
import jax.numpy as jnp
from jax import lax
from jax.experimental import pallas as pl
from jax.experimental.pallas import tpu as pltpu

F32 = jnp.float32
BF16 = jnp.bfloat16
I32 = jnp.int32

D_MODEL = 1024
CHUNK = 64
HEAD_DIM = 64
NEG_INF = -1e30
H_A = 4
H_IDX = 4
D_IDX = 32
TOPK_MAX = 256
H_B = 4
CONV_K = 4
H_C = 4
N_PREV_CHUNKS = 8
MAX_REL_PAST = 128
H_D = 4
DQ_D = 32
W_BRANCH = 256
N_BRANCH = 4
N_GROUPS = 4
EXPERTS_PER_GROUP = 8
N_EXPERTS = 32
TOP_K_INNER = 2
F_EXPERT = 512
EPS = 1e-6

VMEM_LIMIT_BYTES = 52 * 1024 * 1024
LANES = 128

INT_MIN = -(2 ** 31)
I16_MIN = -(2 ** 15)
I16 = jnp.int16
M_INIT = -5e29

_COL_SIZES = (256, 256, 256, 128, 32, 4, 256, 256, 256, 4, 4, 256, 256, 256, 256, 256, 256, 256)
_COL_NAMES = ("qa", "ka", "va", "qi", "ki", "wi", "qb", "kb", "vb", "ib", "fb", "ob",
              "qc", "kc", "vc", "qd", "kd", "vd")
_COL_OFF = {}
_o = 0
for _n, _s in zip(_COL_NAMES, _COL_SIZES):
    _COL_OFF[_n] = (_o, _s)
    _o += _s
C_IN = _o
MISC_KI = 0
MISC_WI = 32
MISC_IB = 36
MISC_FB = 40


def _cparams(sem):
    return pltpu.CompilerParams(dimension_semantics=sem, vmem_limit_bytes=VMEM_LIMIT_BYTES)


def _whole(shape):
    nd = len(shape)
    return pl.BlockSpec(shape, lambda *_: (0,) * nd)


_K1_NAT = (("ka", 256, BF16), ("misc", 128, F32), ("qkb", 512, F32), ("vb", 256, F32), ("ob", 256, F32),
           ("qc", 256, BF16), ("vc", 256, BF16), ("kd", 256, BF16))
_K1_TR = (("qaT", 256, BF16), ("qiT", 128, BF16), ("vaT", 256, BF16), ("kcT", 256, BF16),
          ("qdT", 256, BF16), ("vdT", 256, BF16), ("wiT", 8, F32))
K1_NAT_WIDTH = sum(w for _, w, _ in _K1_NAT)
K1_TR_WIDTH = sum(w for _, w, _ in _K1_TR)
K1_TM = 512
V_AUG = 80
_K1_VAUG = ("vaT", "vdT")


def _k1_tr_rows(name, width):
    return (width // HEAD_DIM) * V_AUG if name in _K1_VAUG else width


def _rearrange_w_in(w_in):
    def cols(name):
        o, s = _COL_OFF[name]
        return w_in[:, o:o + s]
    d = w_in.shape[0]
    misc = jnp.concatenate([cols("ki"), cols("wi"), cols("ib"), cols("fb"),
                            jnp.zeros((d, LANES - 44), w_in.dtype)], axis=1)
    nat = [cols("ka"), misc, cols("qb"), cols("kb"), cols("vb"), cols("ob"), cols("qc"), cols("vc"),
           cols("kd")]
    tr = [cols("qa"), cols("qi"), cols("va"), cols("kc"), cols("qd"), cols("vd"), cols("wi"),
          jnp.zeros((d, 8 - H_IDX), w_in.dtype)]
    return (jnp.concatenate(nat, axis=1).astype(BF16), jnp.concatenate(tr, axis=1).T.astype(BF16))


def _rms(x, g):
    ms = jnp.mean(x * x, axis=-1, keepdims=True)
    return x * lax.rsqrt(ms + EPS) * g


def _k1_kernel(x_ref, g_ref, wn_ref, wt_ref, *out_refs):
    h = _rms(x_ref[...], g_ref[...]).astype(BF16)
    refs = dict(zip([n for n, _, _ in _K1_NAT] + ["misc_bf"] + [n for n, _, _ in _K1_TR], out_refs))
    off = 0
    for name, width, _ in _K1_NAT:
        r = jnp.dot(h, wn_ref[:, off:off + width], preferred_element_type=F32)
        refs[name][...] = r.astype(refs[name].dtype)
        if name == "misc":
            refs["misc_bf"][...] = r.astype(BF16)
        off += width
    off = 0
    for name, width, _ in _K1_TR:
        r = lax.dot_general(wt_ref[off:off + width, :], h, (((1,), (1,)), ((), ())),
                            preferred_element_type=F32)
        if name in _K1_VAUG:
            tail = (lax.broadcasted_iota(I32, (V_AUG - HEAD_DIM, r.shape[1]), 0) == 0)
            tail = jnp.where(tail, 1.0, 0.0).astype(BF16)
            for hh in range(width // HEAD_DIM):
                refs[name][hh * V_AUG:hh * V_AUG + HEAD_DIM, :] = \
                    r[hh * HEAD_DIM:(hh + 1) * HEAD_DIM, :].astype(BF16)
                refs[name][hh * V_AUG + HEAD_DIM:(hh + 1) * V_AUG, :] = tail
        else:
            refs[name][...] = r.astype(refs[name].dtype)
        off += width


def _k1_call(x2, g, w_nat, w_trT):
    s = x2.shape[0]
    tm = K1_TM
    outs = (tuple(jax.ShapeDtypeStruct((s, w), dt) for _, w, dt in _K1_NAT)
            + (jax.ShapeDtypeStruct((s, LANES), BF16),)
            + tuple(jax.ShapeDtypeStruct((_k1_tr_rows(n, w), s), dt) for n, w, dt in _K1_TR))
    out_specs = (tuple(pl.BlockSpec((tm, w), lambda i: (i, 0)) for _, w, _ in _K1_NAT)
                 + (pl.BlockSpec((tm, LANES), lambda i: (i, 0)),)
                 + tuple(pl.BlockSpec((_k1_tr_rows(n, w), tm), lambda i: (0, i)) for n, w, _ in _K1_TR))
    res = pl.pallas_call(
        _k1_kernel,
        out_shape=outs,
        grid=(s // tm,),
        in_specs=[pl.BlockSpec((tm, D_MODEL), lambda i: (i, 0)),
                  _whole((1, D_MODEL)),
                  _whole((D_MODEL, K1_NAT_WIDTH)),
                  _whole((K1_TR_WIDTH, D_MODEL))],
        out_specs=out_specs,
        compiler_params=_cparams(("parallel",)),
        name="k1_norm_proj",
    )(x2, g.reshape(1, D_MODEL), w_nat, w_trT)
    return dict(zip([n for n, _, _ in _K1_NAT] + ["misc_bf"] + [n for n, _, _ in _K1_TR], res))


A_TQ = 256
A_KT = 512
A_SLAB = 256
A_SWEEP_SLABS = 8
LOG2E = 1.4426950408889634
QK_AHEAD = 4
BF16_ROWS = 16


def _flash_step(s, i, v_t, m_ref, acc_ref):
    kt, tq = s.shape
    m_old = m_ref[i]
    m_halves, p_halves = [], []
    for c in range(tq // LANES):
        sc = s[:, c * LANES:(c + 1) * LANES]
        parts = sc.reshape(kt // BF16_ROWS, BF16_ROWS, LANES)
        parts = [parts[g] for g in range(kt // BF16_ROWS)]
        while len(parts) > 1:
            parts = [jnp.maximum(parts[g], parts[g + 1]) for g in range(0, len(parts), 2)]
        m_c = jnp.maximum(m_old[:, c * LANES:(c + 1) * LANES],
                          jnp.max(parts[0].astype(F32), axis=0, keepdims=True))
        m_halves.append(m_c)
        p_halves.append(jnp.exp2(sc - m_c.astype(BF16)))
    m_new = jnp.concatenate(m_halves, axis=1)
    alpha = jnp.exp2(m_old - m_new)
    p = jnp.concatenate(p_halves, axis=1)
    acc_ref[i] = alpha * acc_ref[i] + jnp.dot(v_t, p, preferred_element_type=F32)
    m_ref[i] = m_new


def _flash_result(acc_ref, i):
    return acc_ref[i, 0:HEAD_DIM, :] / acc_ref[i, HEAD_DIM:HEAD_DIM + 1, :]


def _logits(k_t, q_pad):
    return jnp.dot(k_t, q_pad, preferred_element_type=F32).astype(BF16)


def _qk_prologue(k_t, qpad_ref, n, s_ref):
    for i in range(min(QK_AHEAD, n)):
        s_ref[i] = _logits(k_t, qpad_ref[i])


def _flash_tile(k_t, qpad_ref, n, v_tile, bias, m_ref, acc_ref, s_ref=None, k_next=None):
    a = min(QK_AHEAD, n)
    if s_ref is None:
        pend = [_logits(k_t, qpad_ref[i]) for i in range(a)]
    else:
        pend = [s_ref[i] for i in range(a)]
    for i in range(n):
        s = pend.pop(0)
        if i + a < n:
            pend.append(_logits(k_t, qpad_ref[i + a]))
        elif s_ref is not None:
            s_ref[i + a - n] = _logits(k_next, qpad_ref[i + a - n])
        if bias is not None:
            s = s + bias
        _flash_step(s, i, v_tile(i), m_ref, acc_ref)


def _bit_transpose32(words):
    a = list(words)
    j, m = 16, 0x0000FFFF
    while j:
        k = 0
        while k < 32:
            t = (a[k] ^ lax.shift_right_logical(a[k + j], j)) & m
            a[k] = a[k] ^ t
            a[k + j] = a[k + j] ^ lax.shift_left(t, j)
            k = (k + j + 1) & ~j
        j >>= 1
        m = (m ^ (m << j)) & 0xFFFFFFFF
    return a


def _dsa_kernel(topk, qaT_ref, qiT_ref, wiT_ref, qiTn_ref, wiTn_ref, ka_ref, vaT_ref, mb_ref, tri_ref,
                o_ref, planes_ref, cand_ref, above_ref, qpad_ref, qipad_ref, acc_ref, m_ref, carry_ref):
    tq, kt = A_TQ, A_KT
    b = pl.program_id(0)
    nb = pl.num_programs(0)
    ntiles = ((b + 1) * tq + kt - 1) // kt

    def vis_end_of(blk):
        q_pos = blk * tq + lax.broadcasted_iota(I32, (1, tq), 1)
        return (q_pos // CHUNK + 1) * CHUNK

    def set_qipad(src_ref):
        qiT = src_ref[...]
        for h in range(H_IDX):
            qipad_ref[h, 0:D_IDX, :] = qiT[h * D_IDX:(h + 1) * D_IDX, :]
            qipad_ref[h, D_IDX:LANES, :] = jnp.zeros((LANES - D_IDX, tq), BF16)

    def p1(j, w_ref, vis_end):
        s0 = pl.multiple_of(j * kt, kt)
        mb = mb_ref[pl.ds(s0, kt), :]
        score = jnp.zeros((kt, tq), F32)
        for h in range(H_IDX):
            r = jnp.dot(mb, qipad_ref[h], preferred_element_type=F32)
            score = score + jnp.maximum(r, 0.0) * w_ref[h:h + 1, :]
        bits = lax.bitcast_convert_type(score, I32)
        ukey = bits ^ (lax.shift_right_arithmetic(bits, 31) | INT_MIN)
        if vis_end is not None:
            s_pos = s0 + lax.broadcasted_iota(I32, (kt, 1), 0)
            ukey = jnp.where(s_pos < vis_end, ukey, 0)
        u4 = ukey.reshape(kt // A_SLAB, 32, 8, tq)
        for s2 in range(kt // A_SLAB):
            planes = _bit_transpose32([u4[s2, v] for v in range(32)])
            for r in range(32):
                planes_ref[j * (kt // A_SLAB) + s2, r] = planes[r]

    @pl.when(b == 0)
    def _():
        set_qipad(qiT_ref)
        vis0 = vis_end_of(0)

        def first_block(j, carry):
            p1(j, wiT_ref, vis0)
            return carry
        lax.fori_loop(0, ntiles, first_block, 0)

    kf = float(topk)
    nslab = ntiles * (kt // A_SLAB)
    nstep = (nslab + A_SWEEP_SLABS - 1) // A_SWEEP_SLABS

    def init_sets(sl, carry):
        cand_ref[sl] = jnp.full((8, tq), -1, I32)
        above_ref[sl] = jnp.zeros((8, tq), I32)
        return carry
    lax.fori_loop(0, nslab, init_sets, 0)

    def init_pad(sl, carry):
        cand_ref[sl] = jnp.zeros((8, tq), I32)
        above_ref[sl] = jnp.zeros((8, tq), I32)
        planes_ref[sl] = jnp.zeros((32, 8, tq), I32)
        return carry
    lax.fori_loop(nslab, nstep * A_SWEEP_SLABS, init_pad, 0)

    def apply_decision(sl, prev_plane, took_one):
        cand = cand_ref[sl]
        ones = cand & prev_plane
        cand = jnp.where(took_one, ones, cand ^ ones)
        above_ref[sl] = jnp.where(took_one, above_ref[sl], above_ref[sl] | ones)
        cand_ref[sl] = cand
        return cand

    def sweep(i, carry):
        n_above, took, tau = carry
        took_one = jnp.broadcast_to(took, (8, tq)) != 0
        first = i == 0

        def step(jj, acc):
            for s2 in range(A_SWEEP_SLABS):
                sl = jj * A_SWEEP_SLABS + s2
                prev_plane = jnp.where(first, -1, planes_ref[sl, jnp.maximum(i - 1, 0)])
                cand = apply_decision(sl, prev_plane, took_one)
                acc = acc + lax.population_count(cand & planes_ref[sl, i])
            return acc
        acc = lax.fori_loop(0, nstep, step, jnp.zeros((8, tq), I32))
        n_one = jnp.sum(acc.astype(F32), axis=0, keepdims=True)
        take = (n_above + n_one) >= kf
        n_above = jnp.where(take, n_above, n_above + n_one)
        tau = jnp.where(take, tau | lax.shift_left(jnp.int32(1), 31 - i), tau)
        return n_above, jnp.where(take, 1, 0), tau
    n_above, took, tau = lax.fori_loop(
        0, 32, sweep, (jnp.zeros((1, tq), F32), jnp.ones((1, tq), I32), jnp.zeros((1, tq), I32)))

    def last_decision(sl, carry):
        apply_decision(sl, planes_ref[sl, 31], jnp.broadcast_to(took, (8, tq)) != 0)
        return carry
    lax.fori_loop(0, nslab, last_decision, 0)
    need = jnp.where(tau == 0, 0.0, kf - n_above)

    m_ref[...] = jnp.full(m_ref.shape, M_INIT, F32)
    acc_ref[...] = jnp.zeros(acc_ref.shape, F32)
    carry_ref[...] = jnp.zeros(carry_ref.shape, F32)
    qaT = (qaT_ref[...].astype(F32) * (HEAD_DIM ** -0.5 * LOG2E)).astype(BF16)
    row = lax.broadcasted_iota(I32, qaT.shape, 0)
    for h in range(H_A):
        qpad_ref[h] = jnp.where((row >= h * HEAD_DIM) & (row < (h + 1) * HEAD_DIM), qaT,
                                jnp.zeros_like(qaT))

    def slab_rows(ref, sl):
        word = ref[sl]
        return jnp.concatenate([lax.shift_right_logical(word, 31 - v) & 1 for v in range(32)], axis=0)

    def p3(j):
        s0 = pl.multiple_of(j * kt, kt)
        slabs = [j * (kt // A_SLAB) + s2 for s2 in range(kt // A_SLAB)]
        gt = jnp.concatenate([slab_rows(above_ref, sl) for sl in slabs], axis=0) != 0
        eq_i = jnp.concatenate([slab_rows(cand_ref, sl) for sl in slabs], axis=0)
        eq = eq_i != 0
        eqf = eq_i.astype(F32)
        pref = jnp.dot(tri_ref[...], eqf.astype(BF16), preferred_element_type=F32)
        seen = carry_ref[...]
        sel = gt | (eq & (pref < need - seen))
        bias = jnp.where(sel, 0.0, NEG_INF).astype(BF16)
        carry_ref[...] = seen + pref[kt - 1:kt, :] + eqf[kt - 1:kt, :]
        k_t = ka_ref[pl.ds(s0, kt), :]
        _flash_tile(k_t, qpad_ref, H_A, lambda i: vaT_ref[i * V_AUG:(i + 1) * V_AUG, pl.ds(s0, kt)],
                    bias, m_ref, acc_ref)

    def p3_only(j, carry):
        p3(j)
        return carry

    @pl.when(b + 1 < nb)
    def _():
        set_qipad(qiTn_ref)
        ntiles_next = ((b + 2) * tq + kt - 1) // kt
        nfull_next = ((b + 1) * tq) // kt
        vis_next = vis_end_of(b + 1)

        def both(j, carry):
            p1(j, wiTn_ref, None)
            p3(j)
            return carry
        lax.fori_loop(0, nfull_next, both, 0)
        lax.fori_loop(nfull_next, ntiles, p3_only, 0)

        def next_masked(j, carry):
            p1(j, wiTn_ref, vis_next)
            return carry
        lax.fori_loop(nfull_next, ntiles_next, next_masked, 0)

    @pl.when(b + 1 == nb)
    def _():
        lax.fori_loop(0, ntiles, p3_only, 0)

    ys = [_flash_result(acc_ref, h) for h in range(H_A)]
    o_ref[...] = jnp.concatenate(ys, axis=0).T.astype(o_ref.dtype)


def _dsa_call(qaT, qiT, wiT, ka, vaT, misc_bf):
    s = ka.shape[0]
    topk = min(TOPK_MAX, s // 4)
    tri = jnp.tril(jnp.ones((A_KT, A_KT), F32), k=-1).astype(BF16)
    vm = pl.BlockSpec(memory_space=pltpu.VMEM)
    assert s % (A_SLAB * A_SWEEP_SLABS) == 0 and s % A_KT == 0, "sequence length not supported"
    nb = s // A_TQ
    nxt = lambda i: (0, jnp.minimum(i + 1, nb - 1))
    return pl.pallas_call(
        functools.partial(_dsa_kernel, topk),
        out_shape=jax.ShapeDtypeStruct((s, W_BRANCH), BF16),
        grid=(nb,),
        in_specs=[pl.BlockSpec((256, A_TQ), lambda i: (0, i)),
                  pl.BlockSpec((LANES, A_TQ), lambda i: (0, i)),
                  pl.BlockSpec((8, A_TQ), lambda i: (0, i)),
                  pl.BlockSpec((LANES, A_TQ), nxt),
                  pl.BlockSpec((8, A_TQ), nxt),
                  vm, vm, vm, vm],
        out_specs=pl.BlockSpec((A_TQ, 256), lambda i: (i, 0)),
        scratch_shapes=[pltpu.VMEM((s // A_SLAB, 32, 8, A_TQ), I32),
                        pltpu.VMEM((s // A_SLAB, 8, A_TQ), I32),
                        pltpu.VMEM((s // A_SLAB, 8, A_TQ), I32),
                        pltpu.VMEM((H_A, 256, A_TQ), BF16),
                        pltpu.VMEM((H_IDX, LANES, A_TQ), BF16),
                        pltpu.VMEM((H_A, V_AUG, A_TQ), F32),
                        pltpu.VMEM((H_A, 1, A_TQ), F32),
                        pltpu.VMEM((1, A_TQ), F32)],
        compiler_params=_cparams(("arbitrary",)),
        name="dsa_mixer",
    )(qaT, qiT, wiT, qiT, wiT, ka, vaT, misc_bf, tri)


def _pair_select(lo, hi):
    lane = lax.broadcasted_iota(I32, lo.shape, 1)
    return jnp.where(lane < HEAD_DIM, lo, hi)


def _pair_head_rms(o, g):
    lane = lax.broadcasted_iota(I32, o.shape, 1)
    low = lane < HEAD_DIM
    sq = o * o
    ms_lo = jnp.sum(jnp.where(low, sq, 0.0), axis=1, keepdims=True) * (1.0 / HEAD_DIM)
    ms_hi = jnp.sum(jnp.where(low, 0.0, sq), axis=1, keepdims=True) * (1.0 / HEAD_DIM)
    ms = jnp.where(low, ms_lo, ms_hi)
    return o * lax.rsqrt(ms + EPS) * g


D_TQ = 256
D_KT = 1024


def _diff_kernel(lam_init, qT_ref, kd_ref, vT_ref, lam_ref, g_ref, o_ref, qpad_ref, acc_ref, m_ref, s_ref):
    tq = D_TQ
    b = pl.program_id(0)
    q_pos = b * tq + lax.broadcasted_iota(I32, (1, tq), 1)
    vis_end = (q_pos // CHUNK + 1) * CHUNK
    m_ref[...] = jnp.full(m_ref.shape, M_INIT, F32)
    acc_ref[...] = jnp.zeros(acc_ref.shape, F32)
    qT = (qT_ref[...].astype(F32) * (DQ_D ** -0.5 * LOG2E)).astype(BF16)
    row = lax.broadcasted_iota(I32, qT.shape, 0)
    for i in range(2 * H_D):
        qpad_ref[i] = jnp.where((row >= i * DQ_D) & (row < (i + 1) * DQ_D), qT, jnp.zeros_like(qT))

    def tile(s0, kt, masked, s_ref=None, s0_next=None):
        k_t = kd_ref[pl.ds(s0, kt), :]
        k_next = None if s0_next is None else kd_ref[pl.ds(s0_next, kt), :]
        bias = None
        if masked:
            vis = (s0 + lax.broadcasted_iota(I32, (kt, 1), 0)) < vis_end
            bias = jnp.where(vis, 0.0, NEG_INF).astype(BF16)
        _flash_tile(k_t, qpad_ref, 2 * H_D, lambda i: vT_ref[(i // 2) * V_AUG:(i // 2 + 1) * V_AUG, pl.ds(s0, kt)],
                    bias, m_ref, acc_ref, s_ref, k_next)

    n_big = (b * tq) // D_KT

    @pl.when(n_big > 0)
    def _():
        _qk_prologue(kd_ref[pl.ds(0, D_KT), :], qpad_ref, 2 * H_D, s_ref)

    def full_tile(j, carry):
        j_next = jnp.minimum(j + 1, n_big - 1)
        tile(pl.multiple_of(j * D_KT, D_KT), D_KT, False, s_ref, pl.multiple_of(j_next * D_KT, D_KT))
        return carry
    lax.fori_loop(0, n_big, full_tile, 0)

    def small_tile(j, carry):
        tile(pl.multiple_of(j * tq, tq), tq, False)
        return carry
    lax.fori_loop(n_big * (D_KT // tq), b, small_tile, 0)
    tile(pl.multiple_of(b * tq, tq), tq, True)

    lp = lam_ref[...]
    lam = (jnp.exp(jnp.sum(lp[0:1] * lp[1:2], axis=1, keepdims=True))
           - jnp.exp(jnp.sum(lp[2:3] * lp[3:4], axis=1, keepdims=True)) + lam_init)
    ys = []
    for h in range(H_D):
        o = _flash_result(acc_ref, 2 * h) - lam * _flash_result(acc_ref, 2 * h + 1)
        ms = jnp.mean(o * o, axis=0, keepdims=True)
        ys.append(o * lax.rsqrt(ms + EPS) * g_ref[h * HEAD_DIM:(h + 1) * HEAD_DIM, :] * (1.0 - lam_init))
    o_ref[...] = jnp.concatenate(ys, axis=0).T.astype(o_ref.dtype)


def _diff_call(qdT, kd, vdT, lam_p, dnorm_g, lam_init):
    s = kd.shape[0]
    return pl.pallas_call(
        functools.partial(_diff_kernel, lam_init),
        out_shape=jax.ShapeDtypeStruct((s, W_BRANCH), BF16),
        grid=(s // D_TQ,),
        in_specs=[pl.BlockSpec((256, D_TQ), lambda i: (0, i)),
                  pl.BlockSpec(memory_space=pltpu.VMEM),
                  pl.BlockSpec(memory_space=pltpu.VMEM),
                  _whole((4, DQ_D)),
                  _whole((256, 1))],
        out_specs=pl.BlockSpec((D_TQ, 256), lambda i: (i, 0)),
        scratch_shapes=[pltpu.VMEM((2 * H_D, 256, D_TQ), BF16),
                        pltpu.VMEM((2 * H_D, V_AUG, D_TQ), F32),
                        pltpu.VMEM((2 * H_D, 1, D_TQ), F32),
                        pltpu.VMEM((QK_AHEAD, D_KT, D_TQ), BF16)],
        compiler_params=_cparams(("parallel",)),
        name="diff_attention",
    )(qdT, kd, vdT, lam_p, dnorm_g.reshape(256, 1))


C_TQ = 128
C_PAD = N_PREV_CHUNKS * CHUNK
C_WIN = C_PAD + C_TQ
C_LINE = C_TQ + C_WIN


def _band_bias_line(rel_bias):
    dist_desc = jnp.arange(C_TQ - 1 + C_PAD, C_PAD - C_WIN, -1)
    line = rel_bias[:, jnp.clip(dist_desc, -(CHUNK - 1), MAX_REL_PAST) + (CHUNK - 1)].astype(F32)
    return jnp.pad(line, ((0, 0), (0, C_LINE - line.shape[1])))


def _band_table_init(line_ref, bias_ref):
    tq = C_TQ

    @pl.when(pl.program_id(0) == 0)
    def _():
        i = lax.broadcasted_iota(I32, (tq, C_WIN), 0)
        w = lax.broadcasted_iota(I32, (tq, C_WIN), 1)
        in_band = (w // CHUNK >= i // CHUNK) & (w // CHUNK <= i // CHUNK + N_PREV_CHUNKS)
        for h in range(H_C):
            x = jnp.broadcast_to(line_ref[h:h + 1, :], (tq, C_LINE))
            y = pltpu.roll(x, 1, 1, stride=1, stride_axis=0)
            bias_ref[h] = jnp.where(in_band, y[:, C_LINE - C_WIN:], NEG_INF)


def _band_body(qc_ref, kcT_ref, vc_ref, o_ref, bias_ref):
    tq = C_TQ
    b = pl.program_id(0)
    w0 = pl.multiple_of(b * tq, tq)
    q_all = qc_ref[...] * 0.125
    key_abs = b * tq - C_PAD + lax.broadcasted_iota(I32, (1, C_WIN), 1)
    ok = key_abs >= 0
    logits = [jnp.dot(q_all[:, h * HEAD_DIM:(h + 1) * HEAD_DIM],
                      kcT_ref[h * HEAD_DIM:(h + 1) * HEAD_DIM, pl.ds(w0, C_WIN)],
                      preferred_element_type=F32) for h in range(H_C)]
    heads = []
    for h in range(H_C):
        s = jnp.where(ok, logits[h] + bias_ref[h], NEG_INF)
        m = jnp.max(s, axis=1, keepdims=True)
        p = jnp.exp(s - m)
        l = jnp.sum(p, axis=1, keepdims=True)
        pv = jnp.dot(p.astype(BF16), vc_ref[pl.ds(w0, C_WIN), (h // 2) * LANES:(h // 2 + 1) * LANES],
                     preferred_element_type=F32)
        heads.append(pv / l)
    for pr in range(H_C // 2):
        o_ref[:, pr * LANES:(pr + 1) * LANES] = _pair_select(heads[2 * pr], heads[2 * pr + 1]).astype(o_ref.dtype)


def _band_kernel(qc_ref, kcT_ref, vc_ref, line_ref, o_ref, bias_ref):
    _band_table_init(line_ref, bias_ref)
    _band_body(qc_ref, kcT_ref, vc_ref, o_ref, bias_ref)


def _band_call(qc, kcT_pad, vc_pad, bias_line):
    s = qc.shape[0]
    return pl.pallas_call(
        _band_kernel,
        out_shape=jax.ShapeDtypeStruct((s, W_BRANCH), BF16),
        grid=(s // C_TQ,),
        in_specs=[pl.BlockSpec((C_TQ, 256), lambda i: (i, 0)),
                  pl.BlockSpec(memory_space=pltpu.VMEM),
                  pl.BlockSpec(memory_space=pltpu.VMEM),
                  _whole((H_C, C_LINE))],
        out_specs=pl.BlockSpec((C_TQ, 256), lambda i: (i, 0)),
        scratch_shapes=[pltpu.VMEM((H_C, C_TQ, C_WIN), F32)],
        compiler_params=_cparams(("arbitrary",)),
        name="band_attention",
    )(qc, kcT_pad, vc_pad, bias_line)


B_NCH = 2
B_L = B_NCH * CHUNK
HIGHEST = lax.Precision.HIGHEST


def _mlstm_state_init(tail_ref, ct_ref, n_ref, m_ref):
    @pl.when(pl.program_id(0) == 0)
    def _():
        tail_ref[...] = jnp.zeros(tail_ref.shape, F32)
        ct_ref[...] = jnp.zeros(ct_ref.shape, F32)
        n_ref[...] = jnp.zeros(n_ref.shape, F32)
        m_ref[...] = jnp.zeros(m_ref.shape, F32)


def _mlstm_kernel(qk_ref, vb_ref, ob_ref, misc_ref, cw_ref, cb_ref, gb_ref, ng_ref, o_ref,
                  tail_ref, ct_ref, n_ref, m_ref):
    _mlstm_state_init(tail_ref, ct_ref, n_ref, m_ref)
    _mlstm_body(qk_ref, vb_ref, ob_ref, misc_ref, cw_ref, cb_ref, gb_ref, ng_ref, o_ref,
                tail_ref, ct_ref, n_ref, m_ref)


def _mlstm_body(qk_ref, vb_ref, ob_ref, misc_ref, cw_ref, cb_ref, gb_ref, ng_ref, o_ref,
                tail_ref, ct_ref, n_ref, m_ref):
    L = CHUNK
    T = B_L
    hd = HEAD_DIM

    x = qk_ref[...]
    xx = jnp.concatenate([tail_ref[...], x], axis=0)
    y = jnp.broadcast_to(cb_ref[...], x.shape)
    for j in range(CONV_K):
        y = y + cw_ref[j:j + 1, :] * xx[8 - (CONV_K - 1) + j:8 - (CONV_K - 1) + j + T, :]
    tail_ref[...] = x[T - 8:T, :]
    qk = y * jax.nn.sigmoid(y)
    q_all = qk[:, :W_BRANCH]
    k_all = qk[:, W_BRANCH:] * 0.125
    v_all = vb_ref[...]
    o_gate = jax.nn.sigmoid(ob_ref[...])

    gts = misc_ref[...] + gb_ref[...]
    lf = jnp.minimum(gts, 0.0) - jnp.log1p(jnp.exp(-jnp.abs(gts)))
    r_t = lax.broadcasted_iota(I32, (T, T), 0)
    c_t = lax.broadcasted_iota(I32, (T, T), 1)
    ltri = jnp.where((c_t <= r_t) & (c_t // L == r_t // L), 1.0, 0.0)
    cum = jnp.dot(ltri, lf, precision=HIGHEST, preferred_element_type=F32)
    lane = lax.broadcasted_iota(I32, (T, LANES), 1)
    mixed = jnp.where(lane < MISC_FB, gts, cum)
    sel_r = lax.broadcasted_iota(I32, (8, LANES), 0)
    sel_c = lax.broadcasted_iota(I32, (8, LANES), 1)
    sel = jnp.where(sel_c == sel_r + MISC_IB, 1.0, 0.0)
    rows = lax.dot_general(sel, mixed, (((1,), (1,)), ((), ())), precision=HIGHEST,
                           preferred_element_type=F32)
    causal = lax.broadcasted_iota(I32, (L, L), 1) <= lax.broadcasted_iota(I32, (L, L), 0)

    heads = range(H_B)
    sl = lambda a, t0, h: a[t0:t0 + L, h * hd:(h + 1) * hd]
    pre = []
    for ci in range(B_NCH):
        t0 = ci * L
        per_head = []
        for h in heads:
            q, k, v = sl(q_all, t0, h), sl(k_all, t0, h), sl(v_all, t0, h)
            qb, kb, vbf = q.astype(BF16), k.astype(BF16), v.astype(BF16)
            qkt = lax.dot_general(qb, kb, (((1,), (1,)), ((), ())), preferred_element_type=F32)
            it_r = rows[h:h + 1, t0:t0 + L]
            cum_r = rows[H_B + h:H_B + h + 1, t0:t0 + L]
            it_c = gts[t0:t0 + L, MISC_IB + h:MISC_IB + h + 1]
            cum_c = cum[t0:t0 + L, MISC_FB + h:MISC_FB + h + 1]
            dmat = jnp.where(causal, cum_c - cum_r + it_r, NEG_INF)
            dmax = jnp.max(dmat, axis=1, keepdims=True)
            cum_end = cum_c[L - 1:L, :]
            g = cum_end - cum_c + it_c
            gmax = jnp.max(g, axis=0, keepdims=True)
            per_head.append((q, k, qb, vbf, qkt, dmat, dmax, cum_c, cum_end, g, gmax))
        pre.append(per_head)

    state = [(ct_ref[h], n_ref[h], m_ref[h]) for h in heads]
    out_chunks = []
    for ci in range(B_NCH):
        t0 = ci * L
        qc = [jnp.dot(pre[ci][h][2], state[h][0].astype(BF16), preferred_element_type=F32) for h in heads]
        mid = []
        for h in heads:
            q, k, qb, vbf, qkt, dmat, dmax, cum_c, cum_end, g, gmax = pre[ci][h]
            ct, n_row, m_prev = state[h]
            m_inter = cum_c + m_prev
            m_t = jnp.maximum(m_inter, dmax)
            w = jnp.exp(dmat - m_t) * qkt
            inter = jnp.exp(m_inter - m_t)
            m_new = jnp.maximum(cum_end + m_prev, gmax)
            carry_scale = jnp.exp(cum_end + m_prev - m_new)
            src_k = jnp.exp(g - m_new) * k
            mid.append((w, inter, m_t, m_new, carry_scale, src_k))
        wv = [jnp.dot(mid[h][0].astype(BF16), pre[ci][h][3], preferred_element_type=F32) for h in heads]
        upd = [lax.dot_general(mid[h][5].astype(BF16), pre[ci][h][3], (((0,), (0,)), ((), ())),
                               preferred_element_type=F32) for h in heads]
        outs = []
        for h in heads:
            q = pre[ci][h][0]
            w, inter, m_t, m_new, carry_scale, src_k = mid[h]
            ct, n_row, _ = state[h]
            num = inter * qc[h] + wv[h]
            den = inter * jnp.sum(q * n_row, axis=1, keepdims=True) + jnp.sum(w, axis=1, keepdims=True)
            h_t = num / jnp.maximum(jnp.abs(den), jnp.exp(-m_t))
            state[h] = (carry_scale * ct + upd[h],
                        carry_scale * n_row + jnp.sum(src_k, axis=0, keepdims=True), m_new)
            ms = jnp.mean(h_t * h_t, axis=1, keepdims=True)
            hn = h_t * lax.rsqrt(ms + EPS) * ng_ref[:, h * hd:(h + 1) * hd]
            outs.append(sl(o_gate, t0, h) * hn)
        out_chunks.append(jnp.concatenate(outs, axis=1))
    for h in heads:
        ct_ref[h], n_ref[h], m_ref[h] = state[h]
    o_ref[...] = jnp.concatenate(out_chunks, axis=0).astype(o_ref.dtype)


def _mlstm_call(qkb, vb, ob, misc, conv_w, conv_b, i_bias, f_bias, norm_g):
    s = qkb.shape[0]
    gbias = jnp.zeros((1, LANES), F32)
    gbias = gbias.at[0, MISC_IB:MISC_IB + H_B].set(i_bias).at[0, MISC_FB:MISC_FB + H_B].set(f_bias)
    row = lambda w: pl.BlockSpec((B_L, w), lambda i: (i, 0))
    return pl.pallas_call(
        _mlstm_kernel,
        out_shape=jax.ShapeDtypeStruct((s, W_BRANCH), BF16),
        grid=(s // B_L,),
        in_specs=[row(512), row(256), row(256), row(128),
                  _whole((CONV_K, 512)), _whole((1, 512)), _whole((1, LANES)), _whole((1, 256))],
        out_specs=row(256),
        scratch_shapes=[pltpu.VMEM((8, 512), F32),
                        pltpu.VMEM((H_B, HEAD_DIM, HEAD_DIM), F32),
                        pltpu.VMEM((H_B, 1, HEAD_DIM), F32),
                        pltpu.VMEM((H_B, 1, 1), F32)],
        compiler_params=_cparams(("arbitrary",)),
        name="mlstm_mixer",
    )(qkb, vb, ob, misc, conv_w, conv_b.reshape(1, 512), gbias, norm_g.reshape(1, 256))


N_B_IN, N_C_IN = 8, 4


def _mlstm_band_kernel(*refs):
    b_in = refs[:N_B_IN]
    c_in = refs[N_B_IN:N_B_IN + N_C_IN]
    yb_ref, yc_ref = refs[N_B_IN + N_C_IN:N_B_IN + N_C_IN + 2]
    b_scr = refs[N_B_IN + N_C_IN + 2:N_B_IN + N_C_IN + 6]
    (bias_ref,) = refs[N_B_IN + N_C_IN + 6:]
    qc_ref, kcT_ref, vc_ref, line_ref = c_in
    _mlstm_state_init(*b_scr)
    _band_table_init(line_ref, bias_ref)
    _mlstm_body(*b_in, yb_ref, *b_scr)
    _band_body(qc_ref, kcT_ref, vc_ref, yc_ref, bias_ref)


def _mlstm_band_call(qkb, vb, ob, misc, conv_w, conv_b, i_bias, f_bias, norm_g,
                     qc, kcT_pad, vc_pad, bias_line):
    assert B_L == C_TQ
    s = qkb.shape[0]
    gbias = jnp.zeros((1, LANES), F32)
    gbias = gbias.at[0, MISC_IB:MISC_IB + H_B].set(i_bias).at[0, MISC_FB:MISC_FB + H_B].set(f_bias)
    row = lambda w: pl.BlockSpec((B_L, w), lambda i: (i, 0))
    vm = pl.BlockSpec(memory_space=pltpu.VMEM)
    return pl.pallas_call(
        _mlstm_band_kernel,
        out_shape=(jax.ShapeDtypeStruct((s, W_BRANCH), BF16), jax.ShapeDtypeStruct((s, W_BRANCH), BF16)),
        grid=(s // B_L,),
        in_specs=[row(512), row(256), row(256), row(128),
                  _whole((CONV_K, 512)), _whole((1, 512)), _whole((1, LANES)), _whole((1, 256)),
                  row(256), vm, vm, _whole((H_C, C_LINE))],
        out_specs=(row(256), row(256)),
        scratch_shapes=[pltpu.VMEM((8, 512), F32),
                        pltpu.VMEM((H_B, HEAD_DIM, HEAD_DIM), F32),
                        pltpu.VMEM((H_B, 1, HEAD_DIM), F32),
                        pltpu.VMEM((H_B, 1, 1), F32),
                        pltpu.VMEM((H_C, C_TQ, C_WIN), F32)],
        compiler_params=_cparams(("arbitrary",)),
        name="mlstm_band",
    )(qkb, vb, ob, misc, conv_w, conv_b.reshape(1, 512), gbias, norm_g.reshape(1, 256),
      qc, kcT_pad, vc_pad, bias_line)


M_TM = 256
ROUTE_LOGIT0 = N_GROUPS
RT_E1, RT_E2, RT_G1, RT_G2 = 0, 1, 2, 3


def _lane_argmax(vals, lane):
    v = jnp.max(vals, axis=1, keepdims=True)
    idx = jnp.min(jnp.where(vals == v, lane, float(LANES)), axis=1, keepdims=True)
    return v, idx


def _merge_kernel(x_ref, ya_ref, yb_ref, yc_ref, yd_ref, gm_ref, wg_ref, wb_ref, wo_ref, gf_ref,
                  wr_ref, rb_ref, xo_ref, h2_ref, rt_ref):
    x = x_ref[...]
    h = _rms(x, gm_ref[...]).astype(BF16)
    mixed = jnp.zeros(x.shape, F32)
    for n, y_ref in enumerate((ya_ref, yb_ref, yc_ref, yd_ref)):
        gate = jax.nn.sigmoid(jnp.dot(h, wg_ref[:, n * D_MODEL:(n + 1) * D_MODEL],
                                      preferred_element_type=F32))
        up = jnp.dot(y_ref[...], wb_ref[n], preferred_element_type=F32)
        mixed = mixed + gate * up
    xn = x + jnp.dot(mixed.astype(BF16), wo_ref[...], preferred_element_type=F32)
    xo_ref[...] = xn
    h2 = _rms(xn, gf_ref[...])
    _store_row_tiled(h2_ref, h2)
    logits = jnp.dot(h2.astype(BF16), wr_ref[...], preferred_element_type=F32) + rb_ref[...]

    lane = lax.broadcasted_iota(I32, logits.shape, 1).astype(F32)
    neg = -jnp.inf
    gmask = lane < N_GROUPS
    gmax, g_sel = _lane_argmax(jnp.where(gmask, logits, neg), lane)
    g_gate = 1.0 / jnp.sum(jnp.where(gmask, jnp.exp(logits - gmax), 0.0), axis=1, keepdims=True)
    e_lo = ROUTE_LOGIT0 + EXPERTS_PER_GROUP * g_sel
    el = jnp.where((lane >= e_lo) & (lane < e_lo + EXPERTS_PER_GROUP), logits, neg)
    v1, i1 = _lane_argmax(el, lane)
    v2, i2 = _lane_argmax(jnp.where(lane == i1, neg, el), lane)
    e = jnp.exp(v2 - v1)
    p1 = 1.0 / (1.0 + e)
    p2 = e / (1.0 + e)
    rt = jnp.where(lane == RT_E1, i1 - ROUTE_LOGIT0,
                   jnp.where(lane == RT_E2, i2 - ROUTE_LOGIT0,
                             jnp.where(lane == RT_G1, p1 * g_gate,
                                       jnp.where(lane == RT_G2, p2 * g_gate, 0.0))))
    rt_ref[...] = rt


def _merge_call(x2, ya, yb, yc, yd, g_mix, w_gate, w_branch, w_out, g_ffn, w_route, b_route):
    s = x2.shape[0]
    row = lambda w: pl.BlockSpec((M_TM, w), lambda i: (i, 0))
    vm = pl.BlockSpec(memory_space=pltpu.VMEM)
    return pl.pallas_call(
        _merge_kernel,
        out_shape=(jax.ShapeDtypeStruct((s, D_MODEL), F32),
                   jax.ShapeDtypeStruct((s * ROW_TILES, LANES), F32),
                   jax.ShapeDtypeStruct((s, LANES), F32)),
        grid=(s // M_TM,),
        in_specs=[row(D_MODEL), row(256), row(256), row(256), row(256),
                  vm, vm, vm, vm, vm, vm, vm],
        out_specs=(row(D_MODEL), pl.BlockSpec((M_TM * ROW_TILES, LANES), lambda i: (i, 0)), row(LANES)),
        compiler_params=_cparams(("parallel",)),
        name="merge_route",
    )(x2, ya, yb, yc, yd, g_mix.reshape(1, D_MODEL), w_gate, w_branch, w_out,
      g_ffn.reshape(1, D_MODEL), w_route, b_route)


E_BM = 256
GATHER_UNROLL = 8


ROW_TILES = D_MODEL // LANES


def _store_row_tiled(ref, x):
    m = x.shape[0]
    for c in range(ROW_TILES):
        ref[pl.ds(c, m, stride=ROW_TILES), :] = x[:, c * LANES:(c + 1) * LANES]


def _load_row_tiled(ref, m, row0=0):
    return jnp.concatenate([ref[pl.ds(row0 * ROW_TILES + c, m, stride=ROW_TILES), :]
                            for c in range(ROW_TILES)], axis=1)


def _gather_rows(src_hbm, idx_ref, dst, sem, n, wait, inline=False):
    def one(r, parity):
        src_row = pl.multiple_of(idx_ref[0, 0, r] * ROW_TILES, ROW_TILES)
        dst_row = r * ROW_TILES if isinstance(r, int) else pl.multiple_of(r * ROW_TILES, ROW_TILES)
        cp = pltpu.make_async_copy(src_hbm.at[pl.ds(src_row, ROW_TILES)],
                                   dst.at[pl.ds(dst_row, ROW_TILES)], sem)
        if wait:
            cp.wait()
        else:
            cp.start(priority=parity)

    if inline:
        for r in range(n):
            one(r, r % 2)
    else:
        def group(g, c):
            for u in range(GATHER_UNROLL):
                one(g * GATHER_UNROLL + u, u % 2)
            return c
        lax.fori_loop(0, n // GATHER_UNROLL, group, 0)


E_AHEAD = 2
E_NBUF = E_AHEAD + 1


def _moe_kernel(be_ref, nused_ref, tok_ref, tok1_ref, tok_next_ref, h2_hbm, w1_ref, w3_ref, w2_ref, o_ref,
                xbuf, w1b, w3b, w2b, sem):
    i = pl.program_id(0)
    nused = nused_ref[0]
    used = i < nused
    slot = i % E_NBUF
    slot_next = (i + E_AHEAD) % E_NBUF

    @pl.when((i == 0) & used)
    def _():
        _gather_rows(h2_hbm, tok_ref, xbuf.at[0], sem.at[0], E_BM, wait=False)
        _gather_rows(h2_hbm, tok1_ref, xbuf.at[1], sem.at[1], E_BM, wait=False)

    @pl.when(used)
    def _():
        prev = be_ref[jnp.maximum(i - 1, 0)]

        @pl.when((i == 0) | (be_ref[i] != prev))
        def _():
            w1b[...] = w1_ref[0, 0].astype(BF16)
            w3b[...] = w3_ref[0, 0].astype(BF16)
            w2b[...] = w2_ref[0, 0].astype(BF16)

        _gather_rows(h2_hbm, tok_ref, xbuf.at[slot], sem.at[slot], E_BM, wait=True)
        _gather_rows(h2_hbm, tok_next_ref, xbuf.at[slot_next], sem.at[slot_next], E_BM, wait=False, inline=True)
        xb = _load_row_tiled(xbuf.at[slot], E_BM).astype(BF16)
        a = jnp.dot(xb, w1b[...], preferred_element_type=F32)
        g = jnp.dot(xb, w3b[...], preferred_element_type=F32)
        hid = (a * jax.nn.sigmoid(a) * g).astype(BF16)
        _store_row_tiled(o_ref, jnp.dot(hid, w2b[...], preferred_element_type=F32))

    @pl.when(jnp.logical_not(used))
    def _():
        @pl.when(i < nused + E_AHEAD)
        def _():
            _gather_rows(h2_hbm, tok_ref, xbuf.at[slot], sem.at[slot], E_BM, wait=True)
        o_ref[...] = jnp.zeros(o_ref.shape, F32)


def _moe_call(h2, row_tok, block_e, nused, w1, w3, w2, layer):
    n_rows = row_tok.shape[0]
    n_blocks = n_rows // E_BM
    wspec = lambda shp: pl.BlockSpec((1, 1) + shp, lambda i, be, nu: (layer, be[i], 0, 0))
    grid_spec = pltpu.PrefetchScalarGridSpec(
        num_scalar_prefetch=2,
        grid=(n_blocks,),
        in_specs=[pl.BlockSpec((1, 1, E_BM), lambda i, be, nu: (i, 0, 0), memory_space=pltpu.SMEM),
                  pl.BlockSpec((1, 1, E_BM), lambda i, be, nu: (jnp.minimum(i + 1, n_blocks - 1), 0, 0),
                               memory_space=pltpu.SMEM),
                  pl.BlockSpec((1, 1, E_BM), lambda i, be, nu: (jnp.minimum(i + E_AHEAD, n_blocks - 1), 0, 0),
                               memory_space=pltpu.SMEM),
                  pl.BlockSpec(memory_space=pl.ANY),
                  wspec((D_MODEL, F_EXPERT)), wspec((D_MODEL, F_EXPERT)), wspec((F_EXPERT, D_MODEL))],
        out_specs=pl.BlockSpec((E_BM * ROW_TILES, LANES), lambda i, be, nu: (i, 0)),
        scratch_shapes=[pltpu.VMEM((E_NBUF, E_BM * ROW_TILES, LANES), F32),
                        pltpu.VMEM((D_MODEL, F_EXPERT), BF16),
                        pltpu.VMEM((D_MODEL, F_EXPERT), BF16),
                        pltpu.VMEM((F_EXPERT, D_MODEL), BF16),
                        pltpu.SemaphoreType.DMA((E_NBUF,))],
    )
    tok3 = row_tok.reshape(n_blocks, 1, E_BM)
    return pl.pallas_call(
        _moe_kernel,
        out_shape=jax.ShapeDtypeStruct((n_rows * ROW_TILES, LANES), F32),
        grid_spec=grid_spec,
        compiler_params=_cparams(("arbitrary",)),
        name="moe_experts",
    )(block_e, nused, tok3, tok3, tok3, h2, w1, w3, w2)


CB_TM = 256


def _combine_kernel(final, pos_ref, pos_next_ref, x_ref, rt_ref, ys_hbm, gfin_ref, o_ref, buf, sem):
    i = pl.program_id(0)
    slot = i % 2
    n = 2 * CB_TM

    @pl.when(i == 0)
    def _():
        _gather_rows(ys_hbm, pos_ref, buf.at[0], sem.at[0], n, wait=False)

    @pl.when(i + 1 < pl.num_programs(0))
    def _():
        _gather_rows(ys_hbm, pos_next_ref, buf.at[1 - slot], sem.at[1 - slot], n, wait=False)
    _gather_rows(ys_hbm, pos_ref, buf.at[slot], sem.at[slot], n, wait=True)

    rt = rt_ref[...]
    y = (_load_row_tiled(buf.at[slot], CB_TM) * rt[:, RT_G1:RT_G1 + 1]
         + _load_row_tiled(buf.at[slot], CB_TM, CB_TM) * rt[:, RT_G2:RT_G2 + 1])
    xn = x_ref[...] + y
    if final:
        xn = _rms(xn, gfin_ref[...])
    o_ref[...] = xn


def _combine_call(x2, rt, ys, pos, g_final, final):
    s = x2.shape[0]
    nb = s // CB_TM
    pos_b = pos.reshape(nb, CB_TM, 2).transpose(0, 2, 1).reshape(nb, 1, 2 * CB_TM)
    return pl.pallas_call(
        functools.partial(_combine_kernel, final),
        out_shape=jax.ShapeDtypeStruct((s, D_MODEL), F32),
        grid=(nb,),
        in_specs=[pl.BlockSpec((1, 1, 2 * CB_TM), lambda i: (i, 0, 0), memory_space=pltpu.SMEM),
                  pl.BlockSpec((1, 1, 2 * CB_TM), lambda i: (jnp.minimum(i + 1, nb - 1), 0, 0),
                               memory_space=pltpu.SMEM),
                  pl.BlockSpec((CB_TM, D_MODEL), lambda i: (i, 0)),
                  pl.BlockSpec((CB_TM, LANES), lambda i: (i, 0)),
                  pl.BlockSpec(memory_space=pl.ANY),
                  _whole((1, D_MODEL))],
        out_specs=pl.BlockSpec((CB_TM, D_MODEL), lambda i: (i, 0)),
        scratch_shapes=[pltpu.VMEM((2, 2 * CB_TM * ROW_TILES, LANES), F32), pltpu.SemaphoreType.DMA((2,))],
        compiler_params=_cparams(("arbitrary",)),
        name="moe_combine",
    )(pos_b, pos_b, x2, rt, ys, g_final.reshape(1, D_MODEL))


PLAN_BLOCK = 256


def _dispatch_plan(rt):
    t = rt.shape[0]
    n_assign = t * TOP_K_INNER
    eid = rt[:, RT_E1:RT_E2 + 1].astype(I32).reshape(n_assign)
    onehot = (eid[:, None] == jnp.arange(N_EXPERTS, dtype=I32)[None, :]).astype(F32)
    pb = PLAN_BLOCK
    oh3 = onehot.reshape(n_assign // pb, pb, N_EXPERTS)
    tri = jnp.tril(jnp.ones((pb, pb), F32), k=-1)
    within = jnp.einsum('ij,bjk->bik', tri, oh3)
    block_tot = jnp.sum(oh3, axis=1)
    block_off = jnp.cumsum(block_tot, axis=0) - block_tot
    counts = (block_off[-1] + block_tot[-1]).astype(I32)
    rank = jnp.sum(oh3 * (within + block_off[:, None, :]), axis=-1).reshape(n_assign)
    padded = (counts + E_BM - 1) // E_BM * E_BM
    ends_pad = jnp.cumsum(padded)
    starts_pad = ends_pad - padded
    dest = (jnp.sum(onehot * starts_pad.astype(F32)[None, :], axis=-1) + rank).astype(I32)
    n_rows = (-(-(n_assign + N_EXPERTS * (E_BM - 1)) // E_BM) + E_AHEAD) * E_BM
    n_blocks = n_rows // E_BM
    tok = jnp.repeat(jnp.arange(t, dtype=I32), TOP_K_INNER)
    row_tok = jnp.zeros((n_rows,), I32).at[dest].set(tok)
    block_start = jnp.arange(n_blocks, dtype=I32) * E_BM
    block_e = jnp.minimum(jnp.sum((ends_pad[None, :] <= block_start[:, None]).astype(I32), axis=1),
                          N_EXPERTS - 1)
    nused = (ends_pad[-1] // E_BM).astype(I32).reshape(1)
    return row_tok, block_e, nused, dest.reshape(t, TOP_K_INNER)


def _layer(x2, layer_idx, p, experts, final_g, final):
    (norm_mix_g, w_in, conv_w, conv_b, i_bias, f_bias, mnorm_g, rel_bias, lam_p, dnorm_g,
     w_branch, w_gate, w_out, norm_ffn_g, rgw, rgb, rew, reb) = p
    o = _k1_call(x2, norm_mix_g, *_rearrange_w_in(w_in))

    ya = _dsa_call(o["qaT"], o["qiT"], o["wiT"], o["ka"], o["vaT"], o["misc_bf"])
    kcT_pad = jnp.pad(o["kcT"], ((0, 0), (C_PAD, 0)))
    vc_pad = jnp.pad(o["vc"], ((C_PAD, 0), (0, 0)))
    yb, yc = _mlstm_band_call(o["qkb"], o["vb"], o["ob"], o["misc"], conv_w, conv_b, i_bias, f_bias, mnorm_g,
                              o["qc"], kcT_pad, vc_pad, _band_bias_line(rel_bias))
    lam_init = 0.8 - 0.6 * math.exp(-0.3 * layer_idx)
    yd = _diff_call(o["qdT"], o["kd"], o["vdT"], lam_p, dnorm_g, lam_init)

    w_route = jnp.concatenate([rgw, rew, jnp.zeros((D_MODEL, LANES - N_GROUPS - N_EXPERTS), F32)],
                              axis=1).astype(BF16)
    b_route = jnp.concatenate([rgb, reb, jnp.zeros((LANES - N_GROUPS - N_EXPERTS,), F32)]).reshape(1, LANES)
    xn, h2, rt = _merge_call(x2, ya, yb, yc, yd, norm_mix_g, w_gate.astype(BF16), w_branch.astype(BF16),
                             w_out.astype(BF16), norm_ffn_g, w_route, b_route)
    row_tok, block_e, nused, pos = _dispatch_plan(rt)
    ys = _moe_call(h2, row_tok, block_e, nused, *experts, layer_idx)
    return _combine_call(xn, rt, ys, pos, final_g, final)


def kernel(x, norm_mix_g, w_in, conv_w, conv_b, mlstm_i_bias, mlstm_f_bias, mlstm_norm_g, relpos_bias, diff_lambda, diff_norm_g, w_branch, w_gate, w_out, norm_ffn_g, router_group_w, router_group_b, router_expert_w, router_expert_b, expert_w1, expert_w3, expert_w2, final_norm_g):
    assert x.shape[0] == 1 and x.shape[2] == D_MODEL
    params = (norm_mix_g, w_in, conv_w, conv_b, mlstm_i_bias, mlstm_f_bias, mlstm_norm_g, relpos_bias,
              diff_lambda, diff_norm_g, w_branch, w_gate, w_out, norm_ffn_g, router_group_w,
              router_group_b, router_expert_w, router_expert_b)
    experts = (expert_w1, expert_w3, expert_w2)
    depth = norm_mix_g.shape[0]
    x2 = x[0]
    for l in range(depth):
        x2 = _layer(x2, l, tuple(a[l] for a in params), experts, final_norm_g, l == depth - 1)
    return x2[None]
```
